```python
import jax
import jax.numpy as jnp
from jax import lax
import numpy as np

D_MODEL = 2048
BATCH = 8
SEQ = 2048
DEPTH = 2
DEC_BATCH = 128
DEC_SEQ = 1
PAST_LEN = 8192
PAGE_SIZE = 128

N_MEM = 256
D_FF = 5632
EPS = 1e-6
ROPE_THETA = 10000.0
NEG_INF = -1e30
Q_BLOCK = 128
GLA_HEADS = 4
GLA_DK = 128
GLA_DV = 256
GLA_GATE_RANK = 16
GLA_GATE_TAU = 16.0
GLA_CHUNK = 64
DSA_HEADS = 8
DSA_KV_HEADS = 2
DSA_HD = 128
IDX_HEADS = 8
IDX_HD = 64
DSA_TOPK_MAX = 256
SWA_HEADS = 32
SWA_KV_HEADS = 4
SWA_HD = 64
WINDOW = 128
MEM_HEADS = 4
MEM_HD = 128

AB_SIZES = (GLA_HEADS * GLA_DK, GLA_HEADS * GLA_DK, GLA_HEADS * GLA_DV, GLA_HEADS * GLA_DV, GLA_GATE_RANK,
            DSA_HEADS * DSA_HD, DSA_KV_HEADS * DSA_HD, DSA_KV_HEADS * DSA_HD,
            IDX_HEADS * IDX_HD, IDX_HEADS, IDX_HD)
C_SIZES = (SWA_HEADS * SWA_HD, SWA_KV_HEADS * SWA_HD, SWA_KV_HEADS * SWA_HD)

kernel_name = 'hybrid_gla_dsa_swa_macaron_step'


def rmsnorm(x, g):
    xf = x.astype(jnp.float32)
    y = xf * lax.rsqrt(jnp.mean(xf * xf, axis=-1, keepdims=True) + EPS)
    return (y * g.astype(jnp.float32)).astype(x.dtype)


def rope(x, pos):
    half = x.shape[-1] // 2
    inv = ROPE_THETA ** (-jnp.arange(half, dtype=jnp.float32) / half)
    ang = pos.astype(jnp.float32)[:, None] * inv[None, :]
    cos = jnp.cos(ang)[:, None, :]
    sin = jnp.sin(ang)[:, None, :]
    xf = x.astype(jnp.float32)
    x1, x2 = xf[..., :half], xf[..., half:]
    return jnp.concatenate([x1 * cos - x2 * sin, x2 * cos + x1 * sin], axis=-1).astype(x.dtype)


def split_cols(z, sizes):
    offs = np.cumsum((0,) + tuple(sizes))
    return [z[..., int(offs[i]):int(offs[i + 1])] for i in range(len(sizes))]


def swiglu(h, wg, wu, wd):
    return (jax.nn.silu(h @ wg) * (h @ wu)) @ wd


def to_pages(a):
    return a.reshape((a.shape[0], a.shape[1] // PAGE_SIZE, PAGE_SIZE) + a.shape[2:])


def ab_features(z, pos, w_gate_up, b_gate, idx_k_norm):
    B, T, _ = z.shape
    gq, gk, gv, gr, gd, dq, dk, dv, iq, iw, ik = split_cols(z, AB_SIZES)
    log_a = jax.nn.log_sigmoid((gd @ w_gate_up + b_gate).astype(jnp.float32)) / GLA_GATE_TAU
    gla = (gq.reshape(B, T, GLA_HEADS, GLA_DK) * (GLA_DK ** -0.5),
           gk.reshape(B, T, GLA_HEADS, GLA_DK),
           gv.reshape(B, T, GLA_HEADS, GLA_DV),
           log_a.reshape(B, T, GLA_HEADS, GLA_DK),
           gr.reshape(B, T, GLA_HEADS, GLA_DV))
    dsa = (rope(dq.reshape(B, T, DSA_HEADS, DSA_HD), pos),
           rope(dk.reshape(B, T, DSA_KV_HEADS, DSA_HD), pos),
           dv.reshape(B, T, DSA_KV_HEADS, DSA_HD),
           rope(iq.reshape(B, T, IDX_HEADS, IDX_HD), pos) * (IDX_HD ** -0.5),
           iw * (IDX_HEADS ** -0.5),
           rope(rmsnorm(ik, idx_k_norm)[:, :, None, :], pos)[:, :, 0, :])
    return gla, dsa


def c_features(z, pos):
    B, T, _ = z.shape
    q, k, v = split_cols(z, C_SIZES)
    return (rope(q.reshape(B, T, SWA_HEADS, SWA_HD), pos),
            rope(k.reshape(B, T, SWA_KV_HEADS, SWA_HD), pos),
            v.reshape(B, T, SWA_KV_HEADS, SWA_HD))


def gla_chunked(q, k, v, log_a, s0):
    B, T, H, DK = q.shape
    DV = v.shape[-1]
    N = T // GLA_CHUNK

    def chunks(a):
        return a.astype(jnp.float32).reshape(B, N, GLA_CHUNK, H, a.shape[-1]).transpose(1, 0, 3, 2, 4)

    qc, kc, vc = chunks(q), chunks(k), chunks(v)
    b = jnp.cumsum(chunks(log_a), axis=3)
    b_end = b[:, :, :, -1:, :]
    q_in = qc * jnp.exp(b)
    k_in = kc * jnp.exp(-b)
    k_end = kc * jnp.exp(b_end - b)
    decay = jnp.exp(b_end[:, :, :, 0, :])
    causal = jnp.tril(jnp.ones((GLA_CHUNK, GLA_CHUNK), dtype=bool))

    def step(S, inp):
        qi, ki, ke, vi, di = inp
        att = jnp.where(causal, jnp.einsum('bhtd,bhsd->bhts', qi, ki), 0.0)
        o = jnp.einsum('bhtd,bhde->bhte', qi, S) + jnp.einsum('bhts,bhse->bhte', att, vi)
        S = S * di[..., None] + jnp.einsum('bhsd,bhse->bhde', ke, vi)
        return S, o

    S, o = lax.scan(step, s0, (q_in, k_in, k_end, vc, decay))
    return o.transpose(1, 0, 3, 2, 4).reshape(B, T, H, DV), S


def gla_recurrent(q, k, v, log_a, s0):
    f = lambda a: jnp.moveaxis(a.astype(jnp.float32), 1, 0)

    def step(S, inp):
        qt, kt, vt, lat = inp
        S = S * jnp.exp(lat)[..., None] + kt[..., :, None] * vt[..., None, :]
        return S, jnp.einsum('bhd,bhde->bhe', qt, S)

    S, o = lax.scan(step, s0, (f(q), f(k), f(v), f(log_a)))
    return jnp.moveaxis(o, 0, 1), S


def gla_output(o, r, gain):
    B, T, H, DV = o.shape
    g = rmsnorm(o, gain) * jax.nn.silu(r.astype(jnp.float32))
    return g.reshape(B, T, H * DV).astype(r.dtype)


def indexer_scores(iq, iw, ik):
    dots = jnp.einsum('bqhd,bsd->bqhs', iq.astype(jnp.float32), ik.astype(jnp.float32))
    return jnp.einsum('bqhs,bqh->bqs', jax.nn.relu(dots), iw.astype(jnp.float32))


def sparse_attend(q, ks, vs, ok):
    B, Q, H, hd = q.shape
    KV = ks.shape[-2]
    qg = q.reshape(B, Q, KV, H // KV, hd).astype(jnp.float32)
    s = jnp.einsum('bqvgd,bqnvd->bqvgn', qg, ks.astype(jnp.float32)) * (hd ** -0.5)
    s = jnp.where(ok[:, :, None, None, :], s, NEG_INF)
    p = jax.nn.softmax(s, axis=-1)
    o = jnp.einsum('bqvgn,bqnvd->bqvgd', p, vs.astype(jnp.float32))
    return o.reshape(B, Q, H, hd).astype(q.dtype)


def dsa_prompt(q, k, v, iq, iw, ik):
    B, T, H, hd = q.shape
    topk = min(DSA_TOPK_MAX, T // 4)
    nb = T // Q_BLOCK
    to_blocks = lambda a: jnp.moveaxis(a.reshape((B, nb, Q_BLOCK) + a.shape[2:]), 1, 0)
    key_pos = jnp.arange(T)
    gather_rows = jax.vmap(lambda rows, idx: rows[idx])

    def one_block(args):
        b_idx, q_b, iq_b, iw_b = args
        q_pos = b_idx * Q_BLOCK + jnp.arange(Q_BLOCK)
        score = indexer_scores(iq_b, iw_b, ik)
        score = jnp.where(key_pos[None, None, :] <= q_pos[None, :, None], score, NEG_INF)
        _, sel = lax.top_k(score, topk)
        ok = sel <= q_pos[None, :, None]
        return sparse_attend(q_b, gather_rows(k, sel), gather_rows(v, sel), ok)

    out = lax.map(one_block, (jnp.arange(nb), to_blocks(q), to_blocks(iq), to_blocks(iw)))
    return jnp.moveaxis(out, 0, 1).reshape(B, T, H * hd)


def dsa_step(q, k, v, iq, iw, ik, pool_k, pool_v, pool_ik, page_table):
    B, S, H, hd = q.shape
    L = PAST_LEN + S
    topk = min(DSA_TOPK_MAX, L // 4)
    past_ik = pool_ik[page_table].reshape(B, PAST_LEN, IDX_HD)
    score = indexer_scores(iq, iw, jnp.concatenate([past_ik.astype(ik.dtype), ik], axis=1))
    q_pos = PAST_LEN + jnp.arange(S)
    score = jnp.where(jnp.arange(L)[None, None, :] <= q_pos[None, :, None], score, NEG_INF)
    _, sel = lax.top_k(score, topk)
    ok = sel <= q_pos[None, :, None]
    p_idx = jnp.minimum(sel, PAST_LEN - 1)
    phys = jax.vmap(lambda pt, i: pt[i])(page_table, p_idx // PAGE_SIZE)
    off = p_idx % PAGE_SIZE
    n_idx = jnp.clip(sel - PAST_LEN, 0, S - 1)
    is_new = (sel >= PAST_LEN)[..., None, None]
    gather_rows = jax.vmap(lambda rows, idx: rows[idx])
    ks = jnp.where(is_new, gather_rows(k, n_idx), pool_k[phys, off])
    vs = jnp.where(is_new, gather_rows(v, n_idx), pool_v[phys, off])
    return sparse_attend(q, ks, vs, ok).reshape(B, S, H * hd)


def sink_attend(q, k, v, ok, sinks):
    B, Q, KV, G, hd = q.shape
    s = jnp.einsum('bqvgd,bkvd->bvgqk', q.astype(jnp.float32), k.astype(jnp.float32)) * (hd ** -0.5)
    s = jnp.where(ok, s, NEG_INF)
    sink = sinks.astype(jnp.float32).reshape(1, KV, G, 1, 1)
    m = jnp.maximum(jnp.max(s, axis=-1, keepdims=True), sink)
    p = jnp.exp(s - m)
    p = p / (jnp.sum(p, axis=-1, keepdims=True) + jnp.exp(sink - m))
    o = jnp.einsum('bvgqk,bkvd->bqvgd', p, v.astype(jnp.float32))
    return o.reshape(B, Q, KV * G * hd).astype(q.dtype)


def swa_prompt(q, k, v, sinks):
    B, T, H, hd = q.shape
    KV = k.shape[2]
    nb = T // Q_BLOCK
    qb = q.reshape(B, nb, Q_BLOCK, KV, H // KV, hd)

    def with_prev(a):
        ab = a.reshape(B, nb, Q_BLOCK, KV, hd)
        prev = jnp.concatenate([jnp.zeros_like(ab[:, :1]), ab[:, :-1]], axis=1)
        return jnp.concatenate([prev, ab], axis=2)

    kb, vb = with_prev(k), with_prev(v)
    rel = (Q_BLOCK + jnp.arange(Q_BLOCK))[:, None] - jnp.arange(2 * Q_BLOCK)[None, :]
    band = (rel >= 0) & (rel <= WINDOW)

    def one_block(args):
        b_idx, q_b, k_b, v_b = args
        key_ok = band & (((b_idx - 1) * Q_BLOCK + jnp.arange(2 * Q_BLOCK)) >= 0)[None, :]
        return sink_attend(q_b, k_b, v_b, key_ok, sinks)

    out = lax.map(one_block, (jnp.arange(nb), jnp.moveaxis(qb, 1, 0), jnp.moveaxis(kb, 1, 0), jnp.moveaxis(vb, 1, 0)))
    return jnp.moveaxis(out, 0, 1).reshape(B, T, H * hd)


def swa_step(q, k, v, buf_k, buf_v, sinks):
    B, S, H, hd = q.shape
    KV = k.shape[2]
    wb = buf_k.shape[1]
    keys = jnp.concatenate([buf_k.astype(k.dtype), k], axis=1)
    vals = jnp.concatenate([buf_v.astype(v.dtype), v], axis=1)
    k_pos = PAST_LEN - wb + jnp.arange(wb + S)
    q_pos = PAST_LEN + jnp.arange(S)
    rel = q_pos[:, None] - k_pos[None, :]
    ok = (rel >= 0) & (rel <= WINDOW)
    out = sink_attend(q.reshape(B, S, KV, H // KV, hd), keys, vals, ok, sinks)
    return out, keys[:, -WINDOW:], vals[:, -WINDOW:]


def mem_project(mem, g, w_kv):
    B, M, _ = mem.shape
    kv = rmsnorm(mem, g) @ w_kv
    k, v = jnp.split(kv, 2, axis=-1)
    return k.reshape(B, M, MEM_HEADS, MEM_HD), v.reshape(B, M, MEM_HEADS, MEM_HD)


def mem_attend(h, mk, mv, w_q, w_o):
    B, T, _ = h.shape
    q = (h @ w_q).reshape(B, T, MEM_HEADS, MEM_HD)
    s = jnp.einsum('bthd,bmhd->bhtm', q.astype(jnp.float32), mk.astype(jnp.float32)) * (MEM_HD ** -0.5)
    p = jax.nn.softmax(s, axis=-1)
    o = jnp.einsum('bhtm,bmhd->bthd', p, mv.astype(jnp.float32)).reshape(B, T, MEM_HEADS * MEM_HD)
    return o.astype(h.dtype) @ w_o


def trunk(x, mem_k, mem_v, mix_ab, mix_c, norm_ffn, w_ffn_gate, w_ffn_up, w_ffn_down, norm_mix,
          w_in_ab, w_out_ab, w_in_c, w_out_c, norm_mem_q, w_mem_q, w_mem_o, final_norm):
    states = []
    for layer in range(DEPTH):
        x = x + 0.5 * swiglu(rmsnorm(x, norm_ffn[layer, 0]), w_ffn_gate[layer, 0], w_ffn_up[layer, 0], w_ffn_down[layer, 0])
        h = rmsnorm(x, norm_mix[layer])
        i = layer // 2
        if layer % 2 == 0:
            y, st = mix_ab(h @ w_in_ab[i], i)
            x = x + y @ w_out_ab[i]
        else:
            y, st = mix_c(h @ w_in_c[i], i)
            x = x + y @ w_out_c[i]
        states.append(st)
        x = x + mem_attend(rmsnorm(x, norm_mem_q[layer]), mem_k[layer], mem_v[layer], w_mem_q[layer], w_mem_o[layer])
        x = x + 0.5 * swiglu(rmsnorm(x, norm_ffn[layer, 1]), w_ffn_gate[layer, 1], w_ffn_up[layer, 1], w_ffn_down[layer, 1])
    return rmsnorm(x, final_norm), states


def setup_inputs(seed: int = 0) -> dict:
    key = jax.random.key(seed)
    ks = iter(jax.random.split(key, 48))
    f32 = jnp.float32
    n_ab = (DEPTH + 1) // 2
    n_c = DEPTH // 2
    n_pages = PAST_LEN // PAGE_SIZE
    n_used = DEC_BATCH * n_pages
    n_pool = n_used + (n_used + 3) // 4
    ab_width = sum(AB_SIZES)
    c_width = sum(C_SIZES)
    mem_w = MEM_HEADS * MEM_HD

    def nrm(shape, scale=1.0):
        return jax.random.normal(next(ks), shape, f32) * scale

    def w(shape):
        return nrm(shape, shape[-2] ** -0.5)

    def gain(shape):
        return 1.0 + nrm(shape, 0.02)

    page_table = jax.random.permutation(next(ks), n_pool)[:n_used].reshape(DEC_BATCH, n_pages).astype(jnp.int32)
    return {
        'x_prompt': nrm((BATCH, SEQ, D_MODEL)),
        'x_sample': nrm((DEC_BATCH, DEC_SEQ, D_MODEL)),
        'mem_prompt': nrm((BATCH, N_MEM, D_MODEL)),
        'cache_dsa_k': nrm((n_ab, n_pool, PAGE_SIZE, DSA_KV_HEADS, DSA_HD)),
        'cache_dsa_v': nrm((n_ab, n_pool, PAGE_SIZE, DSA_KV_HEADS, DSA_HD)),
        'cache_dsa_idx_k': nrm((n_ab, n_pool, PAGE_SIZE, IDX_HD)),
        'state_gla': nrm((n_ab, DEC_BATCH, GLA_HEADS, GLA_DK, GLA_DV), 2.0),
        'cache_swa_k': nrm((n_c, DEC_BATCH, WINDOW, SWA_KV_HEADS, SWA_HD)),
        'cache_swa_v': nrm((n_c, DEC_BATCH, WINDOW, SWA_KV_HEADS, SWA_HD)),
        'cache_mem_k': nrm((DEPTH, DEC_BATCH, N_MEM, MEM_HEADS, MEM_HD)),
        'cache_mem_v': nrm((DEPTH, DEC_BATCH, N_MEM, MEM_HEADS, MEM_HD)),
        'page_table': page_table,
        'norm_ffn': gain((DEPTH, 2, D_MODEL)),
        'w_ffn_gate': w((DEPTH, 2, D_MODEL, D_FF)),
        'w_ffn_up': w((DEPTH, 2, D_MODEL, D_FF)),
        'w_ffn_down': w((DEPTH, 2, D_FF, D_MODEL)),
        'norm_mix': gain((DEPTH, D_MODEL)),
        'w_in_ab': w((n_ab, D_MODEL, ab_width)),
        'w_gla_gate_up': w((n_ab, GLA_GATE_RANK, GLA_HEADS * GLA_DK)),
        'b_gla_gate': nrm((n_ab, GLA_HEADS * GLA_DK), 0.1),
        'gla_out_norm': gain((n_ab, GLA_DV)),
        'idx_k_norm': gain((n_ab, IDX_HD)),
        'w_out_ab': w((n_ab, GLA_HEADS * GLA_DV + DSA_HEADS * DSA_HD, D_MODEL)),
        'w_in_c': w((n_c, D_MODEL, c_width)),
        'swa_sinks': nrm((n_c, SWA_HEADS), 0.5),
        'w_out_c': w((n_c, SWA_HEADS * SWA_HD, D_MODEL)),
        'norm_mem_q': gain((DEPTH, D_MODEL)),
        'norm_mem_src': gain((DEPTH, D_MODEL)),
        'w_mem_q': w((DEPTH, D_MODEL, mem_w)),
        'w_mem_kv': w((DEPTH, D_MODEL, 2 * mem_w)),
        'w_mem_o': w((DEPTH, mem_w, D_MODEL)),
        'final_norm': gain((D_MODEL,)),
    }


def reference(x_prompt, x_sample, mem_prompt, cache_dsa_k, cache_dsa_v, cache_dsa_idx_k, state_gla,
              cache_swa_k, cache_swa_v, cache_mem_k, cache_mem_v, page_table,
              norm_ffn, w_ffn_gate, w_ffn_up, w_ffn_down, norm_mix, w_in_ab, w_gla_gate_up, b_gla_gate,
              gla_out_norm, idx_k_norm, w_out_ab, w_in_c, swa_sinks, w_out_c, norm_mem_q, norm_mem_src,
              w_mem_q, w_mem_kv, w_mem_o, final_norm):
    bp = x_prompt.shape[0]
    pos_p = jnp.arange(x_prompt.shape[1])
    pos_s = PAST_LEN + jnp.arange(x_sample.shape[1])

    def mix_ab_prompt(z, i):
        (gq, gk, gv, la, gr), (dq, dk, dv, iq, iw, ik) = ab_features(z, pos_p, w_gla_gate_up[i], b_gla_gate[i], idx_k_norm[i])
        s0 = jnp.zeros((bp, GLA_HEADS, GLA_DK, GLA_DV), jnp.float32)
        o, s_new = gla_chunked(gq, gk, gv, la, s0)
        y = jnp.concatenate([gla_output(o, gr, gla_out_norm[i]), dsa_prompt(dq, dk, dv, iq, iw, ik)], axis=-1)
        return y, (to_pages(dk), to_pages(dv), to_pages(ik), s_new)

    def mix_c_prompt(z, i):
        q, k, v = c_features(z, pos_p)
        return swa_prompt(q, k, v, swa_sinks[i]), (k[:, -WINDOW:], v[:, -WINDOW:])

    mem_kv_p = [mem_project(mem_prompt, norm_mem_src[l], w_mem_kv[l]) for l in range(DEPTH)]
    mem_k_p = [kv[0] for kv in mem_kv_p]
    mem_v_p = [kv[1] for kv in mem_kv_p]
    y_prompt, st_p = trunk(x_prompt, mem_k_p, mem_v_p, mix_ab_prompt, mix_c_prompt, norm_ffn, w_ffn_gate, w_ffn_up,
                           w_ffn_down, norm_mix, w_in_ab, w_out_ab, w_in_c, w_out_c, norm_mem_q, w_mem_q, w_mem_o,
                           final_norm)

    def mix_ab_sample(z, i):
        (gq, gk, gv, la, gr), (dq, dk, dv, iq, iw, ik) = ab_features(z, pos_s, w_gla_gate_up[i], b_gla_gate[i], idx_k_norm[i])
        o, s_new = gla_recurrent(gq, gk, gv, la, state_gla[i].astype(jnp.float32))
        y_b = dsa_step(dq, dk, dv, iq, iw, ik, cache_dsa_k[i], cache_dsa_v[i], cache_dsa_idx_k[i], page_table)
        y = jnp.concatenate([gla_output(o, gr, gla_out_norm[i]), y_b], axis=-1)
        return y, (dk, dv, ik, s_new)

    def mix_c_sample(z, i):
        q, k, v = c_features(z, pos_s)
        y, nk, nv = swa_step(q, k, v, cache_swa_k[i], cache_swa_v[i], swa_sinks[i])
        return y, (nk, nv)

    mem_k_s = [cache_mem_k[l] for l in range(DEPTH)]
    mem_v_s = [cache_mem_v[l] for l in range(DEPTH)]
    y_sample, st_s = trunk(x_sample, mem_k_s, mem_v_s, mix_ab_sample, mix_c_sample, norm_ffn, w_ffn_gate, w_ffn_up,
                           w_ffn_down, norm_mix, w_in_ab, w_out_ab, w_in_c, w_out_c, norm_mem_q, w_mem_q, w_mem_o,
                           final_norm)

    stk = lambda sts, j: jnp.stack([s[j] for s in sts])
    ab_p, c_p = st_p[0::2], st_p[1::2]
    ab_s, c_s = st_s[0::2], st_s[1::2]
    dsa_k_prompt, dsa_v_prompt, dsa_idx_k_prompt = stk(ab_p, 0), stk(ab_p, 1), stk(ab_p, 2)
    dsa_k_sample, dsa_v_sample, dsa_idx_k_sample = stk(ab_s, 0), stk(ab_s, 1), stk(ab_s, 2)
    gla_state_prompt, gla_state_sample = stk(ab_p, 3), stk(ab_s, 3)
    swa_k_prompt, swa_v_prompt = stk(c_p, 0), stk(c_p, 1)
    swa_k_sample, swa_v_sample = stk(c_s, 0), stk(c_s, 1)
    mem_k_prompt, mem_v_prompt = jnp.stack(mem_k_p), jnp.stack(mem_v_p)
    return (y_prompt, y_sample, dsa_k_prompt, dsa_v_prompt, dsa_idx_k_prompt, dsa_k_sample, dsa_v_sample,
            dsa_idx_k_sample, gla_state_prompt, gla_state_sample, swa_k_prompt, swa_v_prompt, swa_k_sample,
            swa_v_sample, mem_k_prompt, mem_v_prompt)
```

```python
import functools

import jax
import jax.numpy as jnp
import numpy as np
from jax import lax
from jax.experimental import pallas as pl
from jax.experimental.pallas import tpu as pltpu

F32 = jnp.float32
BF16 = jnp.bfloat16

D_MODEL = 2048
D_FF = 5632
EPS = 1e-6
ROPE_THETA = 10000.0
NEG_INF = -1e30
PAST_LEN = 8192
PAGE_SIZE = 128
Q_BLOCK = 128
GLA_HEADS, GLA_DK, GLA_DV = 4, 128, 256
GLA_GATE_RANK = 16
GLA_GATE_TAU = 16.0
GLA_CHUNK = 64
DSA_HEADS, DSA_KV_HEADS, DSA_HD = 8, 2, 128
IDX_HEADS, IDX_HD = 8, 64
DSA_TOPK_MAX = 256
SWA_HEADS, SWA_KV_HEADS, SWA_HD = 32, 4, 64
WINDOW = 128
MEM_HEADS, MEM_HD = 4, 128
N_MEM = 256

LANES = 128
SUBLANES = 8
VMEM_LIMIT_BYTES = 56 * 1024 * 1024

AB_GV, AB_GR, AB_DQ = 0, 1024, 2048
AB_GQ, AB_GK, AB_IQ = 3072, 3584, 4096
AB_DK, AB_DV = 4608, 4864
AB_SMALL = 5120
AB_WIDTH = 5376
SM_IK, SM_GD, SM_IW = 0, 64, 80
C_Q, C_K, C_V, C_WIDTH = 0, 2048, 2304, 2560


def _cparams(sem):
    return pltpu.CompilerParams(dimension_semantics=sem, vmem_limit_bytes=VMEM_LIMIT_BYTES)


def _rms(x, g):
    y = x * lax.rsqrt(jnp.mean(x * x, axis=-1, keepdims=True) + EPS)
    return y * g


def _ffn_kernel(x_ref, g_ref, wg_ref, wu_ref, wd_ref, fg_ref, o_ref, h_ref, acc_ref, *, final_norm):
    j = pl.program_id(1)

    @pl.when(j == 0)
    def _():
        h_ref[...] = _rms(x_ref[...], g_ref[...]).astype(BF16)
        acc_ref[...] = jnp.zeros_like(acc_ref)

    h = h_ref[...]
    a = jnp.dot(h, wg_ref[...], preferred_element_type=F32)
    u = jnp.dot(h, wu_ref[...], preferred_element_type=F32)
    act = (a * jax.nn.sigmoid(a) * u).astype(BF16)
    acc_ref[...] += jnp.dot(act, wd_ref[...], preferred_element_type=F32)

    @pl.when(j == pl.num_programs(1) - 1)
    def _():
        y = x_ref[...] + 0.5 * acc_ref[...]
        if final_norm:
            y = _rms(y, fg_ref[...])
        o_ref[...] = y


def ffn(x, g, wg, wu, wd, final_g=None, *, tm, tf):
    m = x.shape[0]
    tm = min(tm, m)
    fg = g if final_g is None else final_g
    return pl.pallas_call(
        functools.partial(_ffn_kernel, final_norm=final_g is not None),
        grid=(m // tm, D_FF // tf),
        in_specs=[
            pl.BlockSpec((tm, D_MODEL), lambda i, j: (i, 0)),
            pl.BlockSpec((1, D_MODEL), lambda i, j: (0, 0)),
            pl.BlockSpec((D_MODEL, tf), lambda i, j: (0, j)),
            pl.BlockSpec((D_MODEL, tf), lambda i, j: (0, j)),
            pl.BlockSpec((tf, D_MODEL), lambda i, j: (j, 0)),
            pl.BlockSpec((1, D_MODEL), lambda i, j: (0, 0)),
        ],
        out_specs=pl.BlockSpec((tm, D_MODEL), lambda i, j: (i, 0)),
        out_shape=jax.ShapeDtypeStruct((m, D_MODEL), F32),
        scratch_shapes=[pltpu.VMEM((tm, D_MODEL), BF16), pltpu.VMEM((tm, D_MODEL), F32)],
        compiler_params=_cparams(("parallel", "arbitrary")),
        name="ffn",
    )(x, g.reshape(1, D_MODEL), wg, wu, wd, fg.reshape(1, D_MODEL))


def _proj_kernel(x_ref, g_ref, w_ref, o_ref, h_ref):
    @pl.when(pl.program_id(1) == 0)
    def _():
        h_ref[...] = _rms(x_ref[...], g_ref[...]).astype(BF16)

    o_ref[...] = jnp.dot(h_ref[...], w_ref[...], preferred_element_type=F32)


def norm_proj(x, g, w, *, tm, tn):
    m, n = x.shape[0], w.shape[1]
    tm = min(tm, m)
    return pl.pallas_call(
        _proj_kernel,
        grid=(m // tm, n // tn),
        in_specs=[
            pl.BlockSpec((tm, D_MODEL), lambda i, j: (i, 0)),
            pl.BlockSpec((1, D_MODEL), lambda i, j: (0, 0)),
            pl.BlockSpec((D_MODEL, tn), lambda i, j: (0, j)),
        ],
        out_specs=pl.BlockSpec((tm, tn), lambda i, j: (i, j)),
        out_shape=jax.ShapeDtypeStruct((m, n), F32),
        scratch_shapes=[pltpu.VMEM((tm, D_MODEL), BF16)],
        compiler_params=_cparams(("parallel", "arbitrary")),
        name="norm_proj",
    )(x, g.reshape(1, D_MODEL), w)


def _outproj_kernel(*refs, n_in):
    x_ref = refs[0]
    y_refs = refs[1:1 + n_in]
    w_refs = refs[1 + n_in:1 + 2 * n_in]
    o_ref = refs[1 + 2 * n_in]
    acc = x_ref[...]
    for y_ref, w_ref in zip(y_refs, w_refs):
        acc = acc + jnp.dot(y_ref[...].astype(BF16), w_ref[...], preferred_element_type=F32)
    o_ref[...] = acc


def out_proj(x, ys, ws, *, tm):
    m = x.shape[0]
    tm = min(tm, m)
    n_in = len(ys)
    in_specs = [pl.BlockSpec((tm, D_MODEL), lambda i: (i, 0))]
    in_specs += [pl.BlockSpec((tm, y.shape[1]), lambda i: (i, 0)) for y in ys]
    in_specs += [pl.BlockSpec(w.shape, lambda i: (0, 0)) for w in ws]
    return pl.pallas_call(
        functools.partial(_outproj_kernel, n_in=n_in),
        grid=(m // tm,),
        in_specs=in_specs,
        out_specs=pl.BlockSpec((tm, D_MODEL), lambda i: (i, 0)),
        out_shape=jax.ShapeDtypeStruct((m, D_MODEL), F32),
        compiler_params=_cparams(("parallel",)),
        name="out_proj",
    )(x, *ys, *ws)


def _rope_tables(pos, hd):
    half = hd // 2
    inv = ROPE_THETA ** (-jnp.arange(half, dtype=F32) / half)
    ang = pos.astype(F32)[:, None] * inv[None, :]
    cos, sin = jnp.cos(ang), jnp.sin(ang)
    reps = LANES // hd
    return (jnp.concatenate([cos, cos] * reps, axis=-1),
            jnp.concatenate([-sin, sin] * reps, axis=-1))


def _rope128(x, cos, sin):
    return x * cos + pltpu.roll(x, 64, 1) * sin


def _rope64(x, cos, sin, lower):
    partner = jnp.where(lower, pltpu.roll(x, 96, 1), pltpu.roll(x, 32, 1))
    return x * cos + partner * sin


def _lower32_mask(rows):
    lane = lax.broadcasted_iota(jnp.int32, (rows, LANES), 1)
    return (lane % 64) < 32


def _ab_feat_kernel(dq_ref, dk_ref, dv_ref, iq_ref, sm_ref, c128_ref, s128_ref, c64_ref, s64_ref, gik_ref,
                    q_ref, k_ref, kb_ref, vb_ref, iqo_ref, ik_ref, ikb_ref, iw_ref):
    rows = dq_ref.shape[0]
    c128, s128 = c128_ref[...], s128_ref[...]
    c64, s64 = c64_ref[...], s64_ref[...]
    lower = _lower32_mask(rows)
    for h in range(DSA_HEADS):
        sl = slice(h * LANES, (h + 1) * LANES)
        q_ref[:, sl] = (_rope128(dq_ref[:, sl], c128, s128) * (DSA_HD ** -0.5)).astype(BF16)
    for h in range(DSA_KV_HEADS):
        sl = slice(h * LANES, (h + 1) * LANES)
        kr = _rope128(dk_ref[:, sl], c128, s128)
        k_ref[:, sl] = kr
        kb_ref[:, sl] = kr.astype(BF16)
    vb_ref[...] = dv_ref[...].astype(BF16)
    for p in range(IDX_HEADS * IDX_HD // LANES):
        sl = slice(p * LANES, (p + 1) * LANES)
        iqo_ref[:, sl] = (_rope64(iq_ref[:, sl], c64, s64, lower) * (IDX_HD ** -0.5)).astype(BF16)
    sm = sm_ref[...]
    lane = lax.broadcasted_iota(jnp.int32, (rows, LANES), 1)
    ik = jnp.where(lane < IDX_HD, sm, 0.0)
    ik = ik * lax.rsqrt(jnp.sum(ik * ik, axis=-1, keepdims=True) / IDX_HD + EPS) * gik_ref[...]
    ik = _rope64(ik, c64, s64, lower)
    ik_ref[...] = ik[:, :IDX_HD]
    ikb_ref[...] = jnp.where(lane < IDX_HD, ik, pltpu.roll(ik, 64, 1)).astype(BF16)
    iw_ref[...] = pltpu.roll(sm, LANES - SM_IW, 1) * (IDX_HEADS ** -0.5)


def ab_features(z, tabs, gik, *, seq, tr):
    m = z.shape[0]
    tr = min(tr, m)
    c128, s128, c64, s64 = tabs
    if c128.shape[0] == 1:
        tab_spec = pl.BlockSpec((1, LANES), lambda i: (0, 0))
    else:
        nt = seq // tr
        tab_spec = pl.BlockSpec((tr, LANES), lambda i: (i % nt, 0))
    col = lambda w, off: pl.BlockSpec((tr, w), lambda i: (i, off // w))
    row = lambda w: pl.BlockSpec((tr, w), lambda i: (i, 0))
    gik_pad = jnp.zeros((1, LANES), F32).at[0, :IDX_HD].set(gik)
    return pl.pallas_call(
        _ab_feat_kernel,
        grid=(m // tr,),
        in_specs=[col(1024, AB_DQ), col(256, AB_DK), col(256, AB_DV), col(512, AB_IQ), col(128, AB_SMALL),
                  tab_spec, tab_spec, tab_spec, tab_spec, pl.BlockSpec((1, LANES), lambda i: (0, 0))],
        out_specs=[row(1024), row(256), row(256), row(256), row(512), row(IDX_HD), row(128), row(128)],
        out_shape=[jax.ShapeDtypeStruct((m, 1024), BF16),
                   jax.ShapeDtypeStruct((m, 256), F32),
                   jax.ShapeDtypeStruct((m, 256), BF16),
                   jax.ShapeDtypeStruct((m, 256), BF16),
                   jax.ShapeDtypeStruct((m, 512), BF16),
                   jax.ShapeDtypeStruct((m, IDX_HD), F32),
                   jax.ShapeDtypeStruct((m, 128), BF16),
                   jax.ShapeDtypeStruct((m, 128), F32)],
        compiler_params=_cparams(("parallel",)),
        name="ab_features",
    )(z, z, z, z, z, c128, s128, c64, s64, gik_pad)


def _c_feat_kernel(q_ref, k_ref, v_ref, c64_ref, s64_ref, qo_ref, ko_ref, kb_ref, kbs_ref, vb_ref, vbs_ref):
    rows = q_ref.shape[0]
    c64, s64 = c64_ref[...], s64_ref[...]
    lower = _lower32_mask(rows)
    for p in range(SWA_HEADS * SWA_HD // LANES):
        sl = slice(p * LANES, (p + 1) * LANES)
        qo_ref[:, sl] = (_rope64(q_ref[:, sl], c64, s64, lower) * (SWA_HD ** -0.5)).astype(BF16)
    for p in range(SWA_KV_HEADS * SWA_HD // LANES):
        sl = slice(p * LANES, (p + 1) * LANES)
        kr = _rope64(k_ref[:, sl], c64, s64, lower)
        v = v_ref[:, sl]
        ko_ref[:, sl] = kr
        kb_ref[:, sl] = kr.astype(BF16)
        kbs_ref[:, sl] = pltpu.roll(kr, 64, 1).astype(BF16)
        vb_ref[:, sl] = v.astype(BF16)
        vbs_ref[:, sl] = pltpu.roll(v, 64, 1).astype(BF16)


def c_features(z, tabs, *, seq, tr):
    m = z.shape[0]
    tr = min(tr, m)
    c64, s64 = tabs
    if c64.shape[0] == 1:
        tab_spec = pl.BlockSpec((1, LANES), lambda i: (0, 0))
    else:
        nt = seq // tr
        tab_spec = pl.BlockSpec((tr, LANES), lambda i: (i % nt, 0))
    col = lambda w, off: pl.BlockSpec((tr, w), lambda i: (i, off // w))
    row = lambda w: pl.BlockSpec((tr, w), lambda i: (i, 0))
    return pl.pallas_call(
        _c_feat_kernel,
        grid=(m // tr,),
        in_specs=[col(2048, C_Q), col(256, C_K), col(256, C_V), tab_spec, tab_spec],
        out_specs=[row(2048), row(256), row(256), row(256), row(256), row(256)],
        out_shape=[jax.ShapeDtypeStruct((m, 2048), BF16),
                   jax.ShapeDtypeStruct((m, 256), F32),
                   jax.ShapeDtypeStruct((m, 256), BF16),
                   jax.ShapeDtypeStruct((m, 256), BF16),
                   jax.ShapeDtypeStruct((m, 256), BF16),
                   jax.ShapeDtypeStruct((m, 256), BF16)],
        compiler_params=_cparams(("parallel",)),
        name="c_features",
    )(z, z, z, c64, s64)


def _dot_nt(a, b, **kw):
    return lax.dot_general(a, b, (((1,), (1,)), ((), ())), preferred_element_type=F32, **kw)


def _dot_tn(a, b, **kw):
    return lax.dot_general(a, b, (((0,), (0,)), ((), ())), preferred_element_type=F32, **kw)


_HI = lax.Precision.HIGHEST


def _log_decay(sm, wgate, bgate):
    pre = jnp.dot(sm, wgate, preferred_element_type=F32, precision=_HI) + bgate
    return (jnp.minimum(pre, 0.0) - jnp.log1p(jnp.exp(-jnp.abs(pre)))) / GLA_GATE_TAU


def _gla_gate_out(o, r, gain):
    g = o * lax.rsqrt(jnp.mean(o * o, axis=-1, keepdims=True) + EPS) * gain
    return g * (r * jax.nn.sigmoid(r))


def _gla_prompt_kernel(gv_ref, gr_ref, gq_ref, gk_ref, sm_ref, wgate_ref, bgate_ref, gain_ref,
                       y_ref, st_ref, s_ref, *, n_chunks):
    c = pl.program_id(1)

    @pl.when(c == 0)
    def _():
        s_ref[...] = jnp.zeros_like(s_ref)

    ri = lax.broadcasted_iota(jnp.int32, (GLA_CHUNK, GLA_CHUNK), 0)
    ci = lax.broadcasted_iota(jnp.int32, (GLA_CHUNK, GLA_CHUNK), 1)
    causal = ri >= ci
    tril = jnp.where(causal, 1.0, 0.0).astype(F32)
    gain = gain_ref[...]
    for n in range(n_chunks):
        rows = slice(n * GLA_CHUNK, (n + 1) * GLA_CHUNK)
        la = _log_decay(sm_ref[rows, :], wgate_ref[...], bgate_ref[...])
        b = jnp.dot(tril, la, preferred_element_type=F32, precision=_HI)
        b_end = b[GLA_CHUNK - 1:GLA_CHUNK, :]
        k = gk_ref[rows, :]
        q_in = gq_ref[rows, :] * (GLA_DK ** -0.5) * jnp.exp(b)
        k_in = k * jnp.exp(-b)
        k_end = k * jnp.exp(b_end - b)
        decay = jnp.exp(b_end)
        for h in range(GLA_HEADS):
            dk = slice(h * GLA_DK, (h + 1) * GLA_DK)
            dv = slice(h * GLA_DV, (h + 1) * GLA_DV)
            v = gv_ref[rows, dv]
            st = s_ref[h]
            att = jnp.where(causal, _dot_nt(q_in[:, dk], k_in[:, dk], precision=_HI), 0.0)
            o = _dot_nt(q_in[:, dk], st, precision=_HI) + jnp.dot(att, v, preferred_element_type=F32, precision=_HI)
            s_ref[h] = st * decay[:, dk] + _dot_tn(v, k_end[:, dk], precision=_HI)
            y_ref[rows, dv] = _gla_gate_out(o, gr_ref[rows, dv], gain)

    @pl.when(c == pl.num_programs(1) - 1)
    def _():
        st_ref[0] = s_ref[...]


def gla_prompt(z, wgate, bgate, gain, *, batch, seq, tc):
    m = z.shape[0]
    nb = seq // tc
    col = lambda w, off: pl.BlockSpec((tc, w), lambda b, c: (b * nb + c, off // w))
    const = lambda shape: pl.BlockSpec(shape, lambda b, c: (0,) * len(shape))
    return pl.pallas_call(
        functools.partial(_gla_prompt_kernel, n_chunks=tc // GLA_CHUNK),
        grid=(batch, nb),
        in_specs=[col(1024, AB_GV), col(1024, AB_GR), col(512, AB_GQ), col(512, AB_GK), col(128, AB_SMALL),
                  const((LANES, GLA_HEADS * GLA_DK)), const((1, GLA_HEADS * GLA_DK)), const((1, GLA_DV))],
        out_specs=[pl.BlockSpec((tc, GLA_HEADS * GLA_DV), lambda b, c: (b * nb + c, 0)),
                   pl.BlockSpec((1, GLA_HEADS, GLA_DV, GLA_DK), lambda b, c: (b, 0, 0, 0))],
        out_shape=[jax.ShapeDtypeStruct((m, GLA_HEADS * GLA_DV), F32),
                   jax.ShapeDtypeStruct((batch, GLA_HEADS, GLA_DV, GLA_DK), F32)],
        scratch_shapes=[pltpu.VMEM((GLA_HEADS, GLA_DV, GLA_DK), F32)],
        compiler_params=_cparams(("parallel", "arbitrary")),
        name="gla_prompt",
    )(z, z, z, z, z, wgate, bgate, gain)


INT_MIN = -2 ** 31


def _order_key(score):
    score = jnp.where(score == 0.0, 0.0, score)
    bits = lax.bitcast_convert_type(score, jnp.int32)
    return jnp.where(bits < 0, bits ^ jnp.int32(0x7FFFFFFF), bits)


def _lane_total(x):
    return jnp.dot(x.astype(BF16), jnp.ones((LANES, LANES), BF16), preferred_element_type=F32)


def _kth_largest_key(key_ref, n_blocks, k):
    rows = key_ref.shape[0]

    def body(it, t):
        cand = t + lax.shift_left(jnp.int32(1), 31 - it)
        acc = jnp.zeros((rows, LANES), F32)
        for c in range(n_blocks):
            acc = acc + jnp.where(key_ref[:, c * LANES:(c + 1) * LANES] >= cand, 1.0, 0.0)
        return jnp.where(_lane_total(acc) >= k, cand, t)

    return lax.fori_loop(0, 32, body, jnp.full((rows, LANES), INT_MIN, jnp.int32))


def _select_bias(key_ref, bias_ref, n_blocks, k, valid_fn):
    rows = key_ref.shape[0]
    t = _kth_largest_key(key_ref, n_blocks, k)
    acc = jnp.zeros((rows, LANES), F32)
    for c in range(n_blocks):
        acc = acc + jnp.where(key_ref[:, c * LANES:(c + 1) * LANES] > t, 1.0, 0.0)
    need = k - _lane_total(acc)
    ri = lax.broadcasted_iota(jnp.int32, (LANES, LANES), 0)
    ci = lax.broadcasted_iota(jnp.int32, (LANES, LANES), 1)
    before = jnp.where(ri < ci, 1.0, 0.0).astype(BF16)
    run = jnp.zeros((rows, LANES), F32)
    for c in range(n_blocks):
        sl = slice(c * LANES, (c + 1) * LANES)
        kc = key_ref[:, sl]
        eq = jnp.where(kc == t, 1.0, 0.0)
        eqb = eq.astype(BF16)
        rank = jnp.dot(eqb, before, preferred_element_type=F32) + run
        take = jnp.where(kc > t, 1.0, jnp.where(rank < need, eq, 0.0))
        ok = take > 0.0
        if valid_fn is not None:
            ok = jnp.logical_and(ok, valid_fn(c))
        bias_ref[:, sl] = jnp.where(ok, 0.0, NEG_INF)
        run = run + _lane_total(eq)


def _half_mask(rows, upper):
    lane = lax.broadcasted_iota(jnp.int32, (rows, LANES), 1)
    return (lane >= 64) if upper else (lane < 64)


def _dsa_prompt_kernel(q_ref, iq_ref, iw_ref, kb_ref, vb_ref, ikb_ref, y_ref, key_ref, bias_ref, *, seq, topk):
    i = pl.program_id(1)
    n_blocks = seq // LANES
    qpos = i * Q_BLOCK + lax.broadcasted_iota(jnp.int32, (Q_BLOCK, LANES), 0)
    lane = lax.broadcasted_iota(jnp.int32, (Q_BLOCK, LANES), 1)

    ikb = ikb_ref[...]
    iw = iw_ref[...]
    score = jnp.zeros((Q_BLOCK, seq), F32)
    for h in range(IDX_HEADS):
        pair = iq_ref[:, (h // 2) * LANES:(h // 2 + 1) * LANES]
        qh = jnp.where(_half_mask(Q_BLOCK, h % 2 == 1), pair, jnp.zeros_like(pair))
        score = score + jnp.maximum(_dot_nt(qh, ikb), 0.0) * iw[:, h:h + 1]
    for c in range(n_blocks):
        sl = slice(c * LANES, (c + 1) * LANES)
        causal = (c * LANES + lane) <= qpos
        key_ref[:, sl] = _order_key(jnp.where(causal, score[:, sl], NEG_INF))

    _select_bias(key_ref, bias_ref, n_blocks, topk, lambda c: (c * LANES + lane) <= qpos)

    bias = bias_ref[...]
    group = DSA_HEADS // DSA_KV_HEADS
    for h in range(DSA_HEADS):
        kv = slice((h // group) * DSA_HD, (h // group + 1) * DSA_HD)
        s = _dot_nt(q_ref[:, h * DSA_HD:(h + 1) * DSA_HD], kb_ref[:, kv]) + bias
        p = jnp.exp(s - jnp.max(s, axis=-1, keepdims=True))
        o = jnp.dot(p.astype(BF16), vb_ref[:, kv], preferred_element_type=F32)
        y_ref[:, h * DSA_HD:(h + 1) * DSA_HD] = o / jnp.sum(p, axis=-1, keepdims=True)


def dsa_prompt(q, iq, iw, kb, vb, ikb, *, batch, seq):
    m = q.shape[0]
    nq = seq // Q_BLOCK
    topk = min(DSA_TOPK_MAX, seq // 4)
    qrow = lambda w: pl.BlockSpec((Q_BLOCK, w), lambda b, i: (b * nq + i, 0))
    full = lambda w: pl.BlockSpec((seq, w), lambda b, i: (b, 0))
    return pl.pallas_call(
        functools.partial(_dsa_prompt_kernel, seq=seq, topk=topk),
        grid=(batch, nq),
        in_specs=[qrow(1024), qrow(512), qrow(128), full(256), full(256), full(128)],
        out_specs=qrow(1024),
        out_shape=jax.ShapeDtypeStruct((m, DSA_HEADS * DSA_HD), F32),
        scratch_shapes=[pltpu.VMEM((Q_BLOCK, seq), jnp.int32), pltpu.VMEM((Q_BLOCK, seq), F32)],
        compiler_params=_cparams(("parallel", "arbitrary")),
        name="dsa_prompt",
    )(q, iq, iw, kb, vb, ikb)


def _swa_head_plan(h):
    group = SWA_HEADS // SWA_KV_HEADS
    kv = h // group
    return h // 2, h % 2, kv // 2, (kv % 2) != (h % 2)


def _swa_prompt_kernel(sink_ref, q_ref, kp_ref, kc_ref, kps_ref, kcs_ref, vp_ref, vc_ref, vps_ref, vcs_ref, y_ref):
    i = pl.program_id(1)
    r = lax.broadcasted_iota(jnp.int32, (Q_BLOCK, 2 * Q_BLOCK), 0)
    c = lax.broadcasted_iota(jnp.int32, (Q_BLOCK, 2 * Q_BLOCK), 1)
    rel = Q_BLOCK + r - c
    ok = (rel >= 0) & (rel <= WINDOW) & ((i - 1) * Q_BLOCK + c >= 0)
    bias = jnp.where(ok, 0.0, NEG_INF)
    keys = (jnp.concatenate([kp_ref[...], kc_ref[...]], axis=0), jnp.concatenate([kps_ref[...], kcs_ref[...]], axis=0))
    vals = (jnp.concatenate([vp_ref[...], vc_ref[...]], axis=0), jnp.concatenate([vps_ref[...], vcs_ref[...]], axis=0))
    lower = _half_mask(Q_BLOCK, False)
    for p in range(SWA_HEADS // 2):
        qpair = q_ref[:, p * LANES:(p + 1) * LANES]
        outs = []
        for h in (2 * p, 2 * p + 1):
            _, half, ks, swapped = _swa_head_plan(h)
            qh = jnp.where(_half_mask(Q_BLOCK, half == 1), qpair, jnp.zeros_like(qpair))
            kk = keys[int(swapped)][:, ks * LANES:(ks + 1) * LANES]
            vv = vals[int(swapped)][:, ks * LANES:(ks + 1) * LANES]
            s = _dot_nt(qh, kk) + bias
            sink = sink_ref[h]
            mx = jnp.maximum(jnp.max(s, axis=-1, keepdims=True), sink)
            pr = jnp.exp(s - mx)
            den = jnp.sum(pr, axis=-1, keepdims=True) + jnp.exp(sink - mx)
            outs.append(jnp.dot(pr.astype(BF16), vv, preferred_element_type=F32) / den)
        y_ref[:, p * LANES:(p + 1) * LANES] = jnp.where(lower, outs[0], outs[1])


def swa_prompt(q, kb, kbs, vb, vbs, sinks, *, batch, seq):
    m = q.shape[0]
    nq = seq // Q_BLOCK
    cur = pl.BlockSpec((Q_BLOCK, 256), lambda b, i, s: (b * nq + i, 0))
    prev = pl.BlockSpec((Q_BLOCK, 256), lambda b, i, s: (b * nq + jnp.maximum(i - 1, 0), 0))
    qspec = pl.BlockSpec((Q_BLOCK, 2048), lambda b, i, s: (b * nq + i, 0))
    return pl.pallas_call(
        _swa_prompt_kernel,
        grid_spec=pltpu.PrefetchScalarGridSpec(
            num_scalar_prefetch=1,
            grid=(batch, nq),
            in_specs=[qspec, prev, cur, prev, cur, prev, cur, prev, cur],
            out_specs=qspec,
        ),
        out_shape=jax.ShapeDtypeStruct((m, SWA_HEADS * SWA_HD), F32),
        compiler_params=_cparams(("parallel", "arbitrary")),
        name="swa_prompt",
    )(sinks, q, kb, kb, kbs, kbs, vb, vb, vbs, vbs)


def _mem_prompt_kernel(x_ref, g_ref, wq_ref, kv_ref, wo_ref, o_ref):
    x = x_ref[...]
    h = _rms(x, g_ref[...]).astype(BF16)
    q = (jnp.dot(h, wq_ref[...], preferred_element_type=F32) * (MEM_HD ** -0.5)).astype(BF16)
    width = MEM_HEADS * MEM_HD
    outs = []
    for hd in range(MEM_HEADS):
        sl = slice(hd * MEM_HD, (hd + 1) * MEM_HD)
        k = kv_ref[:, sl].astype(BF16)
        v = kv_ref[:, width + hd * MEM_HD:width + (hd + 1) * MEM_HD].astype(BF16)
        s = _dot_nt(q[:, sl], k)
        p = jnp.exp(s - jnp.max(s, axis=-1, keepdims=True))
        o = jnp.dot(p.astype(BF16), v, preferred_element_type=F32) / jnp.sum(p, axis=-1, keepdims=True)
        outs.append(o.astype(BF16))
    o_ref[...] = x + jnp.dot(jnp.concatenate(outs, axis=-1), wo_ref[...], preferred_element_type=F32)


def mem_attn_prompt(x, g, wq, kv, wo, *, batch, seq, tq):
    m = x.shape[0]
    nq = seq // tq
    width = MEM_HEADS * MEM_HD
    xspec = pl.BlockSpec((tq, D_MODEL), lambda b, i: (b * nq + i, 0))
    return pl.pallas_call(
        _mem_prompt_kernel,
        grid=(batch, nq),
        in_specs=[xspec,
                  pl.BlockSpec((1, D_MODEL), lambda b, i: (0, 0)),
                  pl.BlockSpec((D_MODEL, width), lambda b, i: (0, 0)),
                  pl.BlockSpec((N_MEM, 2 * width), lambda b, i: (b, 0)),
                  pl.BlockSpec((width, D_MODEL), lambda b, i: (0, 0))],
        out_specs=xspec,
        out_shape=jax.ShapeDtypeStruct((m, D_MODEL), F32),
        compiler_params=_cparams(("parallel", "arbitrary")),
        name="mem_attn_prompt",
    )(x, g.reshape(1, D_MODEL), wq, kv, wo)


def _gla_prep_kernel(gq_ref, gk_ref, sm_ref, wgate_ref, bgate_ref, o_ref):
    w = GLA_HEADS * GLA_DK
    la = _log_decay(sm_ref[...], wgate_ref[...], bgate_ref[...])
    o_ref[:, 0:w] = jnp.exp(la)
    o_ref[:, w:2 * w] = gk_ref[...]
    o_ref[:, 2 * w:3 * w] = gq_ref[...] * (GLA_DK ** -0.5)
    o_ref[:, 3 * w:4 * w] = jnp.zeros((gq_ref.shape[0], w), F32)


def gla_sample_prep(z, wgate, bgate):
    m = z.shape[0]
    w = GLA_HEADS * GLA_DK
    col = lambda wd, off: pl.BlockSpec((m, wd), lambda i: (0, off // wd))
    return pl.pallas_call(
        _gla_prep_kernel,
        grid=(1,),
        in_specs=[col(512, AB_GQ), col(512, AB_GK), col(128, AB_SMALL),
                  pl.BlockSpec((LANES, w), lambda i: (0, 0)), pl.BlockSpec((1, w), lambda i: (0, 0))],
        out_specs=pl.BlockSpec((m, 4 * w), lambda i: (0, 0)),
        out_shape=jax.ShapeDtypeStruct((m, 4 * w), F32),
        compiler_params=_cparams(("arbitrary",)),
        name="gla_sample_prep",
    )(z, z, z, wgate, bgate)


def _gla_step_kernel(p_ref, gv_ref, gr_ref, s_ref, gain_ref, so_ref, y_ref, *, bs):
    gain = gain_ref[...]
    for s in range(bs):
        xt = p_ref[s].T
        for h in range(GLA_HEADS):
            dv = slice(h * GLA_DV, (h + 1) * GLA_DV)
            st = s_ref[s, h] * xt[:, h:h + 1] + xt[:, GLA_HEADS + h:GLA_HEADS + h + 1] * gv_ref[s:s + 1, dv]
            so_ref[s, h] = st
            o = jnp.sum(xt[:, 2 * GLA_HEADS + h:2 * GLA_HEADS + h + 1] * st, axis=0, keepdims=True)
            y_ref[s:s + 1, dv] = _gla_gate_out(o, gr_ref[s:s + 1, dv], gain)


def gla_sample(p, z, state, gain, *, bs):
    m = z.shape[0]
    col = lambda w, off: pl.BlockSpec((bs, w), lambda i: (i, off // w))
    sspec = pl.BlockSpec((bs, GLA_HEADS, GLA_DK, GLA_DV), lambda i: (i, 0, 0, 0))
    return pl.pallas_call(
        functools.partial(_gla_step_kernel, bs=bs),
        grid=(m // bs,),
        in_specs=[pl.BlockSpec((bs, 16, GLA_DK), lambda i: (i, 0, 0)), col(1024, AB_GV), col(1024, AB_GR), sspec,
                  pl.BlockSpec((1, GLA_DV), lambda i: (0, 0))],
        out_specs=[sspec, pl.BlockSpec((bs, GLA_HEADS * GLA_DV), lambda i: (i, 0))],
        out_shape=[jax.ShapeDtypeStruct(state.shape, F32), jax.ShapeDtypeStruct((m, GLA_HEADS * GLA_DV), F32)],
        compiler_params=_cparams(("parallel",)),
        name="gla_sample",
    )(p, z, z, state, gain)


def _dsa_scores_kernel(pt_ref, iq_ref, iw_ref, *refs, pg):
    page_refs, o_ref = refs[:pg], refs[pg]
    q8 = iq_ref[0]
    iw = iw_ref[0]
    for j in range(pg):
        dots = _dot_nt(q8, page_refs[j][0].astype(BF16))
        o_ref[0, j:j + 1, :] = jnp.sum(jnp.maximum(dots, 0.0) * iw, axis=0, keepdims=True)


def dsa_sample_scores(page_table, iq, iw, pool_ik, *, pg):
    m, n_pages = page_table.shape
    page_spec = lambda j: pl.BlockSpec((1, PAGE_SIZE, IDX_HD), lambda b, g, pt: (pt[b, g * pg + j], 0, 0))
    return pl.pallas_call(
        functools.partial(_dsa_scores_kernel, pg=pg),
        grid_spec=pltpu.PrefetchScalarGridSpec(
            num_scalar_prefetch=1,
            grid=(m, n_pages // pg),
            in_specs=[pl.BlockSpec((1, IDX_HEADS, IDX_HD), lambda b, g, pt: (b, 0, 0)),
                      pl.BlockSpec((1, IDX_HEADS, 1), lambda b, g, pt: (b, 0, 0))]
                     + [page_spec(j) for j in range(pg)],
            out_specs=pl.BlockSpec((1, pg, PAGE_SIZE), lambda b, g, pt: (b, g, 0)),
        ),
        out_shape=jax.ShapeDtypeStruct((m, n_pages, PAGE_SIZE), F32),
        compiler_params=_cparams(("parallel", "arbitrary")),
        name="dsa_sample_scores",
    )(page_table, iq, iw, *([pool_ik] * pg))


def dsa_sample_select(scores, iq, iw, ikb, *, topk):
    m, n_past = scores.shape
    full = lambda a: pl.BlockSpec(a.shape, lambda i: (0,) * a.ndim)
    return pl.pallas_call(
        functools.partial(_dsa_select_kernel_split, n_past=n_past, topk=topk),
        grid=(1,),
        in_specs=[full(scores), full(iq), full(iw), full(ikb)],
        out_specs=[pl.BlockSpec((m, n_past), lambda i: (0, 0)), pl.BlockSpec((m, LANES), lambda i: (0, 0))],
        out_shape=[jax.ShapeDtypeStruct((m, n_past), F32), jax.ShapeDtypeStruct((m, LANES), F32)],
        scratch_shapes=[pltpu.VMEM((m, n_past + LANES), jnp.int32), pltpu.VMEM((m, n_past + LANES), F32)],
        compiler_params=_cparams(("arbitrary",)),
        name="dsa_sample_select",
    )(scores, iq, iw, ikb)


def _dsa_select_kernel_split(sc_ref, iq_ref, iw_ref, ikb_ref, bias_ref, bnew_ref, key_ref, tmp_ref, *, n_past, topk):
    rows = sc_ref.shape[0]
    n_blocks = n_past // LANES
    for c in range(n_blocks):
        sl = slice(c * LANES, (c + 1) * LANES)
        key_ref[:, sl] = _order_key(sc_ref[:, sl])
    ik = ikb_ref[...].astype(F32)
    iw = iw_ref[...]
    s_new = jnp.zeros((rows, 1), F32)
    for h in range(IDX_HEADS):
        pair = iq_ref[:, (h // 2) * LANES:(h // 2 + 1) * LANES].astype(F32)
        qh = jnp.where(_half_mask(rows, h % 2 == 1), pair, 0.0)
        s_new = s_new + jnp.maximum(jnp.sum(qh * ik, axis=-1, keepdims=True), 0.0) * iw[:, h:h + 1]
    lane = lax.broadcasted_iota(jnp.int32, (rows, LANES), 1)
    key_ref[:, n_past:n_past + LANES] = _order_key(jnp.where(lane == 0, s_new, -jnp.inf))
    _select_bias(key_ref, tmp_ref, n_blocks + 1, topk, None)
    bias_ref[...] = tmp_ref[:, :n_past]
    bnew_ref[...] = tmp_ref[:, n_past:n_past + LANES]


def _dsa_step_kernel(pt_ref, q_ref, bias_ref, bnew_ref, kn_ref, vn_ref, *refs, pg):
    k_refs, v_refs = refs[:pg], refs[pg:2 * pg]
    y_ref, m_ref, l_ref, acc_ref = refs[2 * pg:]
    g = pl.program_id(1)

    @pl.when(g == 0)
    def _():
        m_ref[...] = jnp.full_like(m_ref, NEG_INF)
        l_ref[...] = jnp.zeros_like(l_ref)
        acc_ref[...] = jnp.zeros_like(acc_ref)

    q8 = q_ref[0]
    first = lax.broadcasted_iota(jnp.int32, (DSA_HEADS, DSA_HD), 0) < (DSA_HEADS // DSA_KV_HEADS)
    for j in range(pg):
        kp = k_refs[j][0].astype(BF16)
        vp = v_refs[j][0].astype(BF16)
        b = bias_ref[0, j:j + 1, :]
        s = jnp.where(first, _dot_nt(q8, kp[:, :DSA_HD]), _dot_nt(q8, kp[:, DSA_HD:])) + b
        m_old = m_ref[...]
        m_new = jnp.maximum(m_old, jnp.max(s, axis=-1, keepdims=True))
        p = jnp.where(b == 0.0, jnp.exp(s - m_new), 0.0)
        alpha = jnp.exp(m_old - m_new)
        pb = p.astype(BF16)
        pv = jnp.where(first, jnp.dot(pb, vp[:, :DSA_HD], preferred_element_type=F32),
                       jnp.dot(pb, vp[:, DSA_HD:], preferred_element_type=F32))
        l_ref[...] = alpha * l_ref[...] + jnp.sum(p, axis=-1, keepdims=True)
        acc_ref[...] = alpha * acc_ref[...] + pv
        m_ref[...] = m_new

    @pl.when(g == pl.num_programs(1) - 1)
    def _():
        kn = kn_ref[0].astype(BF16).astype(F32)
        vn = vn_ref[0].astype(BF16).astype(F32)
        bn = bnew_ref[0][:, 0:1]
        s_new = jnp.sum(q8.astype(F32) * jnp.where(first, kn[:, :DSA_HD], kn[:, DSA_HD:]), axis=-1, keepdims=True) + bn
        m_old = m_ref[...]
        m_new = jnp.maximum(m_old, s_new)
        p_new = jnp.where(bn == 0.0, jnp.exp(s_new - m_new), 0.0)
        alpha = jnp.exp(m_old - m_new)
        acc = alpha * acc_ref[...] + p_new * jnp.where(first, vn[:, :DSA_HD], vn[:, DSA_HD:])
        y_ref[0] = acc / (alpha * l_ref[...] + p_new)


def dsa_sample_attend(page_table, q, bias, bnew, kn, vn, pool_k, pool_v, *, pg):
    m, n_pages = page_table.shape
    page_spec = lambda j: pl.BlockSpec((1, PAGE_SIZE, DSA_KV_HEADS * DSA_HD), lambda b, g, pt: (pt[b, g * pg + j], 0, 0))
    per_b = lambda shape: pl.BlockSpec((1,) + shape, lambda b, g, pt: (b, 0, 0))
    return pl.pallas_call(
        functools.partial(_dsa_step_kernel, pg=pg),
        grid_spec=pltpu.PrefetchScalarGridSpec(
            num_scalar_prefetch=1,
            grid=(m, n_pages // pg),
            in_specs=[per_b((DSA_HEADS, DSA_HD)),
                      pl.BlockSpec((1, pg, PAGE_SIZE), lambda b, g, pt: (b, g, 0)),
                      per_b((1, LANES)), per_b((1, DSA_KV_HEADS * DSA_HD)), per_b((1, DSA_KV_HEADS * DSA_HD))]
                     + [page_spec(j) for j in range(pg)] * 2,
            out_specs=per_b((DSA_HEADS, DSA_HD)),
            scratch_shapes=[pltpu.VMEM((DSA_HEADS, DSA_HD), F32)] * 3,
        ),
        out_shape=jax.ShapeDtypeStruct((m, DSA_HEADS, DSA_HD), F32),
        compiler_params=_cparams(("parallel", "arbitrary")),
        name="dsa_sample_attend",
    )(page_table, q, bias, bnew, kn, vn, *([pool_k] * pg), *([pool_v] * pg))


def _swa_step_kernel(q_ref, kc_ref, vc_ref, kn_ref, vn_ref, sink_ref, y_ref, *, bs):
    group = SWA_HEADS // SWA_KV_HEADS
    for s in range(bs):
        for kv in range(SWA_KV_HEADS):
            sl = slice(kv * SWA_HD, (kv + 1) * SWA_HD)
            qv = q_ref[s, kv * group:(kv + 1) * group, :]
            kn = kn_ref[s:s + 1, sl].astype(BF16).astype(F32)
            vn = vn_ref[s:s + 1, sl].astype(BF16).astype(F32)
            sc = _dot_nt(qv, kc_ref[s, :, sl].astype(BF16))
            s_new = jnp.sum(qv.astype(F32) * kn, axis=-1, keepdims=True)
            sink = sink_ref[kv]
            mx = jnp.maximum(jnp.maximum(jnp.max(sc, axis=-1, keepdims=True), s_new), sink)
            p = jnp.exp(sc - mx)
            p_new = jnp.exp(s_new - mx)
            den = jnp.sum(p, axis=-1, keepdims=True) + p_new + jnp.exp(sink - mx)
            o = jnp.dot(p.astype(BF16), vc_ref[s, :, sl].astype(BF16), preferred_element_type=F32) + p_new * vn
            y_ref[s, kv * group:(kv + 1) * group, :] = o / den


def swa_sample(q, kc, vc, kn, vn, sinks, *, bs):
    m = q.shape[0]
    group = SWA_HEADS // SWA_KV_HEADS
    width = SWA_KV_HEADS * SWA_HD
    return pl.pallas_call(
        functools.partial(_swa_step_kernel, bs=bs),
        grid=(m // bs,),
        in_specs=[pl.BlockSpec((bs, SWA_HEADS, SWA_HD), lambda i: (i, 0, 0)),
                  pl.BlockSpec((bs, WINDOW, width), lambda i: (i, 0, 0)),
                  pl.BlockSpec((bs, WINDOW, width), lambda i: (i, 0, 0)),
                  pl.BlockSpec((bs, width), lambda i: (i, 0)),
                  pl.BlockSpec((bs, width), lambda i: (i, 0)),
                  pl.BlockSpec((SWA_KV_HEADS, group, 1), lambda i: (0, 0, 0))],
        out_specs=pl.BlockSpec((bs, SWA_HEADS, SWA_HD), lambda i: (i, 0, 0)),
        out_shape=jax.ShapeDtypeStruct((m, SWA_HEADS, SWA_HD), F32),
        compiler_params=_cparams(("parallel",)),
        name="swa_sample",
    )(q, kc, vc, kn, vn, sinks.reshape(SWA_KV_HEADS, group, 1))


def _mem_step_kernel(q_ref, k_ref, v_ref, y_ref, *, bs):
    rows = q_ref.shape[1]
    row = lax.broadcasted_iota(jnp.int32, (rows, 1), 0)
    for s in range(bs):
        q = (q_ref[s] * (MEM_HD ** -0.5)).astype(BF16)
        sc = jnp.zeros((rows, N_MEM), F32)
        for h in range(MEM_HEADS):
            sl = slice(h * MEM_HD, (h + 1) * MEM_HD)
            sc = sc + jnp.where(row == h, _dot_nt(q, k_ref[s, :, sl].astype(BF16)), 0.0)
        p = jnp.exp(sc - jnp.max(sc, axis=-1, keepdims=True))
        den = jnp.sum(p, axis=-1, keepdims=True)
        pb = p.astype(BF16)
        o = jnp.zeros((rows, MEM_HD), F32)
        for h in range(MEM_HEADS):
            sl = slice(h * MEM_HD, (h + 1) * MEM_HD)
            o = o + jnp.where(row == h, jnp.dot(pb, v_ref[s, :, sl].astype(BF16), preferred_element_type=F32), 0.0)
        y_ref[s] = o / den


def mem_attn_sample(q, mk, mv, *, bs):
    m, rows, _ = q.shape
    width = MEM_HEADS * MEM_HD
    return pl.pallas_call(
        functools.partial(_mem_step_kernel, bs=bs),
        grid=(m // bs,),
        in_specs=[pl.BlockSpec((bs, rows, MEM_HD), lambda i: (i, 0, 0)),
                  pl.BlockSpec((bs, N_MEM, width), lambda i: (i, 0, 0)),
                  pl.BlockSpec((bs, N_MEM, width), lambda i: (i, 0, 0))],
        out_specs=pl.BlockSpec((bs, rows, MEM_HD), lambda i: (i, 0, 0)),
        out_shape=jax.ShapeDtypeStruct((m, rows, MEM_HD), F32),
        compiler_params=_cparams(("parallel",)),
        name="mem_attn_sample",
    )(q, mk, mv)


TM_FFN, TF_FFN = 512, 512
TM_PROJ, TN_AB, TN_C, TN_MEM = 512, 768, 512, 512
TM_OUT = 256
TR_FEAT = 512
TC_GLA = 256
TQ_MEM = 512
BS_SAMPLE = 8
PG_SAMPLE = 8


def _prep_w_in_ab(w):
    sizes = (GLA_HEADS * GLA_DK, GLA_HEADS * GLA_DK, GLA_HEADS * GLA_DV, GLA_HEADS * GLA_DV, GLA_GATE_RANK,
             DSA_HEADS * DSA_HD, DSA_KV_HEADS * DSA_HD, DSA_KV_HEADS * DSA_HD, IDX_HEADS * IDX_HD, IDX_HEADS, IDX_HD)
    offs = np.cumsum((0,) + sizes)
    gq, gk, gv, gr, gd, dq, dk, dv, iq, iw, ik = [w[:, int(offs[j]):int(offs[j + 1])] for j in range(len(sizes))]
    pad = lambda n: jnp.zeros((w.shape[0], n), w.dtype)
    small = jnp.concatenate([ik, gd, iw, pad(LANES - IDX_HD - GLA_GATE_RANK - IDX_HEADS)], axis=1)
    out = jnp.concatenate([gv, gr, dq, gq, gk, iq, dk, dv, small, pad(AB_WIDTH - AB_SMALL - LANES)], axis=1)
    return out.astype(BF16)


def _prep_gate(w_up):
    return jnp.zeros((LANES, GLA_HEADS * GLA_DK), F32).at[SM_GD:SM_GD + GLA_GATE_RANK].set(w_up)


def kernel(x_prompt, x_sample, mem_prompt, cache_dsa_k, cache_dsa_v, cache_dsa_idx_k, state_gla, cache_swa_k, cache_swa_v, cache_mem_k, cache_mem_v, page_table, norm_ffn, w_ffn_gate, w_ffn_up, w_ffn_down, norm_mix, w_in_ab, w_gla_gate_up, b_gla_gate, gla_out_norm, idx_k_norm, w_out_ab, w_in_c, swa_sinks, w_out_c, norm_mem_q, norm_mem_src, w_mem_q, w_mem_kv, w_mem_o, final_norm):
    depth = norm_mix.shape[0]
    bp, seq, _ = x_prompt.shape
    bs = x_sample.shape[0]
    n_pool = cache_dsa_k.shape[1]
    gla_w = GLA_HEADS * GLA_DV

    wg, wu, wd = w_ffn_gate.astype(BF16), w_ffn_up.astype(BF16), w_ffn_down.astype(BF16)
    w_ab = [_prep_w_in_ab(w_in_ab[i]) for i in range(w_in_ab.shape[0])]
    w_gate = [_prep_gate(w_gla_gate_up[i]) for i in range(w_in_ab.shape[0])]
    w_oab = w_out_ab.astype(BF16)
    w_c, w_oc = w_in_c.astype(BF16), w_out_c.astype(BF16)
    w_mq, w_mkv, w_mo = w_mem_q.astype(BF16), w_mem_kv.astype(BF16), w_mem_o.astype(BF16)

    def rope_tabs(pos):
        return _rope_tables(pos, DSA_HD) + _rope_tables(pos, IDX_HD)

    def ffn_pair(x, layer, half, last):
        return ffn(x, norm_ffn[layer, half], wg[layer, half], wu[layer, half], wd[layer, half],
                   final_norm if last else None, tm=TM_FFN, tf=TF_FFN)

    tabs_p = rope_tabs(jnp.arange(seq))
    x = x_prompt.reshape(bp * seq, D_MODEL)
    mem = mem_prompt.reshape(bp * N_MEM, D_MODEL)
    mem_kv = [norm_proj(mem, norm_mem_src[l], w_mkv[l], tm=TM_PROJ, tn=TN_MEM) for l in range(depth)]
    st_ab_p, st_c_p = [], []
    for l in range(depth):
        i = l // 2
        x = ffn_pair(x, l, 0, False)
        if l % 2 == 0:
            z = norm_proj(x, norm_mix[l], w_ab[i], tm=TM_PROJ, tn=TN_AB)
            q, k, kb, vb, iq, ik, ikb, iw = ab_features(z, tabs_p, idx_k_norm[i], seq=seq, tr=TR_FEAT)
            y_gla, st_t = gla_prompt(z, w_gate[i], b_gla_gate[i].reshape(1, -1), gla_out_norm[i].reshape(1, -1),
                                     batch=bp, seq=seq, tc=TC_GLA)
            y_dsa = dsa_prompt(q, iq, iw, kb, vb, ikb, batch=bp, seq=seq)
            x = out_proj(x, [y_gla, y_dsa], [w_oab[i, :gla_w], w_oab[i, gla_w:]], tm=TM_OUT)
            n_pg = seq // PAGE_SIZE
            st_ab_p.append((k.reshape(bp, n_pg, PAGE_SIZE, DSA_KV_HEADS, DSA_HD),
                            z[:, AB_DV:AB_DV + 256].reshape(bp, n_pg, PAGE_SIZE, DSA_KV_HEADS, DSA_HD),
                            ik.reshape(bp, n_pg, PAGE_SIZE, IDX_HD),
                            jnp.swapaxes(st_t, 2, 3)))
        else:
            z = norm_proj(x, norm_mix[l], w_c[i], tm=TM_PROJ, tn=TN_C)
            q, k, kb, kbs, vb, vbs = c_features(z, tabs_p[2:], seq=seq, tr=TR_FEAT)
            y = swa_prompt(q, kb, kbs, vb, vbs, swa_sinks[i], batch=bp, seq=seq)
            x = out_proj(x, [y], [w_oc[i]], tm=TM_OUT)
            st_c_p.append((k.reshape(bp, seq, SWA_KV_HEADS, SWA_HD)[:, -WINDOW:],
                           z[:, C_V:C_V + 256].reshape(bp, seq, SWA_KV_HEADS, SWA_HD)[:, -WINDOW:]))
        x = mem_attn_prompt(x, norm_mem_q[l], w_mq[l], mem_kv[l], w_mo[l], batch=bp, seq=seq, tq=TQ_MEM)
        x = ffn_pair(x, l, 1, l == depth - 1)
    y_prompt = x.reshape(bp, seq, D_MODEL)

    tabs_s = rope_tabs(PAST_LEN + jnp.arange(1))
    n_pages = page_table.shape[1]
    topk = min(DSA_TOPK_MAX, (PAST_LEN + 1) // 4)
    x = x_sample.reshape(bs, D_MODEL)
    st_ab_s, st_c_s = [], []
    for l in range(depth):
        i = l // 2
        x = ffn_pair(x, l, 0, False)
        if l % 2 == 0:
            z = norm_proj(x, norm_mix[l], w_ab[i], tm=TM_PROJ, tn=TN_AB)
            q, k, kb, vb, iq, ik, ikb, iw = ab_features(z, tabs_s, idx_k_norm[i], seq=1, tr=TR_FEAT)
            p = gla_sample_prep(z, w_gate[i], b_gla_gate[i].reshape(1, -1)).reshape(bs, 16, GLA_DK)
            st_new, y_gla = gla_sample(p, z, state_gla[i], gla_out_norm[i].reshape(1, -1), bs=BS_SAMPLE)
            scores = dsa_sample_scores(page_table, iq.reshape(bs, IDX_HEADS, IDX_HD),
                                       iw[:, :IDX_HEADS].reshape(bs, IDX_HEADS, 1), cache_dsa_idx_k[i], pg=PG_SAMPLE)
            bias, bnew = dsa_sample_select(scores.reshape(bs, n_pages * PAGE_SIZE), iq, iw, ikb, topk=topk)
            v_new = z[:, AB_DV:AB_DV + 256]
            y_dsa = dsa_sample_attend(
                page_table, q.reshape(bs, DSA_HEADS, DSA_HD), bias.reshape(bs, n_pages, PAGE_SIZE),
                bnew.reshape(bs, 1, LANES), k.reshape(bs, 1, 256), v_new.reshape(bs, 1, 256),
                cache_dsa_k[i].reshape(n_pool, PAGE_SIZE, 256), cache_dsa_v[i].reshape(n_pool, PAGE_SIZE, 256),
                pg=PG_SAMPLE).reshape(bs, DSA_HEADS * DSA_HD)
            x = out_proj(x, [y_gla, y_dsa], [w_oab[i, :gla_w], w_oab[i, gla_w:]], tm=TM_OUT)
            st_ab_s.append((k.reshape(bs, 1, DSA_KV_HEADS, DSA_HD), v_new.reshape(bs, 1, DSA_KV_HEADS, DSA_HD),
                            ik.reshape(bs, 1, IDX_HD), st_new))
        else:
            z = norm_proj(x, norm_mix[l], w_c[i], tm=TM_PROJ, tn=TN_C)
            q, k, kb, kbs, vb, vbs = c_features(z, tabs_s[2:], seq=1, tr=TR_FEAT)
            v_new = z[:, C_V:C_V + 256]
            y = swa_sample(q.reshape(bs, SWA_HEADS, SWA_HD), cache_swa_k[i].reshape(bs, WINDOW, 256),
                           cache_swa_v[i].reshape(bs, WINDOW, 256), k, v_new, swa_sinks[i], bs=BS_SAMPLE)
            x = out_proj(x, [y.reshape(bs, SWA_HEADS * SWA_HD)], [w_oc[i]], tm=TM_OUT)
            st_c_s.append((jnp.concatenate([cache_swa_k[i][:, 1:], k.reshape(bs, 1, SWA_KV_HEADS, SWA_HD)], axis=1),
                           jnp.concatenate([cache_swa_v[i][:, 1:], v_new.reshape(bs, 1, SWA_KV_HEADS, SWA_HD)], axis=1)))
        qm = norm_proj(x, norm_mem_q[l], w_mq[l], tm=TM_PROJ, tn=TN_MEM).reshape(bs, MEM_HEADS, MEM_HD)
        qm = jnp.pad(qm, ((0, 0), (0, SUBLANES - MEM_HEADS), (0, 0)))
        om = mem_attn_sample(qm, cache_mem_k[l].reshape(bs, N_MEM, MEM_HEADS * MEM_HD),
                             cache_mem_v[l].reshape(bs, N_MEM, MEM_HEADS * MEM_HD), bs=BS_SAMPLE)
        x = out_proj(x, [om[:, :MEM_HEADS].reshape(bs, MEM_HEADS * MEM_HD)], [w_mo[l]], tm=TM_OUT)
        x = ffn_pair(x, l, 1, l == depth - 1)
    y_sample = x.reshape(bs, 1, D_MODEL)

    stk = lambda sts, j: jnp.stack([s[j] for s in sts])
    mw = MEM_HEADS * MEM_HD
    mem_k_p = jnp.stack([kv[:, :mw].reshape(bp, N_MEM, MEM_HEADS, MEM_HD) for kv in mem_kv])
    mem_v_p = jnp.stack([kv[:, mw:].reshape(bp, N_MEM, MEM_HEADS, MEM_HD) for kv in mem_kv])
    return (y_prompt, y_sample, stk(st_ab_p, 0), stk(st_ab_p, 1), stk(st_ab_p, 2),
            stk(st_ab_s, 0), stk(st_ab_s, 1), stk(st_ab_s, 2), stk(st_ab_p, 3), stk(st_ab_s, 3),
            stk(st_c_p, 0), stk(st_c_p, 1), stk(st_c_s, 0), stk(st_c_s, 1), mem_k_p, mem_v_p)
```

```python
import functools

import jax
import jax.numpy as jnp
import numpy as np
from jax import lax
from jax.experimental import pallas as pl
from jax.experimental.pallas import tpu as pltpu

F32 = jnp.float32
BF16 = jnp.bfloat16

D_MODEL = 2048
D_FF = 5632
EPS = 1e-6
ROPE_THETA = 10000.0
NEG_INF = -1e30
PAST_LEN = 8192
PAGE_SIZE = 128
Q_BLOCK = 128
GLA_HEADS, GLA_DK, GLA_DV = 4, 128, 256
GLA_GATE_RANK = 16
GLA_GATE_TAU = 16.0
GLA_CHUNK = 64
DSA_HEADS, DSA_KV_HEADS, DSA_HD = 8, 2, 128
IDX_HEADS, IDX_HD = 8, 64
DSA_TOPK_MAX = 256
SWA_HEADS, SWA_KV_HEADS, SWA_HD = 32, 4, 64
WINDOW = 128
MEM_HEADS, MEM_HD = 4, 128
N_MEM = 256

LANES = 128
SUBLANES = 8
VMEM_LIMIT_BYTES = 56 * 1024 * 1024

AB_GV, AB_GR, AB_DQ = 0, 1024, 2048
AB_GQ, AB_GK, AB_IQ = 3072, 3584, 4096
AB_DK, AB_DV = 4608, 4864
AB_SMALL = 5120
AB_WIDTH = 5376
SM_IK, SM_GD, SM_IW = 0, 64, 80
C_Q, C_K, C_V, C_WIDTH = 0, 2048, 2304, 2560


def _cparams(sem):
    return pltpu.CompilerParams(dimension_semantics=sem, vmem_limit_bytes=VMEM_LIMIT_BYTES)


def _rms(x, g):
    y = x * lax.rsqrt(jnp.mean(x * x, axis=-1, keepdims=True) + EPS)
    return y * g


def _ffn_kernel(x_ref, g_ref, wg_ref, wu_ref, wd_ref, fg_ref, o_ref, h_ref, *, final_norm, sub):
    j = pl.program_id(1)
    tm = x_ref.shape[0]

    @pl.when(j == 0)
    def _():
        for r in range(tm // sub):
            rows = slice(r * sub, (r + 1) * sub)
            h_ref[rows, :] = _rms(x_ref[rows, :], g_ref[...]).astype(BF16)
        o_ref[...] = jnp.zeros_like(o_ref)

    for r in range(tm // sub):
        rows = slice(r * sub, (r + 1) * sub)
        h = h_ref[rows, :]
        a = jnp.dot(h, wg_ref[...], preferred_element_type=F32)
        u = jnp.dot(h, wu_ref[...], preferred_element_type=F32)
        act = (a * jax.nn.sigmoid(a) * u).astype(BF16)
        o_ref[rows, :] += jnp.dot(act, wd_ref[...], preferred_element_type=F32)

    @pl.when(j == pl.num_programs(1) - 1)
    def _():
        for r in range(tm // sub):
            rows = slice(r * sub, (r + 1) * sub)
            y = x_ref[rows, :] + 0.5 * o_ref[rows, :]
            if final_norm:
                y = _rms(y, fg_ref[...])
            o_ref[rows, :] = y


def ffn(x, g, wg, wu, wd, sel, final_g=None, *, tm, tf, sub):
    m = x.shape[0]
    tm = min(tm, m)
    sub = min(sub, tm)
    layer, half = sel
    fg = g if final_g is None else final_g
    return pl.pallas_call(
        functools.partial(_ffn_kernel, final_norm=final_g is not None, sub=sub),
        grid=(m // tm, D_FF // tf),
        in_specs=[
            pl.BlockSpec((tm, D_MODEL), lambda i, j: (i, 0), pipeline_mode=pl.Buffered(1)),
            pl.BlockSpec((1, D_MODEL), lambda i, j: (0, 0)),
            pl.BlockSpec((None, None, D_MODEL, tf), lambda i, j: (layer, half, 0, j)),
            pl.BlockSpec((None, None, D_MODEL, tf), lambda i, j: (layer, half, 0, j)),
            pl.BlockSpec((None, None, tf, D_MODEL), lambda i, j: (layer, half, j, 0)),
            pl.BlockSpec((1, D_MODEL), lambda i, j: (0, 0)),
        ],
        out_specs=pl.BlockSpec((tm, D_MODEL), lambda i, j: (i, 0)),
        out_shape=jax.ShapeDtypeStruct((m, D_MODEL), F32),
        scratch_shapes=[pltpu.VMEM((tm, D_MODEL), BF16)],
        compiler_params=_cparams(("parallel", "arbitrary")),
        name="ffn",
    )(x, g.reshape(1, D_MODEL), wg, wu, wd, fg.reshape(1, D_MODEL))


def _proj_kernel(x_ref, g_ref, w_ref, o_ref, h_ref, *, sub):
    tm = x_ref.shape[0]

    @pl.when(pl.program_id(1) == 0)
    def _():
        for r in range(tm // sub):
            rows = slice(r * sub, (r + 1) * sub)
            h_ref[rows, :] = _rms(x_ref[rows, :], g_ref[...]).astype(BF16)

    o_ref[...] = jnp.dot(h_ref[...], w_ref[...], preferred_element_type=F32)


def norm_proj(x, g, w, *, tm, tn, sub=512):
    m, n = x.shape[0], w.shape[1]
    tm = min(tm, m)
    return pl.pallas_call(
        functools.partial(_proj_kernel, sub=min(sub, tm)),
        grid=(m // tm, n // tn),
        in_specs=[
            pl.BlockSpec((tm, D_MODEL), lambda i, j: (i, 0), pipeline_mode=pl.Buffered(1)),
            pl.BlockSpec((1, D_MODEL), lambda i, j: (0, 0)),
            pl.BlockSpec((D_MODEL, tn), lambda i, j: (0, j)),
        ],
        out_specs=pl.BlockSpec((tm, tn), lambda i, j: (i, j)),
        out_shape=jax.ShapeDtypeStruct((m, n), F32),
        scratch_shapes=[pltpu.VMEM((tm, D_MODEL), BF16)],
        compiler_params=_cparams(("parallel", "arbitrary")),
        name="norm_proj",
    )(x, g.reshape(1, D_MODEL), w)


def _outproj_kernel(*refs, n_in):
    x_ref = refs[0]
    y_refs = refs[1:1 + n_in]
    w_refs = refs[1 + n_in:1 + 2 * n_in]
    o_ref = refs[1 + 2 * n_in]
    acc = x_ref[...]
    for y_ref, w_ref in zip(y_refs, w_refs):
        acc = acc + jnp.dot(y_ref[...].astype(BF16), w_ref[...], preferred_element_type=F32)
    o_ref[...] = acc


def out_proj(x, ys, ws, *, tm):
    m = x.shape[0]
    tm = min(tm, m)
    n_in = len(ys)
    in_specs = [pl.BlockSpec((tm, D_MODEL), lambda i: (i, 0))]
    in_specs += [pl.BlockSpec((tm, y.shape[1]), lambda i: (i, 0)) for y in ys]
    in_specs += [pl.BlockSpec(w.shape, lambda i: (0, 0)) for w in ws]
    return pl.pallas_call(
        functools.partial(_outproj_kernel, n_in=n_in),
        grid=(m // tm,),
        in_specs=in_specs,
        out_specs=pl.BlockSpec((tm, D_MODEL), lambda i: (i, 0)),
        out_shape=jax.ShapeDtypeStruct((m, D_MODEL), F32),
        compiler_params=_cparams(("parallel",)),
        name="out_proj",
    )(x, *ys, *ws)


def _rope_tables(pos, hd):
    half = hd // 2
    inv = ROPE_THETA ** (-jnp.arange(half, dtype=F32) / half)
    ang = pos.astype(F32)[:, None] * inv[None, :]
    cos, sin = jnp.cos(ang), jnp.sin(ang)
    reps = LANES // hd
    return (jnp.concatenate([cos, cos] * reps, axis=-1),
            jnp.concatenate([-sin, sin] * reps, axis=-1))


def _rope128(x, cos, sin):
    return x * cos + pltpu.roll(x, 64, 1) * sin


def _rope64(x, cos, sin, lower):
    partner = jnp.where(lower, pltpu.roll(x, 96, 1), pltpu.roll(x, 32, 1))
    return x * cos + partner * sin


def _lower32_mask(rows):
    lane = lax.broadcasted_iota(jnp.int32, (rows, LANES), 1)
    return (lane % 64) < 32


def _ab_feat_kernel(dq_ref, dk_ref, dv_ref, iq_ref, sm_ref, c128_ref, s128_ref, c64_ref, s64_ref, gik_ref,
                    q_ref, k_ref, kb_ref, vb_ref, iqo_ref, ik_ref, ikb_ref, iw_ref):
    rows = dq_ref.shape[0]
    c128, s128 = c128_ref[...], s128_ref[...]
    c64, s64 = c64_ref[...], s64_ref[...]
    lower = _lower32_mask(rows)
    for h in range(DSA_HEADS):
        sl = slice(h * LANES, (h + 1) * LANES)
        q_ref[:, sl] = (_rope128(dq_ref[:, sl], c128, s128) * (DSA_HD ** -0.5)).astype(BF16)
    for h in range(DSA_KV_HEADS):
        sl = slice(h * LANES, (h + 1) * LANES)
        kr = _rope128(dk_ref[:, sl], c128, s128)
        k_ref[:, sl] = kr
        kb_ref[:, sl] = kr.astype(BF16)
    vb_ref[...] = dv_ref[...].astype(BF16)
    for p in range(IDX_HEADS * IDX_HD // LANES):
        sl = slice(p * LANES, (p + 1) * LANES)
        iqo_ref[:, sl] = (_rope64(iq_ref[:, sl], c64, s64, lower) * (IDX_HD ** -0.5)).astype(BF16)
    sm = sm_ref[...]
    lane = lax.broadcasted_iota(jnp.int32, (rows, LANES), 1)
    ik = jnp.where(lane < IDX_HD, sm, 0.0)
    ik = ik * lax.rsqrt(jnp.sum(ik * ik, axis=-1, keepdims=True) / IDX_HD + EPS) * gik_ref[...]
    ik = _rope64(ik, c64, s64, lower)
    ik_ref[...] = ik[:, :IDX_HD]
    ikb_ref[...] = jnp.where(lane < IDX_HD, ik, pltpu.roll(ik, 64, 1)).astype(BF16)
    iw_ref[...] = pltpu.roll(sm, LANES - SM_IW, 1) * (IDX_HEADS ** -0.5)


def ab_features(z, tabs, gik, *, seq, tr):
    m = z.shape[0]
    tr = min(tr, m)
    c128, s128, c64, s64 = tabs
    if c128.shape[0] == 1:
        tab_spec = pl.BlockSpec((1, LANES), lambda i: (0, 0))
    else:
        nt = seq // tr
        tab_spec = pl.BlockSpec((tr, LANES), lambda i: (i % nt, 0))
    col = lambda w, off: pl.BlockSpec((tr, w), lambda i: (i, off // w))
    row = lambda w: pl.BlockSpec((tr, w), lambda i: (i, 0))
    gik_pad = jnp.zeros((1, LANES), F32).at[0, :IDX_HD].set(gik)
    return pl.pallas_call(
        _ab_feat_kernel,
        grid=(m // tr,),
        in_specs=[col(1024, AB_DQ), col(256, AB_DK), col(256, AB_DV), col(512, AB_IQ), col(128, AB_SMALL),
                  tab_spec, tab_spec, tab_spec, tab_spec, pl.BlockSpec((1, LANES), lambda i: (0, 0))],
        out_specs=[row(1024), row(256), row(256), row(256), row(512), row(IDX_HD), row(128), row(128)],
        out_shape=[jax.ShapeDtypeStruct((m, 1024), BF16),
                   jax.ShapeDtypeStruct((m, 256), F32),
                   jax.ShapeDtypeStruct((m, 256), BF16),
                   jax.ShapeDtypeStruct((m, 256), BF16),
                   jax.ShapeDtypeStruct((m, 512), BF16),
                   jax.ShapeDtypeStruct((m, IDX_HD), F32),
                   jax.ShapeDtypeStruct((m, 128), BF16),
                   jax.ShapeDtypeStruct((m, 128), F32)],
        compiler_params=_cparams(("parallel",)),
        name="ab_features",
    )(z, z, z, z, z, c128, s128, c64, s64, gik_pad)


def _c_feat_kernel(q_ref, k_ref, v_ref, c64_ref, s64_ref, qo_ref, ko_ref, kb_ref, kbs_ref, vb_ref, vbs_ref):
    rows = q_ref.shape[0]
    c64, s64 = c64_ref[...], s64_ref[...]
    lower = _lower32_mask(rows)
    for p in range(SWA_HEADS * SWA_HD // LANES):
        sl = slice(p * LANES, (p + 1) * LANES)
        qo_ref[:, sl] = (_rope64(q_ref[:, sl], c64, s64, lower) * (SWA_HD ** -0.5)).astype(BF16)
    for p in range(SWA_KV_HEADS * SWA_HD // LANES):
        sl = slice(p * LANES, (p + 1) * LANES)
        kr = _rope64(k_ref[:, sl], c64, s64, lower)
        v = v_ref[:, sl]
        ko_ref[:, sl] = kr
        kb_ref[:, sl] = kr.astype(BF16)
        kbs_ref[:, sl] = pltpu.roll(kr, 64, 1).astype(BF16)
        vb_ref[:, sl] = v.astype(BF16)
        vbs_ref[:, sl] = pltpu.roll(v, 64, 1).astype(BF16)


def c_features(z, tabs, *, seq, tr):
    m = z.shape[0]
    tr = min(tr, m)
    c64, s64 = tabs
    if c64.shape[0] == 1:
        tab_spec = pl.BlockSpec((1, LANES), lambda i: (0, 0))
    else:
        nt = seq // tr
        tab_spec = pl.BlockSpec((tr, LANES), lambda i: (i % nt, 0))
    col = lambda w, off: pl.BlockSpec((tr, w), lambda i: (i, off // w))
    row = lambda w: pl.BlockSpec((tr, w), lambda i: (i, 0))
    return pl.pallas_call(
        _c_feat_kernel,
        grid=(m // tr,),
        in_specs=[col(2048, C_Q), col(256, C_K), col(256, C_V), tab_spec, tab_spec],
        out_specs=[row(2048), row(256), row(256), row(256), row(256), row(256)],
        out_shape=[jax.ShapeDtypeStruct((m, 2048), BF16),
                   jax.ShapeDtypeStruct((m, 256), F32),
                   jax.ShapeDtypeStruct((m, 256), BF16),
                   jax.ShapeDtypeStruct((m, 256), BF16),
                   jax.ShapeDtypeStruct((m, 256), BF16),
                   jax.ShapeDtypeStruct((m, 256), BF16)],
        compiler_params=_cparams(("parallel",)),
        name="c_features",
    )(z, z, z, c64, s64)


def _dot_nt(a, b, **kw):
    return lax.dot_general(a, b, (((1,), (1,)), ((), ())), preferred_element_type=F32, **kw)


def _dot_tn(a, b, **kw):
    return lax.dot_general(a, b, (((0,), (0,)), ((), ())), preferred_element_type=F32, **kw)


_HI = lax.Precision.HIGHEST


def _log_decay(sm, wgate, bgate):
    pre = jnp.dot(sm, wgate, preferred_element_type=F32, precision=_HI) + bgate
    return (jnp.minimum(pre, 0.0) - jnp.log1p(jnp.exp(-jnp.abs(pre)))) / GLA_GATE_TAU


def _gla_gate_out(o, r, gain):
    g = o * lax.rsqrt(jnp.mean(o * o, axis=-1, keepdims=True) + EPS) * gain
    return g * (r * jax.nn.sigmoid(r))


def _gla_prompt_kernel(gv_ref, gr_ref, gq_ref, gk_ref, sm_ref, wgate_ref, bgate_ref, gain_ref,
                       y_ref, st_ref, s_ref, *, n_chunks):
    c = pl.program_id(1)

    @pl.when(c == 0)
    def _():
        s_ref[...] = jnp.zeros_like(s_ref)

    ri = lax.broadcasted_iota(jnp.int32, (GLA_CHUNK, GLA_CHUNK), 0)
    ci = lax.broadcasted_iota(jnp.int32, (GLA_CHUNK, GLA_CHUNK), 1)
    causal = ri >= ci
    tril = jnp.where(causal, 1.0, 0.0).astype(F32)
    gain = gain_ref[...]
    for n in range(n_chunks):
        rows = slice(n * GLA_CHUNK, (n + 1) * GLA_CHUNK)
        la = _log_decay(sm_ref[rows, :], wgate_ref[...], bgate_ref[...])
        b = jnp.dot(tril, la, preferred_element_type=F32, precision=_HI)
        b_end = b[GLA_CHUNK - 1:GLA_CHUNK, :]
        k = gk_ref[rows, :]
        q_in = gq_ref[rows, :] * (GLA_DK ** -0.5) * jnp.exp(b)
        k_in = k * jnp.exp(-b)
        k_end = k * jnp.exp(b_end - b)
        decay = jnp.exp(b_end)
        for h in range(GLA_HEADS):
            dk = slice(h * GLA_DK, (h + 1) * GLA_DK)
            dv = slice(h * GLA_DV, (h + 1) * GLA_DV)
            v = gv_ref[rows, dv].astype(BF16)
            st = s_ref[h]
            qh = q_in[:, dk].astype(BF16)
            att = jnp.where(causal, _dot_nt(qh, k_in[:, dk].astype(BF16)), 0.0)
            o = _dot_nt(qh, st.astype(BF16)) + jnp.dot(att.astype(BF16), v, preferred_element_type=F32)
            s_ref[h] = st * decay[:, dk] + _dot_tn(v, k_end[:, dk].astype(BF16))
            y_ref[rows, dv] = _gla_gate_out(o, gr_ref[rows, dv], gain)

    @pl.when(c == pl.num_programs(1) - 1)
    def _():
        st_ref[0] = s_ref[...]


def gla_prompt(z, wgate, bgate, gain, *, batch, seq, tc):
    m = z.shape[0]
    nb = seq // tc
    col = lambda w, off: pl.BlockSpec((tc, w), lambda b, c: (b * nb + c, off // w))
    const = lambda shape: pl.BlockSpec(shape, lambda b, c: (0,) * len(shape))
    return pl.pallas_call(
        functools.partial(_gla_prompt_kernel, n_chunks=tc // GLA_CHUNK),
        grid=(batch, nb),
        in_specs=[col(1024, AB_GV), col(1024, AB_GR), col(512, AB_GQ), col(512, AB_GK), col(128, AB_SMALL),
                  const((LANES, GLA_HEADS * GLA_DK)), const((1, GLA_HEADS * GLA_DK)), const((1, GLA_DV))],
        out_specs=[pl.BlockSpec((tc, GLA_HEADS * GLA_DV), lambda b, c: (b * nb + c, 0)),
                   pl.BlockSpec((1, GLA_HEADS, GLA_DV, GLA_DK), lambda b, c: (b, 0, 0, 0))],
        out_shape=[jax.ShapeDtypeStruct((m, GLA_HEADS * GLA_DV), F32),
                   jax.ShapeDtypeStruct((batch, GLA_HEADS, GLA_DV, GLA_DK), F32)],
        scratch_shapes=[pltpu.VMEM((GLA_HEADS, GLA_DV, GLA_DK), F32)],
        compiler_params=_cparams(("parallel", "arbitrary")),
        name="gla_prompt",
    )(z, z, z, z, z, wgate, bgate, gain)


INT_MIN = -2 ** 31


def _order_key(score):
    score = jnp.where(score == 0.0, 0.0, score)
    bits = lax.bitcast_convert_type(score, jnp.int32)
    return jnp.where(bits < 0, bits ^ jnp.int32(0x7FFFFFFF), bits)


def _lane_total(x):
    return jnp.dot(x.astype(BF16), jnp.ones((LANES, LANES), BF16), preferred_element_type=F32)


ROW_SUB = 128


def _count_blocks(key_ref, rows, n_blocks, pred):
    acc = jnp.zeros((ROW_SUB, LANES), F32)
    for c in range(n_blocks):
        acc = acc + jnp.where(pred(key_ref[rows, c * LANES:(c + 1) * LANES]), 1.0, 0.0)
    return acc


def _kth_largest_key(key_ref, t_ref, n_blocks, k):
    n_sub = key_ref.shape[0] // ROW_SUB
    t_ref[...] = jnp.full(t_ref.shape, INT_MIN, jnp.int32)

    def body(it, carry):
        step = lax.shift_left(jnp.int32(1), 31 - it)
        cands, accs = [], []
        for rb in range(n_sub):
            rows = slice(rb * ROW_SUB, (rb + 1) * ROW_SUB)
            cand = t_ref[rows, :] + step
            cands.append(cand)
            accs.append(_count_blocks(key_ref, rows, n_blocks, lambda kc, cand=cand: kc >= cand))
        total = _lane_total(jnp.concatenate(accs, axis=0))
        for rb in range(n_sub):
            rows = slice(rb * ROW_SUB, (rb + 1) * ROW_SUB)
            t_ref[rows, :] = jnp.where(total[rows, :] >= k, cands[rb], t_ref[rows, :])
        return carry

    lax.fori_loop(0, 32, body, 0)


def _select_topk(key_ref, t_ref, n_blocks, k, write_fn):
    _kth_largest_key(key_ref, t_ref, n_blocks, k)
    ri = lax.broadcasted_iota(jnp.int32, (LANES, LANES), 0)
    ci = lax.broadcasted_iota(jnp.int32, (LANES, LANES), 1)
    before = jnp.where(ri < ci, 1.0, 0.0).astype(BF16)
    for rb in range(key_ref.shape[0] // ROW_SUB):
        rows = slice(rb * ROW_SUB, (rb + 1) * ROW_SUB)
        t = t_ref[rows, :]
        need = k - _lane_total(_count_blocks(key_ref, rows, n_blocks, lambda kc: kc > t))
        run = jnp.zeros((ROW_SUB, LANES), F32)
        for c in range(n_blocks):
            kc = key_ref[rows, c * LANES:(c + 1) * LANES]
            eq = jnp.where(kc == t, 1.0, 0.0)
            rank = jnp.dot(eq.astype(BF16), before, preferred_element_type=F32) + run
            take = jnp.where(kc > t, 1.0, jnp.where(rank < need, eq, 0.0))
            write_fn(rows, c, take > 0.0)
            run = run + _lane_total(eq)


def _half_mask(rows, upper):
    lane = lax.broadcasted_iota(jnp.int32, (rows, LANES), 1)
    return (lane >= 64) if upper else (lane < 64)


DSA_STRATUM = 512
KEY_CHUNK = 512


def _dsa_select_prompt_kernel(iq_ref, iw_ref, ikb_ref, bias_ref, key_ref, t_ref, *, row0, n_keys, topk):
    rows = iq_ref.shape[0]
    n_blocks = n_keys // LANES
    lane = lax.broadcasted_iota(jnp.int32, (ROW_SUB, LANES), 1)
    sub = lax.broadcasted_iota(jnp.int32, (ROW_SUB, LANES), 0)

    iw = iw_ref[...]
    qh = []
    for h in range(IDX_HEADS):
        pair = iq_ref[:, (h // 2) * LANES:(h // 2 + 1) * LANES]
        qh.append(jnp.where(_half_mask(rows, h % 2 == 1), pair, jnp.zeros_like(pair)))
    for kc in range(n_keys // KEY_CHUNK):
        ik = ikb_ref[kc * KEY_CHUNK:(kc + 1) * KEY_CHUNK, :]
        score = jnp.zeros((rows, KEY_CHUNK), F32)
        for h in range(IDX_HEADS):
            score = score + jnp.maximum(_dot_nt(qh[h], ik), 0.0) * iw[:, h:h + 1]
        for rb in range(rows // ROW_SUB):
            for cb in range(KEY_CHUNK // LANES):
                c = kc * (KEY_CHUNK // LANES) + cb
                causal = (c * LANES + lane) <= (row0 + rb * ROW_SUB + sub)
                part = score[rb * ROW_SUB:(rb + 1) * ROW_SUB, cb * LANES:(cb + 1) * LANES]
                key_ref[rb * ROW_SUB:(rb + 1) * ROW_SUB, c * LANES:(c + 1) * LANES] = _order_key(
                    jnp.where(causal, part, NEG_INF))

    def write(rws, c, sel):
        causal = (c * LANES + lane) <= (row0 + rws.start + sub)
        bias_ref[rws, c * LANES:(c + 1) * LANES] = jnp.where(jnp.logical_and(sel, causal), 0.0, NEG_INF)

    _select_topk(key_ref, t_ref, n_blocks, topk, write)


def _dsa_attend_prompt_kernel(q_ref, bias_ref, kb_ref, vb_ref, y_ref):
    bias = bias_ref[...]
    group = DSA_HEADS // DSA_KV_HEADS
    for kv in range(DSA_KV_HEADS):
        kvs = slice(kv * DSA_HD, (kv + 1) * DSA_HD)
        q4 = jnp.concatenate([q_ref[:, (kv * group + g) * DSA_HD:(kv * group + g + 1) * DSA_HD] for g in range(group)],
                             axis=0)
        s4 = _dot_nt(q4, kb_ref[:, kvs])
        ps, dens = [], []
        for g in range(group):
            s = s4[g * Q_BLOCK:(g + 1) * Q_BLOCK, :] + bias
            p = jnp.exp(s - jnp.max(s, axis=-1, keepdims=True))
            dens.append(jnp.sum(p, axis=-1, keepdims=True))
            ps.append(p.astype(BF16))
        o4 = jnp.dot(jnp.concatenate(ps, axis=0), vb_ref[:, kvs], preferred_element_type=F32)
        for g in range(group):
            h = kv * group + g
            y_ref[:, h * DSA_HD:(h + 1) * DSA_HD] = o4[g * Q_BLOCK:(g + 1) * Q_BLOCK, :] / dens[g]


def dsa_prompt(q, iq, iw, kb, vb, ikb, *, batch, seq):
    topk = min(DSA_TOPK_MAX, seq // 4)
    as3 = lambda a: a.reshape(batch, seq, a.shape[-1])
    q, iq, iw, kb, vb, ikb = (as3(a) for a in (q, iq, iw, kb, vb, ikb))
    nsub = DSA_STRATUM // Q_BLOCK
    outs = []
    for r in range(seq // DSA_STRATUM):
        n_keys = (r + 1) * DSA_STRATUM
        strat = lambda w: pl.BlockSpec((None, DSA_STRATUM, w), lambda b: (b, r, 0))
        keys = lambda w: pl.BlockSpec((None, n_keys, w), lambda b: (b, 0, 0))
        bias = pl.pallas_call(
            functools.partial(_dsa_select_prompt_kernel, row0=r * DSA_STRATUM, n_keys=n_keys, topk=topk),
            grid=(batch,),
            in_specs=[strat(512), strat(128), keys(128)],
            out_specs=pl.BlockSpec((None, DSA_STRATUM, n_keys), lambda b: (b, 0, 0)),
            out_shape=jax.ShapeDtypeStruct((batch, DSA_STRATUM, n_keys), F32),
            scratch_shapes=[pltpu.VMEM((DSA_STRATUM, n_keys), jnp.int32), pltpu.VMEM((DSA_STRATUM, LANES), jnp.int32)],
            compiler_params=_cparams(("parallel",)),
            name="dsa_select_prompt",
        )(iq, iw, ikb)
        qrow = lambda w: pl.BlockSpec((None, Q_BLOCK, w), lambda b, i: (b, r * nsub + i, 0))
        keys2 = lambda w: pl.BlockSpec((None, n_keys, w), lambda b, i: (b, 0, 0))
        outs.append(pl.pallas_call(
            _dsa_attend_prompt_kernel,
            grid=(batch, nsub),
            in_specs=[qrow(1024), pl.BlockSpec((None, Q_BLOCK, n_keys), lambda b, i: (b, i, 0)), keys2(256), keys2(256)],
            out_specs=pl.BlockSpec((None, Q_BLOCK, DSA_HEADS * DSA_HD), lambda b, i: (b, i, 0)),
            out_shape=jax.ShapeDtypeStruct((batch, DSA_STRATUM, DSA_HEADS * DSA_HD), F32),
            compiler_params=_cparams(("parallel", "arbitrary")),
            name="dsa_attend_prompt",
        )(q, bias, kb, vb))
    return jnp.concatenate(outs, axis=1).reshape(batch * seq, DSA_HEADS * DSA_HD)


def _swa_head_plan(h):
    group = SWA_HEADS // SWA_KV_HEADS
    kv = h // group
    return h // 2, h % 2, kv // 2, (kv % 2) != (h % 2)


def _swa_prompt_kernel(sink_ref, q_ref, kp_ref, kc_ref, kps_ref, kcs_ref, vp_ref, vc_ref, vps_ref, vcs_ref, y_ref):
    i = pl.program_id(1)
    r = lax.broadcasted_iota(jnp.int32, (Q_BLOCK, 2 * Q_BLOCK), 0)
    c = lax.broadcasted_iota(jnp.int32, (Q_BLOCK, 2 * Q_BLOCK), 1)
    rel = Q_BLOCK + r - c
    ok = (rel >= 0) & (rel <= WINDOW) & ((i - 1) * Q_BLOCK + c >= 0)
    bias = jnp.where(ok, 0.0, NEG_INF)
    keys = (jnp.concatenate([kp_ref[...], kc_ref[...]], axis=0), jnp.concatenate([kps_ref[...], kcs_ref[...]], axis=0))
    vals = (jnp.concatenate([vp_ref[...], vc_ref[...]], axis=0), jnp.concatenate([vps_ref[...], vcs_ref[...]], axis=0))
    lower = _half_mask(Q_BLOCK, False)
    for p in range(SWA_HEADS // 2):
        qpair = q_ref[:, p * LANES:(p + 1) * LANES]
        outs = []
        for h in (2 * p, 2 * p + 1):
            _, half, ks, swapped = _swa_head_plan(h)
            qh = jnp.where(_half_mask(Q_BLOCK, half == 1), qpair, jnp.zeros_like(qpair))
            kk = keys[int(swapped)][:, ks * LANES:(ks + 1) * LANES]
            vv = vals[int(swapped)][:, ks * LANES:(ks + 1) * LANES]
            s = _dot_nt(qh, kk) + bias
            sink = sink_ref[h]
            mx = jnp.maximum(jnp.max(s, axis=-1, keepdims=True), sink)
            pr = jnp.exp(s - mx)
            den = jnp.sum(pr, axis=-1, keepdims=True) + jnp.exp(sink - mx)
            outs.append(jnp.dot(pr.astype(BF16), vv, preferred_element_type=F32) / den)
        y_ref[:, p * LANES:(p + 1) * LANES] = jnp.where(lower, outs[0], outs[1])


def swa_prompt(q, kb, kbs, vb, vbs, sinks, *, batch, seq):
    m = q.shape[0]
    nq = seq // Q_BLOCK
    cur = pl.BlockSpec((Q_BLOCK, 256), lambda b, i, s: (b * nq + i, 0))
    prev = pl.BlockSpec((Q_BLOCK, 256), lambda b, i, s: (b * nq + jnp.maximum(i - 1, 0), 0))
    qspec = pl.BlockSpec((Q_BLOCK, 2048), lambda b, i, s: (b * nq + i, 0))
    return pl.pallas_call(
        _swa_prompt_kernel,
        grid_spec=pltpu.PrefetchScalarGridSpec(
            num_scalar_prefetch=1,
            grid=(batch, nq),
            in_specs=[qspec, prev, cur, prev, cur, prev, cur, prev, cur],
            out_specs=qspec,
        ),
        out_shape=jax.ShapeDtypeStruct((m, SWA_HEADS * SWA_HD), F32),
        compiler_params=_cparams(("parallel", "arbitrary")),
        name="swa_prompt",
    )(sinks, q, kb, kb, kbs, kbs, vb, vb, vbs, vbs)


def _mem_prompt_kernel(x_ref, g_ref, wq_ref, kv_ref, wo_ref, o_ref):
    x = x_ref[...]
    h = _rms(x, g_ref[...]).astype(BF16)
    q = (jnp.dot(h, wq_ref[...], preferred_element_type=F32) * (MEM_HD ** -0.5)).astype(BF16)
    width = MEM_HEADS * MEM_HD
    outs = []
    for hd in range(MEM_HEADS):
        sl = slice(hd * MEM_HD, (hd + 1) * MEM_HD)
        k = kv_ref[:, sl].astype(BF16)
        v = kv_ref[:, width + hd * MEM_HD:width + (hd + 1) * MEM_HD].astype(BF16)
        s = _dot_nt(q[:, sl], k)
        p = jnp.exp(s - jnp.max(s, axis=-1, keepdims=True))
        o = jnp.dot(p.astype(BF16), v, preferred_element_type=F32) / jnp.sum(p, axis=-1, keepdims=True)
        outs.append(o.astype(BF16))
    o_ref[...] = x + jnp.dot(jnp.concatenate(outs, axis=-1), wo_ref[...], preferred_element_type=F32)


def mem_attn_prompt(x, g, wq, kv, wo, *, batch, seq, tq):
    m = x.shape[0]
    nq = seq // tq
    width = MEM_HEADS * MEM_HD
    xspec = pl.BlockSpec((tq, D_MODEL), lambda b, i: (b * nq + i, 0))
    return pl.pallas_call(
        _mem_prompt_kernel,
        grid=(batch, nq),
        in_specs=[xspec,
                  pl.BlockSpec((1, D_MODEL), lambda b, i: (0, 0)),
                  pl.BlockSpec((D_MODEL, width), lambda b, i: (0, 0)),
                  pl.BlockSpec((N_MEM, 2 * width), lambda b, i: (b, 0)),
                  pl.BlockSpec((width, D_MODEL), lambda b, i: (0, 0))],
        out_specs=xspec,
        out_shape=jax.ShapeDtypeStruct((m, D_MODEL), F32),
        compiler_params=_cparams(("parallel", "arbitrary")),
        name="mem_attn_prompt",
    )(x, g.reshape(1, D_MODEL), wq, kv, wo)


def _gla_prep_kernel(gq_ref, gk_ref, sm_ref, wgate_ref, bgate_ref, o_ref):
    w = GLA_HEADS * GLA_DK
    la = _log_decay(sm_ref[...], wgate_ref[...], bgate_ref[...])
    o_ref[:, 0:w] = jnp.exp(la)
    o_ref[:, w:2 * w] = gk_ref[...]
    o_ref[:, 2 * w:3 * w] = gq_ref[...] * (GLA_DK ** -0.5)
    o_ref[:, 3 * w:4 * w] = jnp.zeros((gq_ref.shape[0], w), F32)


def gla_sample_prep(z, wgate, bgate):
    m = z.shape[0]
    w = GLA_HEADS * GLA_DK
    col = lambda wd, off: pl.BlockSpec((m, wd), lambda i: (0, off // wd))
    return pl.pallas_call(
        _gla_prep_kernel,
        grid=(1,),
        in_specs=[col(512, AB_GQ), col(512, AB_GK), col(128, AB_SMALL),
                  pl.BlockSpec((LANES, w), lambda i: (0, 0)), pl.BlockSpec((1, w), lambda i: (0, 0))],
        out_specs=pl.BlockSpec((m, 4 * w), lambda i: (0, 0)),
        out_shape=jax.ShapeDtypeStruct((m, 4 * w), F32),
        compiler_params=_cparams(("arbitrary",)),
        name="gla_sample_prep",
    )(z, z, z, wgate, bgate)


def _gla_step_kernel(p_ref, gv_ref, gr_ref, s_ref, gain_ref, so_ref, y_ref, *, bs):
    gain = gain_ref[...]
    for s in range(bs):
        xt = p_ref[s].T
        for h in range(GLA_HEADS):
            dv = slice(h * GLA_DV, (h + 1) * GLA_DV)
            st = s_ref[s, h] * xt[:, h:h + 1] + xt[:, GLA_HEADS + h:GLA_HEADS + h + 1] * gv_ref[s:s + 1, dv]
            so_ref[s, h] = st
            o = jnp.sum(xt[:, 2 * GLA_HEADS + h:2 * GLA_HEADS + h + 1] * st, axis=0, keepdims=True)
            y_ref[s:s + 1, dv] = _gla_gate_out(o, gr_ref[s:s + 1, dv], gain)


def gla_sample(p, z, state, gain, *, bs):
    m = z.shape[0]
    col = lambda w, off: pl.BlockSpec((bs, w), lambda i: (i, off // w))
    sspec = pl.BlockSpec((bs, GLA_HEADS, GLA_DK, GLA_DV), lambda i: (i, 0, 0, 0))
    return pl.pallas_call(
        functools.partial(_gla_step_kernel, bs=bs),
        grid=(m // bs,),
        in_specs=[pl.BlockSpec((bs, 16, GLA_DK), lambda i: (i, 0, 0)), col(1024, AB_GV), col(1024, AB_GR), sspec,
                  pl.BlockSpec((1, GLA_DV), lambda i: (0, 0))],
        out_specs=[sspec, pl.BlockSpec((bs, GLA_HEADS * GLA_DV), lambda i: (i, 0))],
        out_shape=[jax.ShapeDtypeStruct(state.shape, F32), jax.ShapeDtypeStruct((m, GLA_HEADS * GLA_DV), F32)],
        compiler_params=_cparams(("parallel",)),
        name="gla_sample",
    )(p, z, z, state, gain)


def _dsa_scores_kernel(pt_ref, iq_ref, iw_ref, *refs, pg):
    page_refs, o_ref = refs[:pg], refs[pg]
    q8 = iq_ref[0]
    iw = iw_ref[0]
    for j in range(pg):
        dots = jnp.dot(q8, page_refs[j][0].astype(BF16), preferred_element_type=F32)
        o_ref[0, j:j + 1, :] = jnp.sum(jnp.maximum(dots, 0.0) * iw, axis=0, keepdims=True)


def dsa_sample_scores(page_table, iq, iw, pool_ik_t, page_base, *, pg):
    m, n_pages = page_table.shape
    pool_ik = pool_ik_t
    page_spec = lambda j: pl.BlockSpec((1, IDX_HD, PAGE_SIZE),
                                       lambda b, g, pt: (page_base + pt[b, g * pg + j], 0, 0))
    return pl.pallas_call(
        functools.partial(_dsa_scores_kernel, pg=pg),
        grid_spec=pltpu.PrefetchScalarGridSpec(
            num_scalar_prefetch=1,
            grid=(m, n_pages // pg),
            in_specs=[pl.BlockSpec((1, IDX_HEADS, IDX_HD), lambda b, g, pt: (b, 0, 0)),
                      pl.BlockSpec((1, IDX_HEADS, 1), lambda b, g, pt: (b, 0, 0))]
                     + [page_spec(j) for j in range(pg)],
            out_specs=pl.BlockSpec((1, pg, PAGE_SIZE), lambda b, g, pt: (b, g, 0)),
        ),
        out_shape=jax.ShapeDtypeStruct((m, n_pages, PAGE_SIZE), F32),
        compiler_params=_cparams(("parallel", "arbitrary")),
        name="dsa_sample_scores",
    )(page_table, iq, iw, *([pool_ik] * pg))


def _dsa_select_sample_kernel(sc_ref, iq_ref, iw_ref, ikb_ref, bias_ref, bnew_ref, key_ref, t_ref, *, n_past, topk):
    rows = sc_ref.shape[0]
    n_blocks = n_past // LANES
    for c in range(n_blocks):
        sl = slice(c * LANES, (c + 1) * LANES)
        key_ref[:, sl] = _order_key(sc_ref[:, sl])
    ik = ikb_ref[...].astype(F32)
    iw = iw_ref[...]
    s_new = jnp.zeros((rows, 1), F32)
    for h in range(IDX_HEADS):
        pair = iq_ref[:, (h // 2) * LANES:(h // 2 + 1) * LANES].astype(F32)
        qh = jnp.where(_half_mask(rows, h % 2 == 1), pair, 0.0)
        s_new = s_new + jnp.maximum(jnp.sum(qh * ik, axis=-1, keepdims=True), 0.0) * iw[:, h:h + 1]
    lane = lax.broadcasted_iota(jnp.int32, (rows, LANES), 1)
    key_ref[:, n_past:n_past + LANES] = _order_key(jnp.where(lane == 0, s_new, -jnp.inf))

    ri = lax.broadcasted_iota(jnp.int32, (LANES, DSA_KV_HEADS * LANES), 0)
    ci = lax.broadcasted_iota(jnp.int32, (LANES, DSA_KV_HEADS * LANES), 1)
    spread = jnp.where(ci // DSA_KV_HEADS == ri, 1.0, 0.0).astype(BF16)

    def write(rws, c, sel):
        if c == n_blocks:
            bnew_ref[rws, :] = jnp.where(sel, 0.0, NEG_INF)
        else:
            wide = jnp.dot(jnp.where(sel, 1.0, 0.0).astype(BF16), spread, preferred_element_type=F32)
            w = DSA_KV_HEADS * LANES
            bias_ref[rws, c * w:(c + 1) * w] = jnp.where(wide > 0.5, 0.0, NEG_INF)

    _select_topk(key_ref, t_ref, n_blocks + 1, topk, write)


def dsa_sample_select(scores, iq, iw, ikb, *, topk):
    m, n_past = scores.shape
    full = lambda a: pl.BlockSpec(a.shape, lambda i: (0,) * a.ndim)
    wide = DSA_KV_HEADS * n_past
    return pl.pallas_call(
        functools.partial(_dsa_select_sample_kernel, n_past=n_past, topk=topk),
        grid=(1,),
        in_specs=[full(scores), full(iq), full(iw), full(ikb)],
        out_specs=[pl.BlockSpec((m, wide), lambda i: (0, 0)), pl.BlockSpec((m, LANES), lambda i: (0, 0))],
        out_shape=[jax.ShapeDtypeStruct((m, wide), F32), jax.ShapeDtypeStruct((m, LANES), F32)],
        scratch_shapes=[pltpu.VMEM((m, n_past + LANES), jnp.int32), pltpu.VMEM((m, LANES), jnp.int32)],
        compiler_params=_cparams(("arbitrary",)),
        name="dsa_sample_select",
    )(scores, iq, iw, ikb)


def _dsa_step_kernel(pt_ref, q_ref, bias_ref, bnew_ref, kn_ref, vn_ref, *refs, pg):
    k_refs, v_refs = refs[:pg], refs[pg:2 * pg]
    y_ref, m_ref, l_ref, acc_ref = refs[2 * pg:]
    g = pl.program_id(1)

    @pl.when(g == 0)
    def _():
        m_ref[...] = jnp.full_like(m_ref, NEG_INF)
        l_ref[...] = jnp.zeros_like(l_ref)
        acc_ref[...] = jnp.zeros_like(acc_ref)

    q8 = q_ref[0]
    group = DSA_HEADS // DSA_KV_HEADS
    first = lax.broadcasted_iota(jnp.int32, (DSA_HEADS, DSA_HD), 0) < group
    wide = DSA_KV_HEADS * PAGE_SIZE
    hrow = lax.broadcasted_iota(jnp.int32, (DSA_HEADS, wide), 0)
    col = lax.broadcasted_iota(jnp.int32, (DSA_HEADS, wide), 1)
    own = (col % DSA_KV_HEADS) == (hrow // group)
    ss, oks = [], []
    for j in range(pg):
        ok = jnp.logical_and(own, bias_ref[0, j:j + 1, :] == 0.0)
        ss.append(jnp.where(ok, _dot_nt(q8, k_refs[j][...].astype(BF16)), NEG_INF))
        oks.append(ok)
    mx = ss[0]
    for s in ss[1:]:
        mx = jnp.maximum(mx, s)
    m_old = m_ref[...]
    m_new = jnp.maximum(m_old, jnp.max(mx, axis=-1, keepdims=True))
    psum = jnp.zeros((DSA_HEADS, wide), F32)
    pv = jnp.zeros((DSA_HEADS, DSA_HD), F32)
    for j in range(pg):
        p = jnp.where(oks[j], jnp.exp(ss[j] - m_new[:, 0:1]), 0.0)
        psum = psum + p
        pv = pv + jnp.dot(p.astype(BF16), v_refs[j][...].astype(BF16), preferred_element_type=F32)
    alpha = jnp.exp(m_old - m_new)
    l_ref[...] = alpha * l_ref[...] + jnp.sum(psum, axis=-1, keepdims=True)
    acc_ref[...] = alpha * acc_ref[...] + pv
    m_ref[...] = m_new

    @pl.when(g == pl.num_programs(1) - 1)
    def _():
        kn = kn_ref[0].astype(BF16).astype(F32)
        vn = vn_ref[0].astype(BF16).astype(F32)
        bn = bnew_ref[0][:, 0:1]
        s_new = jnp.sum(q8.astype(F32) * jnp.where(first, kn[:, :DSA_HD], kn[:, DSA_HD:]), axis=-1, keepdims=True) + bn
        m_old = m_ref[...]
        m_new = jnp.maximum(m_old, s_new)
        p_new = jnp.where(bn == 0.0, jnp.exp(s_new - m_new), 0.0)
        alpha = jnp.exp(m_old - m_new)
        acc = alpha * acc_ref[...] + p_new * jnp.where(first, vn[:, :DSA_HD], vn[:, DSA_HD:])
        y_ref[0] = acc / (alpha * l_ref[...] + p_new)


def dsa_sample_attend(page_table, q, bias, bnew, kn, vn, pool_k, pool_v, page_base, *, pg):
    m, n_pages = page_table.shape
    wide = DSA_KV_HEADS * PAGE_SIZE
    page_spec = lambda j: pl.BlockSpec((wide, DSA_HD), lambda b, g, pt: (page_base + pt[b, g * pg + j], 0))
    per_b = lambda shape: pl.BlockSpec((1,) + shape, lambda b, g, pt: (b, 0, 0))
    return pl.pallas_call(
        functools.partial(_dsa_step_kernel, pg=pg),
        grid_spec=pltpu.PrefetchScalarGridSpec(
            num_scalar_prefetch=1,
            grid=(m, n_pages // pg),
            in_specs=[per_b((DSA_HEADS, DSA_HD)),
                      pl.BlockSpec((1, pg, wide), lambda b, g, pt: (b, g, 0)),
                      per_b((1, LANES)), per_b((1, DSA_KV_HEADS * DSA_HD)), per_b((1, DSA_KV_HEADS * DSA_HD))]
                     + [page_spec(j) for j in range(pg)] * 2,
            out_specs=per_b((DSA_HEADS, DSA_HD)),
            scratch_shapes=[pltpu.VMEM((DSA_HEADS, DSA_HD), F32)] * 3,
        ),
        out_shape=jax.ShapeDtypeStruct((m, DSA_HEADS, DSA_HD), F32),
        compiler_params=_cparams(("parallel", "arbitrary")),
        name="dsa_sample_attend",
    )(page_table, q, bias, bnew, kn, vn, *([pool_k] * pg), *([pool_v] * pg))


def _swa_step_kernel(q_ref, kc_ref, vc_ref, kn_ref, vn_ref, sink_ref, y_ref, *, bs):
    group = SWA_HEADS // SWA_KV_HEADS
    for s in range(bs):
        for kv in range(SWA_KV_HEADS):
            sl = slice(kv * SWA_HD, (kv + 1) * SWA_HD)
            qv = q_ref[s, kv * group:(kv + 1) * group, :]
            kn = kn_ref[s:s + 1, sl].astype(BF16).astype(F32)
            vn = vn_ref[s:s + 1, sl].astype(BF16).astype(F32)
            sc = _dot_nt(qv, kc_ref[s, :, sl].astype(BF16))
            s_new = jnp.sum(qv.astype(F32) * kn, axis=-1, keepdims=True)
            sink = sink_ref[kv]
            mx = jnp.maximum(jnp.maximum(jnp.max(sc, axis=-1, keepdims=True), s_new), sink)
            p = jnp.exp(sc - mx)
            p_new = jnp.exp(s_new - mx)
            den = jnp.sum(p, axis=-1, keepdims=True) + p_new + jnp.exp(sink - mx)
            o = jnp.dot(p.astype(BF16), vc_ref[s, :, sl].astype(BF16), preferred_element_type=F32) + p_new * vn
            y_ref[s, kv * group:(kv + 1) * group, :] = o / den


def swa_sample(q, kc, vc, kn, vn, sinks, *, bs):
    m = q.shape[0]
    group = SWA_HEADS // SWA_KV_HEADS
    width = SWA_KV_HEADS * SWA_HD
    return pl.pallas_call(
        functools.partial(_swa_step_kernel, bs=bs),
        grid=(m // bs,),
        in_specs=[pl.BlockSpec((bs, SWA_HEADS, SWA_HD), lambda i: (i, 0, 0)),
                  pl.BlockSpec((bs, WINDOW, width), lambda i: (i, 0, 0)),
                  pl.BlockSpec((bs, WINDOW, width), lambda i: (i, 0, 0)),
                  pl.BlockSpec((bs, width), lambda i: (i, 0)),
                  pl.BlockSpec((bs, width), lambda i: (i, 0)),
                  pl.BlockSpec((SWA_KV_HEADS, group, 1), lambda i: (0, 0, 0))],
        out_specs=pl.BlockSpec((bs, SWA_HEADS, SWA_HD), lambda i: (i, 0, 0)),
        out_shape=jax.ShapeDtypeStruct((m, SWA_HEADS, SWA_HD), F32),
        compiler_params=_cparams(("parallel",)),
        name="swa_sample",
    )(q, kc, vc, kn, vn, sinks.reshape(SWA_KV_HEADS, group, 1))


def _mem_step_kernel(q_ref, k_ref, v_ref, y_ref, *, bs):
    rows = q_ref.shape[1]
    n = N_MEM * MEM_HEADS
    hrow = lax.broadcasted_iota(jnp.int32, (rows, n), 0)
    col = lax.broadcasted_iota(jnp.int32, (rows, n), 1)
    own = (col % MEM_HEADS) == (hrow % MEM_HEADS)
    for s in range(bs):
        q = (q_ref[s] * (MEM_HD ** -0.5)).astype(BF16)
        sc = jnp.where(own, _dot_nt(q, k_ref[s * n:(s + 1) * n, :].astype(BF16)), NEG_INF)
        p = jnp.where(own, jnp.exp(sc - jnp.max(sc, axis=-1, keepdims=True)), 0.0)
        o = jnp.dot(p.astype(BF16), v_ref[s * n:(s + 1) * n, :].astype(BF16), preferred_element_type=F32)
        y_ref[s] = o / jnp.sum(p, axis=-1, keepdims=True)


def mem_attn_sample(q, mk, mv, layer, *, bs):
    m, rows, _ = q.shape
    n = N_MEM * MEM_HEADS
    nb = m // bs
    cache = pl.BlockSpec((bs * n, MEM_HD), lambda i: (layer * nb + i, 0))
    return pl.pallas_call(
        functools.partial(_mem_step_kernel, bs=bs),
        grid=(nb,),
        in_specs=[pl.BlockSpec((bs, rows, MEM_HD), lambda i: (i, 0, 0)), cache, cache],
        out_specs=pl.BlockSpec((bs, rows, MEM_HD), lambda i: (i, 0, 0)),
        out_shape=jax.ShapeDtypeStruct((m, rows, MEM_HD), F32),
        compiler_params=_cparams(("parallel",)),
        name="mem_attn_sample",
    )(q, mk, mv)


TM_FFN, TF_FFN, SUB_FFN = 1024, 512, 512
TM_PROJ, TN_AB, TN_C, TN_MEM = 1024, 768, 512, 512
TM_OUT = 256
TR_FEAT = 512
TC_GLA = 256
TQ_MEM = 512
BS_SAMPLE = 8
PG_SCORES = 16
PG_ATTEND = 8


def _prep_w_in_ab(w):
    sizes = (GLA_HEADS * GLA_DK, GLA_HEADS * GLA_DK, GLA_HEADS * GLA_DV, GLA_HEADS * GLA_DV, GLA_GATE_RANK,
             DSA_HEADS * DSA_HD, DSA_KV_HEADS * DSA_HD, DSA_KV_HEADS * DSA_HD, IDX_HEADS * IDX_HD, IDX_HEADS, IDX_HD)
    offs = np.cumsum((0,) + sizes)
    gq, gk, gv, gr, gd, dq, dk, dv, iq, iw, ik = [w[:, int(offs[j]):int(offs[j + 1])] for j in range(len(sizes))]
    pad = lambda n: jnp.zeros((w.shape[0], n), w.dtype)
    small = jnp.concatenate([ik, gd, iw, pad(LANES - IDX_HD - GLA_GATE_RANK - IDX_HEADS)], axis=1)
    out = jnp.concatenate([gv, gr, dq, gq, gk, iq, dk, dv, small, pad(AB_WIDTH - AB_SMALL - LANES)], axis=1)
    return out.astype(BF16)


def _prep_gate(w_up):
    return jnp.zeros((LANES, GLA_HEADS * GLA_DK), F32).at[SM_GD:SM_GD + GLA_GATE_RANK].set(w_up)


def kernel(x_prompt, x_sample, mem_prompt, cache_dsa_k, cache_dsa_v, cache_dsa_idx_k, state_gla, cache_swa_k, cache_swa_v, cache_mem_k, cache_mem_v, page_table, norm_ffn, w_ffn_gate, w_ffn_up, w_ffn_down, norm_mix, w_in_ab, w_gla_gate_up, b_gla_gate, gla_out_norm, idx_k_norm, w_out_ab, w_in_c, swa_sinks, w_out_c, norm_mem_q, norm_mem_src, w_mem_q, w_mem_kv, w_mem_o, final_norm):
    depth = norm_mix.shape[0]
    bp, seq, _ = x_prompt.shape
    bs = x_sample.shape[0]
    n_pool = cache_dsa_k.shape[1]
    gla_w = GLA_HEADS * GLA_DV

    wg, wu, wd = w_ffn_gate.astype(BF16), w_ffn_up.astype(BF16), w_ffn_down.astype(BF16)
    w_ab = [_prep_w_in_ab(w_in_ab[i]) for i in range(w_in_ab.shape[0])]
    w_gate = [_prep_gate(w_gla_gate_up[i]) for i in range(w_in_ab.shape[0])]
    w_oab = w_out_ab.astype(BF16)
    w_c, w_oc = w_in_c.astype(BF16), w_out_c.astype(BF16)
    w_mq, w_mkv, w_mo = w_mem_q.astype(BF16), w_mem_kv.astype(BF16), w_mem_o.astype(BF16)

    def rope_tabs(pos):
        return _rope_tables(pos, DSA_HD) + _rope_tables(pos, IDX_HD)

    def ffn_pair(x, layer, half, last):
        return ffn(x, norm_ffn[layer, half], wg, wu, wd, (layer, half),
                   final_norm if last else None, tm=TM_FFN, tf=TF_FFN, sub=SUB_FFN)

    tabs_p = rope_tabs(jnp.arange(seq))
    x = x_prompt.reshape(bp * seq, D_MODEL)
    mem = mem_prompt.reshape(bp * N_MEM, D_MODEL)
    mem_kv = [norm_proj(mem, norm_mem_src[l], w_mkv[l], tm=TM_PROJ, tn=TN_MEM) for l in range(depth)]
    st_ab_p, st_c_p = [], []
    for l in range(depth):
        i = l // 2
        x = ffn_pair(x, l, 0, False)
        if l % 2 == 0:
            z = norm_proj(x, norm_mix[l], w_ab[i], tm=TM_PROJ, tn=TN_AB)
            q, k, kb, vb, iq, ik, ikb, iw = ab_features(z, tabs_p, idx_k_norm[i], seq=seq, tr=TR_FEAT)
            y_gla, st_t = gla_prompt(z, w_gate[i], b_gla_gate[i].reshape(1, -1), gla_out_norm[i].reshape(1, -1),
                                     batch=bp, seq=seq, tc=TC_GLA)
            y_dsa = dsa_prompt(q, iq, iw, kb, vb, ikb, batch=bp, seq=seq)
            x = out_proj(x, [y_gla, y_dsa], [w_oab[i, :gla_w], w_oab[i, gla_w:]], tm=TM_OUT)
            n_pg = seq // PAGE_SIZE
            st_ab_p.append((k.reshape(bp, n_pg, PAGE_SIZE, DSA_KV_HEADS, DSA_HD),
                            z[:, AB_DV:AB_DV + 256].reshape(bp, n_pg, PAGE_SIZE, DSA_KV_HEADS, DSA_HD),
                            ik.reshape(bp, n_pg, PAGE_SIZE, IDX_HD),
                            jnp.swapaxes(st_t, 2, 3)))
        else:
            z = norm_proj(x, norm_mix[l], w_c[i], tm=TM_PROJ, tn=TN_C)
            q, k, kb, kbs, vb, vbs = c_features(z, tabs_p[2:], seq=seq, tr=TR_FEAT)
            y = swa_prompt(q, kb, kbs, vb, vbs, swa_sinks[i], batch=bp, seq=seq)
            x = out_proj(x, [y], [w_oc[i]], tm=TM_OUT)
            st_c_p.append((k.reshape(bp, seq, SWA_KV_HEADS, SWA_HD)[:, -WINDOW:],
                           z[:, C_V:C_V + 256].reshape(bp, seq, SWA_KV_HEADS, SWA_HD)[:, -WINDOW:]))
        x = mem_attn_prompt(x, norm_mem_q[l], w_mq[l], mem_kv[l], w_mo[l], batch=bp, seq=seq, tq=TQ_MEM)
        x = ffn_pair(x, l, 1, l == depth - 1)
    y_prompt = x.reshape(bp, seq, D_MODEL)

    tabs_s = rope_tabs(PAST_LEN + jnp.arange(1))
    n_pages = page_table.shape[1]
    topk = min(DSA_TOPK_MAX, (PAST_LEN + 1) // 4)
    pool_k = cache_dsa_k.reshape(-1, DSA_HD)
    pool_v = cache_dsa_v.reshape(-1, DSA_HD)
    pool_ik_t = jnp.swapaxes(cache_dsa_idx_k, 2, 3).reshape(-1, IDX_HD, PAGE_SIZE)
    mem_k_rows = cache_mem_k.reshape(-1, MEM_HD)
    mem_v_rows = cache_mem_v.reshape(-1, MEM_HD)
    x = x_sample.reshape(bs, D_MODEL)
    st_ab_s, st_c_s = [], []
    for l in range(depth):
        i = l // 2
        x = ffn_pair(x, l, 0, False)
        if l % 2 == 0:
            z = norm_proj(x, norm_mix[l], w_ab[i], tm=TM_PROJ, tn=TN_AB)
            q, k, kb, vb, iq, ik, ikb, iw = ab_features(z, tabs_s, idx_k_norm[i], seq=1, tr=TR_FEAT)
            p = gla_sample_prep(z, w_gate[i], b_gla_gate[i].reshape(1, -1)).reshape(bs, 16, GLA_DK)
            st_new, y_gla = gla_sample(p, z, state_gla[i], gla_out_norm[i].reshape(1, -1), bs=BS_SAMPLE)
            scores = dsa_sample_scores(page_table, iq.reshape(bs, IDX_HEADS, IDX_HD),
                                       iw[:, :IDX_HEADS].reshape(bs, IDX_HEADS, 1), pool_ik_t, i * n_pool, pg=PG_SCORES)
            bias, bnew = dsa_sample_select(scores.reshape(bs, n_pages * PAGE_SIZE), iq, iw, ikb, topk=topk)
            v_new = z[:, AB_DV:AB_DV + 256]
            y_dsa = dsa_sample_attend(
                page_table, q.reshape(bs, DSA_HEADS, DSA_HD), bias.reshape(bs, n_pages, DSA_KV_HEADS * PAGE_SIZE),
                bnew.reshape(bs, 1, LANES), k.reshape(bs, 1, 256), v_new.reshape(bs, 1, 256),
                pool_k, pool_v, i * n_pool, pg=PG_ATTEND).reshape(bs, DSA_HEADS * DSA_HD)
            x = out_proj(x, [y_gla, y_dsa], [w_oab[i, :gla_w], w_oab[i, gla_w:]], tm=TM_OUT)
            st_ab_s.append((k.reshape(bs, 1, DSA_KV_HEADS, DSA_HD), v_new.reshape(bs, 1, DSA_KV_HEADS, DSA_HD),
                            ik.reshape(bs, 1, IDX_HD), st_new))
        else:
            z = norm_proj(x, norm_mix[l], w_c[i], tm=TM_PROJ, tn=TN_C)
            q, k, kb, kbs, vb, vbs = c_features(z, tabs_s[2:], seq=1, tr=TR_FEAT)
            v_new = z[:, C_V:C_V + 256]
            y = swa_sample(q.reshape(bs, SWA_HEADS, SWA_HD), cache_swa_k[i].reshape(bs, WINDOW, 256),
                           cache_swa_v[i].reshape(bs, WINDOW, 256), k, v_new, swa_sinks[i], bs=BS_SAMPLE)
            x = out_proj(x, [y.reshape(bs, SWA_HEADS * SWA_HD)], [w_oc[i]], tm=TM_OUT)
            st_c_s.append((jnp.concatenate([cache_swa_k[i][:, 1:], k.reshape(bs, 1, SWA_KV_HEADS, SWA_HD)], axis=1),
                           jnp.concatenate([cache_swa_v[i][:, 1:], v_new.reshape(bs, 1, SWA_KV_HEADS, SWA_HD)], axis=1)))
        qm = norm_proj(x, norm_mem_q[l], w_mq[l], tm=TM_PROJ, tn=TN_MEM).reshape(bs, MEM_HEADS, MEM_HD)
        qm = jnp.pad(qm, ((0, 0), (0, SUBLANES - MEM_HEADS), (0, 0)))
        om = mem_attn_sample(qm, mem_k_rows, mem_v_rows, l, bs=BS_SAMPLE)
        x = out_proj(x, [om[:, :MEM_HEADS].reshape(bs, MEM_HEADS * MEM_HD)], [w_mo[l]], tm=TM_OUT)
        x = ffn_pair(x, l, 1, l == depth - 1)
    y_sample = x.reshape(bs, 1, D_MODEL)

    stk = lambda sts, j: jnp.stack([s[j] for s in sts])
    mw = MEM_HEADS * MEM_HD
    mem_k_p = jnp.stack([kv[:, :mw].reshape(bp, N_MEM, MEM_HEADS, MEM_HD) for kv in mem_kv])
    mem_v_p = jnp.stack([kv[:, mw:].reshape(bp, N_MEM, MEM_HEADS, MEM_HD) for kv in mem_kv])
    return (y_prompt, y_sample, stk(st_ab_p, 0), stk(st_ab_p, 1), stk(st_ab_p, 2),
            stk(st_ab_s, 0), stk(st_ab_s, 1), stk(st_ab_s, 2), stk(st_ab_p, 3), stk(st_ab_s, 3),
            stk(st_c_p, 0), stk(st_c_p, 1), stk(st_c_s, 0), stk(st_c_s, 1), mem_k_p, mem_v_p)
```

```python
import functools

import jax
import jax.numpy as jnp
import numpy as np
from jax import lax
from jax.experimental import pallas as pl
from jax.experimental.pallas import tpu as pltpu

F32 = jnp.float32
BF16 = jnp.bfloat16

D_MODEL = 2048
D_FF = 5632
EPS = 1e-6
ROPE_THETA = 10000.0
NEG_INF = -1e30
PAST_LEN = 8192
PAGE_SIZE = 128
Q_BLOCK = 128
GLA_HEADS, GLA_DK, GLA_DV = 4, 128, 256
GLA_GATE_RANK = 16
GLA_GATE_TAU = 16.0
GLA_CHUNK = 64
DSA_HEADS, DSA_KV_HEADS, DSA_HD = 8, 2, 128
IDX_HEADS, IDX_HD = 8, 64
DSA_TOPK_MAX = 256
SWA_HEADS, SWA_KV_HEADS, SWA_HD = 32, 4, 64
WINDOW = 128
MEM_HEADS, MEM_HD = 4, 128
N_MEM = 256

LANES = 128
SUBLANES = 8
VMEM_LIMIT_BYTES = 60000 * 1024
ELEM_SUB = 128

AB_GV, AB_GR, AB_DQ = 0, 1024, 2048
AB_GQ, AB_GK, AB_IQ = 3072, 3584, 4096
AB_DK, AB_DV = 4608, 4864
AB_SMALL = 5120
AB_WIDTH = 5376
SM_IK, SM_GD, SM_IW = 0, 64, 80
C_Q, C_K, C_V, C_WIDTH = 0, 2048, 2304, 2560


def _cparams(sem):
    return pltpu.CompilerParams(dimension_semantics=sem, vmem_limit_bytes=VMEM_LIMIT_BYTES)


def _rms(x, g):
    y = x * lax.rsqrt(jnp.mean(x * x, axis=-1, keepdims=True) + EPS)
    return y * g


def _ffn_kernel(x_ref, g_ref, wg_ref, wu_ref, wd_ref, fg_ref, o_ref, h_ref, *, final_norm, sub):
    j = pl.program_id(1)
    tm = x_ref.shape[0]

    esub = min(ELEM_SUB, tm)

    def row_group(r):
        return pl.ds(pl.multiple_of(r * esub, esub), esub)

    @pl.when(j == 0)
    def _():
        def body(r, carry):
            rows = row_group(r)
            h_ref[rows, :] = _rms(x_ref[rows, :], g_ref[...]).astype(BF16)
            o_ref[rows, :] = jnp.zeros((esub, D_MODEL), F32)
            return carry

        lax.fori_loop(0, tm // esub, body, 0)

    for r in range(tm // sub):
        rows = slice(r * sub, (r + 1) * sub)
        h = h_ref[rows, :]
        a = jnp.dot(h, wg_ref[...], preferred_element_type=F32)
        u = jnp.dot(h, wu_ref[...], preferred_element_type=F32)
        act = (a * jax.nn.sigmoid(a) * u).astype(BF16)
        o_ref[rows, :] += jnp.dot(act, wd_ref[...], preferred_element_type=F32)

    @pl.when(j == pl.num_programs(1) - 1)
    def _():
        def body(r, carry):
            rows = row_group(r)
            y = x_ref[rows, :] + 0.5 * o_ref[rows, :]
            if final_norm:
                y = _rms(y, fg_ref[...])
            o_ref[rows, :] = y
            return carry

        lax.fori_loop(0, tm // esub, body, 0)


def ffn(x, g, wg, wu, wd, sel, final_g=None, *, tm, tf, sub):
    m = x.shape[0]
    tm = min(tm, m)
    sub = min(sub, tm)
    layer, half = sel
    fg = g if final_g is None else final_g
    return pl.pallas_call(
        functools.partial(_ffn_kernel, final_norm=final_g is not None, sub=sub),
        grid=(m // tm, D_FF // tf),
        in_specs=[
            pl.BlockSpec((tm, D_MODEL), lambda i, j: (i, 0)),
            pl.BlockSpec((1, D_MODEL), lambda i, j: (0, 0)),
            pl.BlockSpec((None, None, D_MODEL, tf), lambda i, j: (layer, half, 0, j)),
            pl.BlockSpec((None, None, D_MODEL, tf), lambda i, j: (layer, half, 0, j)),
            pl.BlockSpec((None, None, tf, D_MODEL), lambda i, j: (layer, half, j, 0)),
            pl.BlockSpec((1, D_MODEL), lambda i, j: (0, 0)),
        ],
        out_specs=pl.BlockSpec((tm, D_MODEL), lambda i, j: (i, 0)),
        out_shape=jax.ShapeDtypeStruct((m, D_MODEL), F32),
        scratch_shapes=[pltpu.VMEM((tm, D_MODEL), BF16)],
        compiler_params=_cparams(("parallel", "arbitrary")),
        name="ffn",
    )(x, g.reshape(1, D_MODEL), wg, wu, wd, fg.reshape(1, D_MODEL))


def _proj_kernel(x_ref, g_ref, w_ref, o_ref, h_ref, *, sub):
    tm = x_ref.shape[0]

    @pl.when(pl.program_id(1) == 0)
    def _():
        for r in range(tm // sub):
            rows = slice(r * sub, (r + 1) * sub)
            h_ref[rows, :] = _rms(x_ref[rows, :], g_ref[...]).astype(BF16)

    o_ref[...] = jnp.dot(h_ref[...], w_ref[...], preferred_element_type=F32)


def norm_proj(x, g, w, *, tm, tn, sub=512):
    m, n = x.shape[0], w.shape[1]
    tm = min(tm, m)
    return pl.pallas_call(
        functools.partial(_proj_kernel, sub=min(sub, tm)),
        grid=(m // tm, n // tn),
        in_specs=[
            pl.BlockSpec((tm, D_MODEL), lambda i, j: (i, 0)),
            pl.BlockSpec((1, D_MODEL), lambda i, j: (0, 0)),
            pl.BlockSpec((D_MODEL, tn), lambda i, j: (0, j)),
        ],
        out_specs=pl.BlockSpec((tm, tn), lambda i, j: (i, j)),
        out_shape=jax.ShapeDtypeStruct((m, n), F32),
        scratch_shapes=[pltpu.VMEM((tm, D_MODEL), BF16)],
        compiler_params=_cparams(("parallel", "arbitrary")),
        name="norm_proj",
    )(x, g.reshape(1, D_MODEL), w)


def _outproj_kernel(*refs, n_in):
    x_ref = refs[0]
    y_refs = refs[1:1 + n_in]
    w_refs = refs[1 + n_in:1 + 2 * n_in]
    o_ref = refs[1 + 2 * n_in]
    acc = x_ref[...]
    for y_ref, w_ref in zip(y_refs, w_refs):
        acc = acc + jnp.dot(y_ref[...].astype(BF16), w_ref[...], preferred_element_type=F32)
    o_ref[...] = acc


def out_proj(x, ys, ws, *, tm):
    m = x.shape[0]
    tm = min(tm, m)
    n_in = len(ys)
    in_specs = [pl.BlockSpec((tm, D_MODEL), lambda i: (i, 0))]
    in_specs += [pl.BlockSpec((tm, y.shape[1]), lambda i: (i, 0)) for y in ys]
    in_specs += [pl.BlockSpec(w.shape, lambda i: (0, 0)) for w in ws]
    return pl.pallas_call(
        functools.partial(_outproj_kernel, n_in=n_in),
        grid=(m // tm,),
        in_specs=in_specs,
        out_specs=pl.BlockSpec((tm, D_MODEL), lambda i: (i, 0)),
        out_shape=jax.ShapeDtypeStruct((m, D_MODEL), F32),
        compiler_params=_cparams(("parallel",)),
        name="out_proj",
    )(x, *ys, *ws)


def _rope_tables(pos, hd):
    half = hd // 2
    inv = ROPE_THETA ** (-jnp.arange(half, dtype=F32) / half)
    ang = pos.astype(F32)[:, None] * inv[None, :]
    cos, sin = jnp.cos(ang), jnp.sin(ang)
    reps = LANES // hd
    return (jnp.concatenate([cos, cos] * reps, axis=-1),
            jnp.concatenate([-sin, sin] * reps, axis=-1))


def _rope128(x, cos, sin):
    return x * cos + pltpu.roll(x, 64, 1) * sin


def _rope64(x, cos, sin, lower):
    partner = jnp.where(lower, pltpu.roll(x, 96, 1), pltpu.roll(x, 32, 1))
    return x * cos + partner * sin


def _lower32_mask(rows):
    lane = lax.broadcasted_iota(jnp.int32, (rows, LANES), 1)
    return (lane % 64) < 32


def _ab_feat_kernel(dq_ref, dk_ref, dv_ref, iq_ref, sm_ref, c128_ref, s128_ref, c64_ref, s64_ref, gik_ref,
                    q_ref, k_ref, kb_ref, vb_ref, iqo_ref, ik_ref, ikb_ref, iw_ref):
    rows = dq_ref.shape[0]
    c128, s128 = c128_ref[...], s128_ref[...]
    c64, s64 = c64_ref[...], s64_ref[...]
    lower = _lower32_mask(rows)
    for h in range(DSA_HEADS):
        sl = slice(h * LANES, (h + 1) * LANES)
        q_ref[:, sl] = (_rope128(dq_ref[:, sl], c128, s128) * (DSA_HD ** -0.5)).astype(BF16)
    for h in range(DSA_KV_HEADS):
        sl = slice(h * LANES, (h + 1) * LANES)
        kr = _rope128(dk_ref[:, sl], c128, s128)
        k_ref[:, sl] = kr
        kb_ref[:, sl] = kr.astype(BF16)
    vb_ref[...] = dv_ref[...].astype(BF16)
    for p in range(IDX_HEADS * IDX_HD // LANES):
        sl = slice(p * LANES, (p + 1) * LANES)
        iqo_ref[:, sl] = (_rope64(iq_ref[:, sl], c64, s64, lower) * (IDX_HD ** -0.5)).astype(BF16)
    sm = sm_ref[...]
    lane = lax.broadcasted_iota(jnp.int32, (rows, LANES), 1)
    ik = jnp.where(lane < IDX_HD, sm, 0.0)
    ik = ik * lax.rsqrt(jnp.sum(ik * ik, axis=-1, keepdims=True) / IDX_HD + EPS) * gik_ref[...]
    ik = _rope64(ik, c64, s64, lower)
    ik_ref[...] = ik[:, :IDX_HD]
    ikb_ref[...] = jnp.where(lane < IDX_HD, ik, pltpu.roll(ik, 64, 1)).astype(BF16)
    iw_ref[...] = pltpu.roll(sm, LANES - SM_IW, 1) * (IDX_HEADS ** -0.5)


def ab_features(z, tabs, gik, *, seq, tr):
    m = z.shape[0]
    tr = min(tr, m)
    c128, s128, c64, s64 = tabs
    if c128.shape[0] == 1:
        tab_spec = pl.BlockSpec((1, LANES), lambda i: (0, 0))
    else:
        nt = seq // tr
        tab_spec = pl.BlockSpec((tr, LANES), lambda i: (i % nt, 0))
    col = lambda w, off: pl.BlockSpec((tr, w), lambda i: (i, off // w))
    row = lambda w: pl.BlockSpec((tr, w), lambda i: (i, 0))
    gik_pad = jnp.zeros((1, LANES), F32).at[0, :IDX_HD].set(gik)
    return pl.pallas_call(
        _ab_feat_kernel,
        grid=(m // tr,),
        in_specs=[col(1024, AB_DQ), col(256, AB_DK), col(256, AB_DV), col(512, AB_IQ), col(128, AB_SMALL),
                  tab_spec, tab_spec, tab_spec, tab_spec, pl.BlockSpec((1, LANES), lambda i: (0, 0))],
        out_specs=[row(1024), row(256), row(256), row(256), row(512), row(IDX_HD), row(128), row(128)],
        out_shape=[jax.ShapeDtypeStruct((m, 1024), BF16),
                   jax.ShapeDtypeStruct((m, 256), F32),
                   jax.ShapeDtypeStruct((m, 256), BF16),
                   jax.ShapeDtypeStruct((m, 256), BF16),
                   jax.ShapeDtypeStruct((m, 512), BF16),
                   jax.ShapeDtypeStruct((m, IDX_HD), F32),
                   jax.ShapeDtypeStruct((m, 128), BF16),
                   jax.ShapeDtypeStruct((m, 128), F32)],
        compiler_params=_cparams(("parallel",)),
        name="ab_features",
    )(z, z, z, z, z, c128, s128, c64, s64, gik_pad)


def _c_feat_kernel(q_ref, k_ref, v_ref, c64_ref, s64_ref, qo_ref, ko_ref, kb_ref, kbs_ref, vb_ref, vbs_ref):
    rows = q_ref.shape[0]
    c64, s64 = c64_ref[...], s64_ref[...]
    lower = _lower32_mask(rows)
    for p in range(SWA_HEADS * SWA_HD // LANES):
        sl = slice(p * LANES, (p + 1) * LANES)
        qo_ref[:, sl] = (_rope64(q_ref[:, sl], c64, s64, lower) * (SWA_HD ** -0.5)).astype(BF16)
    for p in range(SWA_KV_HEADS * SWA_HD // LANES):
        sl = slice(p * LANES, (p + 1) * LANES)
        kr = _rope64(k_ref[:, sl], c64, s64, lower)
        v = v_ref[:, sl]
        ko_ref[:, sl] = kr
        kb_ref[:, sl] = kr.astype(BF16)
        kbs_ref[:, sl] = pltpu.roll(kr, 64, 1).astype(BF16)
        vb_ref[:, sl] = v.astype(BF16)
        vbs_ref[:, sl] = pltpu.roll(v, 64, 1).astype(BF16)


def c_features(z, tabs, *, seq, tr):
    m = z.shape[0]
    tr = min(tr, m)
    c64, s64 = tabs
    if c64.shape[0] == 1:
        tab_spec = pl.BlockSpec((1, LANES), lambda i: (0, 0))
    else:
        nt = seq // tr
        tab_spec = pl.BlockSpec((tr, LANES), lambda i: (i % nt, 0))
    col = lambda w, off: pl.BlockSpec((tr, w), lambda i: (i, off // w))
    row = lambda w: pl.BlockSpec((tr, w), lambda i: (i, 0))
    return pl.pallas_call(
        _c_feat_kernel,
        grid=(m // tr,),
        in_specs=[col(2048, C_Q), col(256, C_K), col(256, C_V), tab_spec, tab_spec],
        out_specs=[row(2048), row(256), row(256), row(256), row(256), row(256)],
        out_shape=[jax.ShapeDtypeStruct((m, 2048), BF16),
                   jax.ShapeDtypeStruct((m, 256), F32),
                   jax.ShapeDtypeStruct((m, 256), BF16),
                   jax.ShapeDtypeStruct((m, 256), BF16),
                   jax.ShapeDtypeStruct((m, 256), BF16),
                   jax.ShapeDtypeStruct((m, 256), BF16)],
        compiler_params=_cparams(("parallel",)),
        name="c_features",
    )(z, z, z, c64, s64)


def _dot_nt(a, b, **kw):
    return lax.dot_general(a, b, (((1,), (1,)), ((), ())), preferred_element_type=F32, **kw)


def _dot_tn(a, b, **kw):
    return lax.dot_general(a, b, (((0,), (0,)), ((), ())), preferred_element_type=F32, **kw)


_HI = lax.Precision.HIGHEST


def _log_decay(sm, wgate, bgate):
    pre = jnp.dot(sm, wgate, preferred_element_type=F32, precision=_HI) + bgate
    return (jnp.minimum(pre, 0.0) - jnp.log1p(jnp.exp(-jnp.abs(pre)))) / GLA_GATE_TAU


def _gla_gate_out(o, r, gain):
    g = o * lax.rsqrt(jnp.mean(o * o, axis=-1, keepdims=True) + EPS) * gain
    return g * (r * jax.nn.sigmoid(r))


def _gla_prompt_kernel(gv_ref, gr_ref, gq_ref, gk_ref, sm_ref, wgate_ref, bgate_ref, gain_ref,
                       y_ref, st_ref, s_ref, b_ref, *, n_seq, n_chunks):
    c = pl.program_id(1)

    @pl.when(c == 0)
    def _():
        s_ref[...] = jnp.zeros_like(s_ref)

    tc = n_chunks * GLA_CHUNK
    ri = lax.broadcasted_iota(jnp.int32, (tc, tc), 0)
    ci = lax.broadcasted_iota(jnp.int32, (tc, tc), 1)
    tril = jnp.where(jnp.logical_and(ri // GLA_CHUNK == ci // GLA_CHUNK, ri >= ci), 1.0, 0.0).astype(F32)
    causal = (lax.broadcasted_iota(jnp.int32, (GLA_CHUNK, GLA_CHUNK), 0)
              >= lax.broadcasted_iota(jnp.int32, (GLA_CHUNK, GLA_CHUNK), 1))
    gain = gain_ref[...]
    for s in range(n_seq):
        la = _log_decay(sm_ref[s], wgate_ref[...], bgate_ref[...])
        b_ref[s] = jnp.dot(tril, la, preferred_element_type=F32, precision=_HI)
    for n in range(n_chunks):
        rows = slice(n * GLA_CHUNK, (n + 1) * GLA_CHUNK)
        for s in range(n_seq):
            b = b_ref[s, rows, :]
            b_end = b_ref[s, (n + 1) * GLA_CHUNK - 1:(n + 1) * GLA_CHUNK, :]
            k = gk_ref[s, rows, :]
            q_in = gq_ref[s, rows, :] * (GLA_DK ** -0.5) * jnp.exp(b)
            k_in = k * jnp.exp(-b)
            k_end = k * jnp.exp(b_end - b)
            decay = jnp.exp(b_end)
            for h in range(GLA_HEADS):
                dk = slice(h * GLA_DK, (h + 1) * GLA_DK)
                dv = slice(h * GLA_DV, (h + 1) * GLA_DV)
                v = gv_ref[s, rows, dv].astype(BF16)
                st = s_ref[s, h]
                qh = q_in[:, dk].astype(BF16)
                att = jnp.where(causal, _dot_nt(qh, k_in[:, dk].astype(BF16)), 0.0)
                o = _dot_nt(qh, st.astype(BF16)) + jnp.dot(att.astype(BF16), v, preferred_element_type=F32)
                s_ref[s, h] = st * decay[:, dk] + _dot_tn(v, k_end[:, dk].astype(BF16))
                y_ref[s, rows, dv] = _gla_gate_out(o, gr_ref[s, rows, dv], gain)

    @pl.when(c == pl.num_programs(1) - 1)
    def _():
        st_ref[...] = s_ref[...]


def gla_prompt(z, wgate, bgate, gain, *, batch, seq, tc, n_seq):
    m = z.shape[0]
    z3 = z.reshape(batch, seq, z.shape[1])
    col = lambda w, off: pl.BlockSpec((n_seq, tc, w), lambda b, c: (b, c, off // w))
    const = lambda shape: pl.BlockSpec(shape, lambda b, c: (0,) * len(shape))
    y, st = pl.pallas_call(
        functools.partial(_gla_prompt_kernel, n_seq=n_seq, n_chunks=tc // GLA_CHUNK),
        grid=(batch // n_seq, seq // tc),
        in_specs=[col(1024, AB_GV), col(1024, AB_GR), col(512, AB_GQ), col(512, AB_GK), col(128, AB_SMALL),
                  const((LANES, GLA_HEADS * GLA_DK)), const((1, GLA_HEADS * GLA_DK)), const((1, GLA_DV))],
        out_specs=[pl.BlockSpec((n_seq, tc, GLA_HEADS * GLA_DV), lambda b, c: (b, c, 0)),
                   pl.BlockSpec((n_seq, GLA_HEADS, GLA_DV, GLA_DK), lambda b, c: (b, 0, 0, 0))],
        out_shape=[jax.ShapeDtypeStruct((batch, seq, GLA_HEADS * GLA_DV), F32),
                   jax.ShapeDtypeStruct((batch, GLA_HEADS, GLA_DV, GLA_DK), F32)],
        scratch_shapes=[pltpu.VMEM((n_seq, GLA_HEADS, GLA_DV, GLA_DK), F32),
                        pltpu.VMEM((n_seq, tc, GLA_HEADS * GLA_DK), F32)],
        compiler_params=_cparams(("parallel", "arbitrary")),
        name="gla_prompt",
    )(z3, z3, z3, z3, z3, wgate, bgate, gain)
    return y.reshape(m, GLA_HEADS * GLA_DV), st


INT_MIN = -2 ** 31


def _order_key(score):
    score = jnp.where(score == 0.0, 0.0, score)
    bits = lax.bitcast_convert_type(score, jnp.int32)
    return jnp.where(bits < 0, bits ^ jnp.int32(0x7FFFFFFF), bits)


def _lane_total(x):
    return jnp.dot(x.astype(BF16), jnp.ones((LANES, LANES), BF16), preferred_element_type=F32)


ROW_SUB = 128


def _count_blocks(key_ref, rows, n_blocks, pred):
    acc = jnp.zeros((ROW_SUB, LANES), F32)
    for c in range(n_blocks):
        acc = acc + jnp.where(pred(key_ref[rows, c * LANES:(c + 1) * LANES]), 1.0, 0.0)
    return acc


def _kth_largest_key(key_ref, t_ref, n_blocks, k):
    n_sub = key_ref.shape[0] // ROW_SUB
    t_ref[...] = jnp.full(t_ref.shape, INT_MIN, jnp.int32)

    def body(it, carry):
        step = lax.shift_left(jnp.int32(1), 31 - it)
        cands, accs = [], []
        for rb in range(n_sub):
            rows = slice(rb * ROW_SUB, (rb + 1) * ROW_SUB)
            cand = t_ref[rows, :] + step
            cands.append(cand)
            accs.append(_count_blocks(key_ref, rows, n_blocks, lambda kc, cand=cand: kc >= cand))
        total = _lane_total(jnp.concatenate(accs, axis=0))
        for rb in range(n_sub):
            rows = slice(rb * ROW_SUB, (rb + 1) * ROW_SUB)
            t_ref[rows, :] = jnp.where(total[rows, :] >= k, cands[rb], t_ref[rows, :])
        return carry

    lax.fori_loop(0, 32, body, 0)


def _select_topk(key_ref, t_ref, n_blocks, k, write_fn):
    _kth_largest_key(key_ref, t_ref, n_blocks, k)
    ri = lax.broadcasted_iota(jnp.int32, (LANES, LANES), 0)
    ci = lax.broadcasted_iota(jnp.int32, (LANES, LANES), 1)
    before = jnp.where(ri < ci, 1.0, 0.0).astype(BF16)
    for rb in range(key_ref.shape[0] // ROW_SUB):
        rows = slice(rb * ROW_SUB, (rb + 1) * ROW_SUB)
        t = t_ref[rows, :]
        need = k - _lane_total(_count_blocks(key_ref, rows, n_blocks, lambda kc: kc > t))
        run = jnp.zeros((ROW_SUB, LANES), F32)
        for c in range(n_blocks):
            kc = key_ref[rows, c * LANES:(c + 1) * LANES]
            eq = jnp.where(kc == t, 1.0, 0.0)
            rank = jnp.dot(eq.astype(BF16), before, preferred_element_type=F32) + run
            take = jnp.where(kc > t, 1.0, jnp.where(rank < need, eq, 0.0))
            write_fn(rows, c, take > 0.0)
            run = run + _lane_total(eq)


def _half_mask(rows, upper):
    lane = lax.broadcasted_iota(jnp.int32, (rows, LANES), 1)
    return (lane >= 64) if upper else (lane < 64)


DSA_STRATUM = 512
KEY_CHUNK = 512


def _dsa_select_prompt_kernel(iq_ref, iw_ref, ikb_ref, bias_ref, key_ref, t_ref, *, row0, n_keys, topk):
    rows = iq_ref.shape[0]
    n_blocks = n_keys // LANES
    lane = lax.broadcasted_iota(jnp.int32, (ROW_SUB, LANES), 1)
    sub = lax.broadcasted_iota(jnp.int32, (ROW_SUB, LANES), 0)

    iw = iw_ref[...]
    qh = []
    for h in range(IDX_HEADS):
        pair = iq_ref[:, (h // 2) * LANES:(h // 2 + 1) * LANES]
        qh.append(jnp.where(_half_mask(rows, h % 2 == 1), pair, jnp.zeros_like(pair)))
    for kc in range(n_keys // KEY_CHUNK):
        ik = ikb_ref[kc * KEY_CHUNK:(kc + 1) * KEY_CHUNK, :]
        score = jnp.zeros((rows, KEY_CHUNK), F32)
        for h in range(IDX_HEADS):
            score = score + jnp.maximum(_dot_nt(qh[h], ik), 0.0) * iw[:, h:h + 1]
        for rb in range(rows // ROW_SUB):
            for cb in range(KEY_CHUNK // LANES):
                c = kc * (KEY_CHUNK // LANES) + cb
                causal = (c * LANES + lane) <= (row0 + rb * ROW_SUB + sub)
                part = score[rb * ROW_SUB:(rb + 1) * ROW_SUB, cb * LANES:(cb + 1) * LANES]
                key_ref[rb * ROW_SUB:(rb + 1) * ROW_SUB, c * LANES:(c + 1) * LANES] = _order_key(
                    jnp.where(causal, part, NEG_INF))

    def write(rws, c, sel):
        causal = (c * LANES + lane) <= (row0 + rws.start + sub)
        bias_ref[rws, c * LANES:(c + 1) * LANES] = jnp.where(jnp.logical_and(sel, causal), 0.0, NEG_INF)

    _select_topk(key_ref, t_ref, n_blocks, topk, write)


def _dsa_attend_prompt_kernel(q_ref, bias_ref, kb_ref, vb_ref, y_ref):
    bias = bias_ref[...]
    group = DSA_HEADS // DSA_KV_HEADS
    for kv in range(DSA_KV_HEADS):
        kvs = slice(kv * DSA_HD, (kv + 1) * DSA_HD)
        q4 = jnp.concatenate([q_ref[:, (kv * group + g) * DSA_HD:(kv * group + g + 1) * DSA_HD] for g in range(group)],
                             axis=0)
        s4 = _dot_nt(q4, kb_ref[:, kvs])
        ps, dens = [], []
        for g in range(group):
            s = s4[g * Q_BLOCK:(g + 1) * Q_BLOCK, :] + bias
            p = jnp.exp(s - jnp.max(s, axis=-1, keepdims=True))
            dens.append(jnp.sum(p, axis=-1, keepdims=True))
            ps.append(p.astype(BF16))
        o4 = jnp.dot(jnp.concatenate(ps, axis=0), vb_ref[:, kvs], preferred_element_type=F32)
        for g in range(group):
            h = kv * group + g
            y_ref[:, h * DSA_HD:(h + 1) * DSA_HD] = o4[g * Q_BLOCK:(g + 1) * Q_BLOCK, :] / dens[g]


def dsa_prompt(q, iq, iw, kb, vb, ikb, *, batch, seq):
    topk = min(DSA_TOPK_MAX, seq // 4)
    as3 = lambda a: a.reshape(batch, seq, a.shape[-1])
    q, iq, iw, kb, vb, ikb = (as3(a) for a in (q, iq, iw, kb, vb, ikb))
    nsub = DSA_STRATUM // Q_BLOCK
    outs = []
    for r in range(seq // DSA_STRATUM):
        n_keys = (r + 1) * DSA_STRATUM
        strat = lambda w: pl.BlockSpec((None, DSA_STRATUM, w), lambda b: (b, r, 0))
        keys = lambda w: pl.BlockSpec((None, n_keys, w), lambda b: (b, 0, 0))
        bias = pl.pallas_call(
            functools.partial(_dsa_select_prompt_kernel, row0=r * DSA_STRATUM, n_keys=n_keys, topk=topk),
            grid=(batch,),
            in_specs=[strat(512), strat(128), keys(128)],
            out_specs=pl.BlockSpec((None, DSA_STRATUM, n_keys), lambda b: (b, 0, 0)),
            out_shape=jax.ShapeDtypeStruct((batch, DSA_STRATUM, n_keys), F32),
            scratch_shapes=[pltpu.VMEM((DSA_STRATUM, n_keys), jnp.int32), pltpu.VMEM((DSA_STRATUM, LANES), jnp.int32)],
            compiler_params=_cparams(("parallel",)),
            name="dsa_select_prompt",
        )(iq, iw, ikb)
        qrow = lambda w: pl.BlockSpec((None, Q_BLOCK, w), lambda b, i: (b, r * nsub + i, 0))
        keys2 = lambda w: pl.BlockSpec((None, n_keys, w), lambda b, i: (b, 0, 0))
        outs.append(pl.pallas_call(
            _dsa_attend_prompt_kernel,
            grid=(batch, nsub),
            in_specs=[qrow(1024), pl.BlockSpec((None, Q_BLOCK, n_keys), lambda b, i: (b, i, 0)), keys2(256), keys2(256)],
            out_specs=pl.BlockSpec((None, Q_BLOCK, DSA_HEADS * DSA_HD), lambda b, i: (b, i, 0)),
            out_shape=jax.ShapeDtypeStruct((batch, DSA_STRATUM, DSA_HEADS * DSA_HD), F32),
            compiler_params=_cparams(("parallel", "arbitrary")),
            name="dsa_attend_prompt",
        )(q, bias, kb, vb))
    return jnp.concatenate(outs, axis=1).reshape(batch * seq, DSA_HEADS * DSA_HD)


def _swa_head_plan(h):
    group = SWA_HEADS // SWA_KV_HEADS
    kv = h // group
    return h // 2, h % 2, kv // 2, (kv % 2) != (h % 2)


def _swa_prompt_kernel(sink_ref, q_ref, kp_ref, kc_ref, kps_ref, kcs_ref, vp_ref, vc_ref, vps_ref, vcs_ref, y_ref):
    i = pl.program_id(1)
    r = lax.broadcasted_iota(jnp.int32, (Q_BLOCK, 2 * Q_BLOCK), 0)
    c = lax.broadcasted_iota(jnp.int32, (Q_BLOCK, 2 * Q_BLOCK), 1)
    rel = Q_BLOCK + r - c
    ok = (rel >= 0) & (rel <= WINDOW) & ((i - 1) * Q_BLOCK + c >= 0)
    bias = jnp.where(ok, 0.0, NEG_INF)
    keys = (jnp.concatenate([kp_ref[...], kc_ref[...]], axis=0), jnp.concatenate([kps_ref[...], kcs_ref[...]], axis=0))
    vals = (jnp.concatenate([vp_ref[...], vc_ref[...]], axis=0), jnp.concatenate([vps_ref[...], vcs_ref[...]], axis=0))
    lower = _half_mask(Q_BLOCK, False)
    for p in range(SWA_HEADS // 2):
        qpair = q_ref[:, p * LANES:(p + 1) * LANES]
        outs = []
        for h in (2 * p, 2 * p + 1):
            _, half, ks, swapped = _swa_head_plan(h)
            qh = jnp.where(_half_mask(Q_BLOCK, half == 1), qpair, jnp.zeros_like(qpair))
            kk = keys[int(swapped)][:, ks * LANES:(ks + 1) * LANES]
            vv = vals[int(swapped)][:, ks * LANES:(ks + 1) * LANES]
            s = _dot_nt(qh, kk) + bias
            sink = sink_ref[h]
            mx = jnp.maximum(jnp.max(s, axis=-1, keepdims=True), sink)
            pr = jnp.exp(s - mx)
            den = jnp.sum(pr, axis=-1, keepdims=True) + jnp.exp(sink - mx)
            outs.append(jnp.dot(pr.astype(BF16), vv, preferred_element_type=F32) / den)
        y_ref[:, p * LANES:(p + 1) * LANES] = jnp.where(lower, outs[0], outs[1])


def swa_prompt(q, kb, kbs, vb, vbs, sinks, *, batch, seq):
    m = q.shape[0]
    nq = seq // Q_BLOCK
    cur = pl.BlockSpec((Q_BLOCK, 256), lambda b, i, s: (b * nq + i, 0))
    prev = pl.BlockSpec((Q_BLOCK, 256), lambda b, i, s: (b * nq + jnp.maximum(i - 1, 0), 0))
    qspec = pl.BlockSpec((Q_BLOCK, 2048), lambda b, i, s: (b * nq + i, 0))
    return pl.pallas_call(
        _swa_prompt_kernel,
        grid_spec=pltpu.PrefetchScalarGridSpec(
            num_scalar_prefetch=1,
            grid=(batch, nq),
            in_specs=[qspec, prev, cur, prev, cur, prev, cur, prev, cur],
            out_specs=qspec,
        ),
        out_shape=jax.ShapeDtypeStruct((m, SWA_HEADS * SWA_HD), F32),
        compiler_params=_cparams(("parallel", "arbitrary")),
        name="swa_prompt",
    )(sinks, q, kb, kb, kbs, kbs, vb, vb, vbs, vbs)


def _mem_prompt_kernel(x_ref, g_ref, wq_ref, kv_ref, wo_ref, o_ref):
    x = x_ref[...]
    h = _rms(x, g_ref[...]).astype(BF16)
    q = (jnp.dot(h, wq_ref[...], preferred_element_type=F32) * (MEM_HD ** -0.5)).astype(BF16)
    width = MEM_HEADS * MEM_HD
    outs = []
    for hd in range(MEM_HEADS):
        sl = slice(hd * MEM_HD, (hd + 1) * MEM_HD)
        k = kv_ref[:, sl].astype(BF16)
        v = kv_ref[:, width + hd * MEM_HD:width + (hd + 1) * MEM_HD].astype(BF16)
        s = _dot_nt(q[:, sl], k)
        p = jnp.exp(s - jnp.max(s, axis=-1, keepdims=True))
        o = jnp.dot(p.astype(BF16), v, preferred_element_type=F32) / jnp.sum(p, axis=-1, keepdims=True)
        outs.append(o.astype(BF16))
    o_ref[...] = x + jnp.dot(jnp.concatenate(outs, axis=-1), wo_ref[...], preferred_element_type=F32)


def mem_attn_prompt(x, g, wq, kv, wo, *, batch, seq, tq):
    m = x.shape[0]
    nq = seq // tq
    width = MEM_HEADS * MEM_HD
    xspec = pl.BlockSpec((tq, D_MODEL), lambda b, i: (b * nq + i, 0))
    return pl.pallas_call(
        _mem_prompt_kernel,
        grid=(batch, nq),
        in_specs=[xspec,
                  pl.BlockSpec((1, D_MODEL), lambda b, i: (0, 0)),
                  pl.BlockSpec((D_MODEL, width), lambda b, i: (0, 0)),
                  pl.BlockSpec((N_MEM, 2 * width), lambda b, i: (b, 0)),
                  pl.BlockSpec((width, D_MODEL), lambda b, i: (0, 0))],
        out_specs=xspec,
        out_shape=jax.ShapeDtypeStruct((m, D_MODEL), F32),
        compiler_params=_cparams(("parallel", "arbitrary")),
        name="mem_attn_prompt",
    )(x, g.reshape(1, D_MODEL), wq, kv, wo)


def _gla_prep_kernel(gq_ref, gk_ref, sm_ref, wgate_ref, bgate_ref, o_ref):
    w = GLA_HEADS * GLA_DK
    la = _log_decay(sm_ref[...], wgate_ref[...], bgate_ref[...])
    o_ref[:, 0:w] = jnp.exp(la)
    o_ref[:, w:2 * w] = gk_ref[...]
    o_ref[:, 2 * w:3 * w] = gq_ref[...] * (GLA_DK ** -0.5)
    o_ref[:, 3 * w:4 * w] = jnp.zeros((gq_ref.shape[0], w), F32)


def gla_sample_prep(z, wgate, bgate):
    m = z.shape[0]
    w = GLA_HEADS * GLA_DK
    col = lambda wd, off: pl.BlockSpec((m, wd), lambda i: (0, off // wd))
    return pl.pallas_call(
        _gla_prep_kernel,
        grid=(1,),
        in_specs=[col(512, AB_GQ), col(512, AB_GK), col(128, AB_SMALL),
                  pl.BlockSpec((LANES, w), lambda i: (0, 0)), pl.BlockSpec((1, w), lambda i: (0, 0))],
        out_specs=pl.BlockSpec((m, 4 * w), lambda i: (0, 0)),
        out_shape=jax.ShapeDtypeStruct((m, 4 * w), F32),
        compiler_params=_cparams(("arbitrary",)),
        name="gla_sample_prep",
    )(z, z, z, wgate, bgate)


def _gla_step_kernel(p_ref, gv_ref, gr_ref, s_ref, gain_ref, so_ref, y_ref, *, bs):
    gain = gain_ref[...]
    for s in range(bs):
        xt = p_ref[s].T
        for h in range(GLA_HEADS):
            dv = slice(h * GLA_DV, (h + 1) * GLA_DV)
            st = s_ref[s, h] * xt[:, h:h + 1] + xt[:, GLA_HEADS + h:GLA_HEADS + h + 1] * gv_ref[s:s + 1, dv]
            so_ref[s, h] = st
            o = jnp.sum(xt[:, 2 * GLA_HEADS + h:2 * GLA_HEADS + h + 1] * st, axis=0, keepdims=True)
            y_ref[s:s + 1, dv] = _gla_gate_out(o, gr_ref[s:s + 1, dv], gain)


def gla_sample(p, z, state, gain, *, bs):
    m = z.shape[0]
    col = lambda w, off: pl.BlockSpec((bs, w), lambda i: (i, off // w))
    sspec = pl.BlockSpec((bs, GLA_HEADS, GLA_DK, GLA_DV), lambda i: (i, 0, 0, 0))
    return pl.pallas_call(
        functools.partial(_gla_step_kernel, bs=bs),
        grid=(m // bs,),
        in_specs=[pl.BlockSpec((bs, 16, GLA_DK), lambda i: (i, 0, 0)), col(1024, AB_GV), col(1024, AB_GR), sspec,
                  pl.BlockSpec((1, GLA_DV), lambda i: (0, 0))],
        out_specs=[sspec, pl.BlockSpec((bs, GLA_HEADS * GLA_DV), lambda i: (i, 0))],
        out_shape=[jax.ShapeDtypeStruct(state.shape, F32), jax.ShapeDtypeStruct((m, GLA_HEADS * GLA_DV), F32)],
        compiler_params=_cparams(("parallel",)),
        name="gla_sample",
    )(p, z, z, state, gain)


def _dsa_scores_kernel(pt_ref, iq_ref, iw_ref, *refs, pg):
    page_refs, o_ref = refs[:pg], refs[pg]
    q8 = iq_ref[0]
    iw = iw_ref[0]
    for j in range(pg):
        dots = jnp.dot(q8, page_refs[j][0].astype(BF16), preferred_element_type=F32)
        o_ref[0, j:j + 1, :] = jnp.sum(jnp.maximum(dots, 0.0) * iw, axis=0, keepdims=True)


def dsa_sample_scores(page_table, iq, iw, pool_ik_t, page_base, *, pg):
    m, n_pages = page_table.shape
    pool_ik = pool_ik_t
    page_spec = lambda j: pl.BlockSpec((1, IDX_HD, PAGE_SIZE),
                                       lambda b, g, pt: (page_base + pt[b, g * pg + j], 0, 0))
    return pl.pallas_call(
        functools.partial(_dsa_scores_kernel, pg=pg),
        grid_spec=pltpu.PrefetchScalarGridSpec(
            num_scalar_prefetch=1,
            grid=(m, n_pages // pg),
            in_specs=[pl.BlockSpec((1, IDX_HEADS, IDX_HD), lambda b, g, pt: (b, 0, 0)),
                      pl.BlockSpec((1, IDX_HEADS, 1), lambda b, g, pt: (b, 0, 0))]
                     + [page_spec(j) for j in range(pg)],
            out_specs=pl.BlockSpec((1, pg, PAGE_SIZE), lambda b, g, pt: (b, g, 0)),
        ),
        out_shape=jax.ShapeDtypeStruct((m, n_pages, PAGE_SIZE), F32),
        compiler_params=_cparams(("parallel", "arbitrary")),
        name="dsa_sample_scores",
    )(page_table, iq, iw, *([pool_ik] * pg))


def _dsa_select_sample_kernel(sc_ref, iq_ref, iw_ref, ikb_ref, bias_ref, bnew_ref, key_ref, t_ref, *, n_past, topk):
    rows = sc_ref.shape[0]
    n_blocks = n_past // LANES
    for c in range(n_blocks):
        sl = slice(c * LANES, (c + 1) * LANES)
        key_ref[:, sl] = _order_key(sc_ref[:, sl])
    ik = ikb_ref[...].astype(F32)
    iw = iw_ref[...]
    s_new = jnp.zeros((rows, 1), F32)
    for h in range(IDX_HEADS):
        pair = iq_ref[:, (h // 2) * LANES:(h // 2 + 1) * LANES].astype(F32)
        qh = jnp.where(_half_mask(rows, h % 2 == 1), pair, 0.0)
        s_new = s_new + jnp.maximum(jnp.sum(qh * ik, axis=-1, keepdims=True), 0.0) * iw[:, h:h + 1]
    lane = lax.broadcasted_iota(jnp.int32, (rows, LANES), 1)
    key_ref[:, n_past:n_past + LANES] = _order_key(jnp.where(lane == 0, s_new, -jnp.inf))

    ri = lax.broadcasted_iota(jnp.int32, (LANES, DSA_KV_HEADS * LANES), 0)
    ci = lax.broadcasted_iota(jnp.int32, (LANES, DSA_KV_HEADS * LANES), 1)
    spread = jnp.where(ci // DSA_KV_HEADS == ri, 1.0, 0.0).astype(BF16)

    def write(rws, c, sel):
        if c == n_blocks:
            bnew_ref[rws, :] = jnp.where(sel, 0.0, NEG_INF)
        else:
            wide = jnp.dot(jnp.where(sel, 1.0, 0.0).astype(BF16), spread, preferred_element_type=F32)
            w = DSA_KV_HEADS * LANES
            bias_ref[rws, c * w:(c + 1) * w] = jnp.where(wide > 0.5, 0.0, NEG_INF)

    _select_topk(key_ref, t_ref, n_blocks + 1, topk, write)


def dsa_sample_select(scores, iq, iw, ikb, *, topk):
    m, n_past = scores.shape
    full = lambda a: pl.BlockSpec(a.shape, lambda i: (0,) * a.ndim)
    wide = DSA_KV_HEADS * n_past
    return pl.pallas_call(
        functools.partial(_dsa_select_sample_kernel, n_past=n_past, topk=topk),
        grid=(1,),
        in_specs=[full(scores), full(iq), full(iw), full(ikb)],
        out_specs=[pl.BlockSpec((m, wide), lambda i: (0, 0)), pl.BlockSpec((m, LANES), lambda i: (0, 0))],
        out_shape=[jax.ShapeDtypeStruct((m, wide), F32), jax.ShapeDtypeStruct((m, LANES), F32)],
        scratch_shapes=[pltpu.VMEM((m, n_past + LANES), jnp.int32), pltpu.VMEM((m, LANES), jnp.int32)],
        compiler_params=_cparams(("arbitrary",)),
        name="dsa_sample_select",
    )(scores, iq, iw, ikb)


def _dsa_step_kernel(pt_ref, q_ref, bias_ref, bnew_ref, kn_ref, vn_ref, *refs, pg):
    k_refs, v_refs = refs[:pg], refs[pg:2 * pg]
    y_ref, m_ref, l_ref, acc_ref = refs[2 * pg:]
    g = pl.program_id(1)

    @pl.when(g == 0)
    def _():
        m_ref[...] = jnp.full_like(m_ref, NEG_INF)
        l_ref[...] = jnp.zeros_like(l_ref)
        acc_ref[...] = jnp.zeros_like(acc_ref)

    q8 = q_ref[0]
    group = DSA_HEADS // DSA_KV_HEADS
    first = lax.broadcasted_iota(jnp.int32, (DSA_HEADS, DSA_HD), 0) < group
    wide = DSA_KV_HEADS * PAGE_SIZE
    hrow = lax.broadcasted_iota(jnp.int32, (DSA_HEADS, wide), 0)
    col = lax.broadcasted_iota(jnp.int32, (DSA_HEADS, wide), 1)
    own = (col % DSA_KV_HEADS) == (hrow // group)
    ss, oks = [], []
    for j in range(pg):
        ok = jnp.logical_and(own, bias_ref[0, j:j + 1, :] == 0.0)
        ss.append(jnp.where(ok, _dot_nt(q8, k_refs[j][...].astype(BF16)), NEG_INF))
        oks.append(ok)
    mx = ss[0]
    for s in ss[1:]:
        mx = jnp.maximum(mx, s)
    m_old = m_ref[...]
    m_new = jnp.maximum(m_old, jnp.max(mx, axis=-1, keepdims=True))
    psum = jnp.zeros((DSA_HEADS, wide), F32)
    pv = jnp.zeros((DSA_HEADS, DSA_HD), F32)
    for j in range(pg):
        p = jnp.where(oks[j], jnp.exp(ss[j] - m_new[:, 0:1]), 0.0)
        psum = psum + p
        pv = pv + jnp.dot(p.astype(BF16), v_refs[j][...].astype(BF16), preferred_element_type=F32)
    alpha = jnp.exp(m_old - m_new)
    l_ref[...] = alpha * l_ref[...] + jnp.sum(psum, axis=-1, keepdims=True)
    acc_ref[...] = alpha * acc_ref[...] + pv
    m_ref[...] = m_new

    @pl.when(g == pl.num_programs(1) - 1)
    def _():
        kn = kn_ref[0].astype(BF16).astype(F32)
        vn = vn_ref[0].astype(BF16).astype(F32)
        bn = bnew_ref[0][:, 0:1]
        s_new = jnp.sum(q8.astype(F32) * jnp.where(first, kn[:, :DSA_HD], kn[:, DSA_HD:]), axis=-1, keepdims=True) + bn
        m_old = m_ref[...]
        m_new = jnp.maximum(m_old, s_new)
        p_new = jnp.where(bn == 0.0, jnp.exp(s_new - m_new), 0.0)
        alpha = jnp.exp(m_old - m_new)
        acc = alpha * acc_ref[...] + p_new * jnp.where(first, vn[:, :DSA_HD], vn[:, DSA_HD:])
        y_ref[0] = acc / (alpha * l_ref[...] + p_new)


def dsa_sample_attend(page_table, q, bias, bnew, kn, vn, pool_k, pool_v, page_base, *, pg):
    m, n_pages = page_table.shape
    wide = DSA_KV_HEADS * PAGE_SIZE
    page_spec = lambda j: pl.BlockSpec((wide, DSA_HD), lambda b, g, pt: (page_base + pt[b, g * pg + j], 0))
    per_b = lambda shape: pl.BlockSpec((1,) + shape, lambda b, g, pt: (b, 0, 0))
    return pl.pallas_call(
        functools.partial(_dsa_step_kernel, pg=pg),
        grid_spec=pltpu.PrefetchScalarGridSpec(
            num_scalar_prefetch=1,
            grid=(m, n_pages // pg),
            in_specs=[per_b((DSA_HEADS, DSA_HD)),
                      pl.BlockSpec((1, pg, wide), lambda b, g, pt: (b, g, 0)),
                      per_b((1, LANES)), per_b((1, DSA_KV_HEADS * DSA_HD)), per_b((1, DSA_KV_HEADS * DSA_HD))]
                     + [page_spec(j) for j in range(pg)] * 2,
            out_specs=per_b((DSA_HEADS, DSA_HD)),
            scratch_shapes=[pltpu.VMEM((DSA_HEADS, DSA_HD), F32)] * 3,
        ),
        out_shape=jax.ShapeDtypeStruct((m, DSA_HEADS, DSA_HD), F32),
        compiler_params=_cparams(("parallel", "arbitrary")),
        name="dsa_sample_attend",
    )(page_table, q, bias, bnew, kn, vn, *([pool_k] * pg), *([pool_v] * pg))


def _swa_step_kernel(q_ref, kt_ref, vt_ref, kn_ref, vn_ref, sink_ref, y_ref, *, bs):
    group = SWA_HEADS // SWA_KV_HEADS
    pairs = [(s, kv) for s in range(bs) for kv in range(SWA_KV_HEADS)]
    qs, scs, news = [], [], []
    for s, kv in pairs:
        qv = q_ref[s, kv * group:(kv + 1) * group, :]
        kn = kn_ref[s, kv:kv + 1, :].astype(BF16).astype(F32)
        scs.append(jnp.dot(qv, kt_ref[s, kv].astype(BF16), preferred_element_type=F32))
        news.append(jnp.sum(qv.astype(F32) * kn, axis=-1, keepdims=True))
    sc = jnp.concatenate(scs, axis=0)
    s_new = jnp.concatenate(news, axis=0)
    sink = jnp.concatenate([sink_ref[...]] * bs, axis=0)
    mx = jnp.maximum(jnp.maximum(jnp.max(sc, axis=-1, keepdims=True), s_new), sink)
    p = jnp.exp(sc - mx)
    p_new = jnp.exp(s_new - mx)
    inv = 1.0 / (jnp.sum(p, axis=-1, keepdims=True) + p_new + jnp.exp(sink - mx))
    pb = p.astype(BF16)
    for n, (s, kv) in enumerate(pairs):
        rows = slice(n * group, (n + 1) * group)
        vn = vn_ref[s, kv:kv + 1, :].astype(BF16).astype(F32)
        o = _dot_nt(pb[rows, :], vt_ref[s, kv].astype(BF16)) + p_new[rows, :] * vn
        y_ref[s, kv * group:(kv + 1) * group, :] = o * inv[rows, :]


def swa_sample(q, kt, vt, kn, vn, sinks, *, bs):
    m = q.shape[0]
    cache = pl.BlockSpec((bs, SWA_KV_HEADS, SWA_HD, WINDOW), lambda i: (i, 0, 0, 0))
    new = pl.BlockSpec((bs, SWA_KV_HEADS, SWA_HD), lambda i: (i, 0, 0))
    return pl.pallas_call(
        functools.partial(_swa_step_kernel, bs=bs),
        grid=(m // bs,),
        in_specs=[pl.BlockSpec((bs, SWA_HEADS, SWA_HD), lambda i: (i, 0, 0)), cache, cache, new, new,
                  pl.BlockSpec((SWA_HEADS, 1), lambda i: (0, 0))],
        out_specs=pl.BlockSpec((bs, SWA_HEADS, SWA_HD), lambda i: (i, 0, 0)),
        out_shape=jax.ShapeDtypeStruct((m, SWA_HEADS, SWA_HD), F32),
        compiler_params=_cparams(("parallel",)),
        name="swa_sample",
    )(q, kt, vt, kn, vn, sinks.reshape(SWA_HEADS, 1))


def _mem_step_kernel(q_ref, k_ref, v_ref, y_ref, *, bs):
    rows = q_ref.shape[1]
    n = N_MEM * MEM_HEADS
    hrow = lax.broadcasted_iota(jnp.int32, (rows, n), 0)
    col = lax.broadcasted_iota(jnp.int32, (rows, n), 1)
    own = (col % MEM_HEADS) == (hrow % MEM_HEADS)
    for s in range(bs):
        q = (q_ref[s] * (MEM_HD ** -0.5)).astype(BF16)
        sc = jnp.where(own, _dot_nt(q, k_ref[s * n:(s + 1) * n, :].astype(BF16)), NEG_INF)
        p = jnp.where(own, jnp.exp(sc - jnp.max(sc, axis=-1, keepdims=True)), 0.0)
        o = jnp.dot(p.astype(BF16), v_ref[s * n:(s + 1) * n, :].astype(BF16), preferred_element_type=F32)
        y_ref[s] = o / jnp.sum(p, axis=-1, keepdims=True)


def mem_attn_sample(q, mk, mv, layer, *, bs):
    m, rows, _ = q.shape
    n = N_MEM * MEM_HEADS
    nb = m // bs
    cache = pl.BlockSpec((bs * n, MEM_HD), lambda i: (layer * nb + i, 0))
    return pl.pallas_call(
        functools.partial(_mem_step_kernel, bs=bs),
        grid=(nb,),
        in_specs=[pl.BlockSpec((bs, rows, MEM_HD), lambda i: (i, 0, 0)), cache, cache],
        out_specs=pl.BlockSpec((bs, rows, MEM_HD), lambda i: (i, 0, 0)),
        out_shape=jax.ShapeDtypeStruct((m, rows, MEM_HD), F32),
        compiler_params=_cparams(("parallel",)),
        name="mem_attn_sample",
    )(q, mk, mv)


TM_FFN, TF_FFN, SUB_FFN = 1024, 512, 512
TM_PROJ, TN_AB, TN_C, TN_MEM = 1024, 768, 512, 512
TM_OUT = 512
TR_FEAT = 512
TC_GLA, NSEQ_GLA = 256, 2
TQ_MEM = 512
BS_SAMPLE = 8
PG_SCORES = 32
PG_ATTEND = 16


def _prep_w_in_ab(w):
    sizes = (GLA_HEADS * GLA_DK, GLA_HEADS * GLA_DK, GLA_HEADS * GLA_DV, GLA_HEADS * GLA_DV, GLA_GATE_RANK,
             DSA_HEADS * DSA_HD, DSA_KV_HEADS * DSA_HD, DSA_KV_HEADS * DSA_HD, IDX_HEADS * IDX_HD, IDX_HEADS, IDX_HD)
    offs = np.cumsum((0,) + sizes)
    gq, gk, gv, gr, gd, dq, dk, dv, iq, iw, ik = [w[:, int(offs[j]):int(offs[j + 1])] for j in range(len(sizes))]
    pad = lambda n: jnp.zeros((w.shape[0], n), w.dtype)
    small = jnp.concatenate([ik, gd, iw, pad(LANES - IDX_HD - GLA_GATE_RANK - IDX_HEADS)], axis=1)
    out = jnp.concatenate([gv, gr, dq, gq, gk, iq, dk, dv, small, pad(AB_WIDTH - AB_SMALL - LANES)], axis=1)
    return out.astype(BF16)


def _prep_gate(w_up):
    return jnp.zeros((LANES, GLA_HEADS * GLA_DK), F32).at[SM_GD:SM_GD + GLA_GATE_RANK].set(w_up)


def kernel(x_prompt, x_sample, mem_prompt, cache_dsa_k, cache_dsa_v, cache_dsa_idx_k, state_gla, cache_swa_k, cache_swa_v, cache_mem_k, cache_mem_v, page_table, norm_ffn, w_ffn_gate, w_ffn_up, w_ffn_down, norm_mix, w_in_ab, w_gla_gate_up, b_gla_gate, gla_out_norm, idx_k_norm, w_out_ab, w_in_c, swa_sinks, w_out_c, norm_mem_q, norm_mem_src, w_mem_q, w_mem_kv, w_mem_o, final_norm):
    depth = norm_mix.shape[0]
    bp, seq, _ = x_prompt.shape
    bs = x_sample.shape[0]
    n_pool = cache_dsa_k.shape[1]
    gla_w = GLA_HEADS * GLA_DV

    wg, wu, wd = w_ffn_gate.astype(BF16), w_ffn_up.astype(BF16), w_ffn_down.astype(BF16)
    w_ab = [_prep_w_in_ab(w_in_ab[i]) for i in range(w_in_ab.shape[0])]
    w_gate = [_prep_gate(w_gla_gate_up[i]) for i in range(w_in_ab.shape[0])]
    w_oab = w_out_ab.astype(BF16)
    w_c, w_oc = w_in_c.astype(BF16), w_out_c.astype(BF16)
    w_mq, w_mkv, w_mo = w_mem_q.astype(BF16), w_mem_kv.astype(BF16), w_mem_o.astype(BF16)

    def rope_tabs(pos):
        return _rope_tables(pos, DSA_HD) + _rope_tables(pos, IDX_HD)

    def ffn_pair(x, layer, half, last):
        return ffn(x, norm_ffn[layer, half], wg, wu, wd, (layer, half),
                   final_norm if last else None, tm=TM_FFN, tf=TF_FFN, sub=SUB_FFN)

    tabs_p = rope_tabs(jnp.arange(seq))
    x = x_prompt.reshape(bp * seq, D_MODEL)
    mem = mem_prompt.reshape(bp * N_MEM, D_MODEL)
    mem_kv = [norm_proj(mem, norm_mem_src[l], w_mkv[l], tm=TM_PROJ, tn=TN_MEM) for l in range(depth)]
    st_ab_p, st_c_p = [], []
    for l in range(depth):
        i = l // 2
        x = ffn_pair(x, l, 0, False)
        if l % 2 == 0:
            z = norm_proj(x, norm_mix[l], w_ab[i], tm=TM_PROJ, tn=TN_AB)
            q, k, kb, vb, iq, ik, ikb, iw = ab_features(z, tabs_p, idx_k_norm[i], seq=seq, tr=TR_FEAT)
            y_gla, st_t = gla_prompt(z, w_gate[i], b_gla_gate[i].reshape(1, -1), gla_out_norm[i].reshape(1, -1),
                                     batch=bp, seq=seq, tc=TC_GLA, n_seq=NSEQ_GLA)
            y_dsa = dsa_prompt(q, iq, iw, kb, vb, ikb, batch=bp, seq=seq)
            x = out_proj(x, [y_gla, y_dsa], [w_oab[i, :gla_w], w_oab[i, gla_w:]], tm=TM_OUT)
            n_pg = seq // PAGE_SIZE
            st_ab_p.append((k.reshape(bp, n_pg, PAGE_SIZE, DSA_KV_HEADS, DSA_HD),
                            z[:, AB_DV:AB_DV + 256].reshape(bp, n_pg, PAGE_SIZE, DSA_KV_HEADS, DSA_HD),
                            ik.reshape(bp, n_pg, PAGE_SIZE, IDX_HD),
                            jnp.swapaxes(st_t, 2, 3)))
        else:
            z = norm_proj(x, norm_mix[l], w_c[i], tm=TM_PROJ, tn=TN_C)
            q, k, kb, kbs, vb, vbs = c_features(z, tabs_p[2:], seq=seq, tr=TR_FEAT)
            y = swa_prompt(q, kb, kbs, vb, vbs, swa_sinks[i], batch=bp, seq=seq)
            x = out_proj(x, [y], [w_oc[i]], tm=TM_OUT)
            st_c_p.append((k.reshape(bp, seq, SWA_KV_HEADS, SWA_HD)[:, -WINDOW:],
                           z[:, C_V:C_V + 256].reshape(bp, seq, SWA_KV_HEADS, SWA_HD)[:, -WINDOW:]))
        x = mem_attn_prompt(x, norm_mem_q[l], w_mq[l], mem_kv[l], w_mo[l], batch=bp, seq=seq, tq=TQ_MEM)
        x = ffn_pair(x, l, 1, l == depth - 1)
    y_prompt = x.reshape(bp, seq, D_MODEL)

    tabs_s = rope_tabs(PAST_LEN + jnp.arange(1))
    n_pages = page_table.shape[1]
    topk = min(DSA_TOPK_MAX, (PAST_LEN + 1) // 4)
    pool_k = cache_dsa_k.reshape(-1, DSA_HD)
    pool_v = cache_dsa_v.reshape(-1, DSA_HD)
    pool_ik_t = jnp.swapaxes(cache_dsa_idx_k, 2, 3).reshape(-1, IDX_HD, PAGE_SIZE)
    mem_k_rows = cache_mem_k.reshape(-1, MEM_HD)
    mem_v_rows = cache_mem_v.reshape(-1, MEM_HD)
    x = x_sample.reshape(bs, D_MODEL)
    st_ab_s, st_c_s = [], []
    for l in range(depth):
        i = l // 2
        x = ffn_pair(x, l, 0, False)
        if l % 2 == 0:
            z = norm_proj(x, norm_mix[l], w_ab[i], tm=TM_PROJ, tn=TN_AB)
            q, k, kb, vb, iq, ik, ikb, iw = ab_features(z, tabs_s, idx_k_norm[i], seq=1, tr=TR_FEAT)
            p = gla_sample_prep(z, w_gate[i], b_gla_gate[i].reshape(1, -1)).reshape(bs, 16, GLA_DK)
            st_new, y_gla = gla_sample(p, z, state_gla[i], gla_out_norm[i].reshape(1, -1), bs=BS_SAMPLE)
            scores = dsa_sample_scores(page_table, iq.reshape(bs, IDX_HEADS, IDX_HD),
                                       iw[:, :IDX_HEADS].reshape(bs, IDX_HEADS, 1), pool_ik_t, i * n_pool, pg=PG_SCORES)
            bias, bnew = dsa_sample_select(scores.reshape(bs, n_pages * PAGE_SIZE), iq, iw, ikb, topk=topk)
            v_new = z[:, AB_DV:AB_DV + 256]
            y_dsa = dsa_sample_attend(
                page_table, q.reshape(bs, DSA_HEADS, DSA_HD), bias.reshape(bs, n_pages, DSA_KV_HEADS * PAGE_SIZE),
                bnew.reshape(bs, 1, LANES), k.reshape(bs, 1, 256), v_new.reshape(bs, 1, 256),
                pool_k, pool_v, i * n_pool, pg=PG_ATTEND).reshape(bs, DSA_HEADS * DSA_HD)
            x = out_proj(x, [y_gla, y_dsa], [w_oab[i, :gla_w], w_oab[i, gla_w:]], tm=TM_OUT)
            st_ab_s.append((k.reshape(bs, 1, DSA_KV_HEADS, DSA_HD), v_new.reshape(bs, 1, DSA_KV_HEADS, DSA_HD),
                            ik.reshape(bs, 1, IDX_HD), st_new))
        else:
            z = norm_proj(x, norm_mix[l], w_c[i], tm=TM_PROJ, tn=TN_C)
            q, k, kb, kbs, vb, vbs = c_features(z, tabs_s[2:], seq=1, tr=TR_FEAT)
            v_new = z[:, C_V:C_V + 256]
            y = swa_sample(q.reshape(bs, SWA_HEADS, SWA_HD), jnp.transpose(cache_swa_k[i], (0, 2, 3, 1)),
                           jnp.transpose(cache_swa_v[i], (0, 2, 3, 1)), k.reshape(bs, SWA_KV_HEADS, SWA_HD),
                           v_new.reshape(bs, SWA_KV_HEADS, SWA_HD), swa_sinks[i], bs=BS_SAMPLE)
            x = out_proj(x, [y.reshape(bs, SWA_HEADS * SWA_HD)], [w_oc[i]], tm=TM_OUT)
            st_c_s.append((jnp.concatenate([cache_swa_k[i][:, 1:], k.reshape(bs, 1, SWA_KV_HEADS, SWA_HD)], axis=1),
                           jnp.concatenate([cache_swa_v[i][:, 1:], v_new.reshape(bs, 1, SWA_KV_HEADS, SWA_HD)], axis=1)))
        qm = norm_proj(x, norm_mem_q[l], w_mq[l], tm=TM_PROJ, tn=TN_MEM).reshape(bs, MEM_HEADS, MEM_HD)
        qm = jnp.pad(qm, ((0, 0), (0, SUBLANES - MEM_HEADS), (0, 0)))
        om = mem_attn_sample(qm, mem_k_rows, mem_v_rows, l, bs=BS_SAMPLE)
        x = out_proj(x, [om[:, :MEM_HEADS].reshape(bs, MEM_HEADS * MEM_HD)], [w_mo[l]], tm=TM_OUT)
        x = ffn_pair(x, l, 1, l == depth - 1)
    y_sample = x.reshape(bs, 1, D_MODEL)

    stk = lambda sts, j: jnp.stack([s[j] for s in sts])
    mw = MEM_HEADS * MEM_HD
    mem_k_p = jnp.stack([kv[:, :mw].reshape(bp, N_MEM, MEM_HEADS, MEM_HD) for kv in mem_kv])
    mem_v_p = jnp.stack([kv[:, mw:].reshape(bp, N_MEM, MEM_HEADS, MEM_HD) for kv in mem_kv])
    return (y_prompt, y_sample, stk(st_ab_p, 0), stk(st_ab_p, 1), stk(st_ab_p, 2),
            stk(st_ab_s, 0), stk(st_ab_s, 1), stk(st_ab_s, 2), stk(st_ab_p, 3), stk(st_ab_s, 3),
            stk(st_c_p, 0), stk(st_c_p, 1), stk(st_c_s, 0), stk(st_c_s, 1), mem_k_p, mem_v_p)
```

```python
import functools

import jax
import jax.numpy as jnp
import numpy as np
from jax import lax
from jax.experimental import pallas as pl
from jax.experimental.pallas import tpu as pltpu

F32 = jnp.float32
BF16 = jnp.bfloat16

D_MODEL = 2048
D_FF = 5632
EPS = 1e-6
ROPE_THETA = 10000.0
NEG_INF = -1e30
PAST_LEN = 8192
PAGE_SIZE = 128
Q_BLOCK = 128
GLA_HEADS, GLA_DK, GLA_DV = 4, 128, 256
GLA_GATE_RANK = 16
GLA_GATE_TAU = 16.0
GLA_CHUNK = 64
DSA_HEADS, DSA_KV_HEADS, DSA_HD = 8, 2, 128
IDX_HEADS, IDX_HD = 8, 64
DSA_TOPK_MAX = 256
SWA_HEADS, SWA_KV_HEADS, SWA_HD = 32, 4, 64
WINDOW = 128
MEM_HEADS, MEM_HD = 4, 128
N_MEM = 256

LANES = 128
SUBLANES = 8
VMEM_LIMIT_BYTES = 60000 * 1024
ELEM_SUB = 128

AB_GV, AB_GR, AB_DQ = 0, 1024, 2048
AB_GQ, AB_GK, AB_IQ = 3072, 3584, 4096
AB_DK, AB_DV = 4608, 4864
AB_SMALL = 5120
AB_WIDTH = 5376
SM_IK, SM_GD, SM_IW = 0, 64, 80
C_Q, C_K, C_V, C_WIDTH = 0, 2048, 2304, 2560


def _cparams(sem):
    return pltpu.CompilerParams(dimension_semantics=sem, vmem_limit_bytes=VMEM_LIMIT_BYTES)


def _rms(x, g):
    y = x * lax.rsqrt(jnp.mean(x * x, axis=-1, keepdims=True) + EPS)
    return y * g


def _ffn_kernel(x_ref, g_ref, wg_ref, wu_ref, wd_ref, fg_ref, o_ref, *rest, final_norm, sub, emit_cast):
    h_ref = rest[-1]
    j = pl.program_id(1)
    tm = x_ref.shape[0]
    if emit_cast:
        for src, dst in zip((wg_ref, wu_ref, wd_ref), rest[:3]):
            dst[...] = src[...].astype(BF16)
        wg_ref, wu_ref, wd_ref = rest[:3]

    esub = min(ELEM_SUB, tm)

    def row_group(r):
        return pl.ds(pl.multiple_of(r * esub, esub), esub)

    @pl.when(j == 0)
    def _():
        def body(r, carry):
            rows = row_group(r)
            h_ref[rows, :] = _rms(x_ref[rows, :], g_ref[...]).astype(BF16)
            o_ref[rows, :] = jnp.zeros((esub, D_MODEL), F32)
            return carry

        lax.fori_loop(0, tm // esub, body, 0)

    for r in range(tm // sub):
        rows = slice(r * sub, (r + 1) * sub)
        h = h_ref[rows, :]
        a = jnp.dot(h, wg_ref[...], preferred_element_type=F32)
        u = jnp.dot(h, wu_ref[...], preferred_element_type=F32)
        act = (a * jax.nn.sigmoid(a) * u).astype(BF16)
        o_ref[rows, :] += jnp.dot(act, wd_ref[...], preferred_element_type=F32)

    @pl.when(j == pl.num_programs(1) - 1)
    def _():
        def body(r, carry):
            rows = row_group(r)
            y = x_ref[rows, :] + 0.5 * o_ref[rows, :]
            if final_norm:
                y = _rms(y, fg_ref[...])
            o_ref[rows, :] = y
            return carry

        lax.fori_loop(0, tm // esub, body, 0)


def ffn(x, g, wg, wu, wd, sel, final_g=None, *, tm, tf, sub):
    m = x.shape[0]
    tm = min(tm, m)
    sub = min(sub, tm)
    fg = g if final_g is None else final_g
    emit_cast = sel is not None
    wspec = lambda shape, imap: pl.BlockSpec(shape, imap)
    if emit_cast:
        assert m == tm and wg.dtype == F32
        layer, half = sel
        w_in = [wspec((None, None, D_MODEL, tf), lambda i, j: (layer, half, 0, j)),
                wspec((None, None, D_MODEL, tf), lambda i, j: (layer, half, 0, j)),
                wspec((None, None, tf, D_MODEL), lambda i, j: (layer, half, j, 0))]
    else:
        assert wg.dtype == BF16 and wg.ndim == 2
        w_in = [wspec((D_MODEL, tf), lambda i, j: (0, j)), wspec((D_MODEL, tf), lambda i, j: (0, j)),
                wspec((tf, D_MODEL), lambda i, j: (j, 0))]
    out_specs = [pl.BlockSpec((tm, D_MODEL), lambda i, j: (i, 0))]
    out_shape = [jax.ShapeDtypeStruct((m, D_MODEL), F32)]
    if emit_cast:
        out_specs += [wspec((D_MODEL, tf), lambda i, j: (0, j)), wspec((D_MODEL, tf), lambda i, j: (0, j)),
                      wspec((tf, D_MODEL), lambda i, j: (j, 0))]
        out_shape += [jax.ShapeDtypeStruct((D_MODEL, D_FF), BF16), jax.ShapeDtypeStruct((D_MODEL, D_FF), BF16),
                      jax.ShapeDtypeStruct((D_FF, D_MODEL), BF16)]
    outs = pl.pallas_call(
        functools.partial(_ffn_kernel, final_norm=final_g is not None, sub=sub, emit_cast=emit_cast),
        grid=(m // tm, D_FF // tf),
        in_specs=[pl.BlockSpec((tm, D_MODEL), lambda i, j: (i, 0)), pl.BlockSpec((1, D_MODEL), lambda i, j: (0, 0))]
                 + w_in + [pl.BlockSpec((1, D_MODEL), lambda i, j: (0, 0))],
        out_specs=out_specs,
        out_shape=out_shape,
        scratch_shapes=[pltpu.VMEM((tm, D_MODEL), BF16)],
        compiler_params=_cparams(("parallel", "arbitrary")),
        name="ffn_cast" if emit_cast else "ffn",
    )(x, g.reshape(1, D_MODEL), wg, wu, wd, fg.reshape(1, D_MODEL))
    return tuple(outs) if emit_cast else outs[0]


def _proj_kernel(x_ref, g_ref, w_ref, o_ref, h_ref, *, sub):
    tm = x_ref.shape[0]

    @pl.when(pl.program_id(1) == 0)
    def _():
        for r in range(tm // sub):
            rows = slice(r * sub, (r + 1) * sub)
            h_ref[rows, :] = _rms(x_ref[rows, :], g_ref[...]).astype(BF16)

    o_ref[...] = jnp.dot(h_ref[...], w_ref[...], preferred_element_type=F32)


def norm_proj(x, g, w, *, tm, tn, sub=512):
    m, n = x.shape[0], w.shape[1]
    tm = min(tm, m)
    return pl.pallas_call(
        functools.partial(_proj_kernel, sub=min(sub, tm)),
        grid=(m // tm, n // tn),
        in_specs=[
            pl.BlockSpec((tm, D_MODEL), lambda i, j: (i, 0)),
            pl.BlockSpec((1, D_MODEL), lambda i, j: (0, 0)),
            pl.BlockSpec((D_MODEL, tn), lambda i, j: (0, j)),
        ],
        out_specs=pl.BlockSpec((tm, tn), lambda i, j: (i, j)),
        out_shape=jax.ShapeDtypeStruct((m, n), F32),
        scratch_shapes=[pltpu.VMEM((tm, D_MODEL), BF16)],
        compiler_params=_cparams(("parallel", "arbitrary")),
        name="norm_proj",
    )(x, g.reshape(1, D_MODEL), w)


def _outproj_kernel(*refs, n_in):
    x_ref = refs[0]
    y_refs = refs[1:1 + n_in]
    w_refs = refs[1 + n_in:1 + 2 * n_in]
    o_ref = refs[1 + 2 * n_in]
    acc = x_ref[...]
    for y_ref, w_ref in zip(y_refs, w_refs):
        acc = acc + jnp.dot(y_ref[...].astype(BF16), w_ref[...], preferred_element_type=F32)
    o_ref[...] = acc


def out_proj(x, ys, ws, *, tm):
    m = x.shape[0]
    tm = min(tm, m)
    n_in = len(ys)
    in_specs = [pl.BlockSpec((tm, D_MODEL), lambda i: (i, 0))]
    in_specs += [pl.BlockSpec((tm, y.shape[1]), lambda i: (i, 0)) for y in ys]
    in_specs += [pl.BlockSpec(w.shape, lambda i: (0, 0)) for w in ws]
    return pl.pallas_call(
        functools.partial(_outproj_kernel, n_in=n_in),
        grid=(m // tm,),
        in_specs=in_specs,
        out_specs=pl.BlockSpec((tm, D_MODEL), lambda i: (i, 0)),
        out_shape=jax.ShapeDtypeStruct((m, D_MODEL), F32),
        compiler_params=_cparams(("parallel",)),
        name="out_proj",
    )(x, *ys, *ws)


def _rope_tables(pos, hd):
    half = hd // 2
    inv = ROPE_THETA ** (-jnp.arange(half, dtype=F32) / half)
    ang = pos.astype(F32)[:, None] * inv[None, :]
    cos, sin = jnp.cos(ang), jnp.sin(ang)
    reps = LANES // hd
    return (jnp.concatenate([cos, cos] * reps, axis=-1),
            jnp.concatenate([-sin, sin] * reps, axis=-1))


def _rope128(x, cos, sin):
    return x * cos + pltpu.roll(x, 64, 1) * sin


def _rope64(x, cos, sin, lower):
    partner = jnp.where(lower, pltpu.roll(x, 96, 1), pltpu.roll(x, 32, 1))
    return x * cos + partner * sin


def _lower32_mask(rows):
    lane = lax.broadcasted_iota(jnp.int32, (rows, LANES), 1)
    return (lane % 64) < 32


def _ab_feat_kernel(dq_ref, dk_ref, dv_ref, iq_ref, sm_ref, c128_ref, s128_ref, c64_ref, s64_ref, gik_ref,
                    q_ref, k_ref, kb_ref, vb_ref, iqo_ref, ik_ref, ikb_ref, iw_ref):
    rows = dq_ref.shape[0]
    c128, s128 = c128_ref[...], s128_ref[...]
    c64, s64 = c64_ref[...], s64_ref[...]
    lower = _lower32_mask(rows)
    for h in range(DSA_HEADS):
        sl = slice(h * LANES, (h + 1) * LANES)
        q_ref[:, sl] = (_rope128(dq_ref[:, sl], c128, s128) * (DSA_HD ** -0.5)).astype(BF16)
    for h in range(DSA_KV_HEADS):
        sl = slice(h * LANES, (h + 1) * LANES)
        kr = _rope128(dk_ref[:, sl], c128, s128)
        k_ref[:, sl] = kr
        kb_ref[:, sl] = kr.astype(BF16)
    vb_ref[...] = dv_ref[...].astype(BF16)
    for p in range(IDX_HEADS * IDX_HD // LANES):
        sl = slice(p * LANES, (p + 1) * LANES)
        iqo_ref[:, sl] = (_rope64(iq_ref[:, sl], c64, s64, lower) * (IDX_HD ** -0.5)).astype(BF16)
    sm = sm_ref[...]
    lane = lax.broadcasted_iota(jnp.int32, (rows, LANES), 1)
    ik = jnp.where(lane < IDX_HD, sm, 0.0)
    ik = ik * lax.rsqrt(jnp.sum(ik * ik, axis=-1, keepdims=True) / IDX_HD + EPS) * gik_ref[...]
    ik = _rope64(ik, c64, s64, lower)
    ik_ref[...] = ik[:, :IDX_HD]
    ikb_ref[...] = jnp.where(lane < IDX_HD, ik, pltpu.roll(ik, 64, 1)).astype(BF16)
    iw_ref[...] = pltpu.roll(sm, LANES - SM_IW, 1) * (IDX_HEADS ** -0.5)


def ab_features(z, tabs, gik, *, seq, tr):
    m = z.shape[0]
    tr = min(tr, m)
    c128, s128, c64, s64 = tabs
    if c128.shape[0] == 1:
        tab_spec = pl.BlockSpec((1, LANES), lambda i: (0, 0))
    else:
        nt = seq // tr
        tab_spec = pl.BlockSpec((tr, LANES), lambda i: (i % nt, 0))
    col = lambda w, off: pl.BlockSpec((tr, w), lambda i: (i, off // w))
    row = lambda w: pl.BlockSpec((tr, w), lambda i: (i, 0))
    gik_pad = jnp.zeros((1, LANES), F32).at[0, :IDX_HD].set(gik)
    return pl.pallas_call(
        _ab_feat_kernel,
        grid=(m // tr,),
        in_specs=[col(1024, AB_DQ), col(256, AB_DK), col(256, AB_DV), col(512, AB_IQ), col(128, AB_SMALL),
                  tab_spec, tab_spec, tab_spec, tab_spec, pl.BlockSpec((1, LANES), lambda i: (0, 0))],
        out_specs=[row(1024), row(256), row(256), row(256), row(512), row(IDX_HD), row(128), row(128)],
        out_shape=[jax.ShapeDtypeStruct((m, 1024), BF16),
                   jax.ShapeDtypeStruct((m, 256), F32),
                   jax.ShapeDtypeStruct((m, 256), BF16),
                   jax.ShapeDtypeStruct((m, 256), BF16),
                   jax.ShapeDtypeStruct((m, 512), BF16),
                   jax.ShapeDtypeStruct((m, IDX_HD), F32),
                   jax.ShapeDtypeStruct((m, 128), BF16),
                   jax.ShapeDtypeStruct((m, 128), F32)],
        compiler_params=_cparams(("parallel",)),
        name="ab_features",
    )(z, z, z, z, z, c128, s128, c64, s64, gik_pad)


def _c_feat_kernel(q_ref, k_ref, v_ref, c64_ref, s64_ref, qo_ref, ko_ref, kb_ref, kbs_ref, vb_ref, vbs_ref):
    rows = q_ref.shape[0]
    c64, s64 = c64_ref[...], s64_ref[...]
    lower = _lower32_mask(rows)
    for p in range(SWA_HEADS * SWA_HD // LANES):
        sl = slice(p * LANES, (p + 1) * LANES)
        qo_ref[:, sl] = (_rope64(q_ref[:, sl], c64, s64, lower) * (SWA_HD ** -0.5)).astype(BF16)
    for p in range(SWA_KV_HEADS * SWA_HD // LANES):
        sl = slice(p * LANES, (p + 1) * LANES)
        kr = _rope64(k_ref[:, sl], c64, s64, lower)
        v = v_ref[:, sl]
        ko_ref[:, sl] = kr
        kb_ref[:, sl] = kr.astype(BF16)
        kbs_ref[:, sl] = pltpu.roll(kr, 64, 1).astype(BF16)
        vb_ref[:, sl] = v.astype(BF16)
        vbs_ref[:, sl] = pltpu.roll(v, 64, 1).astype(BF16)


def c_features(z, tabs, *, seq, tr):
    m = z.shape[0]
    tr = min(tr, m)
    c64, s64 = tabs
    if c64.shape[0] == 1:
        tab_spec = pl.BlockSpec((1, LANES), lambda i: (0, 0))
    else:
        nt = seq // tr
        tab_spec = pl.BlockSpec((tr, LANES), lambda i: (i % nt, 0))
    col = lambda w, off: pl.BlockSpec((tr, w), lambda i: (i, off // w))
    row = lambda w: pl.BlockSpec((tr, w), lambda i: (i, 0))
    return pl.pallas_call(
        _c_feat_kernel,
        grid=(m // tr,),
        in_specs=[col(2048, C_Q), col(256, C_K), col(256, C_V), tab_spec, tab_spec],
        out_specs=[row(2048), row(256), row(256), row(256), row(256), row(256)],
        out_shape=[jax.ShapeDtypeStruct((m, 2048), BF16),
                   jax.ShapeDtypeStruct((m, 256), F32),
                   jax.ShapeDtypeStruct((m, 256), BF16),
                   jax.ShapeDtypeStruct((m, 256), BF16),
                   jax.ShapeDtypeStruct((m, 256), BF16),
                   jax.ShapeDtypeStruct((m, 256), BF16)],
        compiler_params=_cparams(("parallel",)),
        name="c_features",
    )(z, z, z, c64, s64)


def _dot_nt(a, b, **kw):
    return lax.dot_general(a, b, (((1,), (1,)), ((), ())), preferred_element_type=F32, **kw)


def _dot_tn(a, b, **kw):
    return lax.dot_general(a, b, (((0,), (0,)), ((), ())), preferred_element_type=F32, **kw)


_HI = lax.Precision.HIGHEST


def _log_decay(sm, wgate, bgate):
    pre = jnp.dot(sm, wgate, preferred_element_type=F32, precision=_HI) + bgate
    return (jnp.minimum(pre, 0.0) - jnp.log1p(jnp.exp(-jnp.abs(pre)))) / GLA_GATE_TAU


def _gla_gate_out(o, r, gain):
    g = o * lax.rsqrt(jnp.mean(o * o, axis=-1, keepdims=True) + EPS) * gain
    return g * (r * jax.nn.sigmoid(r))


def _gla_prompt_kernel(gv_ref, gr_ref, gq_ref, gk_ref, sm_ref, wgate_ref, bgate_ref, gain_ref,
                       y_ref, st_ref, s_ref, b_ref, *, n_seq, n_chunks):
    c = pl.program_id(1)

    @pl.when(c == 0)
    def _():
        s_ref[...] = jnp.zeros_like(s_ref)

    tc = n_chunks * GLA_CHUNK
    ri = lax.broadcasted_iota(jnp.int32, (tc, tc), 0)
    ci = lax.broadcasted_iota(jnp.int32, (tc, tc), 1)
    tril = jnp.where(jnp.logical_and(ri // GLA_CHUNK == ci // GLA_CHUNK, ri >= ci), 1.0, 0.0).astype(F32)
    causal = (lax.broadcasted_iota(jnp.int32, (GLA_CHUNK, GLA_CHUNK), 0)
              >= lax.broadcasted_iota(jnp.int32, (GLA_CHUNK, GLA_CHUNK), 1))
    gain = gain_ref[...]
    for s in range(n_seq):
        la = _log_decay(sm_ref[s], wgate_ref[...], bgate_ref[...])
        b_ref[s] = jnp.dot(tril, la, preferred_element_type=F32, precision=_HI)
    for n in range(n_chunks):
        rows = slice(n * GLA_CHUNK, (n + 1) * GLA_CHUNK)
        for s in range(n_seq):
            b = b_ref[s, rows, :]
            b_end = b_ref[s, (n + 1) * GLA_CHUNK - 1:(n + 1) * GLA_CHUNK, :]
            k = gk_ref[s, rows, :]
            q_in = gq_ref[s, rows, :] * (GLA_DK ** -0.5) * jnp.exp(b)
            k_in = k * jnp.exp(-b)
            k_end = k * jnp.exp(b_end - b)
            decay = jnp.exp(b_end)
            for h in range(GLA_HEADS):
                dk = slice(h * GLA_DK, (h + 1) * GLA_DK)
                dv = slice(h * GLA_DV, (h + 1) * GLA_DV)
                v = gv_ref[s, rows, dv].astype(BF16)
                st = s_ref[s, h]
                qh = q_in[:, dk].astype(BF16)
                att = jnp.where(causal, _dot_nt(qh, k_in[:, dk].astype(BF16)), 0.0)
                o = _dot_nt(qh, st.astype(BF16)) + jnp.dot(att.astype(BF16), v, preferred_element_type=F32)
                s_ref[s, h] = st * decay[:, dk] + _dot_tn(v, k_end[:, dk].astype(BF16))
                y_ref[s, rows, dv] = _gla_gate_out(o, gr_ref[s, rows, dv], gain)

    @pl.when(c == pl.num_programs(1) - 1)
    def _():
        st_ref[...] = s_ref[...]


def gla_prompt(z, wgate, bgate, gain, *, batch, seq, tc, n_seq):
    m = z.shape[0]
    z3 = z.reshape(batch, seq, z.shape[1])
    col = lambda w, off: pl.BlockSpec((n_seq, tc, w), lambda b, c: (b, c, off // w))
    const = lambda shape: pl.BlockSpec(shape, lambda b, c: (0,) * len(shape))
    y, st = pl.pallas_call(
        functools.partial(_gla_prompt_kernel, n_seq=n_seq, n_chunks=tc // GLA_CHUNK),
        grid=(batch // n_seq, seq // tc),
        in_specs=[col(1024, AB_GV), col(1024, AB_GR), col(512, AB_GQ), col(512, AB_GK), col(128, AB_SMALL),
                  const((LANES, GLA_HEADS * GLA_DK)), const((1, GLA_HEADS * GLA_DK)), const((1, GLA_DV))],
        out_specs=[pl.BlockSpec((n_seq, tc, GLA_HEADS * GLA_DV), lambda b, c: (b, c, 0)),
                   pl.BlockSpec((n_seq, GLA_HEADS, GLA_DV, GLA_DK), lambda b, c: (b, 0, 0, 0))],
        out_shape=[jax.ShapeDtypeStruct((batch, seq, GLA_HEADS * GLA_DV), F32),
                   jax.ShapeDtypeStruct((batch, GLA_HEADS, GLA_DV, GLA_DK), F32)],
        scratch_shapes=[pltpu.VMEM((n_seq, GLA_HEADS, GLA_DV, GLA_DK), F32),
                        pltpu.VMEM((n_seq, tc, GLA_HEADS * GLA_DK), F32)],
        compiler_params=_cparams(("parallel", "arbitrary")),
        name="gla_prompt",
    )(z3, z3, z3, z3, z3, wgate, bgate, gain)
    return y.reshape(m, GLA_HEADS * GLA_DV), st


INT_MIN = -2 ** 31


def _order_key(score):
    score = jnp.where(score == 0.0, 0.0, score)
    bits = lax.bitcast_convert_type(score, jnp.int32)
    return jnp.where(bits < 0, bits ^ jnp.int32(0x7FFFFFFF), bits)


def _lane_total(x):
    return jnp.dot(x.astype(BF16), jnp.ones((LANES, LANES), BF16), preferred_element_type=F32)


ROW_SUB = 128


def _count_blocks(key_ref, rows, n_blocks, pred):
    acc = jnp.zeros((ROW_SUB, LANES), F32)
    for c in range(n_blocks):
        acc = acc + jnp.where(pred(key_ref[rows, c * LANES:(c + 1) * LANES]), 1.0, 0.0)
    return acc


I16_MIN = -2 ** 15


def _count16(h_ref, rows, n_blocks, pred):
    acc = jnp.zeros((ROW_SUB, LANES), BF16)
    for c in range(n_blocks):
        acc = acc + jnp.where(pred(h_ref[rows, c * LANES:(c + 1) * LANES]), jnp.ones((), BF16), jnp.zeros((), BF16))
    return acc


def _bisect16(h_ref, t_ref, slot, base_slot, n_blocks, k, offset):
    n_sub = h_ref.shape[0] // ROW_SUB

    def body(it, carry):
        step = lax.shift_left(jnp.int32(1), 15 - it)
        cands, accs = [], []
        for rb in range(n_sub):
            rows = slice(rb * ROW_SUB, (rb + 1) * ROW_SUB)
            cand = t_ref[slot, rows, :] + step
            c16 = (cand - offset).astype(jnp.int16)
            cands.append(cand)
            accs.append(_count16(h_ref, rows, n_blocks, lambda hc, c16=c16: hc >= c16))
        total = _lane_total(jnp.concatenate(accs, axis=0))
        for rb in range(n_sub):
            rows = slice(rb * ROW_SUB, (rb + 1) * ROW_SUB)
            base = 0.0 if base_slot is None else t_ref[base_slot, rows, :].astype(F32)
            t_ref[slot, rows, :] = jnp.where(total[rows, :] + base >= k, cands[rb], t_ref[slot, rows, :])
        return carry

    lax.fori_loop(0, 16, body, 0)


def _kth_largest_key(key_ref, t_ref, h_ref, n_blocks, k):
    n_sub = key_ref.shape[0] // ROW_SUB
    for rb in range(n_sub):
        rows = slice(rb * ROW_SUB, (rb + 1) * ROW_SUB)
        for c in range(n_blocks):
            sl = slice(c * LANES, (c + 1) * LANES)
            h_ref[rows, sl] = lax.shift_right_arithmetic(key_ref[rows, sl], 16).astype(jnp.int16)
    t_ref[0] = jnp.full(t_ref.shape[1:], I16_MIN, jnp.int32)
    _bisect16(h_ref, t_ref, 0, None, n_blocks, k, 0)
    for rb in range(n_sub):
        rows = slice(rb * ROW_SUB, (rb + 1) * ROW_SUB)
        hi = t_ref[0, rows, :].astype(jnp.int16)
        above = _count16(h_ref, rows, n_blocks, lambda hc: hc > hi)
        t_ref[2, rows, :] = _lane_total(above).astype(jnp.int32)
        for c in range(n_blocks):
            sl = slice(c * LANES, (c + 1) * LANES)
            lo = ((key_ref[rows, sl] & 0xFFFF) + I16_MIN).astype(jnp.int16)
            h_ref[rows, sl] = jnp.where(h_ref[rows, sl] == hi, lo, jnp.full((), I16_MIN, jnp.int16))
    t_ref[1] = jnp.zeros(t_ref.shape[1:], jnp.int32)
    _bisect16(h_ref, t_ref, 1, 2, n_blocks, k, -I16_MIN)
    t_ref[0] = lax.shift_left(t_ref[0], 16) | t_ref[1]


def _select_topk(key_ref, t_ref, h_ref, n_blocks, k, write_fn):
    _kth_largest_key(key_ref, t_ref, h_ref, n_blocks, k)
    ri = lax.broadcasted_iota(jnp.int32, (LANES, LANES), 0)
    ci = lax.broadcasted_iota(jnp.int32, (LANES, LANES), 1)
    before = jnp.where(ri < ci, 1.0, 0.0).astype(BF16)
    for rb in range(key_ref.shape[0] // ROW_SUB):
        rows = slice(rb * ROW_SUB, (rb + 1) * ROW_SUB)
        t = t_ref[0, rows, :]
        need = k - _lane_total(_count_blocks(key_ref, rows, n_blocks, lambda kc: kc > t))
        run = jnp.zeros((ROW_SUB, LANES), F32)
        for c in range(n_blocks):
            kc = key_ref[rows, c * LANES:(c + 1) * LANES]
            eq = jnp.where(kc == t, 1.0, 0.0)
            rank = jnp.dot(eq.astype(BF16), before, preferred_element_type=F32) + run
            take = jnp.where(kc > t, 1.0, jnp.where(rank < need, eq, 0.0))
            write_fn(rows, c, take > 0.0)
            run = run + _lane_total(eq)


def _half_mask(rows, upper):
    lane = lax.broadcasted_iota(jnp.int32, (rows, LANES), 1)
    return (lane >= 64) if upper else (lane < 64)


DSA_STRATUM = 512
KEY_CHUNK = 512


def _dsa_select_prompt_kernel(iq_ref, iw_ref, ikb_ref, bias_ref, key_ref, t_ref, h_ref, *, row0, n_keys, topk):
    rows = iq_ref.shape[0]
    n_blocks = n_keys // LANES
    lane = lax.broadcasted_iota(jnp.int32, (ROW_SUB, LANES), 1)
    sub = lax.broadcasted_iota(jnp.int32, (ROW_SUB, LANES), 0)

    iw = iw_ref[...]
    qh = []
    for h in range(IDX_HEADS):
        pair = iq_ref[:, (h // 2) * LANES:(h // 2 + 1) * LANES]
        qh.append(jnp.where(_half_mask(rows, h % 2 == 1), pair, jnp.zeros_like(pair)))
    for kc in range(n_keys // KEY_CHUNK):
        ik = ikb_ref[kc * KEY_CHUNK:(kc + 1) * KEY_CHUNK, :]
        score = jnp.zeros((rows, KEY_CHUNK), F32)
        for h in range(IDX_HEADS):
            score = score + jnp.maximum(_dot_nt(qh[h], ik), 0.0) * iw[:, h:h + 1]
        for rb in range(rows // ROW_SUB):
            for cb in range(KEY_CHUNK // LANES):
                c = kc * (KEY_CHUNK // LANES) + cb
                causal = (c * LANES + lane) <= (row0 + rb * ROW_SUB + sub)
                part = score[rb * ROW_SUB:(rb + 1) * ROW_SUB, cb * LANES:(cb + 1) * LANES]
                key_ref[rb * ROW_SUB:(rb + 1) * ROW_SUB, c * LANES:(c + 1) * LANES] = _order_key(
                    jnp.where(causal, part, NEG_INF))

    def write(rws, c, sel):
        causal = (c * LANES + lane) <= (row0 + rws.start + sub)
        bias_ref[rws, c * LANES:(c + 1) * LANES] = jnp.where(jnp.logical_and(sel, causal), 0.0, NEG_INF)

    _select_topk(key_ref, t_ref, h_ref, n_blocks, topk, write)


def _dsa_attend_prompt_kernel(q_ref, bias_ref, kb_ref, vb_ref, y_ref):
    bias = bias_ref[...]
    group = DSA_HEADS // DSA_KV_HEADS
    for kv in range(DSA_KV_HEADS):
        kvs = slice(kv * DSA_HD, (kv + 1) * DSA_HD)
        q4 = jnp.concatenate([q_ref[:, (kv * group + g) * DSA_HD:(kv * group + g + 1) * DSA_HD] for g in range(group)],
                             axis=0)
        s4 = _dot_nt(q4, kb_ref[:, kvs])
        ps, dens = [], []
        for g in range(group):
            s = s4[g * Q_BLOCK:(g + 1) * Q_BLOCK, :] + bias
            p = jnp.exp(s - jnp.max(s, axis=-1, keepdims=True))
            dens.append(jnp.sum(p, axis=-1, keepdims=True))
            ps.append(p.astype(BF16))
        o4 = jnp.dot(jnp.concatenate(ps, axis=0), vb_ref[:, kvs], preferred_element_type=F32)
        for g in range(group):
            h = kv * group + g
            y_ref[:, h * DSA_HD:(h + 1) * DSA_HD] = o4[g * Q_BLOCK:(g + 1) * Q_BLOCK, :] / dens[g]


def dsa_prompt(q, iq, iw, kb, vb, ikb, *, batch, seq):
    topk = min(DSA_TOPK_MAX, seq // 4)
    as3 = lambda a: a.reshape(batch, seq, a.shape[-1])
    q, iq, iw, kb, vb, ikb = (as3(a) for a in (q, iq, iw, kb, vb, ikb))
    nsub = DSA_STRATUM // Q_BLOCK
    outs = []
    for r in range(seq // DSA_STRATUM):
        n_keys = (r + 1) * DSA_STRATUM
        strat = lambda w: pl.BlockSpec((None, DSA_STRATUM, w), lambda b: (b, r, 0))
        keys = lambda w: pl.BlockSpec((None, n_keys, w), lambda b: (b, 0, 0))
        bias = pl.pallas_call(
            functools.partial(_dsa_select_prompt_kernel, row0=r * DSA_STRATUM, n_keys=n_keys, topk=topk),
            grid=(batch,),
            in_specs=[strat(512), strat(128), keys(128)],
            out_specs=pl.BlockSpec((None, DSA_STRATUM, n_keys), lambda b: (b, 0, 0)),
            out_shape=jax.ShapeDtypeStruct((batch, DSA_STRATUM, n_keys), F32),
            scratch_shapes=[pltpu.VMEM((DSA_STRATUM, n_keys), jnp.int32), pltpu.VMEM((3, DSA_STRATUM, LANES), jnp.int32),
                            pltpu.VMEM((DSA_STRATUM, n_keys), jnp.int16)],
            compiler_params=_cparams(("parallel",)),
            name="dsa_select_prompt",
        )(iq, iw, ikb)
        qrow = lambda w: pl.BlockSpec((None, Q_BLOCK, w), lambda b, i: (b, r * nsub + i, 0))
        keys2 = lambda w: pl.BlockSpec((None, n_keys, w), lambda b, i: (b, 0, 0))
        outs.append(pl.pallas_call(
            _dsa_attend_prompt_kernel,
            grid=(batch, nsub),
            in_specs=[qrow(1024), pl.BlockSpec((None, Q_BLOCK, n_keys), lambda b, i: (b, i, 0)), keys2(256), keys2(256)],
            out_specs=pl.BlockSpec((None, Q_BLOCK, DSA_HEADS * DSA_HD), lambda b, i: (b, i, 0)),
            out_shape=jax.ShapeDtypeStruct((batch, DSA_STRATUM, DSA_HEADS * DSA_HD), F32),
            compiler_params=_cparams(("parallel", "arbitrary")),
            name="dsa_attend_prompt",
        )(q, bias, kb, vb))
    return jnp.concatenate(outs, axis=1).reshape(batch * seq, DSA_HEADS * DSA_HD)


def _swa_head_plan(h):
    group = SWA_HEADS // SWA_KV_HEADS
    kv = h // group
    return h // 2, h % 2, kv // 2, (kv % 2) != (h % 2)


def _swa_prompt_kernel(sink_ref, q_ref, kp_ref, kc_ref, kps_ref, kcs_ref, vp_ref, vc_ref, vps_ref, vcs_ref, y_ref):
    i = pl.program_id(1)
    r = lax.broadcasted_iota(jnp.int32, (Q_BLOCK, 2 * Q_BLOCK), 0)
    c = lax.broadcasted_iota(jnp.int32, (Q_BLOCK, 2 * Q_BLOCK), 1)
    rel = Q_BLOCK + r - c
    ok = (rel >= 0) & (rel <= WINDOW) & ((i - 1) * Q_BLOCK + c >= 0)
    bias = jnp.where(ok, 0.0, NEG_INF)
    keys = (jnp.concatenate([kp_ref[...], kc_ref[...]], axis=0), jnp.concatenate([kps_ref[...], kcs_ref[...]], axis=0))
    vals = (jnp.concatenate([vp_ref[...], vc_ref[...]], axis=0), jnp.concatenate([vps_ref[...], vcs_ref[...]], axis=0))
    lower = _half_mask(Q_BLOCK, False)
    for p in range(SWA_HEADS // 2):
        qpair = q_ref[:, p * LANES:(p + 1) * LANES]
        outs = []
        for h in (2 * p, 2 * p + 1):
            _, half, ks, swapped = _swa_head_plan(h)
            qh = jnp.where(_half_mask(Q_BLOCK, half == 1), qpair, jnp.zeros_like(qpair))
            kk = keys[int(swapped)][:, ks * LANES:(ks + 1) * LANES]
            vv = vals[int(swapped)][:, ks * LANES:(ks + 1) * LANES]
            s = _dot_nt(qh, kk) + bias
            sink = sink_ref[h]
            mx = jnp.maximum(jnp.max(s, axis=-1, keepdims=True), sink)
            pr = jnp.exp(s - mx)
            den = jnp.sum(pr, axis=-1, keepdims=True) + jnp.exp(sink - mx)
            outs.append(jnp.dot(pr.astype(BF16), vv, preferred_element_type=F32) / den)
        y_ref[:, p * LANES:(p + 1) * LANES] = jnp.where(lower, outs[0], outs[1])


def swa_prompt(q, kb, kbs, vb, vbs, sinks, *, batch, seq):
    m = q.shape[0]
    nq = seq // Q_BLOCK
    cur = pl.BlockSpec((Q_BLOCK, 256), lambda b, i, s: (b * nq + i, 0))
    prev = pl.BlockSpec((Q_BLOCK, 256), lambda b, i, s: (b * nq + jnp.maximum(i - 1, 0), 0))
    qspec = pl.BlockSpec((Q_BLOCK, 2048), lambda b, i, s: (b * nq + i, 0))
    return pl.pallas_call(
        _swa_prompt_kernel,
        grid_spec=pltpu.PrefetchScalarGridSpec(
            num_scalar_prefetch=1,
            grid=(batch, nq),
            in_specs=[qspec, prev, cur, prev, cur, prev, cur, prev, cur],
            out_specs=qspec,
        ),
        out_shape=jax.ShapeDtypeStruct((m, SWA_HEADS * SWA_HD), F32),
        compiler_params=_cparams(("parallel", "arbitrary")),
        name="swa_prompt",
    )(sinks, q, kb, kb, kbs, kbs, vb, vb, vbs, vbs)


def _mem_prompt_kernel(x_ref, g_ref, wq_ref, kv_ref, wo_ref, o_ref):
    x = x_ref[...]
    h = _rms(x, g_ref[...]).astype(BF16)
    q = (jnp.dot(h, wq_ref[...], preferred_element_type=F32) * (MEM_HD ** -0.5)).astype(BF16)
    width = MEM_HEADS * MEM_HD
    outs = []
    for hd in range(MEM_HEADS):
        sl = slice(hd * MEM_HD, (hd + 1) * MEM_HD)
        k = kv_ref[:, sl].astype(BF16)
        v = kv_ref[:, width + hd * MEM_HD:width + (hd + 1) * MEM_HD].astype(BF16)
        s = _dot_nt(q[:, sl], k)
        p = jnp.exp(s - jnp.max(s, axis=-1, keepdims=True))
        o = jnp.dot(p.astype(BF16), v, preferred_element_type=F32) / jnp.sum(p, axis=-1, keepdims=True)
        outs.append(o.astype(BF16))
    o_ref[...] = x + jnp.dot(jnp.concatenate(outs, axis=-1), wo_ref[...], preferred_element_type=F32)


def mem_attn_prompt(x, g, wq, kv, wo, *, batch, seq, tq):
    m = x.shape[0]
    nq = seq // tq
    width = MEM_HEADS * MEM_HD
    xspec = pl.BlockSpec((tq, D_MODEL), lambda b, i: (b * nq + i, 0))
    return pl.pallas_call(
        _mem_prompt_kernel,
        grid=(batch, nq),
        in_specs=[xspec,
                  pl.BlockSpec((1, D_MODEL), lambda b, i: (0, 0)),
                  pl.BlockSpec((D_MODEL, width), lambda b, i: (0, 0)),
                  pl.BlockSpec((N_MEM, 2 * width), lambda b, i: (b, 0)),
                  pl.BlockSpec((width, D_MODEL), lambda b, i: (0, 0))],
        out_specs=xspec,
        out_shape=jax.ShapeDtypeStruct((m, D_MODEL), F32),
        compiler_params=_cparams(("parallel", "arbitrary")),
        name="mem_attn_prompt",
    )(x, g.reshape(1, D_MODEL), wq, kv, wo)


def _gla_prep_kernel(gq_ref, gk_ref, sm_ref, wgate_ref, bgate_ref, o_ref):
    w = GLA_HEADS * GLA_DK
    la = _log_decay(sm_ref[...], wgate_ref[...], bgate_ref[...])
    o_ref[:, 0:w] = jnp.exp(la)
    o_ref[:, w:2 * w] = gk_ref[...]
    o_ref[:, 2 * w:3 * w] = gq_ref[...] * (GLA_DK ** -0.5)
    o_ref[:, 3 * w:4 * w] = jnp.zeros((gq_ref.shape[0], w), F32)


def gla_sample_prep(z, wgate, bgate):
    m = z.shape[0]
    w = GLA_HEADS * GLA_DK
    col = lambda wd, off: pl.BlockSpec((m, wd), lambda i: (0, off // wd))
    return pl.pallas_call(
        _gla_prep_kernel,
        grid=(1,),
        in_specs=[col(512, AB_GQ), col(512, AB_GK), col(128, AB_SMALL),
                  pl.BlockSpec((LANES, w), lambda i: (0, 0)), pl.BlockSpec((1, w), lambda i: (0, 0))],
        out_specs=pl.BlockSpec((m, 4 * w), lambda i: (0, 0)),
        out_shape=jax.ShapeDtypeStruct((m, 4 * w), F32),
        compiler_params=_cparams(("arbitrary",)),
        name="gla_sample_prep",
    )(z, z, z, wgate, bgate)


def _gla_step_kernel(p_ref, gv_ref, gr_ref, s_ref, gain_ref, so_ref, y_ref, *, bs):
    gain = gain_ref[...]
    for s in range(bs):
        xt = p_ref[s].T
        for h in range(GLA_HEADS):
            dv = slice(h * GLA_DV, (h + 1) * GLA_DV)
            st = s_ref[s, h] * xt[:, h:h + 1] + xt[:, GLA_HEADS + h:GLA_HEADS + h + 1] * gv_ref[s:s + 1, dv]
            so_ref[s, h] = st
            o = jnp.sum(xt[:, 2 * GLA_HEADS + h:2 * GLA_HEADS + h + 1] * st, axis=0, keepdims=True)
            y_ref[s:s + 1, dv] = _gla_gate_out(o, gr_ref[s:s + 1, dv], gain)


def gla_sample(p, z, state, gain, *, bs):
    m = z.shape[0]
    col = lambda w, off: pl.BlockSpec((bs, w), lambda i: (i, off // w))
    sspec = pl.BlockSpec((bs, GLA_HEADS, GLA_DK, GLA_DV), lambda i: (i, 0, 0, 0))
    return pl.pallas_call(
        functools.partial(_gla_step_kernel, bs=bs),
        grid=(m // bs,),
        in_specs=[pl.BlockSpec((bs, 16, GLA_DK), lambda i: (i, 0, 0)), col(1024, AB_GV), col(1024, AB_GR), sspec,
                  pl.BlockSpec((1, GLA_DV), lambda i: (0, 0))],
        out_specs=[sspec, pl.BlockSpec((bs, GLA_HEADS * GLA_DV), lambda i: (i, 0))],
        out_shape=[jax.ShapeDtypeStruct(state.shape, F32), jax.ShapeDtypeStruct((m, GLA_HEADS * GLA_DV), F32)],
        compiler_params=_cparams(("parallel",)),
        name="gla_sample",
    )(p, z, z, state, gain)


def _dsa_scores_kernel(pt_ref, iq_ref, iw_ref, *refs, pg):
    page_refs, o_ref = refs[:pg], refs[pg]
    q8 = iq_ref[0]
    iw = iw_ref[0]
    for j in range(pg):
        dots = jnp.dot(q8, page_refs[j][0].astype(BF16), preferred_element_type=F32)
        o_ref[0, j:j + 1, :] = jnp.sum(jnp.maximum(dots, 0.0) * iw, axis=0, keepdims=True)


def dsa_sample_scores(page_table, iq, iw, pool_ik_t, page_base, *, pg):
    m, n_pages = page_table.shape
    pool_ik = pool_ik_t
    page_spec = lambda j: pl.BlockSpec((1, IDX_HD, PAGE_SIZE),
                                       lambda b, g, pt: (page_base + pt[b, g * pg + j], 0, 0))
    return pl.pallas_call(
        functools.partial(_dsa_scores_kernel, pg=pg),
        grid_spec=pltpu.PrefetchScalarGridSpec(
            num_scalar_prefetch=1,
            grid=(m, n_pages // pg),
            in_specs=[pl.BlockSpec((1, IDX_HEADS, IDX_HD), lambda b, g, pt: (b, 0, 0)),
                      pl.BlockSpec((1, IDX_HEADS, 1), lambda b, g, pt: (b, 0, 0))]
                     + [page_spec(j) for j in range(pg)],
            out_specs=pl.BlockSpec((1, pg, PAGE_SIZE), lambda b, g, pt: (b, g, 0)),
        ),
        out_shape=jax.ShapeDtypeStruct((m, n_pages, PAGE_SIZE), F32),
        compiler_params=_cparams(("parallel", "arbitrary")),
        name="dsa_sample_scores",
    )(page_table, iq, iw, *([pool_ik] * pg))


def _dsa_select_sample_kernel(sc_ref, iq_ref, iw_ref, ikb_ref, bias_ref, bnew_ref, key_ref, t_ref, h_ref, *,
                              n_past, topk):
    rows = sc_ref.shape[0]
    n_blocks = n_past // LANES
    for c in range(n_blocks):
        sl = slice(c * LANES, (c + 1) * LANES)
        key_ref[:, sl] = _order_key(sc_ref[:, sl])
    ik = ikb_ref[...].astype(F32)
    iw = iw_ref[...]
    s_new = jnp.zeros((rows, 1), F32)
    for h in range(IDX_HEADS):
        pair = iq_ref[:, (h // 2) * LANES:(h // 2 + 1) * LANES].astype(F32)
        qh = jnp.where(_half_mask(rows, h % 2 == 1), pair, 0.0)
        s_new = s_new + jnp.maximum(jnp.sum(qh * ik, axis=-1, keepdims=True), 0.0) * iw[:, h:h + 1]
    lane = lax.broadcasted_iota(jnp.int32, (rows, LANES), 1)
    key_ref[:, n_past:n_past + LANES] = _order_key(jnp.where(lane == 0, s_new, -jnp.inf))

    ri = lax.broadcasted_iota(jnp.int32, (LANES, DSA_KV_HEADS * LANES), 0)
    ci = lax.broadcasted_iota(jnp.int32, (LANES, DSA_KV_HEADS * LANES), 1)
    spread = jnp.where(ci // DSA_KV_HEADS == ri, 1.0, 0.0).astype(BF16)

    def write(rws, c, sel):
        if c == n_blocks:
            bnew_ref[rws, :] = jnp.where(sel, 0.0, NEG_INF)
        else:
            wide = jnp.dot(jnp.where(sel, 1.0, 0.0).astype(BF16), spread, preferred_element_type=F32)
            w = DSA_KV_HEADS * LANES
            bias_ref[rws, c * w:(c + 1) * w] = jnp.where(wide > 0.5, 0.0, NEG_INF)

    _select_topk(key_ref, t_ref, h_ref, n_blocks + 1, topk, write)


def dsa_sample_select(scores, iq, iw, ikb, *, topk):
    m, n_past = scores.shape
    full = lambda a: pl.BlockSpec(a.shape, lambda i: (0,) * a.ndim)
    wide = DSA_KV_HEADS * n_past
    return pl.pallas_call(
        functools.partial(_dsa_select_sample_kernel, n_past=n_past, topk=topk),
        grid=(1,),
        in_specs=[full(scores), full(iq), full(iw), full(ikb)],
        out_specs=[pl.BlockSpec((m, wide), lambda i: (0, 0)), pl.BlockSpec((m, LANES), lambda i: (0, 0))],
        out_shape=[jax.ShapeDtypeStruct((m, wide), F32), jax.ShapeDtypeStruct((m, LANES), F32)],
        scratch_shapes=[pltpu.VMEM((m, n_past + LANES), jnp.int32), pltpu.VMEM((3, m, LANES), jnp.int32),
                        pltpu.VMEM((m, n_past + LANES), jnp.int16)],
        compiler_params=_cparams(("arbitrary",)),
        name="dsa_sample_select",
    )(scores, iq, iw, ikb)


def _dsa_step_kernel(pt_ref, q_ref, bias_ref, bnew_ref, kn_ref, vn_ref, *refs, pg):
    k_refs, v_refs = refs[:pg], refs[pg:2 * pg]
    y_ref, m_ref, l_ref, acc_ref = refs[2 * pg:]
    g = pl.program_id(1)

    @pl.when(g == 0)
    def _():
        m_ref[...] = jnp.full_like(m_ref, NEG_INF)
        l_ref[...] = jnp.zeros_like(l_ref)
        acc_ref[...] = jnp.zeros_like(acc_ref)

    q8 = q_ref[0]
    group = DSA_HEADS // DSA_KV_HEADS
    first = lax.broadcasted_iota(jnp.int32, (DSA_HEADS, DSA_HD), 0) < group
    wide = DSA_KV_HEADS * PAGE_SIZE
    hrow = lax.broadcasted_iota(jnp.int32, (DSA_HEADS, wide), 0)
    col = lax.broadcasted_iota(jnp.int32, (DSA_HEADS, wide), 1)
    own = (col % DSA_KV_HEADS) == (hrow // group)
    ss, oks = [], []
    for j in range(pg):
        ok = jnp.logical_and(own, bias_ref[0, j:j + 1, :] == 0.0)
        ss.append(jnp.where(ok, _dot_nt(q8, k_refs[j][...].astype(BF16)), NEG_INF))
        oks.append(ok)
    mx = ss[0]
    for s in ss[1:]:
        mx = jnp.maximum(mx, s)
    m_old = m_ref[...]
    m_new = jnp.maximum(m_old, jnp.max(mx, axis=-1, keepdims=True))
    psum = jnp.zeros((DSA_HEADS, wide), F32)
    pv = jnp.zeros((DSA_HEADS, DSA_HD), F32)
    for j in range(pg):
        p = jnp.where(oks[j], jnp.exp(ss[j] - m_new[:, 0:1]), 0.0)
        psum = psum + p
        pv = pv + jnp.dot(p.astype(BF16), v_refs[j][...].astype(BF16), preferred_element_type=F32)
    alpha = jnp.exp(m_old - m_new)
    l_ref[...] = alpha * l_ref[...] + jnp.sum(psum, axis=-1, keepdims=True)
    acc_ref[...] = alpha * acc_ref[...] + pv
    m_ref[...] = m_new

    @pl.when(g == pl.num_programs(1) - 1)
    def _():
        kn = kn_ref[0].astype(BF16).astype(F32)
        vn = vn_ref[0].astype(BF16).astype(F32)
        bn = bnew_ref[0][:, 0:1]
        s_new = jnp.sum(q8.astype(F32) * jnp.where(first, kn[:, :DSA_HD], kn[:, DSA_HD:]), axis=-1, keepdims=True) + bn
        m_old = m_ref[...]
        m_new = jnp.maximum(m_old, s_new)
        p_new = jnp.where(bn == 0.0, jnp.exp(s_new - m_new), 0.0)
        alpha = jnp.exp(m_old - m_new)
        acc = alpha * acc_ref[...] + p_new * jnp.where(first, vn[:, :DSA_HD], vn[:, DSA_HD:])
        y_ref[0] = acc / (alpha * l_ref[...] + p_new)


def dsa_sample_attend(page_table, q, bias, bnew, kn, vn, pool_k, pool_v, page_base, *, pg):
    m, n_pages = page_table.shape
    wide = DSA_KV_HEADS * PAGE_SIZE
    page_spec = lambda j: pl.BlockSpec((wide, DSA_HD), lambda b, g, pt: (page_base + pt[b, g * pg + j], 0))
    per_b = lambda shape: pl.BlockSpec((1,) + shape, lambda b, g, pt: (b, 0, 0))
    return pl.pallas_call(
        functools.partial(_dsa_step_kernel, pg=pg),
        grid_spec=pltpu.PrefetchScalarGridSpec(
            num_scalar_prefetch=1,
            grid=(m, n_pages // pg),
            in_specs=[per_b((DSA_HEADS, DSA_HD)),
                      pl.BlockSpec((1, pg, wide), lambda b, g, pt: (b, g, 0)),
                      per_b((1, LANES)), per_b((1, DSA_KV_HEADS * DSA_HD)), per_b((1, DSA_KV_HEADS * DSA_HD))]
                     + [page_spec(j) for j in range(pg)] * 2,
            out_specs=per_b((DSA_HEADS, DSA_HD)),
            scratch_shapes=[pltpu.VMEM((DSA_HEADS, DSA_HD), F32)] * 3,
        ),
        out_shape=jax.ShapeDtypeStruct((m, DSA_HEADS, DSA_HD), F32),
        compiler_params=_cparams(("parallel", "arbitrary")),
        name="dsa_sample_attend",
    )(page_table, q, bias, bnew, kn, vn, *([pool_k] * pg), *([pool_v] * pg))


def _swa_step_kernel(q_ref, kt_ref, vt_ref, kn_ref, vn_ref, sink_ref, y_ref, *, bs):
    group = SWA_HEADS // SWA_KV_HEADS
    pairs = [(s, kv) for s in range(bs) for kv in range(SWA_KV_HEADS)]
    qs, scs, news = [], [], []
    for s, kv in pairs:
        qv = q_ref[s, kv * group:(kv + 1) * group, :]
        kn = kn_ref[s, kv:kv + 1, :].astype(BF16).astype(F32)
        scs.append(jnp.dot(qv, kt_ref[s, kv].astype(BF16), preferred_element_type=F32))
        news.append(jnp.sum(qv.astype(F32) * kn, axis=-1, keepdims=True))
    sc = jnp.concatenate(scs, axis=0)
    s_new = jnp.concatenate(news, axis=0)
    sink = jnp.concatenate([sink_ref[...]] * bs, axis=0)
    mx = jnp.maximum(jnp.maximum(jnp.max(sc, axis=-1, keepdims=True), s_new), sink)
    p = jnp.exp(sc - mx)
    p_new = jnp.exp(s_new - mx)
    inv = 1.0 / (jnp.sum(p, axis=-1, keepdims=True) + p_new + jnp.exp(sink - mx))
    pb = p.astype(BF16)
    for n, (s, kv) in enumerate(pairs):
        rows = slice(n * group, (n + 1) * group)
        vn = vn_ref[s, kv:kv + 1, :].astype(BF16).astype(F32)
        o = _dot_nt(pb[rows, :], vt_ref[s, kv].astype(BF16)) + p_new[rows, :] * vn
        y_ref[s, kv * group:(kv + 1) * group, :] = o * inv[rows, :]


def swa_sample(q, kt, vt, kn, vn, sinks, *, bs):
    m = q.shape[0]
    cache = pl.BlockSpec((bs, SWA_KV_HEADS, SWA_HD, WINDOW), lambda i: (i, 0, 0, 0))
    new = pl.BlockSpec((bs, SWA_KV_HEADS, SWA_HD), lambda i: (i, 0, 0))
    return pl.pallas_call(
        functools.partial(_swa_step_kernel, bs=bs),
        grid=(m // bs,),
        in_specs=[pl.BlockSpec((bs, SWA_HEADS, SWA_HD), lambda i: (i, 0, 0)), cache, cache, new, new,
                  pl.BlockSpec((SWA_HEADS, 1), lambda i: (0, 0))],
        out_specs=pl.BlockSpec((bs, SWA_HEADS, SWA_HD), lambda i: (i, 0, 0)),
        out_shape=jax.ShapeDtypeStruct((m, SWA_HEADS, SWA_HD), F32),
        compiler_params=_cparams(("parallel",)),
        name="swa_sample",
    )(q, kt, vt, kn, vn, sinks.reshape(SWA_HEADS, 1))


def _mem_step_kernel(q_ref, k_ref, v_ref, y_ref, *, bs):
    rows = q_ref.shape[1]
    n = N_MEM * MEM_HEADS
    hrow = lax.broadcasted_iota(jnp.int32, (rows, n), 0)
    col = lax.broadcasted_iota(jnp.int32, (rows, n), 1)
    own = (col % MEM_HEADS) == (hrow % MEM_HEADS)
    for s in range(bs):
        q = (q_ref[s] * (MEM_HD ** -0.5)).astype(BF16)
        sc = jnp.where(own, _dot_nt(q, k_ref[s * n:(s + 1) * n, :].astype(BF16)), NEG_INF)
        p = jnp.where(own, jnp.exp(sc - jnp.max(sc, axis=-1, keepdims=True)), 0.0)
        o = jnp.dot(p.astype(BF16), v_ref[s * n:(s + 1) * n, :].astype(BF16), preferred_element_type=F32)
        y_ref[s] = o / jnp.sum(p, axis=-1, keepdims=True)


def mem_attn_sample(q, mk, mv, layer, *, bs):
    m, rows, _ = q.shape
    n = N_MEM * MEM_HEADS
    nb = m // bs
    cache = pl.BlockSpec((bs * n, MEM_HD), lambda i: (layer * nb + i, 0))
    return pl.pallas_call(
        functools.partial(_mem_step_kernel, bs=bs),
        grid=(nb,),
        in_specs=[pl.BlockSpec((bs, rows, MEM_HD), lambda i: (i, 0, 0)), cache, cache],
        out_specs=pl.BlockSpec((bs, rows, MEM_HD), lambda i: (i, 0, 0)),
        out_shape=jax.ShapeDtypeStruct((m, rows, MEM_HD), F32),
        compiler_params=_cparams(("parallel",)),
        name="mem_attn_sample",
    )(q, mk, mv)


TM_FFN, TF_FFN, SUB_FFN = 1024, 512, 512
TM_PROJ, TN_AB, TN_C, TN_MEM = 1024, 768, 512, 512
TM_OUT = 512
TR_FEAT = 512
TC_GLA, NSEQ_GLA = 256, 2
TQ_MEM = 512
BS_SAMPLE = 8
PG_SCORES = 32
PG_ATTEND = 16


def _prep_w_in_ab(w):
    sizes = (GLA_HEADS * GLA_DK, GLA_HEADS * GLA_DK, GLA_HEADS * GLA_DV, GLA_HEADS * GLA_DV, GLA_GATE_RANK,
             DSA_HEADS * DSA_HD, DSA_KV_HEADS * DSA_HD, DSA_KV_HEADS * DSA_HD, IDX_HEADS * IDX_HD, IDX_HEADS, IDX_HD)
    offs = np.cumsum((0,) + sizes)
    gq, gk, gv, gr, gd, dq, dk, dv, iq, iw, ik = [w[:, int(offs[j]):int(offs[j + 1])] for j in range(len(sizes))]
    pad = lambda n: jnp.zeros((w.shape[0], n), w.dtype)
    small = jnp.concatenate([ik, gd, iw, pad(LANES - IDX_HD - GLA_GATE_RANK - IDX_HEADS)], axis=1)
    out = jnp.concatenate([gv, gr, dq, gq, gk, iq, dk, dv, small, pad(AB_WIDTH - AB_SMALL - LANES)], axis=1)
    return out.astype(BF16)


def _prep_gate(w_up):
    return jnp.zeros((LANES, GLA_HEADS * GLA_DK), F32).at[SM_GD:SM_GD + GLA_GATE_RANK].set(w_up)


def kernel(x_prompt, x_sample, mem_prompt, cache_dsa_k, cache_dsa_v, cache_dsa_idx_k, state_gla, cache_swa_k, cache_swa_v, cache_mem_k, cache_mem_v, page_table, norm_ffn, w_ffn_gate, w_ffn_up, w_ffn_down, norm_mix, w_in_ab, w_gla_gate_up, b_gla_gate, gla_out_norm, idx_k_norm, w_out_ab, w_in_c, swa_sinks, w_out_c, norm_mem_q, norm_mem_src, w_mem_q, w_mem_kv, w_mem_o, final_norm):
    depth = norm_mix.shape[0]
    bp, seq, _ = x_prompt.shape
    bs = x_sample.shape[0]
    n_pool = cache_dsa_k.shape[1]
    gla_w = GLA_HEADS * GLA_DV

    w_ab = [_prep_w_in_ab(w_in_ab[i]) for i in range(w_in_ab.shape[0])]
    w_gate = [_prep_gate(w_gla_gate_up[i]) for i in range(w_in_ab.shape[0])]
    w_oab = w_out_ab.astype(BF16)
    w_c, w_oc = w_in_c.astype(BF16), w_out_c.astype(BF16)
    w_mq, w_mkv, w_mo = w_mem_q.astype(BF16), w_mem_kv.astype(BF16), w_mem_o.astype(BF16)

    def rope_tabs(pos):
        return _rope_tables(pos, DSA_HD) + _rope_tables(pos, IDX_HD)

    ffn_bf16 = {}

    def ffn_pair(x, layer, half, last, sample):
        fin = final_norm if last else None
        if sample:
            y, *ffn_bf16[layer, half] = ffn(x, norm_ffn[layer, half], w_ffn_gate, w_ffn_up, w_ffn_down, (layer, half),
                                            fin, tm=TM_FFN, tf=TF_FFN, sub=SUB_FFN)
            return y
        return ffn(x, norm_ffn[layer, half], *ffn_bf16[layer, half], None, fin, tm=TM_FFN, tf=TF_FFN, sub=SUB_FFN)

    def prompt_group():
        tabs_p = rope_tabs(jnp.arange(seq))
        x = x_prompt.reshape(bp * seq, D_MODEL)
        mem = mem_prompt.reshape(bp * N_MEM, D_MODEL)
        mem_kv = [norm_proj(mem, norm_mem_src[l], w_mkv[l], tm=TM_PROJ, tn=TN_MEM) for l in range(depth)]
        st_ab_p, st_c_p = [], []
        for l in range(depth):
            i = l // 2
            x = ffn_pair(x, l, 0, False, False)
            if l % 2 == 0:
                z = norm_proj(x, norm_mix[l], w_ab[i], tm=TM_PROJ, tn=TN_AB)
                q, k, kb, vb, iq, ik, ikb, iw = ab_features(z, tabs_p, idx_k_norm[i], seq=seq, tr=TR_FEAT)
                y_gla, st_t = gla_prompt(z, w_gate[i], b_gla_gate[i].reshape(1, -1), gla_out_norm[i].reshape(1, -1),
                                         batch=bp, seq=seq, tc=TC_GLA, n_seq=NSEQ_GLA)
                y_dsa = dsa_prompt(q, iq, iw, kb, vb, ikb, batch=bp, seq=seq)
                x = out_proj(x, [y_gla, y_dsa], [w_oab[i, :gla_w], w_oab[i, gla_w:]], tm=TM_OUT)
                n_pg = seq // PAGE_SIZE
                st_ab_p.append((k.reshape(bp, n_pg, PAGE_SIZE, DSA_KV_HEADS, DSA_HD),
                                z[:, AB_DV:AB_DV + 256].reshape(bp, n_pg, PAGE_SIZE, DSA_KV_HEADS, DSA_HD),
                                ik.reshape(bp, n_pg, PAGE_SIZE, IDX_HD),
                                jnp.swapaxes(st_t, 2, 3)))
            else:
                z = norm_proj(x, norm_mix[l], w_c[i], tm=TM_PROJ, tn=TN_C)
                q, k, kb, kbs, vb, vbs = c_features(z, tabs_p[2:], seq=seq, tr=TR_FEAT)
                y = swa_prompt(q, kb, kbs, vb, vbs, swa_sinks[i], batch=bp, seq=seq)
                x = out_proj(x, [y], [w_oc[i]], tm=TM_OUT)
                st_c_p.append((k.reshape(bp, seq, SWA_KV_HEADS, SWA_HD)[:, -WINDOW:],
                               z[:, C_V:C_V + 256].reshape(bp, seq, SWA_KV_HEADS, SWA_HD)[:, -WINDOW:]))
            x = mem_attn_prompt(x, norm_mem_q[l], w_mq[l], mem_kv[l], w_mo[l], batch=bp, seq=seq, tq=TQ_MEM)
            x = ffn_pair(x, l, 1, l == depth - 1, False)
        return x.reshape(bp, seq, D_MODEL), st_ab_p, st_c_p, mem_kv

    tabs_s = rope_tabs(PAST_LEN + jnp.arange(1))
    n_pages = page_table.shape[1]
    topk = min(DSA_TOPK_MAX, (PAST_LEN + 1) // 4)
    pool_k = cache_dsa_k.reshape(-1, DSA_HD)
    pool_v = cache_dsa_v.reshape(-1, DSA_HD)
    pool_ik_t = jnp.swapaxes(cache_dsa_idx_k, 2, 3).reshape(-1, IDX_HD, PAGE_SIZE)
    mem_k_rows = cache_mem_k.reshape(-1, MEM_HD)
    mem_v_rows = cache_mem_v.reshape(-1, MEM_HD)
    x = x_sample.reshape(bs, D_MODEL)
    st_ab_s, st_c_s = [], []
    for l in range(depth):
        i = l // 2
        x = ffn_pair(x, l, 0, False, True)
        if l % 2 == 0:
            z = norm_proj(x, norm_mix[l], w_ab[i], tm=TM_PROJ, tn=TN_AB)
            q, k, kb, vb, iq, ik, ikb, iw = ab_features(z, tabs_s, idx_k_norm[i], seq=1, tr=TR_FEAT)
            p = gla_sample_prep(z, w_gate[i], b_gla_gate[i].reshape(1, -1)).reshape(bs, 16, GLA_DK)
            st_new, y_gla = gla_sample(p, z, state_gla[i], gla_out_norm[i].reshape(1, -1), bs=BS_SAMPLE)
            scores = dsa_sample_scores(page_table, iq.reshape(bs, IDX_HEADS, IDX_HD),
                                       iw[:, :IDX_HEADS].reshape(bs, IDX_HEADS, 1), pool_ik_t, i * n_pool, pg=PG_SCORES)
            bias, bnew = dsa_sample_select(scores.reshape(bs, n_pages * PAGE_SIZE), iq, iw, ikb, topk=topk)
            v_new = z[:, AB_DV:AB_DV + 256]
            y_dsa = dsa_sample_attend(
                page_table, q.reshape(bs, DSA_HEADS, DSA_HD), bias.reshape(bs, n_pages, DSA_KV_HEADS * PAGE_SIZE),
                bnew.reshape(bs, 1, LANES), k.reshape(bs, 1, 256), v_new.reshape(bs, 1, 256),
                pool_k, pool_v, i * n_pool, pg=PG_ATTEND).reshape(bs, DSA_HEADS * DSA_HD)
            x = out_proj(x, [y_gla, y_dsa], [w_oab[i, :gla_w], w_oab[i, gla_w:]], tm=TM_OUT)
            st_ab_s.append((k.reshape(bs, 1, DSA_KV_HEADS, DSA_HD), v_new.reshape(bs, 1, DSA_KV_HEADS, DSA_HD),
                            ik.reshape(bs, 1, IDX_HD), st_new))
        else:
            z = norm_proj(x, norm_mix[l], w_c[i], tm=TM_PROJ, tn=TN_C)
            q, k, kb, kbs, vb, vbs = c_features(z, tabs_s[2:], seq=1, tr=TR_FEAT)
            v_new = z[:, C_V:C_V + 256]
            y = swa_sample(q.reshape(bs, SWA_HEADS, SWA_HD), jnp.transpose(cache_swa_k[i], (0, 2, 3, 1)),
                           jnp.transpose(cache_swa_v[i], (0, 2, 3, 1)), k.reshape(bs, SWA_KV_HEADS, SWA_HD),
                           v_new.reshape(bs, SWA_KV_HEADS, SWA_HD), swa_sinks[i], bs=BS_SAMPLE)
            x = out_proj(x, [y.reshape(bs, SWA_HEADS * SWA_HD)], [w_oc[i]], tm=TM_OUT)
            st_c_s.append((jnp.concatenate([cache_swa_k[i][:, 1:], k.reshape(bs, 1, SWA_KV_HEADS, SWA_HD)], axis=1),
                           jnp.concatenate([cache_swa_v[i][:, 1:], v_new.reshape(bs, 1, SWA_KV_HEADS, SWA_HD)], axis=1)))
        qm = norm_proj(x, norm_mem_q[l], w_mq[l], tm=TM_PROJ, tn=TN_MEM).reshape(bs, MEM_HEADS, MEM_HD)
        qm = jnp.pad(qm, ((0, 0), (0, SUBLANES - MEM_HEADS), (0, 0)))
        om = mem_attn_sample(qm, mem_k_rows, mem_v_rows, l, bs=BS_SAMPLE)
        x = out_proj(x, [om[:, :MEM_HEADS].reshape(bs, MEM_HEADS * MEM_HD)], [w_mo[l]], tm=TM_OUT)
        x = ffn_pair(x, l, 1, l == depth - 1, True)
    y_sample = x.reshape(bs, 1, D_MODEL)

    y_prompt, st_ab_p, st_c_p, mem_kv = prompt_group()

    stk = lambda sts, j: jnp.stack([s[j] for s in sts])
    mw = MEM_HEADS * MEM_HD
    mem_k_p = jnp.stack([kv[:, :mw].reshape(bp, N_MEM, MEM_HEADS, MEM_HD) for kv in mem_kv])
    mem_v_p = jnp.stack([kv[:, mw:].reshape(bp, N_MEM, MEM_HEADS, MEM_HD) for kv in mem_kv])
    return (y_prompt, y_sample, stk(st_ab_p, 0), stk(st_ab_p, 1), stk(st_ab_p, 2),
            stk(st_ab_s, 0), stk(st_ab_s, 1), stk(st_ab_s, 2), stk(st_ab_p, 3), stk(st_ab_s, 3),
            stk(st_c_p, 0), stk(st_c_p, 1), stk(st_c_s, 0), stk(st_c_s, 1), mem_k_p, mem_v_p)
```

```python
import functools

import jax
import jax.numpy as jnp
import numpy as np
from jax import lax
from jax.experimental import pallas as pl
from jax.experimental.pallas import tpu as pltpu

F32 = jnp.float32
BF16 = jnp.bfloat16

D_MODEL = 2048
D_FF = 5632
EPS = 1e-6
ROPE_THETA = 10000.0
NEG_INF = -1e30
PAST_LEN = 8192
PAGE_SIZE = 128
Q_BLOCK = 128
GLA_HEADS, GLA_DK, GLA_DV = 4, 128, 256
GLA_GATE_RANK = 16
GLA_GATE_TAU = 16.0
GLA_CHUNK = 64
DSA_HEADS, DSA_KV_HEADS, DSA_HD = 8, 2, 128
IDX_HEADS, IDX_HD = 8, 64
DSA_TOPK_MAX = 256
SWA_HEADS, SWA_KV_HEADS, SWA_HD = 32, 4, 64
WINDOW = 128
MEM_HEADS, MEM_HD = 4, 128
N_MEM = 256

LANES = 128
SUBLANES = 8
VMEM_LIMIT_BYTES = 60000 * 1024
ELEM_SUB = 128

AB_GV, AB_GR, AB_DQ = 0, 1024, 2048
AB_GQ, AB_GK, AB_IQ = 3072, 3584, 4096
AB_DK, AB_DV = 4608, 4864
AB_SMALL = 5120
AB_WIDTH = 5376
SM_IK, SM_GD, SM_IW = 0, 64, 80
C_Q, C_K, C_V, C_WIDTH = 0, 2048, 2304, 2560


def _cparams(sem):
    return pltpu.CompilerParams(dimension_semantics=sem, vmem_limit_bytes=VMEM_LIMIT_BYTES)


def _rms(x, g):
    y = x * lax.rsqrt(jnp.mean(x * x, axis=-1, keepdims=True) + EPS)
    return y * g


def _ffn_kernel(x_ref, g_ref, wg_ref, wu_ref, wd_ref, fg_ref, o_ref, *rest, final_norm, sub, emit_cast):
    h_ref = rest[-1]
    j = pl.program_id(1)
    tm = x_ref.shape[0]
    if emit_cast:
        for src, dst in zip((wg_ref, wu_ref, wd_ref), rest[:3]):
            dst[...] = src[...].astype(BF16)
        wg_ref, wu_ref, wd_ref = rest[:3]

    esub = min(ELEM_SUB, tm)

    def row_group(r):
        return pl.ds(pl.multiple_of(r * esub, esub), esub)

    @pl.when(j == 0)
    def _():
        def body(r, carry):
            rows = row_group(r)
            h_ref[rows, :] = _rms(x_ref[rows, :], g_ref[...]).astype(BF16)
            o_ref[rows, :] = jnp.zeros((esub, D_MODEL), F32)
            return carry

        lax.fori_loop(0, tm // esub, body, 0)

    for r in range(tm // sub):
        rows = slice(r * sub, (r + 1) * sub)
        h = h_ref[rows, :]
        a = jnp.dot(h, wg_ref[...], preferred_element_type=F32)
        u = jnp.dot(h, wu_ref[...], preferred_element_type=F32)
        act = (a * jax.nn.sigmoid(a) * u).astype(BF16)
        o_ref[rows, :] += jnp.dot(act, wd_ref[...], preferred_element_type=F32)

    @pl.when(j == pl.num_programs(1) - 1)
    def _():
        def body(r, carry):
            rows = row_group(r)
            y = x_ref[rows, :] + 0.5 * o_ref[rows, :]
            if final_norm:
                y = _rms(y, fg_ref[...])
            o_ref[rows, :] = y
            return carry

        lax.fori_loop(0, tm // esub, body, 0)


def ffn(x, g, wg, wu, wd, sel, final_g=None, *, tm, tf, sub):
    m = x.shape[0]
    tm = min(tm, m)
    sub = min(sub, tm)
    fg = g if final_g is None else final_g
    emit_cast = sel is not None
    wspec = lambda shape, imap: pl.BlockSpec(shape, imap)
    if emit_cast:
        assert m == tm and wg.dtype == F32
        layer, half = sel
        w_in = [wspec((None, None, D_MODEL, tf), lambda i, j: (layer, half, 0, j)),
                wspec((None, None, D_MODEL, tf), lambda i, j: (layer, half, 0, j)),
                wspec((None, None, tf, D_MODEL), lambda i, j: (layer, half, j, 0))]
    else:
        assert wg.dtype == BF16 and wg.ndim == 2
        w_in = [wspec((D_MODEL, tf), lambda i, j: (0, j)), wspec((D_MODEL, tf), lambda i, j: (0, j)),
                wspec((tf, D_MODEL), lambda i, j: (j, 0))]
    out_specs = [pl.BlockSpec((tm, D_MODEL), lambda i, j: (i, 0))]
    out_shape = [jax.ShapeDtypeStruct((m, D_MODEL), F32)]
    if emit_cast:
        out_specs += [wspec((D_MODEL, tf), lambda i, j: (0, j)), wspec((D_MODEL, tf), lambda i, j: (0, j)),
                      wspec((tf, D_MODEL), lambda i, j: (j, 0))]
        out_shape += [jax.ShapeDtypeStruct((D_MODEL, D_FF), BF16), jax.ShapeDtypeStruct((D_MODEL, D_FF), BF16),
                      jax.ShapeDtypeStruct((D_FF, D_MODEL), BF16)]
    outs = pl.pallas_call(
        functools.partial(_ffn_kernel, final_norm=final_g is not None, sub=sub, emit_cast=emit_cast),
        grid=(m // tm, D_FF // tf),
        in_specs=[pl.BlockSpec((tm, D_MODEL), lambda i, j: (i, 0)), pl.BlockSpec((1, D_MODEL), lambda i, j: (0, 0))]
                 + w_in + [pl.BlockSpec((1, D_MODEL), lambda i, j: (0, 0))],
        out_specs=out_specs,
        out_shape=out_shape,
        scratch_shapes=[pltpu.VMEM((tm, D_MODEL), BF16)],
        compiler_params=_cparams(("parallel", "arbitrary")),
        name="ffn_cast" if emit_cast else "ffn",
    )(x, g.reshape(1, D_MODEL), wg, wu, wd, fg.reshape(1, D_MODEL))
    return tuple(outs) if emit_cast else outs[0]


def _proj_kernel(x_ref, g_ref, w_ref, o_ref, h_ref, *, sub):
    tm = x_ref.shape[0]

    @pl.when(pl.program_id(1) == 0)
    def _():
        for r in range(tm // sub):
            rows = slice(r * sub, (r + 1) * sub)
            h_ref[rows, :] = _rms(x_ref[rows, :], g_ref[...]).astype(BF16)

    o_ref[...] = jnp.dot(h_ref[...], w_ref[...], preferred_element_type=F32)


def norm_proj(x, g, w, *, tm, tn, sub=512):
    m, n = x.shape[0], w.shape[1]
    tm = min(tm, m)
    return pl.pallas_call(
        functools.partial(_proj_kernel, sub=min(sub, tm)),
        grid=(m // tm, n // tn),
        in_specs=[
            pl.BlockSpec((tm, D_MODEL), lambda i, j: (i, 0)),
            pl.BlockSpec((1, D_MODEL), lambda i, j: (0, 0)),
            pl.BlockSpec((D_MODEL, tn), lambda i, j: (0, j)),
        ],
        out_specs=pl.BlockSpec((tm, tn), lambda i, j: (i, j)),
        out_shape=jax.ShapeDtypeStruct((m, n), F32),
        scratch_shapes=[pltpu.VMEM((tm, D_MODEL), BF16)],
        compiler_params=_cparams(("parallel", "arbitrary")),
        name="norm_proj",
    )(x, g.reshape(1, D_MODEL), w)


def _outproj_kernel(*refs, n_in):
    x_ref = refs[0]
    y_refs = refs[1:1 + n_in]
    w_refs = refs[1 + n_in:1 + 2 * n_in]
    o_ref = refs[1 + 2 * n_in]
    acc = x_ref[...]
    for y_ref, w_ref in zip(y_refs, w_refs):
        acc = acc + jnp.dot(y_ref[...].astype(BF16), w_ref[...], preferred_element_type=F32)
    o_ref[...] = acc


def out_proj(x, ys, ws, *, tm):
    m = x.shape[0]
    tm = min(tm, m)
    n_in = len(ys)
    in_specs = [pl.BlockSpec((tm, D_MODEL), lambda i: (i, 0))]
    in_specs += [pl.BlockSpec((tm, y.shape[1]), lambda i: (i, 0)) for y in ys]
    in_specs += [pl.BlockSpec(w.shape, lambda i: (0, 0)) for w in ws]
    return pl.pallas_call(
        functools.partial(_outproj_kernel, n_in=n_in),
        grid=(m // tm,),
        in_specs=in_specs,
        out_specs=pl.BlockSpec((tm, D_MODEL), lambda i: (i, 0)),
        out_shape=jax.ShapeDtypeStruct((m, D_MODEL), F32),
        compiler_params=_cparams(("parallel",)),
        name="out_proj",
    )(x, *ys, *ws)


def _rope_tables(pos, hd):
    half = hd // 2
    inv = ROPE_THETA ** (-jnp.arange(half, dtype=F32) / half)
    ang = pos.astype(F32)[:, None] * inv[None, :]
    cos, sin = jnp.cos(ang), jnp.sin(ang)
    reps = LANES // hd
    return (jnp.concatenate([cos, cos] * reps, axis=-1),
            jnp.concatenate([-sin, sin] * reps, axis=-1))


def _rope128(x, cos, sin):
    return x * cos + pltpu.roll(x, 64, 1) * sin


def _rope64(x, cos, sin, lower):
    partner = jnp.where(lower, pltpu.roll(x, 96, 1), pltpu.roll(x, 32, 1))
    return x * cos + partner * sin


def _lower32_mask(rows):
    lane = lax.broadcasted_iota(jnp.int32, (rows, LANES), 1)
    return (lane % 64) < 32


def _ab_feat_kernel(dq_ref, dk_ref, dv_ref, iq_ref, sm_ref, c128_ref, s128_ref, c64_ref, s64_ref, gik_ref,
                    q_ref, k_ref, kb_ref, vb_ref, iqo_ref, ik_ref, ikb_ref, iw_ref):
    rows = dq_ref.shape[0]
    c128, s128 = c128_ref[...], s128_ref[...]
    c64, s64 = c64_ref[...], s64_ref[...]
    lower = _lower32_mask(rows)
    for h in range(DSA_HEADS):
        sl = slice(h * LANES, (h + 1) * LANES)
        q_ref[:, sl] = (_rope128(dq_ref[:, sl], c128, s128) * (DSA_HD ** -0.5)).astype(BF16)
    for h in range(DSA_KV_HEADS):
        sl = slice(h * LANES, (h + 1) * LANES)
        kr = _rope128(dk_ref[:, sl], c128, s128)
        k_ref[:, sl] = kr
        kb_ref[:, sl] = kr.astype(BF16)
    vb_ref[...] = dv_ref[...].astype(BF16)
    for p in range(IDX_HEADS * IDX_HD // LANES):
        sl = slice(p * LANES, (p + 1) * LANES)
        iqo_ref[:, sl] = (_rope64(iq_ref[:, sl], c64, s64, lower) * (IDX_HD ** -0.5)).astype(BF16)
    sm = sm_ref[...]
    lane = lax.broadcasted_iota(jnp.int32, (rows, LANES), 1)
    ik = jnp.where(lane < IDX_HD, sm, 0.0)
    ik = ik * lax.rsqrt(jnp.sum(ik * ik, axis=-1, keepdims=True) / IDX_HD + EPS) * gik_ref[...]
    ik = _rope64(ik, c64, s64, lower)
    ik_ref[...] = ik[:, :IDX_HD]
    ikb_ref[...] = jnp.where(lane < IDX_HD, ik, pltpu.roll(ik, 64, 1)).astype(BF16)
    iw_ref[...] = pltpu.roll(sm, LANES - SM_IW, 1) * (IDX_HEADS ** -0.5)


def ab_features(z, tabs, gik, *, seq, tr):
    m = z.shape[0]
    tr = min(tr, m)
    c128, s128, c64, s64 = tabs
    if c128.shape[0] == 1:
        tab_spec = pl.BlockSpec((1, LANES), lambda i: (0, 0))
    else:
        nt = seq // tr
        tab_spec = pl.BlockSpec((tr, LANES), lambda i: (i % nt, 0))
    col = lambda w, off: pl.BlockSpec((tr, w), lambda i: (i, off // w))
    row = lambda w: pl.BlockSpec((tr, w), lambda i: (i, 0))
    gik_pad = jnp.zeros((1, LANES), F32).at[0, :IDX_HD].set(gik)
    return pl.pallas_call(
        _ab_feat_kernel,
        grid=(m // tr,),
        in_specs=[col(1024, AB_DQ), col(256, AB_DK), col(256, AB_DV), col(512, AB_IQ), col(128, AB_SMALL),
                  tab_spec, tab_spec, tab_spec, tab_spec, pl.BlockSpec((1, LANES), lambda i: (0, 0))],
        out_specs=[row(1024), row(256), row(256), row(256), row(512), row(IDX_HD), row(128), row(128)],
        out_shape=[jax.ShapeDtypeStruct((m, 1024), BF16),
                   jax.ShapeDtypeStruct((m, 256), F32),
                   jax.ShapeDtypeStruct((m, 256), BF16),
                   jax.ShapeDtypeStruct((m, 256), BF16),
                   jax.ShapeDtypeStruct((m, 512), BF16),
                   jax.ShapeDtypeStruct((m, IDX_HD), F32),
                   jax.ShapeDtypeStruct((m, 128), BF16),
                   jax.ShapeDtypeStruct((m, 128), F32)],
        compiler_params=_cparams(("parallel",)),
        name="ab_features",
    )(z, z, z, z, z, c128, s128, c64, s64, gik_pad)


def _c_feat_kernel(q_ref, k_ref, v_ref, c64_ref, s64_ref, qo_ref, ko_ref, kb_ref, kbs_ref, vb_ref, vbs_ref):
    rows = q_ref.shape[0]
    c64, s64 = c64_ref[...], s64_ref[...]
    lower = _lower32_mask(rows)
    for p in range(SWA_HEADS * SWA_HD // LANES):
        sl = slice(p * LANES, (p + 1) * LANES)
        qo_ref[:, sl] = (_rope64(q_ref[:, sl], c64, s64, lower) * (SWA_HD ** -0.5)).astype(BF16)
    for p in range(SWA_KV_HEADS * SWA_HD // LANES):
        sl = slice(p * LANES, (p + 1) * LANES)
        kr = _rope64(k_ref[:, sl], c64, s64, lower)
        v = v_ref[:, sl]
        ko_ref[:, sl] = kr
        kb_ref[:, sl] = kr.astype(BF16)
        kbs_ref[:, sl] = pltpu.roll(kr, 64, 1).astype(BF16)
        vb_ref[:, sl] = v.astype(BF16)
        vbs_ref[:, sl] = pltpu.roll(v, 64, 1).astype(BF16)


def c_features(z, tabs, *, seq, tr):
    m = z.shape[0]
    tr = min(tr, m)
    c64, s64 = tabs
    if c64.shape[0] == 1:
        tab_spec = pl.BlockSpec((1, LANES), lambda i: (0, 0))
    else:
        nt = seq // tr
        tab_spec = pl.BlockSpec((tr, LANES), lambda i: (i % nt, 0))
    col = lambda w, off: pl.BlockSpec((tr, w), lambda i: (i, off // w))
    row = lambda w: pl.BlockSpec((tr, w), lambda i: (i, 0))
    return pl.pallas_call(
        _c_feat_kernel,
        grid=(m // tr,),
        in_specs=[col(2048, C_Q), col(256, C_K), col(256, C_V), tab_spec, tab_spec],
        out_specs=[row(2048), row(256), row(256), row(256), row(256), row(256)],
        out_shape=[jax.ShapeDtypeStruct((m, 2048), BF16),
                   jax.ShapeDtypeStruct((m, 256), F32),
                   jax.ShapeDtypeStruct((m, 256), BF16),
                   jax.ShapeDtypeStruct((m, 256), BF16),
                   jax.ShapeDtypeStruct((m, 256), BF16),
                   jax.ShapeDtypeStruct((m, 256), BF16)],
        compiler_params=_cparams(("parallel",)),
        name="c_features",
    )(z, z, z, c64, s64)


def _dot_nt(a, b, **kw):
    return lax.dot_general(a, b, (((1,), (1,)), ((), ())), preferred_element_type=F32, **kw)


def _dot_tn(a, b, **kw):
    return lax.dot_general(a, b, (((0,), (0,)), ((), ())), preferred_element_type=F32, **kw)


_HI = lax.Precision.HIGHEST


def _log_decay(sm, wgate, bgate):
    pre = jnp.dot(sm, wgate, preferred_element_type=F32, precision=_HI) + bgate
    return (jnp.minimum(pre, 0.0) - jnp.log1p(jnp.exp(-jnp.abs(pre)))) / GLA_GATE_TAU


def _gla_gate_out(o, r, gain):
    g = o * lax.rsqrt(jnp.mean(o * o, axis=-1, keepdims=True) + EPS) * gain
    return g * (r * jax.nn.sigmoid(r))


def _gla_prompt_kernel(gv_ref, gr_ref, gq_ref, gk_ref, sm_ref, wgate_ref, bgate_ref, gain_ref,
                       y_ref, st_ref, s_ref, b_ref, *, n_seq, n_chunks):
    c = pl.program_id(1)

    @pl.when(c == 0)
    def _():
        s_ref[...] = jnp.zeros_like(s_ref)

    tc = n_chunks * GLA_CHUNK
    ri = lax.broadcasted_iota(jnp.int32, (tc, tc), 0)
    ci = lax.broadcasted_iota(jnp.int32, (tc, tc), 1)
    tril = jnp.where(jnp.logical_and(ri // GLA_CHUNK == ci // GLA_CHUNK, ri >= ci), 1.0, 0.0).astype(F32)
    causal = (lax.broadcasted_iota(jnp.int32, (GLA_CHUNK, GLA_CHUNK), 0)
              >= lax.broadcasted_iota(jnp.int32, (GLA_CHUNK, GLA_CHUNK), 1))
    gain = gain_ref[...]
    for s in range(n_seq):
        la = _log_decay(sm_ref[s], wgate_ref[...], bgate_ref[...])
        b_ref[s] = jnp.dot(tril, la, preferred_element_type=F32, precision=_HI)
    for n in range(n_chunks):
        rows = slice(n * GLA_CHUNK, (n + 1) * GLA_CHUNK)
        for s in range(n_seq):
            b = b_ref[s, rows, :]
            b_end = b_ref[s, (n + 1) * GLA_CHUNK - 1:(n + 1) * GLA_CHUNK, :]
            k = gk_ref[s, rows, :]
            q_in = gq_ref[s, rows, :] * (GLA_DK ** -0.5) * jnp.exp(b)
            k_in = k * jnp.exp(-b)
            k_end = k * jnp.exp(b_end - b)
            decay = jnp.exp(b_end)
            for h in range(GLA_HEADS):
                dk = slice(h * GLA_DK, (h + 1) * GLA_DK)
                dv = slice(h * GLA_DV, (h + 1) * GLA_DV)
                v = gv_ref[s, rows, dv].astype(BF16)
                st = s_ref[s, h]
                qh = q_in[:, dk].astype(BF16)
                att = jnp.where(causal, _dot_nt(qh, k_in[:, dk].astype(BF16)), 0.0)
                o = _dot_nt(qh, st.astype(BF16)) + jnp.dot(att.astype(BF16), v, preferred_element_type=F32)
                s_ref[s, h] = st * decay[:, dk] + _dot_tn(v, k_end[:, dk].astype(BF16))
                y_ref[s, rows, dv] = _gla_gate_out(o, gr_ref[s, rows, dv], gain)

    @pl.when(c == pl.num_programs(1) - 1)
    def _():
        st_ref[...] = s_ref[...]


def gla_prompt(z, wgate, bgate, gain, *, batch, seq, tc, n_seq):
    m = z.shape[0]
    z3 = z.reshape(batch, seq, z.shape[1])
    col = lambda w, off: pl.BlockSpec((n_seq, tc, w), lambda b, c: (b, c, off // w))
    const = lambda shape: pl.BlockSpec(shape, lambda b, c: (0,) * len(shape))
    y, st = pl.pallas_call(
        functools.partial(_gla_prompt_kernel, n_seq=n_seq, n_chunks=tc // GLA_CHUNK),
        grid=(batch // n_seq, seq // tc),
        in_specs=[col(1024, AB_GV), col(1024, AB_GR), col(512, AB_GQ), col(512, AB_GK), col(128, AB_SMALL),
                  const((LANES, GLA_HEADS * GLA_DK)), const((1, GLA_HEADS * GLA_DK)), const((1, GLA_DV))],
        out_specs=[pl.BlockSpec((n_seq, tc, GLA_HEADS * GLA_DV), lambda b, c: (b, c, 0)),
                   pl.BlockSpec((n_seq, GLA_HEADS, GLA_DV, GLA_DK), lambda b, c: (b, 0, 0, 0))],
        out_shape=[jax.ShapeDtypeStruct((batch, seq, GLA_HEADS * GLA_DV), F32),
                   jax.ShapeDtypeStruct((batch, GLA_HEADS, GLA_DV, GLA_DK), F32)],
        scratch_shapes=[pltpu.VMEM((n_seq, GLA_HEADS, GLA_DV, GLA_DK), F32),
                        pltpu.VMEM((n_seq, tc, GLA_HEADS * GLA_DK), F32)],
        compiler_params=_cparams(("parallel", "arbitrary")),
        name="gla_prompt",
    )(z3, z3, z3, z3, z3, wgate, bgate, gain)
    return y.reshape(m, GLA_HEADS * GLA_DV), st


INT_MIN = -2 ** 31


def _order_key(score):
    score = jnp.where(score == 0.0, 0.0, score)
    bits = lax.bitcast_convert_type(score, jnp.int32)
    return jnp.where(bits < 0, bits ^ jnp.int32(0x7FFFFFFF), bits)


def _lane_total(x):
    return jnp.dot(x.astype(BF16), jnp.ones((LANES, LANES), BF16), preferred_element_type=F32)


ROW_SUB = 128


def _count_blocks(key_ref, rows, n_blocks, pred):
    acc = jnp.zeros((ROW_SUB, LANES), F32)
    for c in range(n_blocks):
        acc = acc + jnp.where(pred(key_ref[rows, c * LANES:(c + 1) * LANES]), 1.0, 0.0)
    return acc


def _kth_largest_key(key_ref, t_ref, n_blocks, k):
    n_sub = key_ref.shape[0] // ROW_SUB
    t_ref[...] = jnp.full(t_ref.shape, INT_MIN, jnp.int32)

    def body(it, carry):
        step = lax.shift_left(jnp.int32(1), 31 - it)
        cands, accs = [], []
        for rb in range(n_sub):
            rows = slice(rb * ROW_SUB, (rb + 1) * ROW_SUB)
            cand = t_ref[rows, :] + step
            cands.append(cand)
            accs.append(_count_blocks(key_ref, rows, n_blocks, lambda kc, cand=cand: kc >= cand))
        total = _lane_total(jnp.concatenate(accs, axis=0))
        for rb in range(n_sub):
            rows = slice(rb * ROW_SUB, (rb + 1) * ROW_SUB)
            t_ref[rows, :] = jnp.where(total[rows, :] >= k, cands[rb], t_ref[rows, :])
        return carry

    lax.fori_loop(0, 32, body, 0)


def _select_topk(key_ref, t_ref, n_blocks, k, write_fn):
    _kth_largest_key(key_ref, t_ref, n_blocks, k)
    ri = lax.broadcasted_iota(jnp.int32, (LANES, LANES), 0)
    ci = lax.broadcasted_iota(jnp.int32, (LANES, LANES), 1)
    before = jnp.where(ri < ci, 1.0, 0.0).astype(BF16)
    for rb in range(key_ref.shape[0] // ROW_SUB):
        rows = slice(rb * ROW_SUB, (rb + 1) * ROW_SUB)
        t = t_ref[rows, :]
        need = k - _lane_total(_count_blocks(key_ref, rows, n_blocks, lambda kc: kc > t))
        run = jnp.zeros((ROW_SUB, LANES), F32)
        for c in range(n_blocks):
            kc = key_ref[rows, c * LANES:(c + 1) * LANES]
            eq = jnp.where(kc == t, 1.0, 0.0)
            rank = jnp.dot(eq.astype(BF16), before, preferred_element_type=F32) + run
            take = jnp.where(kc > t, 1.0, jnp.where(rank < need, eq, 0.0))
            write_fn(rows, c, take > 0.0)
            run = run + _lane_total(eq)


def _half_mask(rows, upper):
    lane = lax.broadcasted_iota(jnp.int32, (rows, LANES), 1)
    return (lane >= 64) if upper else (lane < 64)


DSA_STRATUM = 512
KEY_CHUNK = 512


def _dsa_select_prompt_kernel(iq_ref, iw_ref, ikb_ref, bias_ref, key_ref, t_ref, *, row0, n_keys, topk):
    rows = iq_ref.shape[0]
    n_blocks = n_keys // LANES
    lane = lax.broadcasted_iota(jnp.int32, (ROW_SUB, LANES), 1)
    sub = lax.broadcasted_iota(jnp.int32, (ROW_SUB, LANES), 0)

    iw = iw_ref[...]
    qh = []
    for h in range(IDX_HEADS):
        pair = iq_ref[:, (h // 2) * LANES:(h // 2 + 1) * LANES]
        qh.append(jnp.where(_half_mask(rows, h % 2 == 1), pair, jnp.zeros_like(pair)))
    for kc in range(n_keys // KEY_CHUNK):
        ik = ikb_ref[kc * KEY_CHUNK:(kc + 1) * KEY_CHUNK, :]
        score = jnp.zeros((rows, KEY_CHUNK), F32)
        for h in range(IDX_HEADS):
            score = score + jnp.maximum(_dot_nt(qh[h], ik), 0.0) * iw[:, h:h + 1]
        for rb in range(rows // ROW_SUB):
            for cb in range(KEY_CHUNK // LANES):
                c = kc * (KEY_CHUNK // LANES) + cb
                causal = (c * LANES + lane) <= (row0 + rb * ROW_SUB + sub)
                part = score[rb * ROW_SUB:(rb + 1) * ROW_SUB, cb * LANES:(cb + 1) * LANES]
                key_ref[rb * ROW_SUB:(rb + 1) * ROW_SUB, c * LANES:(c + 1) * LANES] = _order_key(
                    jnp.where(causal, part, NEG_INF))

    def write(rws, c, sel):
        causal = (c * LANES + lane) <= (row0 + rws.start + sub)
        bias_ref[rws, c * LANES:(c + 1) * LANES] = jnp.where(jnp.logical_and(sel, causal), 0.0, NEG_INF)

    _select_topk(key_ref, t_ref, n_blocks, topk, write)


def _dsa_attend_prompt_kernel(q_ref, bias_ref, kb_ref, vb_ref, y_ref):
    bias = bias_ref[...]
    group = DSA_HEADS // DSA_KV_HEADS
    for kv in range(DSA_KV_HEADS):
        kvs = slice(kv * DSA_HD, (kv + 1) * DSA_HD)
        q4 = jnp.concatenate([q_ref[:, (kv * group + g) * DSA_HD:(kv * group + g + 1) * DSA_HD] for g in range(group)],
                             axis=0)
        s4 = _dot_nt(q4, kb_ref[:, kvs])
        ps, dens = [], []
        for g in range(group):
            s = s4[g * Q_BLOCK:(g + 1) * Q_BLOCK, :] + bias
            p = jnp.exp(s - jnp.max(s, axis=-1, keepdims=True))
            dens.append(jnp.sum(p, axis=-1, keepdims=True))
            ps.append(p.astype(BF16))
        o4 = jnp.dot(jnp.concatenate(ps, axis=0), vb_ref[:, kvs], preferred_element_type=F32)
        for g in range(group):
            h = kv * group + g
            y_ref[:, h * DSA_HD:(h + 1) * DSA_HD] = o4[g * Q_BLOCK:(g + 1) * Q_BLOCK, :] / dens[g]


def dsa_prompt(q, iq, iw, kb, vb, ikb, *, batch, seq):
    topk = min(DSA_TOPK_MAX, seq // 4)
    as3 = lambda a: a.reshape(batch, seq, a.shape[-1])
    q, iq, iw, kb, vb, ikb = (as3(a) for a in (q, iq, iw, kb, vb, ikb))
    nsub = DSA_STRATUM // Q_BLOCK
    outs = []
    for r in range(seq // DSA_STRATUM):
        n_keys = (r + 1) * DSA_STRATUM
        strat = lambda w: pl.BlockSpec((None, DSA_STRATUM, w), lambda b: (b, r, 0))
        keys = lambda w: pl.BlockSpec((None, n_keys, w), lambda b: (b, 0, 0))
        bias = pl.pallas_call(
            functools.partial(_dsa_select_prompt_kernel, row0=r * DSA_STRATUM, n_keys=n_keys, topk=topk),
            grid=(batch,),
            in_specs=[strat(512), strat(128), keys(128)],
            out_specs=pl.BlockSpec((None, DSA_STRATUM, n_keys), lambda b: (b, 0, 0)),
            out_shape=jax.ShapeDtypeStruct((batch, DSA_STRATUM, n_keys), F32),
            scratch_shapes=[pltpu.VMEM((DSA_STRATUM, n_keys), jnp.int32), pltpu.VMEM((DSA_STRATUM, LANES), jnp.int32)],
            compiler_params=_cparams(("parallel",)),
            name="dsa_select_prompt",
        )(iq, iw, ikb)
        qrow = lambda w: pl.BlockSpec((None, Q_BLOCK, w), lambda b, i: (b, r * nsub + i, 0))
        keys2 = lambda w: pl.BlockSpec((None, n_keys, w), lambda b, i: (b, 0, 0))
        outs.append(pl.pallas_call(
            _dsa_attend_prompt_kernel,
            grid=(batch, nsub),
            in_specs=[qrow(1024), pl.BlockSpec((None, Q_BLOCK, n_keys), lambda b, i: (b, i, 0)), keys2(256), keys2(256)],
            out_specs=pl.BlockSpec((None, Q_BLOCK, DSA_HEADS * DSA_HD), lambda b, i: (b, i, 0)),
            out_shape=jax.ShapeDtypeStruct((batch, DSA_STRATUM, DSA_HEADS * DSA_HD), F32),
            compiler_params=_cparams(("parallel", "arbitrary")),
            name="dsa_attend_prompt",
        )(q, bias, kb, vb))
    return jnp.concatenate(outs, axis=1).reshape(batch * seq, DSA_HEADS * DSA_HD)


def _swa_head_plan(h):
    group = SWA_HEADS // SWA_KV_HEADS
    kv = h // group
    return h // 2, h % 2, kv // 2, (kv % 2) != (h % 2)


def _swa_prompt_kernel(sink_ref, q_ref, kp_ref, kc_ref, kps_ref, kcs_ref, vp_ref, vc_ref, vps_ref, vcs_ref, y_ref):
    i = pl.program_id(1)
    r = lax.broadcasted_iota(jnp.int32, (Q_BLOCK, 2 * Q_BLOCK), 0)
    c = lax.broadcasted_iota(jnp.int32, (Q_BLOCK, 2 * Q_BLOCK), 1)
    rel = Q_BLOCK + r - c
    ok = (rel >= 0) & (rel <= WINDOW) & ((i - 1) * Q_BLOCK + c >= 0)
    bias = jnp.where(ok, 0.0, NEG_INF)
    keys = (jnp.concatenate([kp_ref[...], kc_ref[...]], axis=0), jnp.concatenate([kps_ref[...], kcs_ref[...]], axis=0))
    vals = (jnp.concatenate([vp_ref[...], vc_ref[...]], axis=0), jnp.concatenate([vps_ref[...], vcs_ref[...]], axis=0))
    lower = _half_mask(Q_BLOCK, False)
    for p in range(SWA_HEADS // 2):
        qpair = q_ref[:, p * LANES:(p + 1) * LANES]
        outs = []
        for h in (2 * p, 2 * p + 1):
            _, half, ks, swapped = _swa_head_plan(h)
            qh = jnp.where(_half_mask(Q_BLOCK, half == 1), qpair, jnp.zeros_like(qpair))
            kk = keys[int(swapped)][:, ks * LANES:(ks + 1) * LANES]
            vv = vals[int(swapped)][:, ks * LANES:(ks + 1) * LANES]
            s = _dot_nt(qh, kk) + bias
            sink = sink_ref[h]
            mx = jnp.maximum(jnp.max(s, axis=-1, keepdims=True), sink)
            pr = jnp.exp(s - mx)
            den = jnp.sum(pr, axis=-1, keepdims=True) + jnp.exp(sink - mx)
            outs.append(jnp.dot(pr.astype(BF16), vv, preferred_element_type=F32) / den)
        y_ref[:, p * LANES:(p + 1) * LANES] = jnp.where(lower, outs[0], outs[1])


def swa_prompt(q, kb, kbs, vb, vbs, sinks, *, batch, seq):
    m = q.shape[0]
    nq = seq // Q_BLOCK
    cur = pl.BlockSpec((Q_BLOCK, 256), lambda b, i, s: (b * nq + i, 0))
    prev = pl.BlockSpec((Q_BLOCK, 256), lambda b, i, s: (b * nq + jnp.maximum(i - 1, 0), 0))
    qspec = pl.BlockSpec((Q_BLOCK, 2048), lambda b, i, s: (b * nq + i, 0))
    return pl.pallas_call(
        _swa_prompt_kernel,
        grid_spec=pltpu.PrefetchScalarGridSpec(
            num_scalar_prefetch=1,
            grid=(batch, nq),
            in_specs=[qspec, prev, cur, prev, cur, prev, cur, prev, cur],
            out_specs=qspec,
        ),
        out_shape=jax.ShapeDtypeStruct((m, SWA_HEADS * SWA_HD), F32),
        compiler_params=_cparams(("parallel", "arbitrary")),
        name="swa_prompt",
    )(sinks, q, kb, kb, kbs, kbs, vb, vb, vbs, vbs)


def _mem_prompt_kernel(x_ref, g_ref, wq_ref, kv_ref, wo_ref, o_ref):
    x = x_ref[...]
    h = _rms(x, g_ref[...]).astype(BF16)
    q = (jnp.dot(h, wq_ref[...], preferred_element_type=F32) * (MEM_HD ** -0.5)).astype(BF16)
    width = MEM_HEADS * MEM_HD
    outs = []
    for hd in range(MEM_HEADS):
        sl = slice(hd * MEM_HD, (hd + 1) * MEM_HD)
        k = kv_ref[:, sl].astype(BF16)
        v = kv_ref[:, width + hd * MEM_HD:width + (hd + 1) * MEM_HD].astype(BF16)
        s = _dot_nt(q[:, sl], k)
        p = jnp.exp(s - jnp.max(s, axis=-1, keepdims=True))
        o = jnp.dot(p.astype(BF16), v, preferred_element_type=F32) / jnp.sum(p, axis=-1, keepdims=True)
        outs.append(o.astype(BF16))
    o_ref[...] = x + jnp.dot(jnp.concatenate(outs, axis=-1), wo_ref[...], preferred_element_type=F32)


def mem_attn_prompt(x, g, wq, kv, wo, *, batch, seq, tq):
    m = x.shape[0]
    nq = seq // tq
    width = MEM_HEADS * MEM_HD
    xspec = pl.BlockSpec((tq, D_MODEL), lambda b, i: (b * nq + i, 0))
    return pl.pallas_call(
        _mem_prompt_kernel,
        grid=(batch, nq),
        in_specs=[xspec,
                  pl.BlockSpec((1, D_MODEL), lambda b, i: (0, 0)),
                  pl.BlockSpec((D_MODEL, width), lambda b, i: (0, 0)),
                  pl.BlockSpec((N_MEM, 2 * width), lambda b, i: (b, 0)),
                  pl.BlockSpec((width, D_MODEL), lambda b, i: (0, 0))],
        out_specs=xspec,
        out_shape=jax.ShapeDtypeStruct((m, D_MODEL), F32),
        compiler_params=_cparams(("parallel", "arbitrary")),
        name="mem_attn_prompt",
    )(x, g.reshape(1, D_MODEL), wq, kv, wo)


def _gla_prep_kernel(gq_ref, gk_ref, sm_ref, wgate_ref, bgate_ref, o_ref):
    w = GLA_HEADS * GLA_DK
    la = _log_decay(sm_ref[...], wgate_ref[...], bgate_ref[...])
    o_ref[:, 0:w] = jnp.exp(la)
    o_ref[:, w:2 * w] = gk_ref[...]
    o_ref[:, 2 * w:3 * w] = gq_ref[...] * (GLA_DK ** -0.5)
    o_ref[:, 3 * w:4 * w] = jnp.zeros((gq_ref.shape[0], w), F32)


def gla_sample_prep(z, wgate, bgate):
    m = z.shape[0]
    w = GLA_HEADS * GLA_DK
    col = lambda wd, off: pl.BlockSpec((m, wd), lambda i: (0, off // wd))
    return pl.pallas_call(
        _gla_prep_kernel,
        grid=(1,),
        in_specs=[col(512, AB_GQ), col(512, AB_GK), col(128, AB_SMALL),
                  pl.BlockSpec((LANES, w), lambda i: (0, 0)), pl.BlockSpec((1, w), lambda i: (0, 0))],
        out_specs=pl.BlockSpec((m, 4 * w), lambda i: (0, 0)),
        out_shape=jax.ShapeDtypeStruct((m, 4 * w), F32),
        compiler_params=_cparams(("arbitrary",)),
        name="gla_sample_prep",
    )(z, z, z, wgate, bgate)


def _gla_step_kernel(p_ref, gv_ref, gr_ref, s_ref, gain_ref, so_ref, y_ref, *, bs):
    gain = gain_ref[...]
    for s in range(bs):
        xt = p_ref[s].T
        for h in range(GLA_HEADS):
            dv = slice(h * GLA_DV, (h + 1) * GLA_DV)
            st = s_ref[s, h] * xt[:, h:h + 1] + xt[:, GLA_HEADS + h:GLA_HEADS + h + 1] * gv_ref[s:s + 1, dv]
            so_ref[s, h] = st
            o = jnp.sum(xt[:, 2 * GLA_HEADS + h:2 * GLA_HEADS + h + 1] * st, axis=0, keepdims=True)
            y_ref[s:s + 1, dv] = _gla_gate_out(o, gr_ref[s:s + 1, dv], gain)


def gla_sample(p, z, state, gain, *, bs):
    m = z.shape[0]
    col = lambda w, off: pl.BlockSpec((bs, w), lambda i: (i, off // w))
    sspec = pl.BlockSpec((bs, GLA_HEADS, GLA_DK, GLA_DV), lambda i: (i, 0, 0, 0))
    return pl.pallas_call(
        functools.partial(_gla_step_kernel, bs=bs),
        grid=(m // bs,),
        in_specs=[pl.BlockSpec((bs, 16, GLA_DK), lambda i: (i, 0, 0)), col(1024, AB_GV), col(1024, AB_GR), sspec,
                  pl.BlockSpec((1, GLA_DV), lambda i: (0, 0))],
        out_specs=[sspec, pl.BlockSpec((bs, GLA_HEADS * GLA_DV), lambda i: (i, 0))],
        out_shape=[jax.ShapeDtypeStruct(state.shape, F32), jax.ShapeDtypeStruct((m, GLA_HEADS * GLA_DV), F32)],
        compiler_params=_cparams(("parallel",)),
        name="gla_sample",
    )(p, z, z, state, gain)


def _dsa_scores_kernel(pt_ref, iq_ref, iw_ref, *refs, pg):
    page_refs, o_ref = refs[:pg], refs[pg]
    q8 = iq_ref[0]
    iw = iw_ref[0]
    for j in range(pg):
        dots = jnp.dot(q8, page_refs[j][0].astype(BF16), preferred_element_type=F32)
        o_ref[0, j:j + 1, :] = jnp.sum(jnp.maximum(dots, 0.0) * iw, axis=0, keepdims=True)


def dsa_sample_scores(page_table, iq, iw, pool_ik_t, page_base, *, pg):
    m, n_pages = page_table.shape
    pool_ik = pool_ik_t
    page_spec = lambda j: pl.BlockSpec((1, IDX_HD, PAGE_SIZE),
                                       lambda b, g, pt: (page_base + pt[b, g * pg + j], 0, 0))
    return pl.pallas_call(
        functools.partial(_dsa_scores_kernel, pg=pg),
        grid_spec=pltpu.PrefetchScalarGridSpec(
            num_scalar_prefetch=1,
            grid=(m, n_pages // pg),
            in_specs=[pl.BlockSpec((1, IDX_HEADS, IDX_HD), lambda b, g, pt: (b, 0, 0)),
                      pl.BlockSpec((1, IDX_HEADS, 1), lambda b, g, pt: (b, 0, 0))]
                     + [page_spec(j) for j in range(pg)],
            out_specs=pl.BlockSpec((1, pg, PAGE_SIZE), lambda b, g, pt: (b, g, 0)),
        ),
        out_shape=jax.ShapeDtypeStruct((m, n_pages, PAGE_SIZE), F32),
        compiler_params=_cparams(("parallel", "arbitrary")),
        name="dsa_sample_scores",
    )(page_table, iq, iw, *([pool_ik] * pg))


def _dsa_select_sample_kernel(sc_ref, iq_ref, iw_ref, ikb_ref, bias_ref, bnew_ref, key_ref, t_ref, *, n_past, topk):
    rows = sc_ref.shape[0]
    n_blocks = n_past // LANES
    for c in range(n_blocks):
        sl = slice(c * LANES, (c + 1) * LANES)
        key_ref[:, sl] = _order_key(sc_ref[:, sl])
    ik = ikb_ref[...].astype(F32)
    iw = iw_ref[...]
    s_new = jnp.zeros((rows, 1), F32)
    for h in range(IDX_HEADS):
        pair = iq_ref[:, (h // 2) * LANES:(h // 2 + 1) * LANES].astype(F32)
        qh = jnp.where(_half_mask(rows, h % 2 == 1), pair, 0.0)
        s_new = s_new + jnp.maximum(jnp.sum(qh * ik, axis=-1, keepdims=True), 0.0) * iw[:, h:h + 1]
    lane = lax.broadcasted_iota(jnp.int32, (rows, LANES), 1)
    key_ref[:, n_past:n_past + LANES] = _order_key(jnp.where(lane == 0, s_new, -jnp.inf))

    ri = lax.broadcasted_iota(jnp.int32, (LANES, DSA_KV_HEADS * LANES), 0)
    ci = lax.broadcasted_iota(jnp.int32, (LANES, DSA_KV_HEADS * LANES), 1)
    spread = jnp.where(ci // DSA_KV_HEADS == ri, 1.0, 0.0).astype(BF16)

    def write(rws, c, sel):
        if c == n_blocks:
            bnew_ref[rws, :] = jnp.where(sel, 0.0, NEG_INF)
        else:
            wide = jnp.dot(jnp.where(sel, 1.0, 0.0).astype(BF16), spread, preferred_element_type=F32)
            w = DSA_KV_HEADS * LANES
            bias_ref[rws, c * w:(c + 1) * w] = jnp.where(wide > 0.5, 0.0, NEG_INF)

    _select_topk(key_ref, t_ref, n_blocks + 1, topk, write)


def dsa_sample_select(scores, iq, iw, ikb, *, topk):
    m, n_past = scores.shape
    full = lambda a: pl.BlockSpec(a.shape, lambda i: (0,) * a.ndim)
    wide = DSA_KV_HEADS * n_past
    return pl.pallas_call(
        functools.partial(_dsa_select_sample_kernel, n_past=n_past, topk=topk),
        grid=(1,),
        in_specs=[full(scores), full(iq), full(iw), full(ikb)],
        out_specs=[pl.BlockSpec((m, wide), lambda i: (0, 0)), pl.BlockSpec((m, LANES), lambda i: (0, 0))],
        out_shape=[jax.ShapeDtypeStruct((m, wide), F32), jax.ShapeDtypeStruct((m, LANES), F32)],
        scratch_shapes=[pltpu.VMEM((m, n_past + LANES), jnp.int32), pltpu.VMEM((m, LANES), jnp.int32)],
        compiler_params=_cparams(("arbitrary",)),
        name="dsa_sample_select",
    )(scores, iq, iw, ikb)


def _dsa_attend_sample_kernel(pt_ref, q_ref, bias_ref, bnew_ref, kn_ref, vn_ref, pk_hbm, pv_hbm, y_ref,
                              kbuf, vbuf, sem, *, page_base, n_pages, chunk):
    b = pl.program_id(0)
    wide = DSA_KV_HEADS * PAGE_SIZE
    n_chunks = n_pages // chunk
    group = DSA_HEADS // DSA_KV_HEADS

    def page_copies(row, c, slot):
        out = []
        for j in range(chunk):
            src = pl.ds(pl.multiple_of((page_base + pt_ref[row, c * chunk + j]) * wide, wide), wide)
            dst = pl.ds(j * wide, wide)
            out.append(pltpu.make_async_copy(pk_hbm.at[src, :], kbuf.at[slot, dst, :], sem.at[0, slot]))
            out.append(pltpu.make_async_copy(pv_hbm.at[src, :], vbuf.at[slot, dst, :], sem.at[1, slot]))
        return out

    @pl.when(b == 0)
    def _():
        for cp in page_copies(0, 0, 0):
            cp.start()

    q8 = q_ref[0]
    first = lax.broadcasted_iota(jnp.int32, (DSA_HEADS, DSA_HD), 0) < group
    hrow = lax.broadcasted_iota(jnp.int32, (DSA_HEADS, wide), 0)
    col = lax.broadcasted_iota(jnp.int32, (DSA_HEADS, wide), 1)
    own = (col % DSA_KV_HEADS) == (hrow // group)
    m_run = jnp.full((DSA_HEADS, DSA_HD), NEG_INF, F32)
    l_run = jnp.zeros((DSA_HEADS, DSA_HD), F32)
    acc = jnp.zeros((DSA_HEADS, DSA_HD), F32)
    for c in range(n_chunks):
        slot = c % 2
        if c + 1 < n_chunks:
            for cp in page_copies(b, c + 1, 1 - slot):
                cp.start()
        else:
            @pl.when(b + 1 < pl.num_programs(0))
            def _():
                for cp in page_copies(b + 1, 0, 1 - slot):
                    cp.start()
        for cp in page_copies(b, c, slot):
            cp.wait()
        ss, oks = [], []
        for j in range(chunk):
            ok = jnp.logical_and(own, bias_ref[0, c * chunk + j:c * chunk + j + 1, :] == 0.0)
            kp = kbuf[slot, j * wide:(j + 1) * wide, :].astype(BF16)
            ss.append(jnp.where(ok, _dot_nt(q8, kp), NEG_INF))
            oks.append(ok)
        mx = ss[0]
        for s in ss[1:]:
            mx = jnp.maximum(mx, s)
        m_new = jnp.maximum(m_run, jnp.max(mx, axis=-1, keepdims=True))
        psum = jnp.zeros((DSA_HEADS, wide), F32)
        pv = jnp.zeros((DSA_HEADS, DSA_HD), F32)
        for j in range(chunk):
            p = jnp.where(oks[j], jnp.exp(ss[j] - m_new[:, 0:1]), 0.0)
            psum = psum + p
            vp = vbuf[slot, j * wide:(j + 1) * wide, :].astype(BF16)
            pv = pv + jnp.dot(p.astype(BF16), vp, preferred_element_type=F32)
        alpha = jnp.exp(m_run - m_new)
        l_run = alpha * l_run + jnp.sum(psum, axis=-1, keepdims=True)
        acc = alpha * acc + pv
        m_run = m_new

    kn = kn_ref[0].astype(BF16).astype(F32)
    vn = vn_ref[0].astype(BF16).astype(F32)
    bn = bnew_ref[0][:, 0:1]
    s_new = jnp.sum(q8.astype(F32) * jnp.where(first, kn[:, :DSA_HD], kn[:, DSA_HD:]), axis=-1, keepdims=True) + bn
    m_new = jnp.maximum(m_run, s_new)
    p_new = jnp.where(bn == 0.0, jnp.exp(s_new - m_new), 0.0)
    alpha = jnp.exp(m_run - m_new)
    acc = alpha * acc + p_new * jnp.where(first, vn[:, :DSA_HD], vn[:, DSA_HD:])
    y_ref[0] = acc / (alpha * l_run + p_new)


def dsa_sample_attend(page_table, q, bias, bnew, kn, vn, pool_k, pool_v, page_base, *, chunk):
    m, n_pages = page_table.shape
    wide = DSA_KV_HEADS * PAGE_SIZE
    assert n_pages % chunk == 0 and (n_pages // chunk) % 2 == 0
    per_b = lambda shape: pl.BlockSpec((1,) + shape, lambda b, pt: (b, 0, 0))
    hbm = pl.BlockSpec(memory_space=pl.ANY)
    return pl.pallas_call(
        functools.partial(_dsa_attend_sample_kernel, page_base=page_base, n_pages=n_pages, chunk=chunk),
        grid_spec=pltpu.PrefetchScalarGridSpec(
            num_scalar_prefetch=1,
            grid=(m,),
            in_specs=[per_b((DSA_HEADS, DSA_HD)), per_b((n_pages, wide)),
                      per_b((1, LANES)), per_b((1, DSA_KV_HEADS * DSA_HD)), per_b((1, DSA_KV_HEADS * DSA_HD)),
                      hbm, hbm],
            out_specs=per_b((DSA_HEADS, DSA_HD)),
            scratch_shapes=[pltpu.VMEM((2, chunk * wide, DSA_HD), F32), pltpu.VMEM((2, chunk * wide, DSA_HD), F32),
                            pltpu.SemaphoreType.DMA((2, 2))],
        ),
        out_shape=jax.ShapeDtypeStruct((m, DSA_HEADS, DSA_HD), F32),
        compiler_params=_cparams(("arbitrary",)),
        name="dsa_sample_attend",
    )(page_table, q, bias, bnew, kn, vn, pool_k, pool_v)


def _swa_step_kernel(q_ref, kt_ref, vt_ref, kn_ref, vn_ref, sink_ref, y_ref, *, bs):
    group = SWA_HEADS // SWA_KV_HEADS
    pairs = [(s, kv) for s in range(bs) for kv in range(SWA_KV_HEADS)]
    qs, scs, news = [], [], []
    for s, kv in pairs:
        qv = q_ref[s, kv * group:(kv + 1) * group, :]
        kn = kn_ref[s, kv:kv + 1, :].astype(BF16).astype(F32)
        scs.append(jnp.dot(qv, kt_ref[s, kv].astype(BF16), preferred_element_type=F32))
        news.append(jnp.sum(qv.astype(F32) * kn, axis=-1, keepdims=True))
    sc = jnp.concatenate(scs, axis=0)
    s_new = jnp.concatenate(news, axis=0)
    sink = jnp.concatenate([sink_ref[...]] * bs, axis=0)
    mx = jnp.maximum(jnp.maximum(jnp.max(sc, axis=-1, keepdims=True), s_new), sink)
    p = jnp.exp(sc - mx)
    p_new = jnp.exp(s_new - mx)
    inv = 1.0 / (jnp.sum(p, axis=-1, keepdims=True) + p_new + jnp.exp(sink - mx))
    pb = p.astype(BF16)
    for n, (s, kv) in enumerate(pairs):
        rows = slice(n * group, (n + 1) * group)
        vn = vn_ref[s, kv:kv + 1, :].astype(BF16).astype(F32)
        o = _dot_nt(pb[rows, :], vt_ref[s, kv].astype(BF16)) + p_new[rows, :] * vn
        y_ref[s, kv * group:(kv + 1) * group, :] = o * inv[rows, :]


def swa_sample(q, kt, vt, kn, vn, sinks, *, bs):
    m = q.shape[0]
    cache = pl.BlockSpec((bs, SWA_KV_HEADS, SWA_HD, WINDOW), lambda i: (i, 0, 0, 0))
    new = pl.BlockSpec((bs, SWA_KV_HEADS, SWA_HD), lambda i: (i, 0, 0))
    return pl.pallas_call(
        functools.partial(_swa_step_kernel, bs=bs),
        grid=(m // bs,),
        in_specs=[pl.BlockSpec((bs, SWA_HEADS, SWA_HD), lambda i: (i, 0, 0)), cache, cache, new, new,
                  pl.BlockSpec((SWA_HEADS, 1), lambda i: (0, 0))],
        out_specs=pl.BlockSpec((bs, SWA_HEADS, SWA_HD), lambda i: (i, 0, 0)),
        out_shape=jax.ShapeDtypeStruct((m, SWA_HEADS, SWA_HD), F32),
        compiler_params=_cparams(("parallel",)),
        name="swa_sample",
    )(q, kt, vt, kn, vn, sinks.reshape(SWA_HEADS, 1))


def _mem_step_kernel(q_ref, k_ref, v_ref, y_ref, *, bs):
    rows = q_ref.shape[1]
    n = N_MEM * MEM_HEADS
    hrow = lax.broadcasted_iota(jnp.int32, (rows, n), 0)
    col = lax.broadcasted_iota(jnp.int32, (rows, n), 1)
    own = (col % MEM_HEADS) == (hrow % MEM_HEADS)
    for s in range(bs):
        q = (q_ref[s] * (MEM_HD ** -0.5)).astype(BF16)
        sc = jnp.where(own, _dot_nt(q, k_ref[s * n:(s + 1) * n, :].astype(BF16)), NEG_INF)
        p = jnp.where(own, jnp.exp(sc - jnp.max(sc, axis=-1, keepdims=True)), 0.0)
        o = jnp.dot(p.astype(BF16), v_ref[s * n:(s + 1) * n, :].astype(BF16), preferred_element_type=F32)
        y_ref[s] = o / jnp.sum(p, axis=-1, keepdims=True)


def mem_attn_sample(q, mk, mv, layer, *, bs):
    m, rows, _ = q.shape
    n = N_MEM * MEM_HEADS
    nb = m // bs
    cache = pl.BlockSpec((bs * n, MEM_HD), lambda i: (layer * nb + i, 0))
    return pl.pallas_call(
        functools.partial(_mem_step_kernel, bs=bs),
        grid=(nb,),
        in_specs=[pl.BlockSpec((bs, rows, MEM_HD), lambda i: (i, 0, 0)), cache, cache],
        out_specs=pl.BlockSpec((bs, rows, MEM_HD), lambda i: (i, 0, 0)),
        out_shape=jax.ShapeDtypeStruct((m, rows, MEM_HD), F32),
        compiler_params=_cparams(("parallel",)),
        name="mem_attn_sample",
    )(q, mk, mv)


TM_FFN, TF_FFN, SUB_FFN = 1024, 512, 512
TM_PROJ, TN_AB, TN_C, TN_MEM = 1024, 768, 512, 512
TM_OUT = 512
TR_FEAT = 512
TC_GLA, NSEQ_GLA = 256, 2
TQ_MEM = 512
BS_SAMPLE = 8
PG_SCORES = 32
PG_ATTEND = 16


def _prep_w_in_ab(w):
    sizes = (GLA_HEADS * GLA_DK, GLA_HEADS * GLA_DK, GLA_HEADS * GLA_DV, GLA_HEADS * GLA_DV, GLA_GATE_RANK,
             DSA_HEADS * DSA_HD, DSA_KV_HEADS * DSA_HD, DSA_KV_HEADS * DSA_HD, IDX_HEADS * IDX_HD, IDX_HEADS, IDX_HD)
    offs = np.cumsum((0,) + sizes)
    gq, gk, gv, gr, gd, dq, dk, dv, iq, iw, ik = [w[:, int(offs[j]):int(offs[j + 1])] for j in range(len(sizes))]
    pad = lambda n: jnp.zeros((w.shape[0], n), w.dtype)
    small = jnp.concatenate([ik, gd, iw, pad(LANES - IDX_HD - GLA_GATE_RANK - IDX_HEADS)], axis=1)
    out = jnp.concatenate([gv, gr, dq, gq, gk, iq, dk, dv, small, pad(AB_WIDTH - AB_SMALL - LANES)], axis=1)
    return out.astype(BF16)


def _prep_gate(w_up):
    return jnp.zeros((LANES, GLA_HEADS * GLA_DK), F32).at[SM_GD:SM_GD + GLA_GATE_RANK].set(w_up)


def kernel(x_prompt, x_sample, mem_prompt, cache_dsa_k, cache_dsa_v, cache_dsa_idx_k, state_gla, cache_swa_k, cache_swa_v, cache_mem_k, cache_mem_v, page_table, norm_ffn, w_ffn_gate, w_ffn_up, w_ffn_down, norm_mix, w_in_ab, w_gla_gate_up, b_gla_gate, gla_out_norm, idx_k_norm, w_out_ab, w_in_c, swa_sinks, w_out_c, norm_mem_q, norm_mem_src, w_mem_q, w_mem_kv, w_mem_o, final_norm):
    depth = norm_mix.shape[0]
    bp, seq, _ = x_prompt.shape
    bs = x_sample.shape[0]
    n_pool = cache_dsa_k.shape[1]
    gla_w = GLA_HEADS * GLA_DV

    w_ab = [_prep_w_in_ab(w_in_ab[i]) for i in range(w_in_ab.shape[0])]
    w_gate = [_prep_gate(w_gla_gate_up[i]) for i in range(w_in_ab.shape[0])]
    w_oab = w_out_ab.astype(BF16)
    w_c, w_oc = w_in_c.astype(BF16), w_out_c.astype(BF16)
    w_mq, w_mkv, w_mo = w_mem_q.astype(BF16), w_mem_kv.astype(BF16), w_mem_o.astype(BF16)

    def rope_tabs(pos):
        return _rope_tables(pos, DSA_HD) + _rope_tables(pos, IDX_HD)

    ffn_bf16 = {}

    def ffn_pair(x, layer, half, last, sample):
        fin = final_norm if last else None
        if sample:
            y, *ffn_bf16[layer, half] = ffn(x, norm_ffn[layer, half], w_ffn_gate, w_ffn_up, w_ffn_down, (layer, half),
                                            fin, tm=TM_FFN, tf=TF_FFN, sub=SUB_FFN)
            return y
        return ffn(x, norm_ffn[layer, half], *ffn_bf16[layer, half], None, fin, tm=TM_FFN, tf=TF_FFN, sub=SUB_FFN)

    def prompt_group():
        tabs_p = rope_tabs(jnp.arange(seq))
        x = x_prompt.reshape(bp * seq, D_MODEL)
        mem = mem_prompt.reshape(bp * N_MEM, D_MODEL)
        mem_kv = [norm_proj(mem, norm_mem_src[l], w_mkv[l], tm=TM_PROJ, tn=TN_MEM) for l in range(depth)]
        st_ab_p, st_c_p = [], []
        for l in range(depth):
            i = l // 2
            x = ffn_pair(x, l, 0, False, False)
            if l % 2 == 0:
                z = norm_proj(x, norm_mix[l], w_ab[i], tm=TM_PROJ, tn=TN_AB)
                q, k, kb, vb, iq, ik, ikb, iw = ab_features(z, tabs_p, idx_k_norm[i], seq=seq, tr=TR_FEAT)
                y_gla, st_t = gla_prompt(z, w_gate[i], b_gla_gate[i].reshape(1, -1), gla_out_norm[i].reshape(1, -1),
                                         batch=bp, seq=seq, tc=TC_GLA, n_seq=NSEQ_GLA)
                y_dsa = dsa_prompt(q, iq, iw, kb, vb, ikb, batch=bp, seq=seq)
                x = out_proj(x, [y_gla, y_dsa], [w_oab[i, :gla_w], w_oab[i, gla_w:]], tm=TM_OUT)
                n_pg = seq // PAGE_SIZE
                st_ab_p.append((k.reshape(bp, n_pg, PAGE_SIZE, DSA_KV_HEADS, DSA_HD),
                                z[:, AB_DV:AB_DV + 256].reshape(bp, n_pg, PAGE_SIZE, DSA_KV_HEADS, DSA_HD),
                                ik.reshape(bp, n_pg, PAGE_SIZE, IDX_HD),
                                jnp.swapaxes(st_t, 2, 3)))
            else:
                z = norm_proj(x, norm_mix[l], w_c[i], tm=TM_PROJ, tn=TN_C)
                q, k, kb, kbs, vb, vbs = c_features(z, tabs_p[2:], seq=seq, tr=TR_FEAT)
                y = swa_prompt(q, kb, kbs, vb, vbs, swa_sinks[i], batch=bp, seq=seq)
                x = out_proj(x, [y], [w_oc[i]], tm=TM_OUT)
                st_c_p.append((k.reshape(bp, seq, SWA_KV_HEADS, SWA_HD)[:, -WINDOW:],
                               z[:, C_V:C_V + 256].reshape(bp, seq, SWA_KV_HEADS, SWA_HD)[:, -WINDOW:]))
            x = mem_attn_prompt(x, norm_mem_q[l], w_mq[l], mem_kv[l], w_mo[l], batch=bp, seq=seq, tq=TQ_MEM)
            x = ffn_pair(x, l, 1, l == depth - 1, False)
        return x.reshape(bp, seq, D_MODEL), st_ab_p, st_c_p, mem_kv

    tabs_s = rope_tabs(PAST_LEN + jnp.arange(1))
    n_pages = page_table.shape[1]
    topk = min(DSA_TOPK_MAX, (PAST_LEN + 1) // 4)
    pool_k = cache_dsa_k.reshape(-1, DSA_HD)
    pool_v = cache_dsa_v.reshape(-1, DSA_HD)
    pool_ik_t = jnp.swapaxes(cache_dsa_idx_k, 2, 3).reshape(-1, IDX_HD, PAGE_SIZE)
    mem_k_rows = cache_mem_k.reshape(-1, MEM_HD)
    mem_v_rows = cache_mem_v.reshape(-1, MEM_HD)
    x = x_sample.reshape(bs, D_MODEL)
    st_ab_s, st_c_s = [], []
    for l in range(depth):
        i = l // 2
        x = ffn_pair(x, l, 0, False, True)
        if l % 2 == 0:
            z = norm_proj(x, norm_mix[l], w_ab[i], tm=TM_PROJ, tn=TN_AB)
            q, k, kb, vb, iq, ik, ikb, iw = ab_features(z, tabs_s, idx_k_norm[i], seq=1, tr=TR_FEAT)
            p = gla_sample_prep(z, w_gate[i], b_gla_gate[i].reshape(1, -1)).reshape(bs, 16, GLA_DK)
            st_new, y_gla = gla_sample(p, z, state_gla[i], gla_out_norm[i].reshape(1, -1), bs=BS_SAMPLE)
            scores = dsa_sample_scores(page_table, iq.reshape(bs, IDX_HEADS, IDX_HD),
                                       iw[:, :IDX_HEADS].reshape(bs, IDX_HEADS, 1), pool_ik_t, i * n_pool, pg=PG_SCORES)
            bias, bnew = dsa_sample_select(scores.reshape(bs, n_pages * PAGE_SIZE), iq, iw, ikb, topk=topk)
            v_new = z[:, AB_DV:AB_DV + 256]
            y_dsa = dsa_sample_attend(
                page_table, q.reshape(bs, DSA_HEADS, DSA_HD), bias.reshape(bs, n_pages, DSA_KV_HEADS * PAGE_SIZE),
                bnew.reshape(bs, 1, LANES), k.reshape(bs, 1, 256), v_new.reshape(bs, 1, 256),
                pool_k, pool_v, i * n_pool, chunk=PG_ATTEND).reshape(bs, DSA_HEADS * DSA_HD)
            x = out_proj(x, [y_gla, y_dsa], [w_oab[i, :gla_w], w_oab[i, gla_w:]], tm=TM_OUT)
            st_ab_s.append((k.reshape(bs, 1, DSA_KV_HEADS, DSA_HD), v_new.reshape(bs, 1, DSA_KV_HEADS, DSA_HD),
                            ik.reshape(bs, 1, IDX_HD), st_new))
        else:
            z = norm_proj(x, norm_mix[l], w_c[i], tm=TM_PROJ, tn=TN_C)
            q, k, kb, kbs, vb, vbs = c_features(z, tabs_s[2:], seq=1, tr=TR_FEAT)
            v_new = z[:, C_V:C_V + 256]
            y = swa_sample(q.reshape(bs, SWA_HEADS, SWA_HD), jnp.transpose(cache_swa_k[i], (0, 2, 3, 1)),
                           jnp.transpose(cache_swa_v[i], (0, 2, 3, 1)), k.reshape(bs, SWA_KV_HEADS, SWA_HD),
                           v_new.reshape(bs, SWA_KV_HEADS, SWA_HD), swa_sinks[i], bs=BS_SAMPLE)
            x = out_proj(x, [y.reshape(bs, SWA_HEADS * SWA_HD)], [w_oc[i]], tm=TM_OUT)
            st_c_s.append((jnp.concatenate([cache_swa_k[i][:, 1:], k.reshape(bs, 1, SWA_KV_HEADS, SWA_HD)], axis=1),
                           jnp.concatenate([cache_swa_v[i][:, 1:], v_new.reshape(bs, 1, SWA_KV_HEADS, SWA_HD)], axis=1)))
        qm = norm_proj(x, norm_mem_q[l], w_mq[l], tm=TM_PROJ, tn=TN_MEM).reshape(bs, MEM_HEADS, MEM_HD)
        qm = jnp.pad(qm, ((0, 0), (0, SUBLANES - MEM_HEADS), (0, 0)))
        om = mem_attn_sample(qm, mem_k_rows, mem_v_rows, l, bs=BS_SAMPLE)
        x = out_proj(x, [om[:, :MEM_HEADS].reshape(bs, MEM_HEADS * MEM_HD)], [w_mo[l]], tm=TM_OUT)
        x = ffn_pair(x, l, 1, l == depth - 1, True)
    y_sample = x.reshape(bs, 1, D_MODEL)

    y_prompt, st_ab_p, st_c_p, mem_kv = prompt_group()

    stk = lambda sts, j: jnp.stack([s[j] for s in sts])
    mw = MEM_HEADS * MEM_HD
    mem_k_p = jnp.stack([kv[:, :mw].reshape(bp, N_MEM, MEM_HEADS, MEM_HD) for kv in mem_kv])
    mem_v_p = jnp.stack([kv[:, mw:].reshape(bp, N_MEM, MEM_HEADS, MEM_HD) for kv in mem_kv])
    return (y_prompt, y_sample, stk(st_ab_p, 0), stk(st_ab_p, 1), stk(st_ab_p, 2),
            stk(st_ab_s, 0), stk(st_ab_s, 1), stk(st_ab_s, 2), stk(st_ab_p, 3), stk(st_ab_s, 3),
            stk(st_c_p, 0), stk(st_c_p, 1), stk(st_c_s, 0), stk(st_c_s, 1), mem_k_p, mem_v_p)
```

```python
import functools

import jax
import jax.numpy as jnp
import numpy as np
from jax import lax
from jax.experimental import pallas as pl
from jax.experimental.pallas import tpu as pltpu

F32 = jnp.float32
BF16 = jnp.bfloat16

D_MODEL = 2048
D_FF = 5632
EPS = 1e-6
ROPE_THETA = 10000.0
NEG_INF = -1e30
PAST_LEN = 8192
PAGE_SIZE = 128
Q_BLOCK = 128
GLA_HEADS, GLA_DK, GLA_DV = 4, 128, 256
GLA_GATE_RANK = 16
GLA_GATE_TAU = 16.0
GLA_CHUNK = 64
DSA_HEADS, DSA_KV_HEADS, DSA_HD = 8, 2, 128
IDX_HEADS, IDX_HD = 8, 64
DSA_TOPK_MAX = 256
SWA_HEADS, SWA_KV_HEADS, SWA_HD = 32, 4, 64
WINDOW = 128
MEM_HEADS, MEM_HD = 4, 128
N_MEM = 256

LANES = 128
SUBLANES = 8
VMEM_LIMIT_BYTES = 60000 * 1024
ELEM_SUB = 128

AB_GV, AB_GR, AB_DQ = 0, 1024, 2048
AB_GQ, AB_GK, AB_IQ = 3072, 3584, 4096
AB_DK, AB_DV = 4608, 4864
AB_SMALL = 5120
AB_WIDTH = 5376
SM_IK, SM_GD, SM_IW = 0, 64, 80
C_Q, C_K, C_V, C_WIDTH = 0, 2048, 2304, 2560


def _cparams(sem):
    return pltpu.CompilerParams(dimension_semantics=sem, vmem_limit_bytes=VMEM_LIMIT_BYTES)


def _rms(x, g):
    y = x * lax.rsqrt(jnp.mean(x * x, axis=-1, keepdims=True) + EPS)
    return y * g


def _ffn_kernel(x_ref, g_ref, wg_ref, wu_ref, wd_ref, fg_ref, o_ref, *rest, final_norm, sub, emit_cast):
    h_ref = rest[-1]
    j = pl.program_id(1)
    tm = x_ref.shape[0]
    if emit_cast:
        for src, dst in zip((wg_ref, wu_ref, wd_ref), rest[:3]):
            dst[...] = src[...].astype(BF16)
        wg_ref, wu_ref, wd_ref = rest[:3]

    esub = min(ELEM_SUB, tm)

    def row_group(r):
        return pl.ds(pl.multiple_of(r * esub, esub), esub)

    @pl.when(j == 0)
    def _():
        def body(r, carry):
            rows = row_group(r)
            h_ref[rows, :] = _rms(x_ref[rows, :], g_ref[...]).astype(BF16)
            o_ref[rows, :] = jnp.zeros((esub, D_MODEL), F32)
            return carry

        lax.fori_loop(0, tm // esub, body, 0)

    for r in range(tm // sub):
        rows = slice(r * sub, (r + 1) * sub)
        h = h_ref[rows, :]
        a = jnp.dot(h, wg_ref[...], preferred_element_type=F32)
        u = jnp.dot(h, wu_ref[...], preferred_element_type=F32)
        act = (a * jax.nn.sigmoid(a) * u).astype(BF16)
        o_ref[rows, :] += jnp.dot(act, wd_ref[...], preferred_element_type=F32)

    @pl.when(j == pl.num_programs(1) - 1)
    def _():
        def body(r, carry):
            rows = row_group(r)
            y = x_ref[rows, :] + 0.5 * o_ref[rows, :]
            if final_norm:
                y = _rms(y, fg_ref[...])
            o_ref[rows, :] = y
            return carry

        lax.fori_loop(0, tm // esub, body, 0)


def ffn(x, g, wg, wu, wd, sel, final_g=None, *, tm, tf, sub):
    m = x.shape[0]
    tm = min(tm, m)
    sub = min(sub, tm)
    fg = g if final_g is None else final_g
    emit_cast = sel is not None
    wspec = lambda shape, imap: pl.BlockSpec(shape, imap)
    if emit_cast:
        assert m == tm and wg.dtype == F32
        layer, half = sel
        w_in = [wspec((None, None, D_MODEL, tf), lambda i, j: (layer, half, 0, j)),
                wspec((None, None, D_MODEL, tf), lambda i, j: (layer, half, 0, j)),
                wspec((None, None, tf, D_MODEL), lambda i, j: (layer, half, j, 0))]
    else:
        assert wg.dtype == BF16 and wg.ndim == 2
        w_in = [wspec((D_MODEL, tf), lambda i, j: (0, j)), wspec((D_MODEL, tf), lambda i, j: (0, j)),
                wspec((tf, D_MODEL), lambda i, j: (j, 0))]
    out_specs = [pl.BlockSpec((tm, D_MODEL), lambda i, j: (i, 0))]
    out_shape = [jax.ShapeDtypeStruct((m, D_MODEL), F32)]
    if emit_cast:
        out_specs += [wspec((D_MODEL, tf), lambda i, j: (0, j)), wspec((D_MODEL, tf), lambda i, j: (0, j)),
                      wspec((tf, D_MODEL), lambda i, j: (j, 0))]
        out_shape += [jax.ShapeDtypeStruct((D_MODEL, D_FF), BF16), jax.ShapeDtypeStruct((D_MODEL, D_FF), BF16),
                      jax.ShapeDtypeStruct((D_FF, D_MODEL), BF16)]
    outs = pl.pallas_call(
        functools.partial(_ffn_kernel, final_norm=final_g is not None, sub=sub, emit_cast=emit_cast),
        grid=(m // tm, D_FF // tf),
        in_specs=[pl.BlockSpec((tm, D_MODEL), lambda i, j: (i, 0)), pl.BlockSpec((1, D_MODEL), lambda i, j: (0, 0))]
                 + w_in + [pl.BlockSpec((1, D_MODEL), lambda i, j: (0, 0))],
        out_specs=out_specs,
        out_shape=out_shape,
        scratch_shapes=[pltpu.VMEM((tm, D_MODEL), BF16)],
        compiler_params=_cparams(("parallel", "arbitrary")),
        name="ffn_cast" if emit_cast else "ffn",
    )(x, g.reshape(1, D_MODEL), wg, wu, wd, fg.reshape(1, D_MODEL))
    return tuple(outs) if emit_cast else outs[0]


def _proj_kernel(x_ref, g_ref, w_ref, o_ref, h_ref, *, sub):
    tm = x_ref.shape[0]

    @pl.when(pl.program_id(1) == 0)
    def _():
        for r in range(tm // sub):
            rows = slice(r * sub, (r + 1) * sub)
            h_ref[rows, :] = _rms(x_ref[rows, :], g_ref[...]).astype(BF16)

    o_ref[...] = jnp.dot(h_ref[...], w_ref[...], preferred_element_type=F32)


def norm_proj(x, g, w, *, tm, tn, sub=512):
    m, n = x.shape[0], w.shape[1]
    tm = min(tm, m)
    return pl.pallas_call(
        functools.partial(_proj_kernel, sub=min(sub, tm)),
        grid=(m // tm, n // tn),
        in_specs=[
            pl.BlockSpec((tm, D_MODEL), lambda i, j: (i, 0)),
            pl.BlockSpec((1, D_MODEL), lambda i, j: (0, 0)),
            pl.BlockSpec((D_MODEL, tn), lambda i, j: (0, j)),
        ],
        out_specs=pl.BlockSpec((tm, tn), lambda i, j: (i, j)),
        out_shape=jax.ShapeDtypeStruct((m, n), F32),
        scratch_shapes=[pltpu.VMEM((tm, D_MODEL), BF16)],
        compiler_params=_cparams(("parallel", "arbitrary")),
        name="norm_proj",
    )(x, g.reshape(1, D_MODEL), w)


def _outproj_kernel(*refs, n_in):
    x_ref = refs[0]
    y_refs = refs[1:1 + n_in]
    w_refs = refs[1 + n_in:1 + 2 * n_in]
    o_ref = refs[1 + 2 * n_in]
    acc = x_ref[...]
    for y_ref, w_ref in zip(y_refs, w_refs):
        acc = acc + jnp.dot(y_ref[...].astype(BF16), w_ref[...], preferred_element_type=F32)
    o_ref[...] = acc


def out_proj(x, ys, ws, *, tm):
    m = x.shape[0]
    tm = min(tm, m)
    n_in = len(ys)
    in_specs = [pl.BlockSpec((tm, D_MODEL), lambda i: (i, 0))]
    in_specs += [pl.BlockSpec((tm, y.shape[1]), lambda i: (i, 0)) for y in ys]
    in_specs += [pl.BlockSpec(w.shape, lambda i: (0, 0)) for w in ws]
    return pl.pallas_call(
        functools.partial(_outproj_kernel, n_in=n_in),
        grid=(m // tm,),
        in_specs=in_specs,
        out_specs=pl.BlockSpec((tm, D_MODEL), lambda i: (i, 0)),
        out_shape=jax.ShapeDtypeStruct((m, D_MODEL), F32),
        compiler_params=_cparams(("parallel",)),
        name="out_proj",
    )(x, *ys, *ws)


def _rope_tables(pos, hd):
    half = hd // 2
    inv = ROPE_THETA ** (-jnp.arange(half, dtype=F32) / half)
    ang = pos.astype(F32)[:, None] * inv[None, :]
    cos, sin = jnp.cos(ang), jnp.sin(ang)
    reps = LANES // hd
    return (jnp.concatenate([cos, cos] * reps, axis=-1),
            jnp.concatenate([-sin, sin] * reps, axis=-1))


def _rope128(x, cos, sin):
    return x * cos + pltpu.roll(x, 64, 1) * sin


def _rope64(x, cos, sin, lower):
    partner = jnp.where(lower, pltpu.roll(x, 96, 1), pltpu.roll(x, 32, 1))
    return x * cos + partner * sin


def _lower32_mask(rows):
    lane = lax.broadcasted_iota(jnp.int32, (rows, LANES), 1)
    return (lane % 64) < 32


def _ab_feat_kernel(dq_ref, dk_ref, dv_ref, iq_ref, sm_ref, c128_ref, s128_ref, c64_ref, s64_ref, gik_ref,
                    q_ref, k_ref, kb_ref, vb_ref, iqo_ref, ik_ref, ikb_ref, iw_ref):
    rows = dq_ref.shape[0]
    c128, s128 = c128_ref[...], s128_ref[...]
    c64, s64 = c64_ref[...], s64_ref[...]
    lower = _lower32_mask(rows)
    for h in range(DSA_HEADS):
        sl = slice(h * LANES, (h + 1) * LANES)
        q_ref[:, sl] = (_rope128(dq_ref[:, sl], c128, s128) * (DSA_HD ** -0.5)).astype(BF16)
    for h in range(DSA_KV_HEADS):
        sl = slice(h * LANES, (h + 1) * LANES)
        kr = _rope128(dk_ref[:, sl], c128, s128)
        k_ref[:, sl] = kr
        kb_ref[:, sl] = kr.astype(BF16)
    vb_ref[...] = dv_ref[...].astype(BF16)
    for p in range(IDX_HEADS * IDX_HD // LANES):
        sl = slice(p * LANES, (p + 1) * LANES)
        iqo_ref[:, sl] = (_rope64(iq_ref[:, sl], c64, s64, lower) * (IDX_HD ** -0.5)).astype(BF16)
    sm = sm_ref[...]
    lane = lax.broadcasted_iota(jnp.int32, (rows, LANES), 1)
    ik = jnp.where(lane < IDX_HD, sm, 0.0)
    ik = ik * lax.rsqrt(jnp.sum(ik * ik, axis=-1, keepdims=True) / IDX_HD + EPS) * gik_ref[...]
    ik = _rope64(ik, c64, s64, lower)
    ik_ref[...] = ik[:, :IDX_HD]
    ikb_ref[...] = jnp.where(lane < IDX_HD, ik, pltpu.roll(ik, 64, 1)).astype(BF16)
    iw_ref[...] = pltpu.roll(sm, LANES - SM_IW, 1) * (IDX_HEADS ** -0.5)


def ab_features(z, tabs, gik, *, seq, tr):
    m = z.shape[0]
    tr = min(tr, m)
    c128, s128, c64, s64 = tabs
    if c128.shape[0] == 1:
        tab_spec = pl.BlockSpec((1, LANES), lambda i: (0, 0))
    else:
        nt = seq // tr
        tab_spec = pl.BlockSpec((tr, LANES), lambda i: (i % nt, 0))
    col = lambda w, off: pl.BlockSpec((tr, w), lambda i: (i, off // w))
    row = lambda w: pl.BlockSpec((tr, w), lambda i: (i, 0))
    gik_pad = jnp.zeros((1, LANES), F32).at[0, :IDX_HD].set(gik)
    return pl.pallas_call(
        _ab_feat_kernel,
        grid=(m // tr,),
        in_specs=[col(1024, AB_DQ), col(256, AB_DK), col(256, AB_DV), col(512, AB_IQ), col(128, AB_SMALL),
                  tab_spec, tab_spec, tab_spec, tab_spec, pl.BlockSpec((1, LANES), lambda i: (0, 0))],
        out_specs=[row(1024), row(256), row(256), row(256), row(512), row(IDX_HD), row(128), row(128)],
        out_shape=[jax.ShapeDtypeStruct((m, 1024), BF16),
                   jax.ShapeDtypeStruct((m, 256), F32),
                   jax.ShapeDtypeStruct((m, 256), BF16),
                   jax.ShapeDtypeStruct((m, 256), BF16),
                   jax.ShapeDtypeStruct((m, 512), BF16),
                   jax.ShapeDtypeStruct((m, IDX_HD), F32),
                   jax.ShapeDtypeStruct((m, 128), BF16),
                   jax.ShapeDtypeStruct((m, 128), F32)],
        compiler_params=_cparams(("parallel",)),
        name="ab_features",
    )(z, z, z, z, z, c128, s128, c64, s64, gik_pad)


def _c_feat_kernel(q_ref, k_ref, v_ref, c64_ref, s64_ref, qo_ref, ko_ref, kb_ref, kbs_ref, vb_ref, vbs_ref):
    rows = q_ref.shape[0]
    c64, s64 = c64_ref[...], s64_ref[...]
    lower = _lower32_mask(rows)
    for p in range(SWA_HEADS * SWA_HD // LANES):
        sl = slice(p * LANES, (p + 1) * LANES)
        qo_ref[:, sl] = (_rope64(q_ref[:, sl], c64, s64, lower) * (SWA_HD ** -0.5)).astype(BF16)
    for p in range(SWA_KV_HEADS * SWA_HD // LANES):
        sl = slice(p * LANES, (p + 1) * LANES)
        kr = _rope64(k_ref[:, sl], c64, s64, lower)
        v = v_ref[:, sl]
        ko_ref[:, sl] = kr
        kb_ref[:, sl] = kr.astype(BF16)
        kbs_ref[:, sl] = pltpu.roll(kr, 64, 1).astype(BF16)
        vb_ref[:, sl] = v.astype(BF16)
        vbs_ref[:, sl] = pltpu.roll(v, 64, 1).astype(BF16)


def c_features(z, tabs, *, seq, tr):
    m = z.shape[0]
    tr = min(tr, m)
    c64, s64 = tabs
    if c64.shape[0] == 1:
        tab_spec = pl.BlockSpec((1, LANES), lambda i: (0, 0))
    else:
        nt = seq // tr
        tab_spec = pl.BlockSpec((tr, LANES), lambda i: (i % nt, 0))
    col = lambda w, off: pl.BlockSpec((tr, w), lambda i: (i, off // w))
    row = lambda w: pl.BlockSpec((tr, w), lambda i: (i, 0))
    return pl.pallas_call(
        _c_feat_kernel,
        grid=(m // tr,),
        in_specs=[col(2048, C_Q), col(256, C_K), col(256, C_V), tab_spec, tab_spec],
        out_specs=[row(2048), row(256), row(256), row(256), row(256), row(256)],
        out_shape=[jax.ShapeDtypeStruct((m, 2048), BF16),
                   jax.ShapeDtypeStruct((m, 256), F32),
                   jax.ShapeDtypeStruct((m, 256), BF16),
                   jax.ShapeDtypeStruct((m, 256), BF16),
                   jax.ShapeDtypeStruct((m, 256), BF16),
                   jax.ShapeDtypeStruct((m, 256), BF16)],
        compiler_params=_cparams(("parallel",)),
        name="c_features",
    )(z, z, z, c64, s64)


def _dot_nt(a, b, **kw):
    return lax.dot_general(a, b, (((1,), (1,)), ((), ())), preferred_element_type=F32, **kw)


def _dot_tn(a, b, **kw):
    return lax.dot_general(a, b, (((0,), (0,)), ((), ())), preferred_element_type=F32, **kw)


_HI = lax.Precision.HIGHEST


def _log_decay(sm, wgate, bgate):
    pre = jnp.dot(sm, wgate, preferred_element_type=F32, precision=_HI) + bgate
    return (jnp.minimum(pre, 0.0) - jnp.log1p(jnp.exp(-jnp.abs(pre)))) / GLA_GATE_TAU


def _gla_gate_out(o, r, gain):
    g = o * lax.rsqrt(jnp.mean(o * o, axis=-1, keepdims=True) + EPS) * gain
    return g * (r * jax.nn.sigmoid(r))


def _gla_prompt_kernel(gv_ref, gr_ref, gq_ref, gk_ref, sm_ref, wgate_ref, bgate_ref, gain_ref,
                       y_ref, st_ref, s_ref, b_ref, *, n_seq, n_chunks):
    c = pl.program_id(1)

    @pl.when(c == 0)
    def _():
        s_ref[...] = jnp.zeros_like(s_ref)

    tc = n_chunks * GLA_CHUNK
    ri = lax.broadcasted_iota(jnp.int32, (tc, tc), 0)
    ci = lax.broadcasted_iota(jnp.int32, (tc, tc), 1)
    tril = jnp.where(jnp.logical_and(ri // GLA_CHUNK == ci // GLA_CHUNK, ri >= ci), 1.0, 0.0).astype(F32)
    causal = (lax.broadcasted_iota(jnp.int32, (GLA_CHUNK, GLA_CHUNK), 0)
              >= lax.broadcasted_iota(jnp.int32, (GLA_CHUNK, GLA_CHUNK), 1))
    gain = gain_ref[...]
    for s in range(n_seq):
        la = _log_decay(sm_ref[s], wgate_ref[...], bgate_ref[...])
        b_ref[s] = jnp.dot(tril, la, preferred_element_type=F32, precision=_HI)
    for n in range(n_chunks):
        rows = slice(n * GLA_CHUNK, (n + 1) * GLA_CHUNK)
        for s in range(n_seq):
            b = b_ref[s, rows, :]
            b_end = b_ref[s, (n + 1) * GLA_CHUNK - 1:(n + 1) * GLA_CHUNK, :]
            k = gk_ref[s, rows, :]
            q_in = gq_ref[s, rows, :] * (GLA_DK ** -0.5) * jnp.exp(b)
            k_in = k * jnp.exp(-b)
            k_end = k * jnp.exp(b_end - b)
            decay = jnp.exp(b_end)
            for h in range(GLA_HEADS):
                dk = slice(h * GLA_DK, (h + 1) * GLA_DK)
                dv = slice(h * GLA_DV, (h + 1) * GLA_DV)
                v = gv_ref[s, rows, dv].astype(BF16)
                st = s_ref[s, h]
                qh = q_in[:, dk].astype(BF16)
                att = jnp.where(causal, _dot_nt(qh, k_in[:, dk].astype(BF16)), 0.0)
                o = _dot_nt(qh, st.astype(BF16)) + jnp.dot(att.astype(BF16), v, preferred_element_type=F32)
                s_ref[s, h] = st * decay[:, dk] + _dot_tn(v, k_end[:, dk].astype(BF16))
                y_ref[s, rows, dv] = _gla_gate_out(o, gr_ref[s, rows, dv], gain)

    @pl.when(c == pl.num_programs(1) - 1)
    def _():
        st_ref[...] = s_ref[...]


def gla_prompt(z, wgate, bgate, gain, *, batch, seq, tc, n_seq):
    m = z.shape[0]
    z3 = z.reshape(batch, seq, z.shape[1])
    col = lambda w, off: pl.BlockSpec((n_seq, tc, w), lambda b, c: (b, c, off // w))
    const = lambda shape: pl.BlockSpec(shape, lambda b, c: (0,) * len(shape))
    y, st = pl.pallas_call(
        functools.partial(_gla_prompt_kernel, n_seq=n_seq, n_chunks=tc // GLA_CHUNK),
        grid=(batch // n_seq, seq // tc),
        in_specs=[col(1024, AB_GV), col(1024, AB_GR), col(512, AB_GQ), col(512, AB_GK), col(128, AB_SMALL),
                  const((LANES, GLA_HEADS * GLA_DK)), const((1, GLA_HEADS * GLA_DK)), const((1, GLA_DV))],
        out_specs=[pl.BlockSpec((n_seq, tc, GLA_HEADS * GLA_DV), lambda b, c: (b, c, 0)),
                   pl.BlockSpec((n_seq, GLA_HEADS, GLA_DV, GLA_DK), lambda b, c: (b, 0, 0, 0))],
        out_shape=[jax.ShapeDtypeStruct((batch, seq, GLA_HEADS * GLA_DV), F32),
                   jax.ShapeDtypeStruct((batch, GLA_HEADS, GLA_DV, GLA_DK), F32)],
        scratch_shapes=[pltpu.VMEM((n_seq, GLA_HEADS, GLA_DV, GLA_DK), F32),
                        pltpu.VMEM((n_seq, tc, GLA_HEADS * GLA_DK), F32)],
        compiler_params=_cparams(("parallel", "arbitrary")),
        name="gla_prompt",
    )(z3, z3, z3, z3, z3, wgate, bgate, gain)
    return y.reshape(m, GLA_HEADS * GLA_DV), st


INT_MIN = -2 ** 31


def _order_key(score):
    score = jnp.where(score == 0.0, 0.0, score)
    bits = lax.bitcast_convert_type(score, jnp.int32)
    return jnp.where(bits < 0, bits ^ jnp.int32(0x7FFFFFFF), bits)


def _lane_total(x):
    return jnp.dot(x.astype(BF16), jnp.ones((LANES, LANES), BF16), preferred_element_type=F32)


ROW_SUB = 128


def _count_blocks(key_ref, rows, n_blocks, pred):
    acc = jnp.zeros((ROW_SUB, LANES), F32)
    for c in range(n_blocks):
        acc = acc + jnp.where(pred(key_ref[rows, c * LANES:(c + 1) * LANES]), 1.0, 0.0)
    return acc


def _masked_key():
    bits = int(np.array(NEG_INF, np.float32).view(np.int32))
    return bits ^ 0x7FFFFFFF


def _kth_largest_key(key_ref, t_ref, n_blocks, k, active):
    n_sub = key_ref.shape[0] // ROW_SUB
    t_ref[...] = jnp.full(t_ref.shape, INT_MIN, jnp.int32)

    def body(it, carry):
        step = lax.shift_left(jnp.int32(1), 31 - it)
        cands, accs = [], []
        for rb in range(n_sub):
            rows = slice(rb * ROW_SUB, (rb + 1) * ROW_SUB)
            cand = t_ref[rows, :] + step
            cands.append(cand)
            accs.append(_count_blocks(key_ref, rows, active(rb), lambda kc, cand=cand: kc >= cand))
        total = _lane_total(jnp.concatenate(accs, axis=0))
        for rb in range(n_sub):
            rows = slice(rb * ROW_SUB, (rb + 1) * ROW_SUB)
            skipped = float((n_blocks - active(rb)) * LANES)
            count = total[rows, :] + jnp.where(cands[rb] <= _masked_key(), skipped, 0.0)
            t_ref[rows, :] = jnp.where(count >= k, cands[rb], t_ref[rows, :])
        return carry

    lax.fori_loop(0, 32, body, 0)


def _select_topk(key_ref, t_ref, n_blocks, k, write_fn, active=None):
    if active is None:
        active = lambda rb: n_blocks
    _kth_largest_key(key_ref, t_ref, n_blocks, k, active)
    ri = lax.broadcasted_iota(jnp.int32, (LANES, LANES), 0)
    ci = lax.broadcasted_iota(jnp.int32, (LANES, LANES), 1)
    before = jnp.where(ri < ci, 1.0, 0.0).astype(BF16)
    for rb in range(key_ref.shape[0] // ROW_SUB):
        rows = slice(rb * ROW_SUB, (rb + 1) * ROW_SUB)
        t = t_ref[rows, :]
        skipped = float((n_blocks - active(rb)) * LANES)
        above = _lane_total(_count_blocks(key_ref, rows, active(rb), lambda kc: kc > t))
        need = k - above - jnp.where(t < _masked_key(), skipped, 0.0)
        run = jnp.zeros((ROW_SUB, LANES), F32)
        for c in range(n_blocks):
            if c >= active(rb):
                write_fn(rows, c, jnp.zeros((ROW_SUB, LANES), jnp.bool_))
                continue
            kc = key_ref[rows, c * LANES:(c + 1) * LANES]
            eq = jnp.where(kc == t, 1.0, 0.0)
            rank = jnp.dot(eq.astype(BF16), before, preferred_element_type=F32) + run
            take = jnp.where(kc > t, 1.0, jnp.where(rank < need, eq, 0.0))
            write_fn(rows, c, take > 0.0)
            run = run + _lane_total(eq)


def _half_mask(rows, upper):
    lane = lax.broadcasted_iota(jnp.int32, (rows, LANES), 1)
    return (lane >= 64) if upper else (lane < 64)


DSA_STRATUM = 512
KEY_CHUNK = 512


def _dsa_select_prompt_kernel(iq_ref, iw_ref, ikb_ref, bias_ref, key_ref, t_ref, *, row0, n_keys, topk):
    rows = iq_ref.shape[0]
    n_blocks = n_keys // LANES
    lane = lax.broadcasted_iota(jnp.int32, (ROW_SUB, LANES), 1)
    sub = lax.broadcasted_iota(jnp.int32, (ROW_SUB, LANES), 0)

    iw = iw_ref[...]
    qh = []
    for h in range(IDX_HEADS):
        pair = iq_ref[:, (h // 2) * LANES:(h // 2 + 1) * LANES]
        qh.append(jnp.where(_half_mask(rows, h % 2 == 1), pair, jnp.zeros_like(pair)))
    for kc in range(n_keys // KEY_CHUNK):
        ik = ikb_ref[kc * KEY_CHUNK:(kc + 1) * KEY_CHUNK, :]
        score = jnp.zeros((rows, KEY_CHUNK), F32)
        for h in range(IDX_HEADS):
            score = score + jnp.maximum(_dot_nt(qh[h], ik), 0.0) * iw[:, h:h + 1]
        for rb in range(rows // ROW_SUB):
            for cb in range(KEY_CHUNK // LANES):
                c = kc * (KEY_CHUNK // LANES) + cb
                causal = (c * LANES + lane) <= (row0 + rb * ROW_SUB + sub)
                part = score[rb * ROW_SUB:(rb + 1) * ROW_SUB, cb * LANES:(cb + 1) * LANES]
                key_ref[rb * ROW_SUB:(rb + 1) * ROW_SUB, c * LANES:(c + 1) * LANES] = _order_key(
                    jnp.where(causal, part, NEG_INF))

    def write(rws, c, sel):
        causal = (c * LANES + lane) <= (row0 + rws.start + sub)
        bias_ref[rws, c * LANES:(c + 1) * LANES] = jnp.where(jnp.logical_and(sel, causal), 0.0, NEG_INF)

    _select_topk(key_ref, t_ref, n_blocks, topk, write, active=lambda rb: row0 // LANES + rb + 1)


def _dsa_attend_prompt_kernel(q_ref, bias_ref, kb_ref, vb_ref, y_ref):
    bias = bias_ref[...]
    group = DSA_HEADS // DSA_KV_HEADS
    for kv in range(DSA_KV_HEADS):
        kvs = slice(kv * DSA_HD, (kv + 1) * DSA_HD)
        q4 = jnp.concatenate([q_ref[:, (kv * group + g) * DSA_HD:(kv * group + g + 1) * DSA_HD] for g in range(group)],
                             axis=0)
        s4 = _dot_nt(q4, kb_ref[:, kvs])
        ps, dens = [], []
        for g in range(group):
            s = s4[g * Q_BLOCK:(g + 1) * Q_BLOCK, :] + bias
            p = jnp.exp(s - jnp.max(s, axis=-1, keepdims=True))
            dens.append(jnp.sum(p, axis=-1, keepdims=True))
            ps.append(p.astype(BF16))
        o4 = jnp.dot(jnp.concatenate(ps, axis=0), vb_ref[:, kvs], preferred_element_type=F32)
        for g in range(group):
            h = kv * group + g
            y_ref[:, h * DSA_HD:(h + 1) * DSA_HD] = o4[g * Q_BLOCK:(g + 1) * Q_BLOCK, :] / dens[g]


def dsa_prompt(q, iq, iw, kb, vb, ikb, *, batch, seq):
    topk = min(DSA_TOPK_MAX, seq // 4)
    as3 = lambda a: a.reshape(batch, seq, a.shape[-1])
    q, iq, iw, kb, vb, ikb = (as3(a) for a in (q, iq, iw, kb, vb, ikb))
    nsub = DSA_STRATUM // Q_BLOCK
    outs = []
    for r in range(seq // DSA_STRATUM):
        n_keys = (r + 1) * DSA_STRATUM
        strat = lambda w: pl.BlockSpec((None, DSA_STRATUM, w), lambda b: (b, r, 0))
        keys = lambda w: pl.BlockSpec((None, n_keys, w), lambda b: (b, 0, 0))
        bias = pl.pallas_call(
            functools.partial(_dsa_select_prompt_kernel, row0=r * DSA_STRATUM, n_keys=n_keys, topk=topk),
            grid=(batch,),
            in_specs=[strat(512), strat(128), keys(128)],
            out_specs=pl.BlockSpec((None, DSA_STRATUM, n_keys), lambda b: (b, 0, 0)),
            out_shape=jax.ShapeDtypeStruct((batch, DSA_STRATUM, n_keys), F32),
            scratch_shapes=[pltpu.VMEM((DSA_STRATUM, n_keys), jnp.int32), pltpu.VMEM((DSA_STRATUM, LANES), jnp.int32)],
            compiler_params=_cparams(("parallel",)),
            name="dsa_select_prompt",
        )(iq, iw, ikb)
        qrow = lambda w: pl.BlockSpec((None, Q_BLOCK, w), lambda b, i: (b, r * nsub + i, 0))
        keys2 = lambda w: pl.BlockSpec((None, n_keys, w), lambda b, i: (b, 0, 0))
        outs.append(pl.pallas_call(
            _dsa_attend_prompt_kernel,
            grid=(batch, nsub),
            in_specs=[qrow(1024), pl.BlockSpec((None, Q_BLOCK, n_keys), lambda b, i: (b, i, 0)), keys2(256), keys2(256)],
            out_specs=pl.BlockSpec((None, Q_BLOCK, DSA_HEADS * DSA_HD), lambda b, i: (b, i, 0)),
            out_shape=jax.ShapeDtypeStruct((batch, DSA_STRATUM, DSA_HEADS * DSA_HD), F32),
            compiler_params=_cparams(("parallel", "arbitrary")),
            name="dsa_attend_prompt",
        )(q, bias, kb, vb))
    return jnp.concatenate(outs, axis=1).reshape(batch * seq, DSA_HEADS * DSA_HD)


def _swa_head_plan(h):
    group = SWA_HEADS // SWA_KV_HEADS
    kv = h // group
    return h // 2, h % 2, kv // 2, (kv % 2) != (h % 2)


def _swa_prompt_kernel(sink_ref, q_ref, kp_ref, kc_ref, kps_ref, kcs_ref, vp_ref, vc_ref, vps_ref, vcs_ref, y_ref):
    i = pl.program_id(1)
    r = lax.broadcasted_iota(jnp.int32, (Q_BLOCK, 2 * Q_BLOCK), 0)
    c = lax.broadcasted_iota(jnp.int32, (Q_BLOCK, 2 * Q_BLOCK), 1)
    rel = Q_BLOCK + r - c
    ok = (rel >= 0) & (rel <= WINDOW) & ((i - 1) * Q_BLOCK + c >= 0)
    bias = jnp.where(ok, 0.0, NEG_INF)
    keys = (jnp.concatenate([kp_ref[...], kc_ref[...]], axis=0), jnp.concatenate([kps_ref[...], kcs_ref[...]], axis=0))
    vals = (jnp.concatenate([vp_ref[...], vc_ref[...]], axis=0), jnp.concatenate([vps_ref[...], vcs_ref[...]], axis=0))
    lower = _half_mask(Q_BLOCK, False)
    for p in range(SWA_HEADS // 2):
        qpair = q_ref[:, p * LANES:(p + 1) * LANES]
        outs = []
        for h in (2 * p, 2 * p + 1):
            _, half, ks, swapped = _swa_head_plan(h)
            qh = jnp.where(_half_mask(Q_BLOCK, half == 1), qpair, jnp.zeros_like(qpair))
            kk = keys[int(swapped)][:, ks * LANES:(ks + 1) * LANES]
            vv = vals[int(swapped)][:, ks * LANES:(ks + 1) * LANES]
            s = _dot_nt(qh, kk) + bias
            sink = sink_ref[h]
            mx = jnp.maximum(jnp.max(s, axis=-1, keepdims=True), sink)
            pr = jnp.exp(s - mx)
            den = jnp.sum(pr, axis=-1, keepdims=True) + jnp.exp(sink - mx)
            outs.append(jnp.dot(pr.astype(BF16), vv, preferred_element_type=F32) / den)
        y_ref[:, p * LANES:(p + 1) * LANES] = jnp.where(lower, outs[0], outs[1])


def swa_prompt(q, kb, kbs, vb, vbs, sinks, *, batch, seq):
    m = q.shape[0]
    nq = seq // Q_BLOCK
    cur = pl.BlockSpec((Q_BLOCK, 256), lambda b, i, s: (b * nq + i, 0))
    prev = pl.BlockSpec((Q_BLOCK, 256), lambda b, i, s: (b * nq + jnp.maximum(i - 1, 0), 0))
    qspec = pl.BlockSpec((Q_BLOCK, 2048), lambda b, i, s: (b * nq + i, 0))
    return pl.pallas_call(
        _swa_prompt_kernel,
        grid_spec=pltpu.PrefetchScalarGridSpec(
            num_scalar_prefetch=1,
            grid=(batch, nq),
            in_specs=[qspec, prev, cur, prev, cur, prev, cur, prev, cur],
            out_specs=qspec,
        ),
        out_shape=jax.ShapeDtypeStruct((m, SWA_HEADS * SWA_HD), F32),
        compiler_params=_cparams(("parallel", "arbitrary")),
        name="swa_prompt",
    )(sinks, q, kb, kb, kbs, kbs, vb, vb, vbs, vbs)


def _mem_prompt_kernel(x_ref, g_ref, wq_ref, kv_ref, wo_ref, *rest, n_in):
    y_refs, w_refs, o_ref = rest[:n_in], rest[n_in:2 * n_in], rest[2 * n_in]
    x = x_ref[...]
    for y_ref, w_ref in zip(y_refs, w_refs):
        x = x + jnp.dot(y_ref[...].astype(BF16), w_ref[...], preferred_element_type=F32)
    h = _rms(x, g_ref[...]).astype(BF16)
    q = (jnp.dot(h, wq_ref[...], preferred_element_type=F32) * (MEM_HD ** -0.5)).astype(BF16)
    width = MEM_HEADS * MEM_HD
    outs = []
    for hd in range(MEM_HEADS):
        sl = slice(hd * MEM_HD, (hd + 1) * MEM_HD)
        k = kv_ref[:, sl].astype(BF16)
        v = kv_ref[:, width + hd * MEM_HD:width + (hd + 1) * MEM_HD].astype(BF16)
        s = _dot_nt(q[:, sl], k)
        p = jnp.exp(s - jnp.max(s, axis=-1, keepdims=True))
        o = jnp.dot(p.astype(BF16), v, preferred_element_type=F32) / jnp.sum(p, axis=-1, keepdims=True)
        outs.append(o.astype(BF16))
    o_ref[...] = x + jnp.dot(jnp.concatenate(outs, axis=-1), wo_ref[...], preferred_element_type=F32)


def mem_attn_prompt(x, ys, ws, g, wq, kv, wo, *, batch, seq, tq):
    m = x.shape[0]
    nq = seq // tq
    width = MEM_HEADS * MEM_HD
    rows = lambda w: pl.BlockSpec((tq, w), lambda b, i: (b * nq + i, 0))
    const = lambda shape: pl.BlockSpec(shape, lambda b, i: (0, 0), pipeline_mode=pl.Buffered(1))
    return pl.pallas_call(
        functools.partial(_mem_prompt_kernel, n_in=len(ys)),
        grid=(batch, nq),
        in_specs=[rows(D_MODEL),
                  pl.BlockSpec((1, D_MODEL), lambda b, i: (0, 0)),
                  const((D_MODEL, width)),
                  pl.BlockSpec((N_MEM, 2 * width), lambda b, i: (b, 0)),
                  const((width, D_MODEL))]
                 + [rows(y.shape[1]) for y in ys] + [const(w.shape) for w in ws],
        out_specs=rows(D_MODEL),
        out_shape=jax.ShapeDtypeStruct((m, D_MODEL), F32),
        compiler_params=_cparams(("parallel", "arbitrary")),
        name="mem_attn_prompt",
    )(x, g.reshape(1, D_MODEL), wq, kv, wo, *ys, *ws)


def _gla_prep_kernel(gq_ref, gk_ref, sm_ref, wgate_ref, bgate_ref, o_ref):
    w = GLA_HEADS * GLA_DK
    la = _log_decay(sm_ref[...], wgate_ref[...], bgate_ref[...])
    o_ref[:, 0:w] = jnp.exp(la)
    o_ref[:, w:2 * w] = gk_ref[...]
    o_ref[:, 2 * w:3 * w] = gq_ref[...] * (GLA_DK ** -0.5)
    o_ref[:, 3 * w:4 * w] = jnp.zeros((gq_ref.shape[0], w), F32)


def gla_sample_prep(z, wgate, bgate):
    m = z.shape[0]
    w = GLA_HEADS * GLA_DK
    col = lambda wd, off: pl.BlockSpec((m, wd), lambda i: (0, off // wd))
    return pl.pallas_call(
        _gla_prep_kernel,
        grid=(1,),
        in_specs=[col(512, AB_GQ), col(512, AB_GK), col(128, AB_SMALL),
                  pl.BlockSpec((LANES, w), lambda i: (0, 0)), pl.BlockSpec((1, w), lambda i: (0, 0))],
        out_specs=pl.BlockSpec((m, 4 * w), lambda i: (0, 0)),
        out_shape=jax.ShapeDtypeStruct((m, 4 * w), F32),
        compiler_params=_cparams(("arbitrary",)),
        name="gla_sample_prep",
    )(z, z, z, wgate, bgate)


def _gla_step_kernel(p_ref, gv_ref, gr_ref, s_ref, gain_ref, so_ref, y_ref, *, bs):
    gain = gain_ref[...]
    for s in range(bs):
        xt = p_ref[s].T
        for h in range(GLA_HEADS):
            dv = slice(h * GLA_DV, (h + 1) * GLA_DV)
            st = s_ref[s, h] * xt[:, h:h + 1] + xt[:, GLA_HEADS + h:GLA_HEADS + h + 1] * gv_ref[s:s + 1, dv]
            so_ref[s, h] = st
            o = jnp.sum(xt[:, 2 * GLA_HEADS + h:2 * GLA_HEADS + h + 1] * st, axis=0, keepdims=True)
            y_ref[s:s + 1, dv] = _gla_gate_out(o, gr_ref[s:s + 1, dv], gain)


def gla_sample(p, z, state, gain, *, bs):
    m = z.shape[0]
    col = lambda w, off: pl.BlockSpec((bs, w), lambda i: (i, off // w))
    sspec = pl.BlockSpec((bs, GLA_HEADS, GLA_DK, GLA_DV), lambda i: (i, 0, 0, 0))
    return pl.pallas_call(
        functools.partial(_gla_step_kernel, bs=bs),
        grid=(m // bs,),
        in_specs=[pl.BlockSpec((bs, 16, GLA_DK), lambda i: (i, 0, 0)), col(1024, AB_GV), col(1024, AB_GR), sspec,
                  pl.BlockSpec((1, GLA_DV), lambda i: (0, 0))],
        out_specs=[sspec, pl.BlockSpec((bs, GLA_HEADS * GLA_DV), lambda i: (i, 0))],
        out_shape=[jax.ShapeDtypeStruct(state.shape, F32), jax.ShapeDtypeStruct((m, GLA_HEADS * GLA_DV), F32)],
        compiler_params=_cparams(("parallel",)),
        name="gla_sample",
    )(p, z, z, state, gain)


def _dsa_scores_kernel(pt_ref, iq_ref, iw_ref, pool_hbm, o_ref, buf, sem, *, page_base, n_pages, chunk):
    b = pl.program_id(0)
    n_chunks = n_pages // chunk

    def page_copies(row, c, slot):
        return [pltpu.make_async_copy(pool_hbm.at[page_base + pt_ref[row, c * chunk + j]], buf.at[slot, j], sem.at[slot])
                for j in range(chunk)]

    @pl.when(b == 0)
    def _():
        for cp in page_copies(0, 0, 0):
            cp.start()

    q8 = iq_ref[0]
    iw = iw_ref[0]
    for c in range(n_chunks):
        slot = c % 2
        if c + 1 < n_chunks:
            for cp in page_copies(b, c + 1, 1 - slot):
                cp.start()
        else:
            @pl.when(b + 1 < pl.num_programs(0))
            def _():
                for cp in page_copies(b + 1, 0, 1 - slot):
                    cp.start()
        for cp in page_copies(b, c, slot):
            cp.wait()
        for j in range(chunk):
            dots = jnp.dot(q8, buf[slot, j].astype(BF16), preferred_element_type=F32)
            o_ref[0, c * chunk + j:c * chunk + j + 1, :] = jnp.sum(jnp.maximum(dots, 0.0) * iw, axis=0, keepdims=True)


def dsa_sample_scores(page_table, iq, iw, pool_ik_t, page_base, *, chunk):
    m, n_pages = page_table.shape
    assert n_pages % chunk == 0 and (n_pages // chunk) % 2 == 0
    return pl.pallas_call(
        functools.partial(_dsa_scores_kernel, page_base=page_base, n_pages=n_pages, chunk=chunk),
        grid_spec=pltpu.PrefetchScalarGridSpec(
            num_scalar_prefetch=1,
            grid=(m,),
            in_specs=[pl.BlockSpec((1, IDX_HEADS, IDX_HD), lambda b, pt: (b, 0, 0)),
                      pl.BlockSpec((1, IDX_HEADS, 1), lambda b, pt: (b, 0, 0)),
                      pl.BlockSpec(memory_space=pl.ANY)],
            out_specs=pl.BlockSpec((1, n_pages, PAGE_SIZE), lambda b, pt: (b, 0, 0)),
            scratch_shapes=[pltpu.VMEM((2, chunk, IDX_HD, PAGE_SIZE), F32), pltpu.SemaphoreType.DMA((2,))],
        ),
        out_shape=jax.ShapeDtypeStruct((m, n_pages, PAGE_SIZE), F32),
        compiler_params=_cparams(("arbitrary",)),
        name="dsa_sample_scores",
    )(page_table, iq, iw, pool_ik_t)


def _dsa_select_sample_kernel(sc_ref, iq_ref, iw_ref, ikb_ref, bias_ref, bnew_ref, key_ref, t_ref, *, n_past, topk):
    rows = sc_ref.shape[0]
    n_blocks = n_past // LANES
    for c in range(n_blocks):
        sl = slice(c * LANES, (c + 1) * LANES)
        key_ref[:, sl] = _order_key(sc_ref[:, sl])
    ik = ikb_ref[...].astype(F32)
    iw = iw_ref[...]
    s_new = jnp.zeros((rows, 1), F32)
    for h in range(IDX_HEADS):
        pair = iq_ref[:, (h // 2) * LANES:(h // 2 + 1) * LANES].astype(F32)
        qh = jnp.where(_half_mask(rows, h % 2 == 1), pair, 0.0)
        s_new = s_new + jnp.maximum(jnp.sum(qh * ik, axis=-1, keepdims=True), 0.0) * iw[:, h:h + 1]
    lane = lax.broadcasted_iota(jnp.int32, (rows, LANES), 1)
    key_ref[:, n_past:n_past + LANES] = _order_key(jnp.where(lane == 0, s_new, -jnp.inf))

    ri = lax.broadcasted_iota(jnp.int32, (LANES, DSA_KV_HEADS * LANES), 0)
    ci = lax.broadcasted_iota(jnp.int32, (LANES, DSA_KV_HEADS * LANES), 1)
    spread = jnp.where(ci // DSA_KV_HEADS == ri, 1.0, 0.0).astype(BF16)

    def write(rws, c, sel):
        if c == n_blocks:
            bnew_ref[rws, :] = jnp.where(sel, 0.0, NEG_INF)
        else:
            wide = jnp.dot(jnp.where(sel, 1.0, 0.0).astype(BF16), spread, preferred_element_type=F32)
            w = DSA_KV_HEADS * LANES
            bias_ref[rws, c * w:(c + 1) * w] = jnp.where(wide > 0.5, 0.0, NEG_INF)

    _select_topk(key_ref, t_ref, n_blocks + 1, topk, write)


def dsa_sample_select(scores, iq, iw, ikb, *, topk):
    m, n_past = scores.shape
    full = lambda a: pl.BlockSpec(a.shape, lambda i: (0,) * a.ndim)
    wide = DSA_KV_HEADS * n_past
    return pl.pallas_call(
        functools.partial(_dsa_select_sample_kernel, n_past=n_past, topk=topk),
        grid=(1,),
        in_specs=[full(scores), full(iq), full(iw), full(ikb)],
        out_specs=[pl.BlockSpec((m, wide), lambda i: (0, 0)), pl.BlockSpec((m, LANES), lambda i: (0, 0))],
        out_shape=[jax.ShapeDtypeStruct((m, wide), F32), jax.ShapeDtypeStruct((m, LANES), F32)],
        scratch_shapes=[pltpu.VMEM((m, n_past + LANES), jnp.int32), pltpu.VMEM((m, LANES), jnp.int32)],
        compiler_params=_cparams(("arbitrary",)),
        name="dsa_sample_select",
    )(scores, iq, iw, ikb)


def _dsa_attend_sample_kernel(pt_ref, q_ref, bias_ref, bnew_ref, kn_ref, vn_ref, pk_hbm, pv_hbm, y_ref,
                              kbuf, vbuf, sem, *, page_base, n_pages, chunk):
    b = pl.program_id(0)
    wide = DSA_KV_HEADS * PAGE_SIZE
    n_chunks = n_pages // chunk
    group = DSA_HEADS // DSA_KV_HEADS

    def page_copies(row, c, slot):
        out = []
        for j in range(chunk):
            src = pl.ds(pl.multiple_of((page_base + pt_ref[row, c * chunk + j]) * wide, wide), wide)
            dst = pl.ds(j * wide, wide)
            out.append(pltpu.make_async_copy(pk_hbm.at[src, :], kbuf.at[slot, dst, :], sem.at[0, slot]))
            out.append(pltpu.make_async_copy(pv_hbm.at[src, :], vbuf.at[slot, dst, :], sem.at[1, slot]))
        return out

    @pl.when(b == 0)
    def _():
        for cp in page_copies(0, 0, 0):
            cp.start()

    q8 = q_ref[0]
    first = lax.broadcasted_iota(jnp.int32, (DSA_HEADS, DSA_HD), 0) < group
    hrow = lax.broadcasted_iota(jnp.int32, (DSA_HEADS, wide), 0)
    col = lax.broadcasted_iota(jnp.int32, (DSA_HEADS, wide), 1)
    own = (col % DSA_KV_HEADS) == (hrow // group)
    m_run = jnp.full((DSA_HEADS, DSA_HD), NEG_INF, F32)
    l_run = jnp.zeros((DSA_HEADS, DSA_HD), F32)
    acc = jnp.zeros((DSA_HEADS, DSA_HD), F32)
    for c in range(n_chunks):
        slot = c % 2
        if c + 1 < n_chunks:
            for cp in page_copies(b, c + 1, 1 - slot):
                cp.start()
        else:
            @pl.when(b + 1 < pl.num_programs(0))
            def _():
                for cp in page_copies(b + 1, 0, 1 - slot):
                    cp.start()
        for cp in page_copies(b, c, slot):
            cp.wait()
        ss, oks = [], []
        for j in range(chunk):
            ok = jnp.logical_and(own, bias_ref[0, c * chunk + j:c * chunk + j + 1, :] == 0.0)
            kp = kbuf[slot, j * wide:(j + 1) * wide, :].astype(BF16)
            ss.append(jnp.where(ok, _dot_nt(q8, kp), NEG_INF))
            oks.append(ok)
        mx = ss[0]
        for s in ss[1:]:
            mx = jnp.maximum(mx, s)
        m_new = jnp.maximum(m_run, jnp.max(mx, axis=-1, keepdims=True))
        psum = jnp.zeros((DSA_HEADS, wide), F32)
        pv = jnp.zeros((DSA_HEADS, DSA_HD), F32)
        for j in range(chunk):
            p = jnp.where(oks[j], jnp.exp(ss[j] - m_new[:, 0:1]), 0.0)
            psum = psum + p
            vp = vbuf[slot, j * wide:(j + 1) * wide, :].astype(BF16)
            pv = pv + jnp.dot(p.astype(BF16), vp, preferred_element_type=F32)
        alpha = jnp.exp(m_run - m_new)
        l_run = alpha * l_run + jnp.sum(psum, axis=-1, keepdims=True)
        acc = alpha * acc + pv
        m_run = m_new

    kn = kn_ref[0].astype(BF16).astype(F32)
    vn = vn_ref[0].astype(BF16).astype(F32)
    bn = bnew_ref[0][:, 0:1]
    s_new = jnp.sum(q8.astype(F32) * jnp.where(first, kn[:, :DSA_HD], kn[:, DSA_HD:]), axis=-1, keepdims=True) + bn
    m_new = jnp.maximum(m_run, s_new)
    p_new = jnp.where(bn == 0.0, jnp.exp(s_new - m_new), 0.0)
    alpha = jnp.exp(m_run - m_new)
    acc = alpha * acc + p_new * jnp.where(first, vn[:, :DSA_HD], vn[:, DSA_HD:])
    y_ref[0] = acc / (alpha * l_run + p_new)


def dsa_sample_attend(page_table, q, bias, bnew, kn, vn, pool_k, pool_v, page_base, *, chunk):
    m, n_pages = page_table.shape
    wide = DSA_KV_HEADS * PAGE_SIZE
    assert n_pages % chunk == 0 and (n_pages // chunk) % 2 == 0
    per_b = lambda shape: pl.BlockSpec((1,) + shape, lambda b, pt: (b, 0, 0))
    hbm = pl.BlockSpec(memory_space=pl.ANY)
    return pl.pallas_call(
        functools.partial(_dsa_attend_sample_kernel, page_base=page_base, n_pages=n_pages, chunk=chunk),
        grid_spec=pltpu.PrefetchScalarGridSpec(
            num_scalar_prefetch=1,
            grid=(m,),
            in_specs=[per_b((DSA_HEADS, DSA_HD)), per_b((n_pages, wide)),
                      per_b((1, LANES)), per_b((1, DSA_KV_HEADS * DSA_HD)), per_b((1, DSA_KV_HEADS * DSA_HD)),
                      hbm, hbm],
            out_specs=per_b((DSA_HEADS, DSA_HD)),
            scratch_shapes=[pltpu.VMEM((2, chunk * wide, DSA_HD), F32), pltpu.VMEM((2, chunk * wide, DSA_HD), F32),
                            pltpu.SemaphoreType.DMA((2, 2))],
        ),
        out_shape=jax.ShapeDtypeStruct((m, DSA_HEADS, DSA_HD), F32),
        compiler_params=_cparams(("arbitrary",)),
        name="dsa_sample_attend",
    )(page_table, q, bias, bnew, kn, vn, pool_k, pool_v)


def _swa_step_kernel(q_ref, kt_ref, vt_ref, kn_ref, vn_ref, sink_ref, y_ref, *, bs):
    group = SWA_HEADS // SWA_KV_HEADS
    pairs = [(s, kv) for s in range(bs) for kv in range(SWA_KV_HEADS)]
    qs, scs, news = [], [], []
    for s, kv in pairs:
        qv = q_ref[s, kv * group:(kv + 1) * group, :]
        kn = kn_ref[s, kv:kv + 1, :].astype(BF16).astype(F32)
        scs.append(jnp.dot(qv, kt_ref[s, kv].astype(BF16), preferred_element_type=F32))
        news.append(jnp.sum(qv.astype(F32) * kn, axis=-1, keepdims=True))
    sc = jnp.concatenate(scs, axis=0)
    s_new = jnp.concatenate(news, axis=0)
    sink = jnp.concatenate([sink_ref[...]] * bs, axis=0)
    mx = jnp.maximum(jnp.maximum(jnp.max(sc, axis=-1, keepdims=True), s_new), sink)
    p = jnp.exp(sc - mx)
    p_new = jnp.exp(s_new - mx)
    inv = 1.0 / (jnp.sum(p, axis=-1, keepdims=True) + p_new + jnp.exp(sink - mx))
    pb = p.astype(BF16)
    for n, (s, kv) in enumerate(pairs):
        rows = slice(n * group, (n + 1) * group)
        vn = vn_ref[s, kv:kv + 1, :].astype(BF16).astype(F32)
        o = _dot_nt(pb[rows, :], vt_ref[s, kv].astype(BF16)) + p_new[rows, :] * vn
        y_ref[s, kv * group:(kv + 1) * group, :] = o * inv[rows, :]


def swa_sample(q, kt, vt, kn, vn, sinks, *, bs):
    m = q.shape[0]
    cache = pl.BlockSpec((bs, SWA_KV_HEADS, SWA_HD, WINDOW), lambda i: (i, 0, 0, 0))
    new = pl.BlockSpec((bs, SWA_KV_HEADS, SWA_HD), lambda i: (i, 0, 0))
    return pl.pallas_call(
        functools.partial(_swa_step_kernel, bs=bs),
        grid=(m // bs,),
        in_specs=[pl.BlockSpec((bs, SWA_HEADS, SWA_HD), lambda i: (i, 0, 0)), cache, cache, new, new,
                  pl.BlockSpec((SWA_HEADS, 1), lambda i: (0, 0))],
        out_specs=pl.BlockSpec((bs, SWA_HEADS, SWA_HD), lambda i: (i, 0, 0)),
        out_shape=jax.ShapeDtypeStruct((m, SWA_HEADS, SWA_HD), F32),
        compiler_params=_cparams(("parallel",)),
        name="swa_sample",
    )(q, kt, vt, kn, vn, sinks.reshape(SWA_HEADS, 1))


def _mem_step_kernel(q_ref, k_ref, v_ref, y_ref, *, bs):
    rows = q_ref.shape[1]
    n = N_MEM * MEM_HEADS
    hrow = lax.broadcasted_iota(jnp.int32, (rows, n), 0)
    col = lax.broadcasted_iota(jnp.int32, (rows, n), 1)
    own = (col % MEM_HEADS) == (hrow % MEM_HEADS)
    for s in range(bs):
        q = (q_ref[s] * (MEM_HD ** -0.5)).astype(BF16)
        sc = jnp.where(own, _dot_nt(q, k_ref[s * n:(s + 1) * n, :].astype(BF16)), NEG_INF)
        p = jnp.where(own, jnp.exp(sc - jnp.max(sc, axis=-1, keepdims=True)), 0.0)
        o = jnp.dot(p.astype(BF16), v_ref[s * n:(s + 1) * n, :].astype(BF16), preferred_element_type=F32)
        y_ref[s] = o / jnp.sum(p, axis=-1, keepdims=True)


def mem_attn_sample(q, mk, mv, layer, *, bs):
    m, rows, _ = q.shape
    n = N_MEM * MEM_HEADS
    nb = m // bs
    cache = pl.BlockSpec((bs * n, MEM_HD), lambda i: (layer * nb + i, 0))
    return pl.pallas_call(
        functools.partial(_mem_step_kernel, bs=bs),
        grid=(nb,),
        in_specs=[pl.BlockSpec((bs, rows, MEM_HD), lambda i: (i, 0, 0)), cache, cache],
        out_specs=pl.BlockSpec((bs, rows, MEM_HD), lambda i: (i, 0, 0)),
        out_shape=jax.ShapeDtypeStruct((m, rows, MEM_HD), F32),
        compiler_params=_cparams(("parallel",)),
        name="mem_attn_sample",
    )(q, mk, mv)


TM_FFN, TF_FFN, SUB_FFN = 1024, 512, 512
TM_PROJ, TN_AB, TN_C, TN_MEM = 1024, 768, 512, 512
TM_OUT = 512
TR_FEAT = 512
TC_GLA, NSEQ_GLA = 256, 2
TQ_MEM = 512
BS_SAMPLE = 8
PG_SCORES = 32
PG_ATTEND = 16


def _prep_w_in_ab(w):
    sizes = (GLA_HEADS * GLA_DK, GLA_HEADS * GLA_DK, GLA_HEADS * GLA_DV, GLA_HEADS * GLA_DV, GLA_GATE_RANK,
             DSA_HEADS * DSA_HD, DSA_KV_HEADS * DSA_HD, DSA_KV_HEADS * DSA_HD, IDX_HEADS * IDX_HD, IDX_HEADS, IDX_HD)
    offs = np.cumsum((0,) + sizes)
    gq, gk, gv, gr, gd, dq, dk, dv, iq, iw, ik = [w[:, int(offs[j]):int(offs[j + 1])] for j in range(len(sizes))]
    pad = lambda n: jnp.zeros((w.shape[0], n), w.dtype)
    small = jnp.concatenate([ik, gd, iw, pad(LANES - IDX_HD - GLA_GATE_RANK - IDX_HEADS)], axis=1)
    out = jnp.concatenate([gv, gr, dq, gq, gk, iq, dk, dv, small, pad(AB_WIDTH - AB_SMALL - LANES)], axis=1)
    return out.astype(BF16)


def _prep_gate(w_up):
    return jnp.zeros((LANES, GLA_HEADS * GLA_DK), F32).at[SM_GD:SM_GD + GLA_GATE_RANK].set(w_up)


def kernel(x_prompt, x_sample, mem_prompt, cache_dsa_k, cache_dsa_v, cache_dsa_idx_k, state_gla, cache_swa_k, cache_swa_v, cache_mem_k, cache_mem_v, page_table, norm_ffn, w_ffn_gate, w_ffn_up, w_ffn_down, norm_mix, w_in_ab, w_gla_gate_up, b_gla_gate, gla_out_norm, idx_k_norm, w_out_ab, w_in_c, swa_sinks, w_out_c, norm_mem_q, norm_mem_src, w_mem_q, w_mem_kv, w_mem_o, final_norm):
    depth = norm_mix.shape[0]
    bp, seq, _ = x_prompt.shape
    bs = x_sample.shape[0]
    n_pool = cache_dsa_k.shape[1]
    gla_w = GLA_HEADS * GLA_DV

    w_ab = [_prep_w_in_ab(w_in_ab[i]) for i in range(w_in_ab.shape[0])]
    w_gate = [_prep_gate(w_gla_gate_up[i]) for i in range(w_in_ab.shape[0])]
    w_oab = w_out_ab.astype(BF16)
    w_c, w_oc = w_in_c.astype(BF16), w_out_c.astype(BF16)
    w_mq, w_mkv, w_mo = w_mem_q.astype(BF16), w_mem_kv.astype(BF16), w_mem_o.astype(BF16)

    def rope_tabs(pos):
        return _rope_tables(pos, DSA_HD) + _rope_tables(pos, IDX_HD)

    ffn_bf16 = {}

    def ffn_pair(x, layer, half, last, sample):
        fin = final_norm if last else None
        if sample:
            y, *ffn_bf16[layer, half] = ffn(x, norm_ffn[layer, half], w_ffn_gate, w_ffn_up, w_ffn_down, (layer, half),
                                            fin, tm=TM_FFN, tf=TF_FFN, sub=SUB_FFN)
            return y
        return ffn(x, norm_ffn[layer, half], *ffn_bf16[layer, half], None, fin, tm=TM_FFN, tf=TF_FFN, sub=SUB_FFN)

    def prompt_group():
        tabs_p = rope_tabs(jnp.arange(seq))
        x = x_prompt.reshape(bp * seq, D_MODEL)
        mem = mem_prompt.reshape(bp * N_MEM, D_MODEL)
        mem_kv = [norm_proj(mem, norm_mem_src[l], w_mkv[l], tm=TM_PROJ, tn=TN_MEM) for l in range(depth)]
        st_ab_p, st_c_p = [], []
        for l in range(depth):
            i = l // 2
            x = ffn_pair(x, l, 0, False, False)
            if l % 2 == 0:
                z = norm_proj(x, norm_mix[l], w_ab[i], tm=TM_PROJ, tn=TN_AB)
                q, k, kb, vb, iq, ik, ikb, iw = ab_features(z, tabs_p, idx_k_norm[i], seq=seq, tr=TR_FEAT)
                y_gla, st_t = gla_prompt(z, w_gate[i], b_gla_gate[i].reshape(1, -1), gla_out_norm[i].reshape(1, -1),
                                         batch=bp, seq=seq, tc=TC_GLA, n_seq=NSEQ_GLA)
                y_dsa = dsa_prompt(q, iq, iw, kb, vb, ikb, batch=bp, seq=seq)
                mix_out = ([y_gla, y_dsa], [w_oab[i, :gla_w], w_oab[i, gla_w:]])
                n_pg = seq // PAGE_SIZE
                st_ab_p.append((k.reshape(bp, n_pg, PAGE_SIZE, DSA_KV_HEADS, DSA_HD),
                                z[:, AB_DV:AB_DV + 256].reshape(bp, n_pg, PAGE_SIZE, DSA_KV_HEADS, DSA_HD),
                                ik.reshape(bp, n_pg, PAGE_SIZE, IDX_HD),
                                jnp.swapaxes(st_t, 2, 3)))
            else:
                z = norm_proj(x, norm_mix[l], w_c[i], tm=TM_PROJ, tn=TN_C)
                q, k, kb, kbs, vb, vbs = c_features(z, tabs_p[2:], seq=seq, tr=TR_FEAT)
                y = swa_prompt(q, kb, kbs, vb, vbs, swa_sinks[i], batch=bp, seq=seq)
                mix_out = ([y], [w_oc[i]])
                st_c_p.append((k.reshape(bp, seq, SWA_KV_HEADS, SWA_HD)[:, -WINDOW:],
                               z[:, C_V:C_V + 256].reshape(bp, seq, SWA_KV_HEADS, SWA_HD)[:, -WINDOW:]))
            x = mem_attn_prompt(x, *mix_out, norm_mem_q[l], w_mq[l], mem_kv[l], w_mo[l], batch=bp, seq=seq, tq=TQ_MEM)
            x = ffn_pair(x, l, 1, l == depth - 1, False)
        return x.reshape(bp, seq, D_MODEL), st_ab_p, st_c_p, mem_kv

    tabs_s = rope_tabs(PAST_LEN + jnp.arange(1))
    n_pages = page_table.shape[1]
    topk = min(DSA_TOPK_MAX, (PAST_LEN + 1) // 4)
    pool_k = cache_dsa_k.reshape(-1, DSA_HD)
    pool_v = cache_dsa_v.reshape(-1, DSA_HD)
    pool_ik_t = jnp.swapaxes(cache_dsa_idx_k, 2, 3).reshape(-1, IDX_HD, PAGE_SIZE)
    mem_k_rows = cache_mem_k.reshape(-1, MEM_HD)
    mem_v_rows = cache_mem_v.reshape(-1, MEM_HD)
    x = x_sample.reshape(bs, D_MODEL)
    st_ab_s, st_c_s = [], []
    for l in range(depth):
        i = l // 2
        x = ffn_pair(x, l, 0, False, True)
        if l % 2 == 0:
            z = norm_proj(x, norm_mix[l], w_ab[i], tm=TM_PROJ, tn=TN_AB)
            q, k, kb, vb, iq, ik, ikb, iw = ab_features(z, tabs_s, idx_k_norm[i], seq=1, tr=TR_FEAT)
            p = gla_sample_prep(z, w_gate[i], b_gla_gate[i].reshape(1, -1)).reshape(bs, 16, GLA_DK)
            st_new, y_gla = gla_sample(p, z, state_gla[i], gla_out_norm[i].reshape(1, -1), bs=BS_SAMPLE)
            scores = dsa_sample_scores(page_table, iq.reshape(bs, IDX_HEADS, IDX_HD),
                                       iw[:, :IDX_HEADS].reshape(bs, IDX_HEADS, 1), pool_ik_t, i * n_pool, chunk=PG_SCORES)
            bias, bnew = dsa_sample_select(scores.reshape(bs, n_pages * PAGE_SIZE), iq, iw, ikb, topk=topk)
            v_new = z[:, AB_DV:AB_DV + 256]
            y_dsa = dsa_sample_attend(
                page_table, q.reshape(bs, DSA_HEADS, DSA_HD), bias.reshape(bs, n_pages, DSA_KV_HEADS * PAGE_SIZE),
                bnew.reshape(bs, 1, LANES), k.reshape(bs, 1, 256), v_new.reshape(bs, 1, 256),
                pool_k, pool_v, i * n_pool, chunk=PG_ATTEND).reshape(bs, DSA_HEADS * DSA_HD)
            x = out_proj(x, [y_gla, y_dsa], [w_oab[i, :gla_w], w_oab[i, gla_w:]], tm=TM_OUT)
            st_ab_s.append((k.reshape(bs, 1, DSA_KV_HEADS, DSA_HD), v_new.reshape(bs, 1, DSA_KV_HEADS, DSA_HD),
                            ik.reshape(bs, 1, IDX_HD), st_new))
        else:
            z = norm_proj(x, norm_mix[l], w_c[i], tm=TM_PROJ, tn=TN_C)
            q, k, kb, kbs, vb, vbs = c_features(z, tabs_s[2:], seq=1, tr=TR_FEAT)
            v_new = z[:, C_V:C_V + 256]
            y = swa_sample(q.reshape(bs, SWA_HEADS, SWA_HD), jnp.transpose(cache_swa_k[i], (0, 2, 3, 1)),
                           jnp.transpose(cache_swa_v[i], (0, 2, 3, 1)), k.reshape(bs, SWA_KV_HEADS, SWA_HD),
                           v_new.reshape(bs, SWA_KV_HEADS, SWA_HD), swa_sinks[i], bs=BS_SAMPLE)
            x = out_proj(x, [y.reshape(bs, SWA_HEADS * SWA_HD)], [w_oc[i]], tm=TM_OUT)
            st_c_s.append((jnp.concatenate([cache_swa_k[i][:, 1:], k.reshape(bs, 1, SWA_KV_HEADS, SWA_HD)], axis=1),
                           jnp.concatenate([cache_swa_v[i][:, 1:], v_new.reshape(bs, 1, SWA_KV_HEADS, SWA_HD)], axis=1)))
        qm = norm_proj(x, norm_mem_q[l], w_mq[l], tm=TM_PROJ, tn=TN_MEM).reshape(bs, MEM_HEADS, MEM_HD)
        qm = jnp.pad(qm, ((0, 0), (0, SUBLANES - MEM_HEADS), (0, 0)))
        om = mem_attn_sample(qm, mem_k_rows, mem_v_rows, l, bs=BS_SAMPLE)
        x = out_proj(x, [om[:, :MEM_HEADS].reshape(bs, MEM_HEADS * MEM_HD)], [w_mo[l]], tm=TM_OUT)
        x = ffn_pair(x, l, 1, l == depth - 1, True)
    y_sample = x.reshape(bs, 1, D_MODEL)

    y_prompt, st_ab_p, st_c_p, mem_kv = prompt_group()

    stk = lambda sts, j: jnp.stack([s[j] for s in sts])
    mw = MEM_HEADS * MEM_HD
    mem_k_p = jnp.stack([kv[:, :mw].reshape(bp, N_MEM, MEM_HEADS, MEM_HD) for kv in mem_kv])
    mem_v_p = jnp.stack([kv[:, mw:].reshape(bp, N_MEM, MEM_HEADS, MEM_HD) for kv in mem_kv])
    return (y_prompt, y_sample, stk(st_ab_p, 0), stk(st_ab_p, 1), stk(st_ab_p, 2),
            stk(st_ab_s, 0), stk(st_ab_s, 1), stk(st_ab_s, 2), stk(st_ab_p, 3), stk(st_ab_s, 3),
            stk(st_c_p, 0), stk(st_c_p, 1), stk(st_c_s, 0), stk(st_c_s, 1), mem_k_p, mem_v_p)
```

```python
import functools

import jax
import jax.numpy as jnp
import numpy as np
from jax import lax
from jax.experimental import pallas as pl
from jax.experimental.pallas import tpu as pltpu

F32 = jnp.float32
BF16 = jnp.bfloat16

D_MODEL = 2048
D_FF = 5632
EPS = 1e-6
ROPE_THETA = 10000.0
NEG_INF = -1e30
PAST_LEN = 8192
PAGE_SIZE = 128
Q_BLOCK = 128
GLA_HEADS, GLA_DK, GLA_DV = 4, 128, 256
GLA_GATE_RANK = 16
GLA_GATE_TAU = 16.0
GLA_CHUNK = 64
DSA_HEADS, DSA_KV_HEADS, DSA_HD = 8, 2, 128
IDX_HEADS, IDX_HD = 8, 64
DSA_TOPK_MAX = 256
SWA_HEADS, SWA_KV_HEADS, SWA_HD = 32, 4, 64
WINDOW = 128
MEM_HEADS, MEM_HD = 4, 128
N_MEM = 256

LANES = 128
SUBLANES = 8
VMEM_LIMIT_BYTES = 60000 * 1024
ELEM_SUB = 128

AB_GV, AB_GR, AB_DQ = 0, 1024, 2048
AB_GQ, AB_GK, AB_IQ = 3072, 3584, 4096
AB_DK, AB_DV = 4608, 4864
AB_SMALL = 5120
AB_WIDTH = 5376
SM_IK, SM_GD, SM_IW = 0, 64, 80
C_Q, C_K, C_V, C_WIDTH = 0, 2048, 2304, 2560


def _cparams(sem):
    return pltpu.CompilerParams(dimension_semantics=sem, vmem_limit_bytes=VMEM_LIMIT_BYTES)


def _rms(x, g):
    y = x * lax.rsqrt(jnp.mean(x * x, axis=-1, keepdims=True) + EPS)
    return y * g


def _ffn_kernel(x_ref, g_ref, wg_ref, wu_ref, wd_ref, fg_ref, o_ref, *rest, final_norm, sub, emit_cast):
    h_ref = rest[-1]
    j = pl.program_id(1)
    tm = x_ref.shape[0]
    if emit_cast:
        for src, dst in zip((wg_ref, wu_ref, wd_ref), rest[:3]):
            dst[...] = src[...].astype(BF16)
        wg_ref, wu_ref, wd_ref = rest[:3]

    esub = min(ELEM_SUB, tm)

    def row_group(r):
        return pl.ds(pl.multiple_of(r * esub, esub), esub)

    @pl.when(j == 0)
    def _():
        def body(r, carry):
            rows = row_group(r)
            h_ref[rows, :] = _rms(x_ref[rows, :], g_ref[...]).astype(BF16)
            o_ref[rows, :] = jnp.zeros((esub, D_MODEL), F32)
            return carry

        lax.fori_loop(0, tm // esub, body, 0)

    for r in range(tm // sub):
        rows = slice(r * sub, (r + 1) * sub)
        h = h_ref[rows, :]
        a = jnp.dot(h, wg_ref[...], preferred_element_type=F32)
        u = jnp.dot(h, wu_ref[...], preferred_element_type=F32)
        act = (a * jax.nn.sigmoid(a) * u).astype(BF16)
        o_ref[rows, :] += jnp.dot(act, wd_ref[...], preferred_element_type=F32)

    @pl.when(j == pl.num_programs(1) - 1)
    def _():
        def body(r, carry):
            rows = row_group(r)
            y = x_ref[rows, :] + 0.5 * o_ref[rows, :]
            if final_norm:
                y = _rms(y, fg_ref[...])
            o_ref[rows, :] = y
            return carry

        lax.fori_loop(0, tm // esub, body, 0)


def ffn(x, g, wg, wu, wd, sel, final_g=None, *, tm, tf, sub):
    m = x.shape[0]
    tm = min(tm, m)
    sub = min(sub, tm)
    fg = g if final_g is None else final_g
    emit_cast = sel is not None
    wspec = lambda shape, imap: pl.BlockSpec(shape, imap)
    if emit_cast:
        assert m == tm and wg.dtype == F32
        layer, half = sel
        w_in = [wspec((None, None, D_MODEL, tf), lambda i, j: (layer, half, 0, j)),
                wspec((None, None, D_MODEL, tf), lambda i, j: (layer, half, 0, j)),
                wspec((None, None, tf, D_MODEL), lambda i, j: (layer, half, j, 0))]
    else:
        assert wg.dtype == BF16 and wg.ndim == 2
        w_in = [wspec((D_MODEL, tf), lambda i, j: (0, j)), wspec((D_MODEL, tf), lambda i, j: (0, j)),
                wspec((tf, D_MODEL), lambda i, j: (j, 0))]
    out_specs = [pl.BlockSpec((tm, D_MODEL), lambda i, j: (i, 0))]
    out_shape = [jax.ShapeDtypeStruct((m, D_MODEL), F32)]
    if emit_cast:
        out_specs += [wspec((D_MODEL, tf), lambda i, j: (0, j)), wspec((D_MODEL, tf), lambda i, j: (0, j)),
                      wspec((tf, D_MODEL), lambda i, j: (j, 0))]
        out_shape += [jax.ShapeDtypeStruct((D_MODEL, D_FF), BF16), jax.ShapeDtypeStruct((D_MODEL, D_FF), BF16),
                      jax.ShapeDtypeStruct((D_FF, D_MODEL), BF16)]
    outs = pl.pallas_call(
        functools.partial(_ffn_kernel, final_norm=final_g is not None, sub=sub, emit_cast=emit_cast),
        grid=(m // tm, D_FF // tf),
        in_specs=[pl.BlockSpec((tm, D_MODEL), lambda i, j: (i, 0)), pl.BlockSpec((1, D_MODEL), lambda i, j: (0, 0))]
                 + w_in + [pl.BlockSpec((1, D_MODEL), lambda i, j: (0, 0))],
        out_specs=out_specs,
        out_shape=out_shape,
        scratch_shapes=[pltpu.VMEM((tm, D_MODEL), BF16)],
        compiler_params=_cparams(("parallel", "arbitrary")),
        name="ffn_cast" if emit_cast else "ffn",
    )(x, g.reshape(1, D_MODEL), wg, wu, wd, fg.reshape(1, D_MODEL))
    return tuple(outs) if emit_cast else outs[0]


def _proj_kernel(x_ref, g_ref, w_ref, o_ref, h_ref, *, sub):
    tm = x_ref.shape[0]

    @pl.when(pl.program_id(1) == 0)
    def _():
        for r in range(tm // sub):
            rows = slice(r * sub, (r + 1) * sub)
            h_ref[rows, :] = _rms(x_ref[rows, :], g_ref[...]).astype(BF16)

    o_ref[...] = jnp.dot(h_ref[...], w_ref[...], preferred_element_type=F32)


def norm_proj(x, g, w, *, tm, tn, sub=512):
    m, n = x.shape[0], w.shape[1]
    tm = min(tm, m)
    return pl.pallas_call(
        functools.partial(_proj_kernel, sub=min(sub, tm)),
        grid=(m // tm, n // tn),
        in_specs=[
            pl.BlockSpec((tm, D_MODEL), lambda i, j: (i, 0)),
            pl.BlockSpec((1, D_MODEL), lambda i, j: (0, 0)),
            pl.BlockSpec((D_MODEL, tn), lambda i, j: (0, j)),
        ],
        out_specs=pl.BlockSpec((tm, tn), lambda i, j: (i, j)),
        out_shape=jax.ShapeDtypeStruct((m, n), F32),
        scratch_shapes=[pltpu.VMEM((tm, D_MODEL), BF16)],
        compiler_params=_cparams(("parallel", "arbitrary")),
        name="norm_proj",
    )(x, g.reshape(1, D_MODEL), w)


def _outproj_kernel(*refs, n_in):
    x_ref = refs[0]
    y_refs = refs[1:1 + n_in]
    w_refs = refs[1 + n_in:1 + 2 * n_in]
    o_ref = refs[1 + 2 * n_in]
    acc = x_ref[...]
    for y_ref, w_ref in zip(y_refs, w_refs):
        acc = acc + jnp.dot(y_ref[...].astype(BF16), w_ref[...], preferred_element_type=F32)
    o_ref[...] = acc


def out_proj(x, ys, ws, *, tm):
    m = x.shape[0]
    tm = min(tm, m)
    n_in = len(ys)
    in_specs = [pl.BlockSpec((tm, D_MODEL), lambda i: (i, 0))]
    in_specs += [pl.BlockSpec((tm, y.shape[1]), lambda i: (i, 0)) for y in ys]
    in_specs += [pl.BlockSpec(w.shape, lambda i: (0, 0)) for w in ws]
    return pl.pallas_call(
        functools.partial(_outproj_kernel, n_in=n_in),
        grid=(m // tm,),
        in_specs=in_specs,
        out_specs=pl.BlockSpec((tm, D_MODEL), lambda i: (i, 0)),
        out_shape=jax.ShapeDtypeStruct((m, D_MODEL), F32),
        compiler_params=_cparams(("parallel",)),
        name="out_proj",
    )(x, *ys, *ws)


def _rope_tables(pos, hd):
    half = hd // 2
    inv = ROPE_THETA ** (-jnp.arange(half, dtype=F32) / half)
    ang = pos.astype(F32)[:, None] * inv[None, :]
    cos, sin = jnp.cos(ang), jnp.sin(ang)
    reps = LANES // hd
    return (jnp.concatenate([cos, cos] * reps, axis=-1),
            jnp.concatenate([-sin, sin] * reps, axis=-1))


def _rope128(x, cos, sin):
    return x * cos + pltpu.roll(x, 64, 1) * sin


def _rope64(x, cos, sin, lower):
    partner = jnp.where(lower, pltpu.roll(x, 96, 1), pltpu.roll(x, 32, 1))
    return x * cos + partner * sin


def _lower32_mask(rows):
    lane = lax.broadcasted_iota(jnp.int32, (rows, LANES), 1)
    return (lane % 64) < 32


def _ab_feat_kernel(dq_ref, dk_ref, dv_ref, iq_ref, sm_ref, c128_ref, s128_ref, c64_ref, s64_ref, gik_ref,
                    q_ref, k_ref, kb_ref, vb_ref, iqo_ref, ik_ref, ikb_ref, iw_ref):
    rows = dq_ref.shape[0]
    c128, s128 = c128_ref[...], s128_ref[...]
    c64, s64 = c64_ref[...], s64_ref[...]
    lower = _lower32_mask(rows)
    for h in range(DSA_HEADS):
        sl = slice(h * LANES, (h + 1) * LANES)
        q_ref[:, sl] = (_rope128(dq_ref[:, sl], c128, s128) * (DSA_HD ** -0.5)).astype(BF16)
    for h in range(DSA_KV_HEADS):
        sl = slice(h * LANES, (h + 1) * LANES)
        kr = _rope128(dk_ref[:, sl], c128, s128)
        k_ref[:, sl] = kr
        kb_ref[:, sl] = kr.astype(BF16)
    vb_ref[...] = dv_ref[...].astype(BF16)
    for p in range(IDX_HEADS * IDX_HD // LANES):
        sl = slice(p * LANES, (p + 1) * LANES)
        iqo_ref[:, sl] = (_rope64(iq_ref[:, sl], c64, s64, lower) * (IDX_HD ** -0.5)).astype(BF16)
    sm = sm_ref[...]
    lane = lax.broadcasted_iota(jnp.int32, (rows, LANES), 1)
    ik = jnp.where(lane < IDX_HD, sm, 0.0)
    ik = ik * lax.rsqrt(jnp.sum(ik * ik, axis=-1, keepdims=True) / IDX_HD + EPS) * gik_ref[...]
    ik = _rope64(ik, c64, s64, lower)
    ik_ref[...] = ik[:, :IDX_HD]
    ikb_ref[...] = jnp.where(lane < IDX_HD, ik, pltpu.roll(ik, 64, 1)).astype(BF16)
    iw_ref[...] = pltpu.roll(sm, LANES - SM_IW, 1) * (IDX_HEADS ** -0.5)


def ab_features(z, tabs, gik, *, seq, tr):
    m = z.shape[0]
    tr = min(tr, m)
    c128, s128, c64, s64 = tabs
    if c128.shape[0] == 1:
        tab_spec = pl.BlockSpec((1, LANES), lambda i: (0, 0))
    else:
        nt = seq // tr
        tab_spec = pl.BlockSpec((tr, LANES), lambda i: (i % nt, 0))
    col = lambda w, off: pl.BlockSpec((tr, w), lambda i: (i, off // w))
    row = lambda w: pl.BlockSpec((tr, w), lambda i: (i, 0))
    gik_pad = jnp.zeros((1, LANES), F32).at[0, :IDX_HD].set(gik)
    return pl.pallas_call(
        _ab_feat_kernel,
        grid=(m // tr,),
        in_specs=[col(1024, AB_DQ), col(256, AB_DK), col(256, AB_DV), col(512, AB_IQ), col(128, AB_SMALL),
                  tab_spec, tab_spec, tab_spec, tab_spec, pl.BlockSpec((1, LANES), lambda i: (0, 0))],
        out_specs=[row(1024), row(256), row(256), row(256), row(512), row(IDX_HD), row(128), row(128)],
        out_shape=[jax.ShapeDtypeStruct((m, 1024), BF16),
                   jax.ShapeDtypeStruct((m, 256), F32),
                   jax.ShapeDtypeStruct((m, 256), BF16),
                   jax.ShapeDtypeStruct((m, 256), BF16),
                   jax.ShapeDtypeStruct((m, 512), BF16),
                   jax.ShapeDtypeStruct((m, IDX_HD), F32),
                   jax.ShapeDtypeStruct((m, 128), BF16),
                   jax.ShapeDtypeStruct((m, 128), F32)],
        compiler_params=_cparams(("parallel",)),
        name="ab_features",
    )(z, z, z, z, z, c128, s128, c64, s64, gik_pad)


def _c_proj_feat_kernel(x_ref, g_ref, w_ref, c64_ref, s64_ref,
                        qo_ref, ko_ref, kb_ref, kbs_ref, vo_ref, vb_ref, vbs_ref, *, sub):
    tm = x_ref.shape[0]
    table_rows = c64_ref.shape[0]
    lower = _lower32_mask(sub)
    for r in range(tm // sub):
        rows = slice(r * sub, (r + 1) * sub)
        trows = rows if table_rows > 1 else slice(0, 1)
        c64, s64 = c64_ref[trows, :], s64_ref[trows, :]
        h = _rms(x_ref[rows, :], g_ref[...]).astype(BF16)
        z = jnp.dot(h, w_ref[...], preferred_element_type=F32)
        for p in range(SWA_HEADS * SWA_HD // LANES):
            sl = slice(p * LANES, (p + 1) * LANES)
            qo_ref[rows, sl] = (_rope64(z[:, C_Q + p * LANES:C_Q + (p + 1) * LANES], c64, s64, lower)
                                * (SWA_HD ** -0.5)).astype(BF16)
        for p in range(SWA_KV_HEADS * SWA_HD // LANES):
            sl = slice(p * LANES, (p + 1) * LANES)
            kr = _rope64(z[:, C_K + p * LANES:C_K + (p + 1) * LANES], c64, s64, lower)
            v = z[:, C_V + p * LANES:C_V + (p + 1) * LANES]
            ko_ref[rows, sl] = kr
            kb_ref[rows, sl] = kr.astype(BF16)
            kbs_ref[rows, sl] = pltpu.roll(kr, 64, 1).astype(BF16)
            vo_ref[rows, sl] = v
            vb_ref[rows, sl] = v.astype(BF16)
            vbs_ref[rows, sl] = pltpu.roll(v, 64, 1).astype(BF16)


def c_proj_features(x, g, w, tabs, *, seq, tm, sub):
    m = x.shape[0]
    tm = min(tm, m)
    sub = min(sub, tm)
    c64, s64 = tabs
    if c64.shape[0] == 1:
        tab_spec = pl.BlockSpec((1, LANES), lambda i: (0, 0))
    else:
        nt = seq // tm
        tab_spec = pl.BlockSpec((tm, LANES), lambda i: (i % nt, 0))
    row = lambda wd: pl.BlockSpec((tm, wd), lambda i: (i, 0))
    return pl.pallas_call(
        functools.partial(_c_proj_feat_kernel, sub=sub),
        grid=(m // tm,),
        in_specs=[row(D_MODEL), pl.BlockSpec((1, D_MODEL), lambda i: (0, 0)),
                  pl.BlockSpec((D_MODEL, C_WIDTH), lambda i: (0, 0), pipeline_mode=pl.Buffered(1)),
                  tab_spec, tab_spec],
        out_specs=[row(2048), row(256), row(256), row(256), row(256), row(256), row(256)],
        out_shape=[jax.ShapeDtypeStruct((m, 2048), BF16),
                   jax.ShapeDtypeStruct((m, 256), F32),
                   jax.ShapeDtypeStruct((m, 256), BF16),
                   jax.ShapeDtypeStruct((m, 256), BF16),
                   jax.ShapeDtypeStruct((m, 256), F32),
                   jax.ShapeDtypeStruct((m, 256), BF16),
                   jax.ShapeDtypeStruct((m, 256), BF16)],
        compiler_params=_cparams(("parallel",)),
        name="c_proj_features",
    )(x, g.reshape(1, D_MODEL), w, c64, s64)


def _dot_nt(a, b, **kw):
    return lax.dot_general(a, b, (((1,), (1,)), ((), ())), preferred_element_type=F32, **kw)


def _dot_tn(a, b, **kw):
    return lax.dot_general(a, b, (((0,), (0,)), ((), ())), preferred_element_type=F32, **kw)


_HI = lax.Precision.HIGHEST


def _log_decay(sm, wgate, bgate):
    pre = jnp.dot(sm, wgate, preferred_element_type=F32, precision=_HI) + bgate
    return (jnp.minimum(pre, 0.0) - jnp.log1p(jnp.exp(-jnp.abs(pre)))) / GLA_GATE_TAU


def _gla_gate_out(o, r, gain):
    g = o * lax.rsqrt(jnp.mean(o * o, axis=-1, keepdims=True) + EPS) * gain
    return g * (r * jax.nn.sigmoid(r))


def _gla_prompt_kernel(gv_ref, gr_ref, gq_ref, gk_ref, sm_ref, wgate_ref, bgate_ref, gain_ref,
                       y_ref, st_ref, s_ref, b_ref, *, n_seq, n_chunks):
    c = pl.program_id(1)

    @pl.when(c == 0)
    def _():
        s_ref[...] = jnp.zeros_like(s_ref)

    tc = n_chunks * GLA_CHUNK
    ri = lax.broadcasted_iota(jnp.int32, (tc, tc), 0)
    ci = lax.broadcasted_iota(jnp.int32, (tc, tc), 1)
    tril = jnp.where(jnp.logical_and(ri // GLA_CHUNK == ci // GLA_CHUNK, ri >= ci), 1.0, 0.0).astype(F32)
    causal = (lax.broadcasted_iota(jnp.int32, (GLA_CHUNK, GLA_CHUNK), 0)
              >= lax.broadcasted_iota(jnp.int32, (GLA_CHUNK, GLA_CHUNK), 1))
    gain = gain_ref[...]
    for s in range(n_seq):
        la = _log_decay(sm_ref[s], wgate_ref[...], bgate_ref[...])
        b_ref[s] = jnp.dot(tril, la, preferred_element_type=F32, precision=_HI)
    for n in range(n_chunks):
        rows = slice(n * GLA_CHUNK, (n + 1) * GLA_CHUNK)
        for s in range(n_seq):
            b = b_ref[s, rows, :]
            b_end = b_ref[s, (n + 1) * GLA_CHUNK - 1:(n + 1) * GLA_CHUNK, :]
            k = gk_ref[s, rows, :]
            q_in = gq_ref[s, rows, :] * (GLA_DK ** -0.5) * jnp.exp(b)
            k_in = k * jnp.exp(-b)
            k_end = k * jnp.exp(b_end - b)
            decay = jnp.exp(b_end)
            for h in range(GLA_HEADS):
                dk = slice(h * GLA_DK, (h + 1) * GLA_DK)
                dv = slice(h * GLA_DV, (h + 1) * GLA_DV)
                v = gv_ref[s, rows, dv].astype(BF16)
                st = s_ref[s, h]
                qh = q_in[:, dk].astype(BF16)
                att = jnp.where(causal, _dot_nt(qh, k_in[:, dk].astype(BF16)), 0.0)
                o = _dot_nt(qh, st.astype(BF16)) + jnp.dot(att.astype(BF16), v, preferred_element_type=F32)
                s_ref[s, h] = st * decay[:, dk] + _dot_tn(v, k_end[:, dk].astype(BF16))
                y_ref[s, rows, dv] = _gla_gate_out(o, gr_ref[s, rows, dv], gain)

    @pl.when(c == pl.num_programs(1) - 1)
    def _():
        st_ref[...] = s_ref[...]


def gla_prompt(z, wgate, bgate, gain, *, batch, seq, tc, n_seq):
    m = z.shape[0]
    z3 = z.reshape(batch, seq, z.shape[1])
    col = lambda w, off: pl.BlockSpec((n_seq, tc, w), lambda b, c: (b, c, off // w))
    const = lambda shape: pl.BlockSpec(shape, lambda b, c: (0,) * len(shape))
    y, st = pl.pallas_call(
        functools.partial(_gla_prompt_kernel, n_seq=n_seq, n_chunks=tc // GLA_CHUNK),
        grid=(batch // n_seq, seq // tc),
        in_specs=[col(1024, AB_GV), col(1024, AB_GR), col(512, AB_GQ), col(512, AB_GK), col(128, AB_SMALL),
                  const((LANES, GLA_HEADS * GLA_DK)), const((1, GLA_HEADS * GLA_DK)), const((1, GLA_DV))],
        out_specs=[pl.BlockSpec((n_seq, tc, GLA_HEADS * GLA_DV), lambda b, c: (b, c, 0)),
                   pl.BlockSpec((n_seq, GLA_HEADS, GLA_DV, GLA_DK), lambda b, c: (b, 0, 0, 0))],
        out_shape=[jax.ShapeDtypeStruct((batch, seq, GLA_HEADS * GLA_DV), F32),
                   jax.ShapeDtypeStruct((batch, GLA_HEADS, GLA_DV, GLA_DK), F32)],
        scratch_shapes=[pltpu.VMEM((n_seq, GLA_HEADS, GLA_DV, GLA_DK), F32),
                        pltpu.VMEM((n_seq, tc, GLA_HEADS * GLA_DK), F32)],
        compiler_params=_cparams(("parallel", "arbitrary")),
        name="gla_prompt",
    )(z3, z3, z3, z3, z3, wgate, bgate, gain)
    return y.reshape(m, GLA_HEADS * GLA_DV), st


INT_MIN = -2 ** 31


def _order_key(score):
    score = jnp.where(score == 0.0, 0.0, score)
    bits = lax.bitcast_convert_type(score, jnp.int32)
    return jnp.where(bits < 0, bits ^ jnp.int32(0x7FFFFFFF), bits)


def _lane_total(x):
    return jnp.dot(x.astype(BF16), jnp.ones((LANES, LANES), BF16), preferred_element_type=F32)


ROW_SUB = 128


def _count_blocks(key_ref, rows, n_blocks, pred):
    acc = jnp.zeros((ROW_SUB, LANES), F32)
    for c in range(n_blocks):
        acc = acc + jnp.where(pred(key_ref[rows, c * LANES:(c + 1) * LANES]), 1.0, 0.0)
    return acc


def _masked_key():
    bits = int(np.array(NEG_INF, np.float32).view(np.int32))
    return bits ^ 0x7FFFFFFF


def _kth_largest_key(key_ref, t_ref, n_blocks, k, active):
    n_sub = key_ref.shape[0] // ROW_SUB
    t_ref[...] = jnp.full(t_ref.shape, INT_MIN, jnp.int32)

    def body(it, carry):
        step = lax.shift_left(jnp.int32(1), 31 - it)
        cands, accs = [], []
        for rb in range(n_sub):
            rows = slice(rb * ROW_SUB, (rb + 1) * ROW_SUB)
            cand = t_ref[rows, :] + step
            cands.append(cand)
            accs.append(_count_blocks(key_ref, rows, active(rb), lambda kc, cand=cand: kc >= cand))
        total = _lane_total(jnp.concatenate(accs, axis=0))
        for rb in range(n_sub):
            rows = slice(rb * ROW_SUB, (rb + 1) * ROW_SUB)
            skipped = float((n_blocks - active(rb)) * LANES)
            count = total[rows, :] + jnp.where(cands[rb] <= _masked_key(), skipped, 0.0)
            t_ref[rows, :] = jnp.where(count >= k, cands[rb], t_ref[rows, :])
        return carry

    lax.fori_loop(0, 32, body, 0)


def _select_topk(key_ref, t_ref, n_blocks, k, write_fn, active=None):
    if active is None:
        active = lambda rb: n_blocks
    _kth_largest_key(key_ref, t_ref, n_blocks, k, active)
    ri = lax.broadcasted_iota(jnp.int32, (LANES, LANES), 0)
    ci = lax.broadcasted_iota(jnp.int32, (LANES, LANES), 1)
    before = jnp.where(ri < ci, 1.0, 0.0).astype(BF16)
    for rb in range(key_ref.shape[0] // ROW_SUB):
        rows = slice(rb * ROW_SUB, (rb + 1) * ROW_SUB)
        t = t_ref[rows, :]
        skipped = float((n_blocks - active(rb)) * LANES)
        above = _lane_total(_count_blocks(key_ref, rows, active(rb), lambda kc: kc > t))
        need = k - above - jnp.where(t < _masked_key(), skipped, 0.0)
        run = jnp.zeros((ROW_SUB, LANES), F32)
        for c in range(n_blocks):
            if c >= active(rb):
                write_fn(rows, c, jnp.zeros((ROW_SUB, LANES), jnp.bool_))
                continue
            kc = key_ref[rows, c * LANES:(c + 1) * LANES]
            eq = jnp.where(kc == t, 1.0, 0.0)
            rank = jnp.dot(eq.astype(BF16), before, preferred_element_type=F32) + run
            take = jnp.where(kc > t, 1.0, jnp.where(rank < need, eq, 0.0))
            write_fn(rows, c, take > 0.0)
            run = run + _lane_total(eq)


def _half_mask(rows, upper):
    lane = lax.broadcasted_iota(jnp.int32, (rows, LANES), 1)
    return (lane >= 64) if upper else (lane < 64)


DSA_STRATUM = 512
KEY_CHUNK = 512


def _dsa_select_prompt_kernel(iq_ref, iw_ref, ikb_ref, bias_ref, key_ref, t_ref, *, row0, n_keys, topk):
    rows = iq_ref.shape[0]
    n_blocks = n_keys // LANES
    lane = lax.broadcasted_iota(jnp.int32, (ROW_SUB, LANES), 1)
    sub = lax.broadcasted_iota(jnp.int32, (ROW_SUB, LANES), 0)

    iw = iw_ref[...]
    qh = []
    for h in range(IDX_HEADS):
        pair = iq_ref[:, (h // 2) * LANES:(h // 2 + 1) * LANES]
        qh.append(jnp.where(_half_mask(rows, h % 2 == 1), pair, jnp.zeros_like(pair)))
    for kc in range(n_keys // KEY_CHUNK):
        ik = ikb_ref[kc * KEY_CHUNK:(kc + 1) * KEY_CHUNK, :]
        score = jnp.zeros((rows, KEY_CHUNK), F32)
        for h in range(IDX_HEADS):
            score = score + jnp.maximum(_dot_nt(qh[h], ik), 0.0) * iw[:, h:h + 1]
        for rb in range(rows // ROW_SUB):
            for cb in range(KEY_CHUNK // LANES):
                c = kc * (KEY_CHUNK // LANES) + cb
                causal = (c * LANES + lane) <= (row0 + rb * ROW_SUB + sub)
                part = score[rb * ROW_SUB:(rb + 1) * ROW_SUB, cb * LANES:(cb + 1) * LANES]
                key_ref[rb * ROW_SUB:(rb + 1) * ROW_SUB, c * LANES:(c + 1) * LANES] = _order_key(
                    jnp.where(causal, part, NEG_INF))

    def write(rws, c, sel):
        causal = (c * LANES + lane) <= (row0 + rws.start + sub)
        bias_ref[rws, c * LANES:(c + 1) * LANES] = jnp.where(jnp.logical_and(sel, causal), 0.0, NEG_INF)

    _select_topk(key_ref, t_ref, n_blocks, topk, write, active=lambda rb: row0 // LANES + rb + 1)


def _dsa_attend_prompt_kernel(q_ref, bias_ref, kb_ref, vb_ref, y_ref):
    bias = bias_ref[...]
    group = DSA_HEADS // DSA_KV_HEADS
    for kv in range(DSA_KV_HEADS):
        kvs = slice(kv * DSA_HD, (kv + 1) * DSA_HD)
        q4 = jnp.concatenate([q_ref[:, (kv * group + g) * DSA_HD:(kv * group + g + 1) * DSA_HD] for g in range(group)],
                             axis=0)
        s4 = _dot_nt(q4, kb_ref[:, kvs])
        ps, dens = [], []
        for g in range(group):
            s = s4[g * Q_BLOCK:(g + 1) * Q_BLOCK, :] + bias
            p = jnp.exp(s - jnp.max(s, axis=-1, keepdims=True))
            dens.append(jnp.sum(p, axis=-1, keepdims=True))
            ps.append(p.astype(BF16))
        o4 = jnp.dot(jnp.concatenate(ps, axis=0), vb_ref[:, kvs], preferred_element_type=F32)
        for g in range(group):
            h = kv * group + g
            y_ref[:, h * DSA_HD:(h + 1) * DSA_HD] = o4[g * Q_BLOCK:(g + 1) * Q_BLOCK, :] / dens[g]


def dsa_prompt(q, iq, iw, kb, vb, ikb, *, batch, seq):
    topk = min(DSA_TOPK_MAX, seq // 4)
    as3 = lambda a: a.reshape(batch, seq, a.shape[-1])
    q, iq, iw, kb, vb, ikb = (as3(a) for a in (q, iq, iw, kb, vb, ikb))
    nsub = DSA_STRATUM // Q_BLOCK
    outs = []
    for r in range(seq // DSA_STRATUM):
        n_keys = (r + 1) * DSA_STRATUM
        strat = lambda w: pl.BlockSpec((None, DSA_STRATUM, w), lambda b: (b, r, 0))
        keys = lambda w: pl.BlockSpec((None, n_keys, w), lambda b: (b, 0, 0))
        bias = pl.pallas_call(
            functools.partial(_dsa_select_prompt_kernel, row0=r * DSA_STRATUM, n_keys=n_keys, topk=topk),
            grid=(batch,),
            in_specs=[strat(512), strat(128), keys(128)],
            out_specs=pl.BlockSpec((None, DSA_STRATUM, n_keys), lambda b: (b, 0, 0)),
            out_shape=jax.ShapeDtypeStruct((batch, DSA_STRATUM, n_keys), F32),
            scratch_shapes=[pltpu.VMEM((DSA_STRATUM, n_keys), jnp.int32), pltpu.VMEM((DSA_STRATUM, LANES), jnp.int32)],
            compiler_params=_cparams(("parallel",)),
            name="dsa_select_prompt",
        )(iq, iw, ikb)
        qrow = lambda w: pl.BlockSpec((None, Q_BLOCK, w), lambda b, i: (b, r * nsub + i, 0))
        keys2 = lambda w: pl.BlockSpec((None, n_keys, w), lambda b, i: (b, 0, 0))
        outs.append(pl.pallas_call(
            _dsa_attend_prompt_kernel,
            grid=(batch, nsub),
            in_specs=[qrow(1024), pl.BlockSpec((None, Q_BLOCK, n_keys), lambda b, i: (b, i, 0)), keys2(256), keys2(256)],
            out_specs=pl.BlockSpec((None, Q_BLOCK, DSA_HEADS * DSA_HD), lambda b, i: (b, i, 0)),
            out_shape=jax.ShapeDtypeStruct((batch, DSA_STRATUM, DSA_HEADS * DSA_HD), F32),
            compiler_params=_cparams(("parallel", "arbitrary")),
            name="dsa_attend_prompt",
        )(q, bias, kb, vb))
    return jnp.concatenate(outs, axis=1).reshape(batch * seq, DSA_HEADS * DSA_HD)


def _swa_head_plan(h):
    group = SWA_HEADS // SWA_KV_HEADS
    kv = h // group
    return h // 2, h % 2, kv // 2, (kv % 2) != (h % 2)


def _swa_prompt_kernel(sink_ref, q_ref, kp_ref, kc_ref, kps_ref, kcs_ref, vp_ref, vc_ref, vps_ref, vcs_ref, y_ref):
    i = pl.program_id(1)
    r = lax.broadcasted_iota(jnp.int32, (Q_BLOCK, 2 * Q_BLOCK), 0)
    c = lax.broadcasted_iota(jnp.int32, (Q_BLOCK, 2 * Q_BLOCK), 1)
    rel = Q_BLOCK + r - c
    ok = (rel >= 0) & (rel <= WINDOW) & ((i - 1) * Q_BLOCK + c >= 0)
    bias = jnp.where(ok, 0.0, NEG_INF)
    keys = (jnp.concatenate([kp_ref[...], kc_ref[...]], axis=0), jnp.concatenate([kps_ref[...], kcs_ref[...]], axis=0))
    vals = (jnp.concatenate([vp_ref[...], vc_ref[...]], axis=0), jnp.concatenate([vps_ref[...], vcs_ref[...]], axis=0))
    lower = _half_mask(Q_BLOCK, False)
    for p in range(SWA_HEADS // 2):
        qpair = q_ref[:, p * LANES:(p + 1) * LANES]
        outs = []
        for h in (2 * p, 2 * p + 1):
            _, half, ks, swapped = _swa_head_plan(h)
            qh = jnp.where(_half_mask(Q_BLOCK, half == 1), qpair, jnp.zeros_like(qpair))
            kk = keys[int(swapped)][:, ks * LANES:(ks + 1) * LANES]
            vv = vals[int(swapped)][:, ks * LANES:(ks + 1) * LANES]
            s = _dot_nt(qh, kk) + bias
            sink = sink_ref[h]
            mx = jnp.maximum(jnp.max(s, axis=-1, keepdims=True), sink)
            pr = jnp.exp(s - mx)
            den = jnp.sum(pr, axis=-1, keepdims=True) + jnp.exp(sink - mx)
            outs.append(jnp.dot(pr.astype(BF16), vv, preferred_element_type=F32) / den)
        y_ref[:, p * LANES:(p + 1) * LANES] = jnp.where(lower, outs[0], outs[1])


def swa_prompt(q, kb, kbs, vb, vbs, sinks, *, batch, seq):
    m = q.shape[0]
    nq = seq // Q_BLOCK
    cur = pl.BlockSpec((Q_BLOCK, 256), lambda b, i, s: (b * nq + i, 0))
    prev = pl.BlockSpec((Q_BLOCK, 256), lambda b, i, s: (b * nq + jnp.maximum(i - 1, 0), 0))
    qspec = pl.BlockSpec((Q_BLOCK, 2048), lambda b, i, s: (b * nq + i, 0))
    return pl.pallas_call(
        _swa_prompt_kernel,
        grid_spec=pltpu.PrefetchScalarGridSpec(
            num_scalar_prefetch=1,
            grid=(batch, nq),
            in_specs=[qspec, prev, cur, prev, cur, prev, cur, prev, cur],
            out_specs=qspec,
        ),
        out_shape=jax.ShapeDtypeStruct((m, SWA_HEADS * SWA_HD), F32),
        compiler_params=_cparams(("parallel", "arbitrary")),
        name="swa_prompt",
    )(sinks, q, kb, kb, kbs, kbs, vb, vb, vbs, vbs)


def _mem_prompt_kernel(x_ref, g_ref, wq_ref, kv_ref, wo_ref, *rest, n_in):
    y_refs, w_refs, o_ref = rest[:n_in], rest[n_in:2 * n_in], rest[2 * n_in]
    x = x_ref[...]
    for y_ref, w_ref in zip(y_refs, w_refs):
        x = x + jnp.dot(y_ref[...].astype(BF16), w_ref[...], preferred_element_type=F32)
    h = _rms(x, g_ref[...]).astype(BF16)
    q = (jnp.dot(h, wq_ref[...], preferred_element_type=F32) * (MEM_HD ** -0.5)).astype(BF16)
    width = MEM_HEADS * MEM_HD
    outs = []
    for hd in range(MEM_HEADS):
        sl = slice(hd * MEM_HD, (hd + 1) * MEM_HD)
        k = kv_ref[:, sl].astype(BF16)
        v = kv_ref[:, width + hd * MEM_HD:width + (hd + 1) * MEM_HD].astype(BF16)
        s = _dot_nt(q[:, sl], k)
        p = jnp.exp(s - jnp.max(s, axis=-1, keepdims=True))
        o = jnp.dot(p.astype(BF16), v, preferred_element_type=F32) / jnp.sum(p, axis=-1, keepdims=True)
        outs.append(o.astype(BF16))
    o_ref[...] = x + jnp.dot(jnp.concatenate(outs, axis=-1), wo_ref[...], preferred_element_type=F32)


def mem_attn_prompt(x, ys, ws, g, wq, kv, wo, *, batch, seq, tq):
    m = x.shape[0]
    nq = seq // tq
    width = MEM_HEADS * MEM_HD
    rows = lambda w: pl.BlockSpec((tq, w), lambda b, i: (b * nq + i, 0))
    const = lambda shape: pl.BlockSpec(shape, lambda b, i: (0, 0), pipeline_mode=pl.Buffered(1))
    return pl.pallas_call(
        functools.partial(_mem_prompt_kernel, n_in=len(ys)),
        grid=(batch, nq),
        in_specs=[rows(D_MODEL),
                  pl.BlockSpec((1, D_MODEL), lambda b, i: (0, 0)),
                  const((D_MODEL, width)),
                  pl.BlockSpec((N_MEM, 2 * width), lambda b, i: (b, 0)),
                  const((width, D_MODEL))]
                 + [rows(y.shape[1]) for y in ys] + [const(w.shape) for w in ws],
        out_specs=rows(D_MODEL),
        out_shape=jax.ShapeDtypeStruct((m, D_MODEL), F32),
        compiler_params=_cparams(("parallel", "arbitrary")),
        name="mem_attn_prompt",
    )(x, g.reshape(1, D_MODEL), wq, kv, wo, *ys, *ws)


def _gla_prep_kernel(gq_ref, gk_ref, sm_ref, wgate_ref, bgate_ref, o_ref):
    w = GLA_HEADS * GLA_DK
    la = _log_decay(sm_ref[...], wgate_ref[...], bgate_ref[...])
    o_ref[:, 0:w] = jnp.exp(la)
    o_ref[:, w:2 * w] = gk_ref[...]
    o_ref[:, 2 * w:3 * w] = gq_ref[...] * (GLA_DK ** -0.5)
    o_ref[:, 3 * w:4 * w] = jnp.zeros((gq_ref.shape[0], w), F32)


def gla_sample_prep(z, wgate, bgate):
    m = z.shape[0]
    w = GLA_HEADS * GLA_DK
    col = lambda wd, off: pl.BlockSpec((m, wd), lambda i: (0, off // wd))
    return pl.pallas_call(
        _gla_prep_kernel,
        grid=(1,),
        in_specs=[col(512, AB_GQ), col(512, AB_GK), col(128, AB_SMALL),
                  pl.BlockSpec((LANES, w), lambda i: (0, 0)), pl.BlockSpec((1, w), lambda i: (0, 0))],
        out_specs=pl.BlockSpec((m, 4 * w), lambda i: (0, 0)),
        out_shape=jax.ShapeDtypeStruct((m, 4 * w), F32),
        compiler_params=_cparams(("arbitrary",)),
        name="gla_sample_prep",
    )(z, z, z, wgate, bgate)


def _gla_step_kernel(p_ref, gv_ref, gr_ref, s_ref, gain_ref, so_ref, y_ref, *, bs):
    gain = gain_ref[...]
    for s in range(bs):
        xt = p_ref[s].T
        for h in range(GLA_HEADS):
            dv = slice(h * GLA_DV, (h + 1) * GLA_DV)
            st = s_ref[s, h] * xt[:, h:h + 1] + xt[:, GLA_HEADS + h:GLA_HEADS + h + 1] * gv_ref[s:s + 1, dv]
            so_ref[s, h] = st
            o = jnp.sum(xt[:, 2 * GLA_HEADS + h:2 * GLA_HEADS + h + 1] * st, axis=0, keepdims=True)
            y_ref[s:s + 1, dv] = _gla_gate_out(o, gr_ref[s:s + 1, dv], gain)


def gla_sample(p, z, state, gain, *, bs):
    m = z.shape[0]
    col = lambda w, off: pl.BlockSpec((bs, w), lambda i: (i, off // w))
    sspec = pl.BlockSpec((bs, GLA_HEADS, GLA_DK, GLA_DV), lambda i: (i, 0, 0, 0))
    return pl.pallas_call(
        functools.partial(_gla_step_kernel, bs=bs),
        grid=(m // bs,),
        in_specs=[pl.BlockSpec((bs, 16, GLA_DK), lambda i: (i, 0, 0)), col(1024, AB_GV), col(1024, AB_GR), sspec,
                  pl.BlockSpec((1, GLA_DV), lambda i: (0, 0))],
        out_specs=[sspec, pl.BlockSpec((bs, GLA_HEADS * GLA_DV), lambda i: (i, 0))],
        out_shape=[jax.ShapeDtypeStruct(state.shape, F32), jax.ShapeDtypeStruct((m, GLA_HEADS * GLA_DV), F32)],
        compiler_params=_cparams(("parallel",)),
        name="gla_sample",
    )(p, z, z, state, gain)


def _dsa_scores_kernel(pt_ref, iq_ref, iw_ref, pool_hbm, o_ref, buf, sem, *, page_base, n_pages, depth):
    b = pl.program_id(0)
    nb = pl.num_programs(0)

    def page_copies(row):
        slot = row % (depth + 1)
        return [pltpu.make_async_copy(pool_hbm.at[page_base + pt_ref[row, j]], buf.at[slot, j], sem.at[slot])
                for j in range(n_pages)]

    @pl.when(b == 0)
    def _():
        for ahead in range(depth):
            for cp in page_copies(ahead):
                cp.start()

    @pl.when(b + depth < nb)
    def _():
        for cp in page_copies(b + depth):
            cp.start()

    for cp in page_copies(b):
        cp.wait()
    slot = b % (depth + 1)
    q8 = iq_ref[0]
    iw = iw_ref[0]
    for j in range(n_pages):
        dots = jnp.dot(q8, buf[slot, j].astype(BF16), preferred_element_type=F32)
        o_ref[0, j:j + 1, :] = jnp.sum(jnp.maximum(dots, 0.0) * iw, axis=0, keepdims=True)


def dsa_sample_scores(page_table, iq, iw, pool_ik_t, page_base, *, depth):
    m, n_pages = page_table.shape
    assert m > depth
    return pl.pallas_call(
        functools.partial(_dsa_scores_kernel, page_base=page_base, n_pages=n_pages, depth=depth),
        grid_spec=pltpu.PrefetchScalarGridSpec(
            num_scalar_prefetch=1,
            grid=(m,),
            in_specs=[pl.BlockSpec((1, IDX_HEADS, IDX_HD), lambda b, pt: (b, 0, 0)),
                      pl.BlockSpec((1, IDX_HEADS, 1), lambda b, pt: (b, 0, 0)),
                      pl.BlockSpec(memory_space=pl.ANY)],
            out_specs=pl.BlockSpec((1, n_pages, PAGE_SIZE), lambda b, pt: (b, 0, 0)),
            scratch_shapes=[pltpu.VMEM((depth + 1, n_pages, IDX_HD, PAGE_SIZE), F32),
                            pltpu.SemaphoreType.DMA((depth + 1,))],
        ),
        out_shape=jax.ShapeDtypeStruct((m, n_pages, PAGE_SIZE), F32),
        compiler_params=_cparams(("arbitrary",)),
        name="dsa_sample_scores",
    )(page_table, iq, iw, pool_ik_t)


def _dsa_select_sample_kernel(sc_ref, iq_ref, iw_ref, ikb_ref, bias_ref, bnew_ref, key_ref, t_ref, *, n_past, topk):
    rows = sc_ref.shape[0]
    n_blocks = n_past // LANES
    for c in range(n_blocks):
        sl = slice(c * LANES, (c + 1) * LANES)
        key_ref[:, sl] = _order_key(sc_ref[:, sl])
    ik = ikb_ref[...].astype(F32)
    iw = iw_ref[...]
    s_new = jnp.zeros((rows, 1), F32)
    for h in range(IDX_HEADS):
        pair = iq_ref[:, (h // 2) * LANES:(h // 2 + 1) * LANES].astype(F32)
        qh = jnp.where(_half_mask(rows, h % 2 == 1), pair, 0.0)
        s_new = s_new + jnp.maximum(jnp.sum(qh * ik, axis=-1, keepdims=True), 0.0) * iw[:, h:h + 1]
    lane = lax.broadcasted_iota(jnp.int32, (rows, LANES), 1)
    key_ref[:, n_past:n_past + LANES] = _order_key(jnp.where(lane == 0, s_new, -jnp.inf))

    ri = lax.broadcasted_iota(jnp.int32, (LANES, DSA_KV_HEADS * LANES), 0)
    ci = lax.broadcasted_iota(jnp.int32, (LANES, DSA_KV_HEADS * LANES), 1)
    spread = jnp.where(ci // DSA_KV_HEADS == ri, 1.0, 0.0).astype(BF16)

    def write(rws, c, sel):
        if c == n_blocks:
            bnew_ref[rws, :] = jnp.where(sel, 0.0, NEG_INF)
        else:
            wide = jnp.dot(jnp.where(sel, 1.0, 0.0).astype(BF16), spread, preferred_element_type=F32)
            w = DSA_KV_HEADS * LANES
            bias_ref[rws, c * w:(c + 1) * w] = jnp.where(wide > 0.5, 0.0, NEG_INF)

    _select_topk(key_ref, t_ref, n_blocks + 1, topk, write)


def dsa_sample_select(scores, iq, iw, ikb, *, topk):
    m, n_past = scores.shape
    full = lambda a: pl.BlockSpec(a.shape, lambda i: (0,) * a.ndim)
    wide = DSA_KV_HEADS * n_past
    return pl.pallas_call(
        functools.partial(_dsa_select_sample_kernel, n_past=n_past, topk=topk),
        grid=(1,),
        in_specs=[full(scores), full(iq), full(iw), full(ikb)],
        out_specs=[pl.BlockSpec((m, wide), lambda i: (0, 0)), pl.BlockSpec((m, LANES), lambda i: (0, 0))],
        out_shape=[jax.ShapeDtypeStruct((m, wide), F32), jax.ShapeDtypeStruct((m, LANES), F32)],
        scratch_shapes=[pltpu.VMEM((m, n_past + LANES), jnp.int32), pltpu.VMEM((m, LANES), jnp.int32)],
        compiler_params=_cparams(("arbitrary",)),
        name="dsa_sample_select",
    )(scores, iq, iw, ikb)


def _dsa_attend_sample_kernel(pt_ref, q_ref, bias_ref, bnew_ref, kn_ref, vn_ref, pk_hbm, pv_hbm, y_ref,
                              kbuf, vbuf, sem, *, page_base, n_pages, chunk):
    b = pl.program_id(0)
    wide = DSA_KV_HEADS * PAGE_SIZE
    n_chunks = n_pages // chunk
    group = DSA_HEADS // DSA_KV_HEADS

    def page_copies(row, c, slot):
        out = []
        for j in range(chunk):
            src = pl.ds(pl.multiple_of((page_base + pt_ref[row, c * chunk + j]) * wide, wide), wide)
            dst = pl.ds(j * wide, wide)
            out.append(pltpu.make_async_copy(pk_hbm.at[src, :], kbuf.at[slot, dst, :], sem.at[0, slot]))
            out.append(pltpu.make_async_copy(pv_hbm.at[src, :], vbuf.at[slot, dst, :], sem.at[1, slot]))
        return out

    @pl.when(b == 0)
    def _():
        for cp in page_copies(0, 0, 0):
            cp.start()

    q8 = q_ref[0]
    first = lax.broadcasted_iota(jnp.int32, (DSA_HEADS, DSA_HD), 0) < group
    hrow = lax.broadcasted_iota(jnp.int32, (DSA_HEADS, wide), 0)
    col = lax.broadcasted_iota(jnp.int32, (DSA_HEADS, wide), 1)
    own = (col % DSA_KV_HEADS) == (hrow // group)
    m_run = jnp.full((DSA_HEADS, DSA_HD), NEG_INF, F32)
    l_run = jnp.zeros((DSA_HEADS, DSA_HD), F32)
    acc = jnp.zeros((DSA_HEADS, DSA_HD), F32)
    for c in range(n_chunks):
        slot = c % 2
        if c + 1 < n_chunks:
            for cp in page_copies(b, c + 1, 1 - slot):
                cp.start()
        else:
            @pl.when(b + 1 < pl.num_programs(0))
            def _():
                for cp in page_copies(b + 1, 0, 1 - slot):
                    cp.start()
        for cp in page_copies(b, c, slot):
            cp.wait()
        ss, oks = [], []
        for j in range(chunk):
            ok = jnp.logical_and(own, bias_ref[0, c * chunk + j:c * chunk + j + 1, :] == 0.0)
            kp = kbuf[slot, j * wide:(j + 1) * wide, :].astype(BF16)
            ss.append(jnp.where(ok, _dot_nt(q8, kp), NEG_INF))
            oks.append(ok)
        mx = ss[0]
        for s in ss[1:]:
            mx = jnp.maximum(mx, s)
        m_new = jnp.maximum(m_run, jnp.max(mx, axis=-1, keepdims=True))
        psum = jnp.zeros((DSA_HEADS, wide), F32)
        pv = jnp.zeros((DSA_HEADS, DSA_HD), F32)
        for j in range(chunk):
            p = jnp.where(oks[j], jnp.exp(ss[j] - m_new[:, 0:1]), 0.0)
            psum = psum + p
            vp = vbuf[slot, j * wide:(j + 1) * wide, :].astype(BF16)
            pv = pv + jnp.dot(p.astype(BF16), vp, preferred_element_type=F32)
        alpha = jnp.exp(m_run - m_new)
        l_run = alpha * l_run + jnp.sum(psum, axis=-1, keepdims=True)
        acc = alpha * acc + pv
        m_run = m_new

    kn = kn_ref[0].astype(BF16).astype(F32)
    vn = vn_ref[0].astype(BF16).astype(F32)
    bn = bnew_ref[0][:, 0:1]
    s_new = jnp.sum(q8.astype(F32) * jnp.where(first, kn[:, :DSA_HD], kn[:, DSA_HD:]), axis=-1, keepdims=True) + bn
    m_new = jnp.maximum(m_run, s_new)
    p_new = jnp.where(bn == 0.0, jnp.exp(s_new - m_new), 0.0)
    alpha = jnp.exp(m_run - m_new)
    acc = alpha * acc + p_new * jnp.where(first, vn[:, :DSA_HD], vn[:, DSA_HD:])
    y_ref[0] = acc / (alpha * l_run + p_new)


def dsa_sample_attend(page_table, q, bias, bnew, kn, vn, pool_k, pool_v, page_base, *, chunk):
    m, n_pages = page_table.shape
    wide = DSA_KV_HEADS * PAGE_SIZE
    assert n_pages % chunk == 0 and (n_pages // chunk) % 2 == 0
    per_b = lambda shape: pl.BlockSpec((1,) + shape, lambda b, pt: (b, 0, 0))
    hbm = pl.BlockSpec(memory_space=pl.ANY)
    return pl.pallas_call(
        functools.partial(_dsa_attend_sample_kernel, page_base=page_base, n_pages=n_pages, chunk=chunk),
        grid_spec=pltpu.PrefetchScalarGridSpec(
            num_scalar_prefetch=1,
            grid=(m,),
            in_specs=[per_b((DSA_HEADS, DSA_HD)), per_b((n_pages, wide)),
                      per_b((1, LANES)), per_b((1, DSA_KV_HEADS * DSA_HD)), per_b((1, DSA_KV_HEADS * DSA_HD)),
                      hbm, hbm],
            out_specs=per_b((DSA_HEADS, DSA_HD)),
            scratch_shapes=[pltpu.VMEM((2, chunk * wide, DSA_HD), F32), pltpu.VMEM((2, chunk * wide, DSA_HD), F32),
                            pltpu.SemaphoreType.DMA((2, 2))],
        ),
        out_shape=jax.ShapeDtypeStruct((m, DSA_HEADS, DSA_HD), F32),
        compiler_params=_cparams(("arbitrary",)),
        name="dsa_sample_attend",
    )(page_table, q, bias, bnew, kn, vn, pool_k, pool_v)


def _swa_step_kernel(q_ref, kt_ref, vt_ref, kn_ref, vn_ref, sink_ref, y_ref, *, bs):
    group = SWA_HEADS // SWA_KV_HEADS
    pairs = [(s, kv) for s in range(bs) for kv in range(SWA_KV_HEADS)]
    qs, scs, news = [], [], []
    for s, kv in pairs:
        qv = q_ref[s, kv * group:(kv + 1) * group, :]
        kn = kn_ref[s, kv:kv + 1, :].astype(BF16).astype(F32)
        scs.append(jnp.dot(qv, kt_ref[s, kv].astype(BF16), preferred_element_type=F32))
        news.append(jnp.sum(qv.astype(F32) * kn, axis=-1, keepdims=True))
    sc = jnp.concatenate(scs, axis=0)
    s_new = jnp.concatenate(news, axis=0)
    sink = jnp.concatenate([sink_ref[...]] * bs, axis=0)
    mx = jnp.maximum(jnp.maximum(jnp.max(sc, axis=-1, keepdims=True), s_new), sink)
    p = jnp.exp(sc - mx)
    p_new = jnp.exp(s_new - mx)
    inv = 1.0 / (jnp.sum(p, axis=-1, keepdims=True) + p_new + jnp.exp(sink - mx))
    pb = p.astype(BF16)
    for n, (s, kv) in enumerate(pairs):
        rows = slice(n * group, (n + 1) * group)
        vn = vn_ref[s, kv:kv + 1, :].astype(BF16).astype(F32)
        o = _dot_nt(pb[rows, :], vt_ref[s, kv].astype(BF16)) + p_new[rows, :] * vn
        y_ref[s, kv * group:(kv + 1) * group, :] = o * inv[rows, :]


def swa_sample(q, kt, vt, kn, vn, sinks, *, bs):
    m = q.shape[0]
    cache = pl.BlockSpec((bs, SWA_KV_HEADS, SWA_HD, WINDOW), lambda i: (i, 0, 0, 0))
    new = pl.BlockSpec((bs, SWA_KV_HEADS, SWA_HD), lambda i: (i, 0, 0))
    return pl.pallas_call(
        functools.partial(_swa_step_kernel, bs=bs),
        grid=(m // bs,),
        in_specs=[pl.BlockSpec((bs, SWA_HEADS, SWA_HD), lambda i: (i, 0, 0)), cache, cache, new, new,
                  pl.BlockSpec((SWA_HEADS, 1), lambda i: (0, 0))],
        out_specs=pl.BlockSpec((bs, SWA_HEADS, SWA_HD), lambda i: (i, 0, 0)),
        out_shape=jax.ShapeDtypeStruct((m, SWA_HEADS, SWA_HD), F32),
        compiler_params=_cparams(("parallel",)),
        name="swa_sample",
    )(q, kt, vt, kn, vn, sinks.reshape(SWA_HEADS, 1))


def _mem_step_kernel(q_ref, k_ref, v_ref, y_ref, *, bs):
    rows = q_ref.shape[1]
    n = N_MEM * MEM_HEADS
    hrow = lax.broadcasted_iota(jnp.int32, (rows, n), 0)
    col = lax.broadcasted_iota(jnp.int32, (rows, n), 1)
    own = (col % MEM_HEADS) == (hrow % MEM_HEADS)
    for s in range(bs):
        q = (q_ref[s] * (MEM_HD ** -0.5)).astype(BF16)
        sc = jnp.where(own, _dot_nt(q, k_ref[s * n:(s + 1) * n, :].astype(BF16)), NEG_INF)
        p = jnp.where(own, jnp.exp(sc - jnp.max(sc, axis=-1, keepdims=True)), 0.0)
        o = jnp.dot(p.astype(BF16), v_ref[s * n:(s + 1) * n, :].astype(BF16), preferred_element_type=F32)
        y_ref[s] = o / jnp.sum(p, axis=-1, keepdims=True)


def mem_attn_sample(q, mk, mv, layer, *, bs):
    m, rows, _ = q.shape
    n = N_MEM * MEM_HEADS
    nb = m // bs
    cache = pl.BlockSpec((bs * n, MEM_HD), lambda i: (layer * nb + i, 0))
    return pl.pallas_call(
        functools.partial(_mem_step_kernel, bs=bs),
        grid=(nb,),
        in_specs=[pl.BlockSpec((bs, rows, MEM_HD), lambda i: (i, 0, 0)), cache, cache],
        out_specs=pl.BlockSpec((bs, rows, MEM_HD), lambda i: (i, 0, 0)),
        out_shape=jax.ShapeDtypeStruct((m, rows, MEM_HD), F32),
        compiler_params=_cparams(("parallel",)),
        name="mem_attn_sample",
    )(q, mk, mv)


TM_FFN, TF_FFN, SUB_FFN = 1024, 512, 512
TM_PROJ, TN_AB, TN_MEM = 1024, 1792, 512
TM_OUT = 512
TR_FEAT = 512
TC_GLA, NSEQ_GLA = 256, 2
TQ_MEM = 512
BS_SAMPLE = 8
DEPTH_SCORES = 2
PG_ATTEND = 16


def _prep_w_in_ab(w):
    sizes = (GLA_HEADS * GLA_DK, GLA_HEADS * GLA_DK, GLA_HEADS * GLA_DV, GLA_HEADS * GLA_DV, GLA_GATE_RANK,
             DSA_HEADS * DSA_HD, DSA_KV_HEADS * DSA_HD, DSA_KV_HEADS * DSA_HD, IDX_HEADS * IDX_HD, IDX_HEADS, IDX_HD)
    offs = np.cumsum((0,) + sizes)
    gq, gk, gv, gr, gd, dq, dk, dv, iq, iw, ik = [w[:, int(offs[j]):int(offs[j + 1])] for j in range(len(sizes))]
    pad = lambda n: jnp.zeros((w.shape[0], n), w.dtype)
    small = jnp.concatenate([ik, gd, iw, pad(LANES - IDX_HD - GLA_GATE_RANK - IDX_HEADS)], axis=1)
    out = jnp.concatenate([gv, gr, dq, gq, gk, iq, dk, dv, small, pad(AB_WIDTH - AB_SMALL - LANES)], axis=1)
    return out.astype(BF16)


def _prep_gate(w_up):
    return jnp.zeros((LANES, GLA_HEADS * GLA_DK), F32).at[SM_GD:SM_GD + GLA_GATE_RANK].set(w_up)


def kernel(x_prompt, x_sample, mem_prompt, cache_dsa_k, cache_dsa_v, cache_dsa_idx_k, state_gla, cache_swa_k, cache_swa_v, cache_mem_k, cache_mem_v, page_table, norm_ffn, w_ffn_gate, w_ffn_up, w_ffn_down, norm_mix, w_in_ab, w_gla_gate_up, b_gla_gate, gla_out_norm, idx_k_norm, w_out_ab, w_in_c, swa_sinks, w_out_c, norm_mem_q, norm_mem_src, w_mem_q, w_mem_kv, w_mem_o, final_norm):
    depth = norm_mix.shape[0]
    bp, seq, _ = x_prompt.shape
    bs = x_sample.shape[0]
    n_pool = cache_dsa_k.shape[1]
    gla_w = GLA_HEADS * GLA_DV

    w_ab = [_prep_w_in_ab(w_in_ab[i]) for i in range(w_in_ab.shape[0])]
    w_gate = [_prep_gate(w_gla_gate_up[i]) for i in range(w_in_ab.shape[0])]
    w_oab = w_out_ab.astype(BF16)
    w_c, w_oc = w_in_c.astype(BF16), w_out_c.astype(BF16)
    w_mq, w_mkv, w_mo = w_mem_q.astype(BF16), w_mem_kv.astype(BF16), w_mem_o.astype(BF16)

    def rope_tabs(pos):
        return _rope_tables(pos, DSA_HD) + _rope_tables(pos, IDX_HD)

    ffn_bf16 = {}

    def ffn_pair(x, layer, half, last, sample):
        fin = final_norm if last else None
        if sample:
            y, *ffn_bf16[layer, half] = ffn(x, norm_ffn[layer, half], w_ffn_gate, w_ffn_up, w_ffn_down, (layer, half),
                                            fin, tm=TM_FFN, tf=TF_FFN, sub=SUB_FFN)
            return y
        return ffn(x, norm_ffn[layer, half], *ffn_bf16[layer, half], None, fin, tm=TM_FFN, tf=TF_FFN, sub=SUB_FFN)

    def prompt_group():
        tabs_p = rope_tabs(jnp.arange(seq))
        x = x_prompt.reshape(bp * seq, D_MODEL)
        mem = mem_prompt.reshape(bp * N_MEM, D_MODEL)
        mem_kv = [norm_proj(mem, norm_mem_src[l], w_mkv[l], tm=TM_PROJ, tn=TN_MEM) for l in range(depth)]
        st_ab_p, st_c_p = [], []
        for l in range(depth):
            i = l // 2
            x = ffn_pair(x, l, 0, False, False)
            if l % 2 == 0:
                z = norm_proj(x, norm_mix[l], w_ab[i], tm=TM_PROJ, tn=TN_AB)
                q, k, kb, vb, iq, ik, ikb, iw = ab_features(z, tabs_p, idx_k_norm[i], seq=seq, tr=TR_FEAT)
                y_gla, st_t = gla_prompt(z, w_gate[i], b_gla_gate[i].reshape(1, -1), gla_out_norm[i].reshape(1, -1),
                                         batch=bp, seq=seq, tc=TC_GLA, n_seq=NSEQ_GLA)
                y_dsa = dsa_prompt(q, iq, iw, kb, vb, ikb, batch=bp, seq=seq)
                mix_out = ([y_gla, y_dsa], [w_oab[i, :gla_w], w_oab[i, gla_w:]])
                n_pg = seq // PAGE_SIZE
                st_ab_p.append((k.reshape(bp, n_pg, PAGE_SIZE, DSA_KV_HEADS, DSA_HD),
                                z[:, AB_DV:AB_DV + 256].reshape(bp, n_pg, PAGE_SIZE, DSA_KV_HEADS, DSA_HD),
                                ik.reshape(bp, n_pg, PAGE_SIZE, IDX_HD),
                                jnp.swapaxes(st_t, 2, 3)))
            else:
                q, k, kb, kbs, v, vb, vbs = c_proj_features(x, norm_mix[l], w_c[i], tabs_p[2:], seq=seq,
                                                            tm=TM_PROJ, sub=SUB_FFN)
                y = swa_prompt(q, kb, kbs, vb, vbs, swa_sinks[i], batch=bp, seq=seq)
                mix_out = ([y], [w_oc[i]])
                st_c_p.append((k.reshape(bp, seq, SWA_KV_HEADS, SWA_HD)[:, -WINDOW:],
                               v.reshape(bp, seq, SWA_KV_HEADS, SWA_HD)[:, -WINDOW:]))
            x = mem_attn_prompt(x, *mix_out, norm_mem_q[l], w_mq[l], mem_kv[l], w_mo[l], batch=bp, seq=seq, tq=TQ_MEM)
            x = ffn_pair(x, l, 1, l == depth - 1, False)
        return x.reshape(bp, seq, D_MODEL), st_ab_p, st_c_p, mem_kv

    tabs_s = rope_tabs(PAST_LEN + jnp.arange(1))
    n_pages = page_table.shape[1]
    topk = min(DSA_TOPK_MAX, (PAST_LEN + 1) // 4)
    pool_k = cache_dsa_k.reshape(-1, DSA_HD)
    pool_v = cache_dsa_v.reshape(-1, DSA_HD)
    pool_ik_t = jnp.swapaxes(cache_dsa_idx_k, 2, 3).reshape(-1, IDX_HD, PAGE_SIZE)
    mem_k_rows = cache_mem_k.reshape(-1, MEM_HD)
    mem_v_rows = cache_mem_v.reshape(-1, MEM_HD)
    x = x_sample.reshape(bs, D_MODEL)
    st_ab_s, st_c_s = [], []
    for l in range(depth):
        i = l // 2
        x = ffn_pair(x, l, 0, False, True)
        if l % 2 == 0:
            z = norm_proj(x, norm_mix[l], w_ab[i], tm=TM_PROJ, tn=TN_AB)
            q, k, kb, vb, iq, ik, ikb, iw = ab_features(z, tabs_s, idx_k_norm[i], seq=1, tr=TR_FEAT)
            p = gla_sample_prep(z, w_gate[i], b_gla_gate[i].reshape(1, -1)).reshape(bs, 16, GLA_DK)
            st_new, y_gla = gla_sample(p, z, state_gla[i], gla_out_norm[i].reshape(1, -1), bs=BS_SAMPLE)
            scores = dsa_sample_scores(page_table, iq.reshape(bs, IDX_HEADS, IDX_HD),
                                       iw[:, :IDX_HEADS].reshape(bs, IDX_HEADS, 1), pool_ik_t, i * n_pool, depth=DEPTH_SCORES)
            bias, bnew = dsa_sample_select(scores.reshape(bs, n_pages * PAGE_SIZE), iq, iw, ikb, topk=topk)
            v_new = z[:, AB_DV:AB_DV + 256]
            y_dsa = dsa_sample_attend(
                page_table, q.reshape(bs, DSA_HEADS, DSA_HD), bias.reshape(bs, n_pages, DSA_KV_HEADS * PAGE_SIZE),
                bnew.reshape(bs, 1, LANES), k.reshape(bs, 1, 256), v_new.reshape(bs, 1, 256),
                pool_k, pool_v, i * n_pool, chunk=PG_ATTEND).reshape(bs, DSA_HEADS * DSA_HD)
            x = out_proj(x, [y_gla, y_dsa], [w_oab[i, :gla_w], w_oab[i, gla_w:]], tm=TM_OUT)
            st_ab_s.append((k.reshape(bs, 1, DSA_KV_HEADS, DSA_HD), v_new.reshape(bs, 1, DSA_KV_HEADS, DSA_HD),
                            ik.reshape(bs, 1, IDX_HD), st_new))
        else:
            q, k, kb, kbs, v_new, vb, vbs = c_proj_features(x, norm_mix[l], w_c[i], tabs_s[2:], seq=1,
                                                            tm=TM_PROJ, sub=SUB_FFN)
            y = swa_sample(q.reshape(bs, SWA_HEADS, SWA_HD), jnp.transpose(cache_swa_k[i], (0, 2, 3, 1)),
                           jnp.transpose(cache_swa_v[i], (0, 2, 3, 1)), k.reshape(bs, SWA_KV_HEADS, SWA_HD),
                           v_new.reshape(bs, SWA_KV_HEADS, SWA_HD), swa_sinks[i], bs=BS_SAMPLE)
            x = out_proj(x, [y.reshape(bs, SWA_HEADS * SWA_HD)], [w_oc[i]], tm=TM_OUT)
            st_c_s.append((jnp.concatenate([cache_swa_k[i][:, 1:], k.reshape(bs, 1, SWA_KV_HEADS, SWA_HD)], axis=1),
                           jnp.concatenate([cache_swa_v[i][:, 1:], v_new.reshape(bs, 1, SWA_KV_HEADS, SWA_HD)], axis=1)))
        qm = norm_proj(x, norm_mem_q[l], w_mq[l], tm=TM_PROJ, tn=TN_MEM).reshape(bs, MEM_HEADS, MEM_HD)
        qm = jnp.pad(qm, ((0, 0), (0, SUBLANES - MEM_HEADS), (0, 0)))
        om = mem_attn_sample(qm, mem_k_rows, mem_v_rows, l, bs=BS_SAMPLE)
        x = out_proj(x, [om[:, :MEM_HEADS].reshape(bs, MEM_HEADS * MEM_HD)], [w_mo[l]], tm=TM_OUT)
        x = ffn_pair(x, l, 1, l == depth - 1, True)
    y_sample = x.reshape(bs, 1, D_MODEL)

    y_prompt, st_ab_p, st_c_p, mem_kv = prompt_group()

    stk = lambda sts, j: jnp.stack([s[j] for s in sts])
    mw = MEM_HEADS * MEM_HD
    mem_k_p = jnp.stack([kv[:, :mw].reshape(bp, N_MEM, MEM_HEADS, MEM_HD) for kv in mem_kv])
    mem_v_p = jnp.stack([kv[:, mw:].reshape(bp, N_MEM, MEM_HEADS, MEM_HD) for kv in mem_kv])
    return (y_prompt, y_sample, stk(st_ab_p, 0), stk(st_ab_p, 1), stk(st_ab_p, 2),
            stk(st_ab_s, 0), stk(st_ab_s, 1), stk(st_ab_s, 2), stk(st_ab_p, 3), stk(st_ab_s, 3),
            stk(st_c_p, 0), stk(st_c_p, 1), stk(st_c_s, 0), stk(st_c_s, 1), mem_k_p, mem_v_p)
```

```python
import functools

import jax
import jax.numpy as jnp
import numpy as np
from jax import lax
from jax.experimental import pallas as pl
from jax.experimental.pallas import tpu as pltpu

F32 = jnp.float32
BF16 = jnp.bfloat16

D_MODEL = 2048
D_FF = 5632
EPS = 1e-6
ROPE_THETA = 10000.0
NEG_INF = -1e30
PAST_LEN = 8192
PAGE_SIZE = 128
Q_BLOCK = 128
GLA_HEADS, GLA_DK, GLA_DV = 4, 128, 256
GLA_GATE_RANK = 16
GLA_GATE_TAU = 16.0
GLA_CHUNK = 64
DSA_HEADS, DSA_KV_HEADS, DSA_HD = 8, 2, 128
IDX_HEADS, IDX_HD = 8, 64
DSA_TOPK_MAX = 256
SWA_HEADS, SWA_KV_HEADS, SWA_HD = 32, 4, 64
WINDOW = 128
MEM_HEADS, MEM_HD = 4, 128
N_MEM = 256

LANES = 128
SUBLANES = 8
VMEM_LIMIT_BYTES = 60000 * 1024
ELEM_SUB = 128

AB_GV, AB_GR, AB_GQ, AB_GK = 0, 1024, 2048, 2560
AB_SMALL = 3072
AB_WIDTH = 3200
D_DQ, D_DK, D_DV, D_IQ, D_SMALL, D_WIDTH = 0, 1024, 1280, 1536, 2048, 2176
SM_IK, SM_GD, SM_IW = 0, 64, 80
C_Q, C_K, C_V, C_WIDTH = 0, 2048, 2304, 2560


def _cparams(sem):
    return pltpu.CompilerParams(dimension_semantics=sem, vmem_limit_bytes=VMEM_LIMIT_BYTES)


def _rms(x, g):
    y = x * lax.rsqrt(jnp.mean(x * x, axis=-1, keepdims=True) + EPS)
    return y * g


def _ffn_kernel(x_ref, g_ref, wg_ref, wu_ref, wd_ref, fg_ref, o_ref, *rest, final_norm, sub, emit_cast):
    h_ref = rest[-1]
    j = pl.program_id(1)
    tm = x_ref.shape[0]
    if emit_cast:
        for src, dst in zip((wg_ref, wu_ref, wd_ref), rest[:3]):
            dst[...] = src[...].astype(BF16)
        wg_ref, wu_ref, wd_ref = rest[:3]

    esub = min(ELEM_SUB, tm)

    def row_group(r):
        return pl.ds(pl.multiple_of(r * esub, esub), esub)

    @pl.when(j == 0)
    def _():
        def body(r, carry):
            rows = row_group(r)
            h_ref[rows, :] = _rms(x_ref[rows, :], g_ref[...]).astype(BF16)
            o_ref[rows, :] = jnp.zeros((esub, D_MODEL), F32)
            return carry

        lax.fori_loop(0, tm // esub, body, 0)

    for r in range(tm // sub):
        rows = slice(r * sub, (r + 1) * sub)
        h = h_ref[rows, :]
        a = jnp.dot(h, wg_ref[...], preferred_element_type=F32)
        u = jnp.dot(h, wu_ref[...], preferred_element_type=F32)
        act = (a * jax.nn.sigmoid(a) * u).astype(BF16)
        o_ref[rows, :] += jnp.dot(act, wd_ref[...], preferred_element_type=F32)

    @pl.when(j == pl.num_programs(1) - 1)
    def _():
        def body(r, carry):
            rows = row_group(r)
            y = x_ref[rows, :] + 0.5 * o_ref[rows, :]
            if final_norm:
                y = _rms(y, fg_ref[...])
            o_ref[rows, :] = y
            return carry

        lax.fori_loop(0, tm // esub, body, 0)


def ffn(x, g, wg, wu, wd, sel, final_g=None, *, tm, tf, sub):
    m = x.shape[0]
    tm = min(tm, m)
    sub = min(sub, tm)
    fg = g if final_g is None else final_g
    emit_cast = sel is not None
    wspec = lambda shape, imap: pl.BlockSpec(shape, imap)
    if emit_cast:
        assert m == tm and wg.dtype == F32
        layer, half = sel
        w_in = [wspec((None, None, D_MODEL, tf), lambda i, j: (layer, half, 0, j)),
                wspec((None, None, D_MODEL, tf), lambda i, j: (layer, half, 0, j)),
                wspec((None, None, tf, D_MODEL), lambda i, j: (layer, half, j, 0))]
    else:
        assert wg.dtype == BF16 and wg.ndim == 2
        w_in = [wspec((D_MODEL, tf), lambda i, j: (0, j)), wspec((D_MODEL, tf), lambda i, j: (0, j)),
                wspec((tf, D_MODEL), lambda i, j: (j, 0))]
    out_specs = [pl.BlockSpec((tm, D_MODEL), lambda i, j: (i, 0))]
    out_shape = [jax.ShapeDtypeStruct((m, D_MODEL), F32)]
    if emit_cast:
        out_specs += [wspec((D_MODEL, tf), lambda i, j: (0, j)), wspec((D_MODEL, tf), lambda i, j: (0, j)),
                      wspec((tf, D_MODEL), lambda i, j: (j, 0))]
        out_shape += [jax.ShapeDtypeStruct((D_MODEL, D_FF), BF16), jax.ShapeDtypeStruct((D_MODEL, D_FF), BF16),
                      jax.ShapeDtypeStruct((D_FF, D_MODEL), BF16)]
    outs = pl.pallas_call(
        functools.partial(_ffn_kernel, final_norm=final_g is not None, sub=sub, emit_cast=emit_cast),
        grid=(m // tm, D_FF // tf),
        in_specs=[pl.BlockSpec((tm, D_MODEL), lambda i, j: (i, 0)), pl.BlockSpec((1, D_MODEL), lambda i, j: (0, 0))]
                 + w_in + [pl.BlockSpec((1, D_MODEL), lambda i, j: (0, 0))],
        out_specs=out_specs,
        out_shape=out_shape,
        scratch_shapes=[pltpu.VMEM((tm, D_MODEL), BF16)],
        compiler_params=_cparams(("parallel", "arbitrary")),
        name="ffn_cast" if emit_cast else "ffn",
    )(x, g.reshape(1, D_MODEL), wg, wu, wd, fg.reshape(1, D_MODEL))
    return tuple(outs) if emit_cast else outs[0]


def _proj_kernel(x_ref, g_ref, w_ref, o_ref, h_ref, *, sub):
    tm = x_ref.shape[0]

    @pl.when(pl.program_id(1) == 0)
    def _():
        for r in range(tm // sub):
            rows = slice(r * sub, (r + 1) * sub)
            h_ref[rows, :] = _rms(x_ref[rows, :], g_ref[...]).astype(BF16)

    o_ref[...] = jnp.dot(h_ref[...], w_ref[...], preferred_element_type=F32)


def norm_proj(x, g, w, *, tm, tn, sub=512):
    m, n = x.shape[0], w.shape[1]
    tm = min(tm, m)
    return pl.pallas_call(
        functools.partial(_proj_kernel, sub=min(sub, tm)),
        grid=(m // tm, n // tn),
        in_specs=[
            pl.BlockSpec((tm, D_MODEL), lambda i, j: (i, 0)),
            pl.BlockSpec((1, D_MODEL), lambda i, j: (0, 0)),
            pl.BlockSpec((D_MODEL, tn), lambda i, j: (0, j)),
        ],
        out_specs=pl.BlockSpec((tm, tn), lambda i, j: (i, j)),
        out_shape=jax.ShapeDtypeStruct((m, n), F32),
        scratch_shapes=[pltpu.VMEM((tm, D_MODEL), BF16)],
        compiler_params=_cparams(("parallel", "arbitrary")),
        name="norm_proj",
    )(x, g.reshape(1, D_MODEL), w)


def _outproj_kernel(*refs, n_in):
    x_ref = refs[0]
    y_refs = refs[1:1 + n_in]
    w_refs = refs[1 + n_in:1 + 2 * n_in]
    o_ref = refs[1 + 2 * n_in]
    acc = x_ref[...]
    for y_ref, w_ref in zip(y_refs, w_refs):
        acc = acc + jnp.dot(y_ref[...].astype(BF16), w_ref[...], preferred_element_type=F32)
    o_ref[...] = acc


def out_proj(x, ys, ws, *, tm):
    m = x.shape[0]
    tm = min(tm, m)
    n_in = len(ys)
    in_specs = [pl.BlockSpec((tm, D_MODEL), lambda i: (i, 0))]
    in_specs += [pl.BlockSpec((tm, y.shape[1]), lambda i: (i, 0)) for y in ys]
    in_specs += [pl.BlockSpec(w.shape, lambda i: (0, 0)) for w in ws]
    return pl.pallas_call(
        functools.partial(_outproj_kernel, n_in=n_in),
        grid=(m // tm,),
        in_specs=in_specs,
        out_specs=pl.BlockSpec((tm, D_MODEL), lambda i: (i, 0)),
        out_shape=jax.ShapeDtypeStruct((m, D_MODEL), F32),
        compiler_params=_cparams(("parallel",)),
        name="out_proj",
    )(x, *ys, *ws)


def _rope_tables(pos, hd):
    half = hd // 2
    inv = ROPE_THETA ** (-jnp.arange(half, dtype=F32) / half)
    ang = pos.astype(F32)[:, None] * inv[None, :]
    cos, sin = jnp.cos(ang), jnp.sin(ang)
    reps = LANES // hd
    return (jnp.concatenate([cos, cos] * reps, axis=-1),
            jnp.concatenate([-sin, sin] * reps, axis=-1))


def _rope128(x, cos, sin):
    return x * cos + pltpu.roll(x, 64, 1) * sin


def _rope64(x, cos, sin, lower):
    partner = jnp.where(lower, pltpu.roll(x, 96, 1), pltpu.roll(x, 32, 1))
    return x * cos + partner * sin


def _lower32_mask(rows):
    lane = lax.broadcasted_iota(jnp.int32, (rows, LANES), 1)
    return (lane % 64) < 32


def _d_proj_feat_kernel(x_ref, g_ref, w_ref, c128_ref, s128_ref, c64_ref, s64_ref, gik_ref,
                        q_ref, k_ref, kb_ref, v_ref, vb_ref, iqo_ref, ik_ref, ikb_ref, iw_ref, *, sub):
    tm = x_ref.shape[0]
    table_rows = c128_ref.shape[0]
    lower = _lower32_mask(sub)
    lane = lax.broadcasted_iota(jnp.int32, (sub, LANES), 1)
    for r in range(tm // sub):
        rows = slice(r * sub, (r + 1) * sub)
        trows = rows if table_rows > 1 else slice(0, 1)
        c128, s128 = c128_ref[trows, :], s128_ref[trows, :]
        c64, s64 = c64_ref[trows, :], s64_ref[trows, :]
        h = _rms(x_ref[rows, :], g_ref[...]).astype(BF16)
        z = jnp.dot(h, w_ref[...], preferred_element_type=F32)
        for hd in range(DSA_HEADS):
            sl = slice(hd * LANES, (hd + 1) * LANES)
            q_ref[rows, sl] = (_rope128(z[:, D_DQ + hd * LANES:D_DQ + (hd + 1) * LANES], c128, s128)
                               * (DSA_HD ** -0.5)).astype(BF16)
        for hd in range(DSA_KV_HEADS):
            sl = slice(hd * LANES, (hd + 1) * LANES)
            kr = _rope128(z[:, D_DK + hd * LANES:D_DK + (hd + 1) * LANES], c128, s128)
            v = z[:, D_DV + hd * LANES:D_DV + (hd + 1) * LANES]
            k_ref[rows, sl] = kr
            kb_ref[rows, sl] = kr.astype(BF16)
            v_ref[rows, sl] = v
            vb_ref[rows, sl] = v.astype(BF16)
        for p in range(IDX_HEADS * IDX_HD // LANES):
            sl = slice(p * LANES, (p + 1) * LANES)
            iqo_ref[rows, sl] = (_rope64(z[:, D_IQ + p * LANES:D_IQ + (p + 1) * LANES], c64, s64, lower)
                                 * (IDX_HD ** -0.5)).astype(BF16)
        sm = z[:, D_SMALL:D_SMALL + LANES]
        ik = jnp.where(lane < IDX_HD, sm, 0.0)
        ik = ik * lax.rsqrt(jnp.sum(ik * ik, axis=-1, keepdims=True) / IDX_HD + EPS) * gik_ref[...]
        ik = _rope64(ik, c64, s64, lower)
        ik_ref[rows, :] = ik[:, :IDX_HD]
        ikb_ref[rows, :] = jnp.where(lane < IDX_HD, ik, pltpu.roll(ik, 64, 1)).astype(BF16)
        iw_ref[rows, :] = pltpu.roll(sm, LANES - SM_IW, 1) * (IDX_HEADS ** -0.5)


def d_proj_features(x, g, w, tabs, gik, *, seq, tm, sub):
    m = x.shape[0]
    tm = min(tm, m)
    sub = min(sub, tm)
    c128, s128, c64, s64 = tabs
    if c128.shape[0] == 1:
        tab_spec = pl.BlockSpec((1, LANES), lambda i: (0, 0))
    else:
        nt = seq // tm
        tab_spec = pl.BlockSpec((tm, LANES), lambda i: (i % nt, 0))
    row = lambda wd: pl.BlockSpec((tm, wd), lambda i: (i, 0))
    gik_pad = jnp.zeros((1, LANES), F32).at[0, :IDX_HD].set(gik)
    return pl.pallas_call(
        functools.partial(_d_proj_feat_kernel, sub=sub),
        grid=(m // tm,),
        in_specs=[row(D_MODEL), pl.BlockSpec((1, D_MODEL), lambda i: (0, 0)),
                  pl.BlockSpec((D_MODEL, D_WIDTH), lambda i: (0, 0), pipeline_mode=pl.Buffered(1)),
                  tab_spec, tab_spec, tab_spec, tab_spec, pl.BlockSpec((1, LANES), lambda i: (0, 0))],
        out_specs=[row(1024), row(256), row(256), row(256), row(256), row(512), row(IDX_HD), row(128), row(128)],
        out_shape=[jax.ShapeDtypeStruct((m, 1024), BF16),
                   jax.ShapeDtypeStruct((m, 256), F32),
                   jax.ShapeDtypeStruct((m, 256), BF16),
                   jax.ShapeDtypeStruct((m, 256), F32),
                   jax.ShapeDtypeStruct((m, 256), BF16),
                   jax.ShapeDtypeStruct((m, 512), BF16),
                   jax.ShapeDtypeStruct((m, IDX_HD), F32),
                   jax.ShapeDtypeStruct((m, 128), BF16),
                   jax.ShapeDtypeStruct((m, 128), F32)],
        compiler_params=_cparams(("parallel",)),
        name="d_proj_features",
    )(x, g.reshape(1, D_MODEL), w, c128, s128, c64, s64, gik_pad)


def _c_proj_feat_kernel(x_ref, g_ref, w_ref, c64_ref, s64_ref,
                        qo_ref, ko_ref, kb_ref, kbs_ref, vo_ref, vb_ref, vbs_ref, *, sub):
    tm = x_ref.shape[0]
    table_rows = c64_ref.shape[0]
    lower = _lower32_mask(sub)
    for r in range(tm // sub):
        rows = slice(r * sub, (r + 1) * sub)
        trows = rows if table_rows > 1 else slice(0, 1)
        c64, s64 = c64_ref[trows, :], s64_ref[trows, :]
        h = _rms(x_ref[rows, :], g_ref[...]).astype(BF16)
        z = jnp.dot(h, w_ref[...], preferred_element_type=F32)
        for p in range(SWA_HEADS * SWA_HD // LANES):
            sl = slice(p * LANES, (p + 1) * LANES)
            qo_ref[rows, sl] = (_rope64(z[:, C_Q + p * LANES:C_Q + (p + 1) * LANES], c64, s64, lower)
                                * (SWA_HD ** -0.5)).astype(BF16)
        for p in range(SWA_KV_HEADS * SWA_HD // LANES):
            sl = slice(p * LANES, (p + 1) * LANES)
            kr = _rope64(z[:, C_K + p * LANES:C_K + (p + 1) * LANES], c64, s64, lower)
            v = z[:, C_V + p * LANES:C_V + (p + 1) * LANES]
            ko_ref[rows, sl] = kr
            kb_ref[rows, sl] = kr.astype(BF16)
            kbs_ref[rows, sl] = pltpu.roll(kr, 64, 1).astype(BF16)
            vo_ref[rows, sl] = v
            vb_ref[rows, sl] = v.astype(BF16)
            vbs_ref[rows, sl] = pltpu.roll(v, 64, 1).astype(BF16)


def c_proj_features(x, g, w, tabs, *, seq, tm, sub):
    m = x.shape[0]
    tm = min(tm, m)
    sub = min(sub, tm)
    c64, s64 = tabs
    if c64.shape[0] == 1:
        tab_spec = pl.BlockSpec((1, LANES), lambda i: (0, 0))
    else:
        nt = seq // tm
        tab_spec = pl.BlockSpec((tm, LANES), lambda i: (i % nt, 0))
    row = lambda wd: pl.BlockSpec((tm, wd), lambda i: (i, 0))
    return pl.pallas_call(
        functools.partial(_c_proj_feat_kernel, sub=sub),
        grid=(m // tm,),
        in_specs=[row(D_MODEL), pl.BlockSpec((1, D_MODEL), lambda i: (0, 0)),
                  pl.BlockSpec((D_MODEL, C_WIDTH), lambda i: (0, 0), pipeline_mode=pl.Buffered(1)),
                  tab_spec, tab_spec],
        out_specs=[row(2048), row(256), row(256), row(256), row(256), row(256), row(256)],
        out_shape=[jax.ShapeDtypeStruct((m, 2048), BF16),
                   jax.ShapeDtypeStruct((m, 256), F32),
                   jax.ShapeDtypeStruct((m, 256), BF16),
                   jax.ShapeDtypeStruct((m, 256), BF16),
                   jax.ShapeDtypeStruct((m, 256), F32),
                   jax.ShapeDtypeStruct((m, 256), BF16),
                   jax.ShapeDtypeStruct((m, 256), BF16)],
        compiler_params=_cparams(("parallel",)),
        name="c_proj_features",
    )(x, g.reshape(1, D_MODEL), w, c64, s64)


def _dot_nt(a, b, **kw):
    return lax.dot_general(a, b, (((1,), (1,)), ((), ())), preferred_element_type=F32, **kw)


def _dot_tn(a, b, **kw):
    return lax.dot_general(a, b, (((0,), (0,)), ((), ())), preferred_element_type=F32, **kw)


_HI = lax.Precision.HIGHEST


def _log_decay(sm, wgate, bgate):
    pre = jnp.dot(sm, wgate, preferred_element_type=F32, precision=_HI) + bgate
    return (jnp.minimum(pre, 0.0) - jnp.log1p(jnp.exp(-jnp.abs(pre)))) / GLA_GATE_TAU


def _gla_gate_out(o, r, gain):
    g = o * lax.rsqrt(jnp.mean(o * o, axis=-1, keepdims=True) + EPS) * gain
    return g * (r * jax.nn.sigmoid(r))


def _gla_prompt_kernel(gv_ref, gr_ref, gq_ref, gk_ref, sm_ref, wgate_ref, bgate_ref, gain_ref,
                       y_ref, st_ref, s_ref, b_ref, *, n_seq, n_chunks):
    c = pl.program_id(1)

    @pl.when(c == 0)
    def _():
        s_ref[...] = jnp.zeros_like(s_ref)

    tc = n_chunks * GLA_CHUNK
    ri = lax.broadcasted_iota(jnp.int32, (tc, tc), 0)
    ci = lax.broadcasted_iota(jnp.int32, (tc, tc), 1)
    tril = jnp.where(jnp.logical_and(ri // GLA_CHUNK == ci // GLA_CHUNK, ri >= ci), 1.0, 0.0).astype(F32)
    causal = (lax.broadcasted_iota(jnp.int32, (GLA_CHUNK, GLA_CHUNK), 0)
              >= lax.broadcasted_iota(jnp.int32, (GLA_CHUNK, GLA_CHUNK), 1))
    gain = gain_ref[...]
    for s in range(n_seq):
        la = _log_decay(sm_ref[s], wgate_ref[...], bgate_ref[...])
        b_ref[s] = jnp.dot(tril, la, preferred_element_type=F32, precision=_HI)
    for n in range(n_chunks):
        rows = slice(n * GLA_CHUNK, (n + 1) * GLA_CHUNK)
        for s in range(n_seq):
            b = b_ref[s, rows, :]
            b_end = b_ref[s, (n + 1) * GLA_CHUNK - 1:(n + 1) * GLA_CHUNK, :]
            k = gk_ref[s, rows, :]
            q_in = gq_ref[s, rows, :] * (GLA_DK ** -0.5) * jnp.exp(b)
            k_in = k * jnp.exp(-b)
            k_end = k * jnp.exp(b_end - b)
            decay = jnp.exp(b_end)
            for h in range(GLA_HEADS):
                dk = slice(h * GLA_DK, (h + 1) * GLA_DK)
                dv = slice(h * GLA_DV, (h + 1) * GLA_DV)
                v = gv_ref[s, rows, dv].astype(BF16)
                st = s_ref[s, h]
                qh = q_in[:, dk].astype(BF16)
                att = jnp.where(causal, _dot_nt(qh, k_in[:, dk].astype(BF16)), 0.0)
                o = _dot_nt(qh, st.astype(BF16)) + jnp.dot(att.astype(BF16), v, preferred_element_type=F32)
                s_ref[s, h] = st * decay[:, dk] + _dot_tn(v, k_end[:, dk].astype(BF16))
                y_ref[s, rows, dv] = _gla_gate_out(o, gr_ref[s, rows, dv], gain)

    @pl.when(c == pl.num_programs(1) - 1)
    def _():
        st_ref[...] = s_ref[...]


def gla_prompt(z, wgate, bgate, gain, *, batch, seq, tc, n_seq):
    m = z.shape[0]
    z3 = z.reshape(batch, seq, z.shape[1])
    col = lambda w, off: pl.BlockSpec((n_seq, tc, w), lambda b, c: (b, c, off // w))
    const = lambda shape: pl.BlockSpec(shape, lambda b, c: (0,) * len(shape))
    y, st = pl.pallas_call(
        functools.partial(_gla_prompt_kernel, n_seq=n_seq, n_chunks=tc // GLA_CHUNK),
        grid=(batch // n_seq, seq // tc),
        in_specs=[col(1024, AB_GV), col(1024, AB_GR), col(512, AB_GQ), col(512, AB_GK), col(128, AB_SMALL),
                  const((LANES, GLA_HEADS * GLA_DK)), const((1, GLA_HEADS * GLA_DK)), const((1, GLA_DV))],
        out_specs=[pl.BlockSpec((n_seq, tc, GLA_HEADS * GLA_DV), lambda b, c: (b, c, 0)),
                   pl.BlockSpec((n_seq, GLA_HEADS, GLA_DV, GLA_DK), lambda b, c: (b, 0, 0, 0))],
        out_shape=[jax.ShapeDtypeStruct((batch, seq, GLA_HEADS * GLA_DV), F32),
                   jax.ShapeDtypeStruct((batch, GLA_HEADS, GLA_DV, GLA_DK), F32)],
        scratch_shapes=[pltpu.VMEM((n_seq, GLA_HEADS, GLA_DV, GLA_DK), F32),
                        pltpu.VMEM((n_seq, tc, GLA_HEADS * GLA_DK), F32)],
        compiler_params=_cparams(("parallel", "arbitrary")),
        name="gla_prompt",
    )(z3, z3, z3, z3, z3, wgate, bgate, gain)
    return y.reshape(m, GLA_HEADS * GLA_DV), st


INT_MIN = -2 ** 31


def _order_key(score):
    score = jnp.where(score == 0.0, 0.0, score)
    bits = lax.bitcast_convert_type(score, jnp.int32)
    return jnp.where(bits < 0, bits ^ jnp.int32(0x7FFFFFFF), bits)


def _lane_total(x):
    return jnp.dot(x.astype(BF16), jnp.ones((LANES, LANES), BF16), preferred_element_type=F32)


ROW_SUB = 128


def _count_blocks(key_ref, rows, n_blocks, pred):
    acc = jnp.zeros((ROW_SUB, LANES), F32)
    for c in range(n_blocks):
        acc = acc + jnp.where(pred(key_ref[rows, c * LANES:(c + 1) * LANES]), 1.0, 0.0)
    return acc


def _masked_key():
    bits = int(np.array(NEG_INF, np.float32).view(np.int32))
    return bits ^ 0x7FFFFFFF


def _kth_largest_key(key_ref, t_ref, n_blocks, k, active):
    n_sub = key_ref.shape[0] // ROW_SUB
    t_ref[...] = jnp.full(t_ref.shape, INT_MIN, jnp.int32)

    def body(it, carry):
        step = lax.shift_left(jnp.int32(1), 31 - it)
        cands, accs = [], []
        for rb in range(n_sub):
            rows = slice(rb * ROW_SUB, (rb + 1) * ROW_SUB)
            cand = t_ref[rows, :] + step
            cands.append(cand)
            accs.append(_count_blocks(key_ref, rows, active(rb), lambda kc, cand=cand: kc >= cand))
        total = _lane_total(jnp.concatenate(accs, axis=0))
        for rb in range(n_sub):
            rows = slice(rb * ROW_SUB, (rb + 1) * ROW_SUB)
            skipped = float((n_blocks - active(rb)) * LANES)
            count = total[rows, :] + jnp.where(cands[rb] <= _masked_key(), skipped, 0.0)
            t_ref[rows, :] = jnp.where(count >= k, cands[rb], t_ref[rows, :])
        return carry

    lax.fori_loop(0, 32, body, 0)


def _select_topk(key_ref, t_ref, n_blocks, k, write_fn, active=None):
    if active is None:
        active = lambda rb: n_blocks
    _kth_largest_key(key_ref, t_ref, n_blocks, k, active)
    ri = lax.broadcasted_iota(jnp.int32, (LANES, LANES), 0)
    ci = lax.broadcasted_iota(jnp.int32, (LANES, LANES), 1)
    before = jnp.where(ri < ci, 1.0, 0.0).astype(BF16)
    for rb in range(key_ref.shape[0] // ROW_SUB):
        rows = slice(rb * ROW_SUB, (rb + 1) * ROW_SUB)
        t = t_ref[rows, :]
        skipped = float((n_blocks - active(rb)) * LANES)
        above = _lane_total(_count_blocks(key_ref, rows, active(rb), lambda kc: kc > t))
        need = k - above - jnp.where(t < _masked_key(), skipped, 0.0)
        run = jnp.zeros((ROW_SUB, LANES), F32)
        for c in range(n_blocks):
            if c >= active(rb):
                write_fn(rows, c, jnp.zeros((ROW_SUB, LANES), jnp.bool_))
                continue
            kc = key_ref[rows, c * LANES:(c + 1) * LANES]
            eq = jnp.where(kc == t, 1.0, 0.0)
            rank = jnp.dot(eq.astype(BF16), before, preferred_element_type=F32) + run
            take = jnp.where(kc > t, 1.0, jnp.where(rank < need, eq, 0.0))
            write_fn(rows, c, take > 0.0)
            run = run + _lane_total(eq)


def _half_mask(rows, upper):
    lane = lax.broadcasted_iota(jnp.int32, (rows, LANES), 1)
    return (lane >= 64) if upper else (lane < 64)


DSA_STRATUM = 512
KEY_CHUNK = 512


def _dsa_select_prompt_kernel(iq_ref, iw_ref, ikb_ref, bias_ref, key_ref, t_ref, *, row0, n_keys, topk):
    rows = iq_ref.shape[0]
    n_blocks = n_keys // LANES
    lane = lax.broadcasted_iota(jnp.int32, (ROW_SUB, LANES), 1)
    sub = lax.broadcasted_iota(jnp.int32, (ROW_SUB, LANES), 0)

    iw = iw_ref[...]
    qh = []
    for h in range(IDX_HEADS):
        pair = iq_ref[:, (h // 2) * LANES:(h // 2 + 1) * LANES]
        qh.append(jnp.where(_half_mask(rows, h % 2 == 1), pair, jnp.zeros_like(pair)))
    for kc in range(n_keys // KEY_CHUNK):
        ik = ikb_ref[kc * KEY_CHUNK:(kc + 1) * KEY_CHUNK, :]
        score = jnp.zeros((rows, KEY_CHUNK), F32)
        for h in range(IDX_HEADS):
            score = score + jnp.maximum(_dot_nt(qh[h], ik), 0.0) * iw[:, h:h + 1]
        for rb in range(rows // ROW_SUB):
            for cb in range(KEY_CHUNK // LANES):
                c = kc * (KEY_CHUNK // LANES) + cb
                causal = (c * LANES + lane) <= (row0 + rb * ROW_SUB + sub)
                part = score[rb * ROW_SUB:(rb + 1) * ROW_SUB, cb * LANES:(cb + 1) * LANES]
                key_ref[rb * ROW_SUB:(rb + 1) * ROW_SUB, c * LANES:(c + 1) * LANES] = _order_key(
                    jnp.where(causal, part, NEG_INF))

    def write(rws, c, sel):
        causal = (c * LANES + lane) <= (row0 + rws.start + sub)
        bias_ref[rws, c * LANES:(c + 1) * LANES] = jnp.where(jnp.logical_and(sel, causal), 0.0, NEG_INF)

    _select_topk(key_ref, t_ref, n_blocks, topk, write, active=lambda rb: row0 // LANES + rb + 1)


def _dsa_attend_prompt_kernel(q_ref, bias_ref, kb_ref, vb_ref, y_ref):
    bias = bias_ref[...]
    group = DSA_HEADS // DSA_KV_HEADS
    for kv in range(DSA_KV_HEADS):
        kvs = slice(kv * DSA_HD, (kv + 1) * DSA_HD)
        q4 = jnp.concatenate([q_ref[:, (kv * group + g) * DSA_HD:(kv * group + g + 1) * DSA_HD] for g in range(group)],
                             axis=0)
        s4 = _dot_nt(q4, kb_ref[:, kvs])
        ps, dens = [], []
        for g in range(group):
            s = s4[g * Q_BLOCK:(g + 1) * Q_BLOCK, :] + bias
            p = jnp.exp(s - jnp.max(s, axis=-1, keepdims=True))
            dens.append(jnp.sum(p, axis=-1, keepdims=True))
            ps.append(p.astype(BF16))
        o4 = jnp.dot(jnp.concatenate(ps, axis=0), vb_ref[:, kvs], preferred_element_type=F32)
        for g in range(group):
            h = kv * group + g
            y_ref[:, h * DSA_HD:(h + 1) * DSA_HD] = o4[g * Q_BLOCK:(g + 1) * Q_BLOCK, :] / dens[g]


def dsa_prompt(q, iq, iw, kb, vb, ikb, *, batch, seq):
    topk = min(DSA_TOPK_MAX, seq // 4)
    as3 = lambda a: a.reshape(batch, seq, a.shape[-1])
    q, iq, iw, kb, vb, ikb = (as3(a) for a in (q, iq, iw, kb, vb, ikb))
    nsub = DSA_STRATUM // Q_BLOCK
    outs = []
    for r in range(seq // DSA_STRATUM):
        n_keys = (r + 1) * DSA_STRATUM
        strat = lambda w: pl.BlockSpec((None, DSA_STRATUM, w), lambda b: (b, r, 0))
        keys = lambda w: pl.BlockSpec((None, n_keys, w), lambda b: (b, 0, 0))
        bias = pl.pallas_call(
            functools.partial(_dsa_select_prompt_kernel, row0=r * DSA_STRATUM, n_keys=n_keys, topk=topk),
            grid=(batch,),
            in_specs=[strat(512), strat(128), keys(128)],
            out_specs=pl.BlockSpec((None, DSA_STRATUM, n_keys), lambda b: (b, 0, 0)),
            out_shape=jax.ShapeDtypeStruct((batch, DSA_STRATUM, n_keys), F32),
            scratch_shapes=[pltpu.VMEM((DSA_STRATUM, n_keys), jnp.int32), pltpu.VMEM((DSA_STRATUM, LANES), jnp.int32)],
            compiler_params=_cparams(("parallel",)),
            name="dsa_select_prompt",
        )(iq, iw, ikb)
        qrow = lambda w: pl.BlockSpec((None, Q_BLOCK, w), lambda b, i: (b, r * nsub + i, 0))
        keys2 = lambda w: pl.BlockSpec((None, n_keys, w), lambda b, i: (b, 0, 0))
        outs.append(pl.pallas_call(
            _dsa_attend_prompt_kernel,
            grid=(batch, nsub),
            in_specs=[qrow(1024), pl.BlockSpec((None, Q_BLOCK, n_keys), lambda b, i: (b, i, 0)), keys2(256), keys2(256)],
            out_specs=pl.BlockSpec((None, Q_BLOCK, DSA_HEADS * DSA_HD), lambda b, i: (b, i, 0)),
            out_shape=jax.ShapeDtypeStruct((batch, DSA_STRATUM, DSA_HEADS * DSA_HD), F32),
            compiler_params=_cparams(("parallel", "arbitrary")),
            name="dsa_attend_prompt",
        )(q, bias, kb, vb))
    return jnp.concatenate(outs, axis=1).reshape(batch * seq, DSA_HEADS * DSA_HD)


def _swa_head_plan(h):
    group = SWA_HEADS // SWA_KV_HEADS
    kv = h // group
    return h // 2, h % 2, kv // 2, (kv % 2) != (h % 2)


def _swa_prompt_kernel(sink_ref, q_ref, kp_ref, kc_ref, kps_ref, kcs_ref, vp_ref, vc_ref, vps_ref, vcs_ref, y_ref):
    i = pl.program_id(1)
    r = lax.broadcasted_iota(jnp.int32, (Q_BLOCK, 2 * Q_BLOCK), 0)
    c = lax.broadcasted_iota(jnp.int32, (Q_BLOCK, 2 * Q_BLOCK), 1)
    rel = Q_BLOCK + r - c
    ok = (rel >= 0) & (rel <= WINDOW) & ((i - 1) * Q_BLOCK + c >= 0)
    bias = jnp.where(ok, 0.0, NEG_INF)
    keys = (jnp.concatenate([kp_ref[...], kc_ref[...]], axis=0), jnp.concatenate([kps_ref[...], kcs_ref[...]], axis=0))
    vals = (jnp.concatenate([vp_ref[...], vc_ref[...]], axis=0), jnp.concatenate([vps_ref[...], vcs_ref[...]], axis=0))
    lower = _half_mask(Q_BLOCK, False)
    for p in range(SWA_HEADS // 2):
        qpair = q_ref[:, p * LANES:(p + 1) * LANES]
        outs = []
        for h in (2 * p, 2 * p + 1):
            _, half, ks, swapped = _swa_head_plan(h)
            qh = jnp.where(_half_mask(Q_BLOCK, half == 1), qpair, jnp.zeros_like(qpair))
            kk = keys[int(swapped)][:, ks * LANES:(ks + 1) * LANES]
            vv = vals[int(swapped)][:, ks * LANES:(ks + 1) * LANES]
            s = _dot_nt(qh, kk) + bias
            sink = sink_ref[h]
            mx = jnp.maximum(jnp.max(s, axis=-1, keepdims=True), sink)
            pr = jnp.exp(s - mx)
            den = jnp.sum(pr, axis=-1, keepdims=True) + jnp.exp(sink - mx)
            outs.append(jnp.dot(pr.astype(BF16), vv, preferred_element_type=F32) / den)
        y_ref[:, p * LANES:(p + 1) * LANES] = jnp.where(lower, outs[0], outs[1])


def swa_prompt(q, kb, kbs, vb, vbs, sinks, *, batch, seq):
    m = q.shape[0]
    nq = seq // Q_BLOCK
    cur = pl.BlockSpec((Q_BLOCK, 256), lambda b, i, s: (b * nq + i, 0))
    prev = pl.BlockSpec((Q_BLOCK, 256), lambda b, i, s: (b * nq + jnp.maximum(i - 1, 0), 0))
    qspec = pl.BlockSpec((Q_BLOCK, 2048), lambda b, i, s: (b * nq + i, 0))
    return pl.pallas_call(
        _swa_prompt_kernel,
        grid_spec=pltpu.PrefetchScalarGridSpec(
            num_scalar_prefetch=1,
            grid=(batch, nq),
            in_specs=[qspec, prev, cur, prev, cur, prev, cur, prev, cur],
            out_specs=qspec,
        ),
        out_shape=jax.ShapeDtypeStruct((m, SWA_HEADS * SWA_HD), F32),
        compiler_params=_cparams(("parallel", "arbitrary")),
        name="swa_prompt",
    )(sinks, q, kb, kb, kbs, kbs, vb, vb, vbs, vbs)


def _mem_prompt_kernel(x_ref, g_ref, wq_ref, kv_ref, wo_ref, *rest, n_in):
    y_refs, w_refs, o_ref = rest[:n_in], rest[n_in:2 * n_in], rest[2 * n_in]
    x = x_ref[...]
    for y_ref, w_ref in zip(y_refs, w_refs):
        x = x + jnp.dot(y_ref[...].astype(BF16), w_ref[...], preferred_element_type=F32)
    h = _rms(x, g_ref[...]).astype(BF16)
    q = (jnp.dot(h, wq_ref[...], preferred_element_type=F32) * (MEM_HD ** -0.5)).astype(BF16)
    width = MEM_HEADS * MEM_HD
    outs = []
    for hd in range(MEM_HEADS):
        sl = slice(hd * MEM_HD, (hd + 1) * MEM_HD)
        k = kv_ref[:, sl].astype(BF16)
        v = kv_ref[:, width + hd * MEM_HD:width + (hd + 1) * MEM_HD].astype(BF16)
        s = _dot_nt(q[:, sl], k)
        p = jnp.exp(s - jnp.max(s, axis=-1, keepdims=True))
        o = jnp.dot(p.astype(BF16), v, preferred_element_type=F32) / jnp.sum(p, axis=-1, keepdims=True)
        outs.append(o.astype(BF16))
    o_ref[...] = x + jnp.dot(jnp.concatenate(outs, axis=-1), wo_ref[...], preferred_element_type=F32)


def mem_attn_prompt(x, ys, ws, g, wq, kv, wo, *, batch, seq, tq):
    m = x.shape[0]
    nq = seq // tq
    width = MEM_HEADS * MEM_HD
    rows = lambda w: pl.BlockSpec((tq, w), lambda b, i: (b * nq + i, 0))
    const = lambda shape: pl.BlockSpec(shape, lambda b, i: (0, 0), pipeline_mode=pl.Buffered(1))
    return pl.pallas_call(
        functools.partial(_mem_prompt_kernel, n_in=len(ys)),
        grid=(batch, nq),
        in_specs=[rows(D_MODEL),
                  pl.BlockSpec((1, D_MODEL), lambda b, i: (0, 0)),
                  const((D_MODEL, width)),
                  pl.BlockSpec((N_MEM, 2 * width), lambda b, i: (b, 0)),
                  const((width, D_MODEL))]
                 + [rows(y.shape[1]) for y in ys] + [const(w.shape) for w in ws],
        out_specs=rows(D_MODEL),
        out_shape=jax.ShapeDtypeStruct((m, D_MODEL), F32),
        compiler_params=_cparams(("parallel", "arbitrary")),
        name="mem_attn_prompt",
    )(x, g.reshape(1, D_MODEL), wq, kv, wo, *ys, *ws)


def _gla_prep_kernel(gq_ref, gk_ref, sm_ref, wgate_ref, bgate_ref, o_ref):
    w = GLA_HEADS * GLA_DK
    la = _log_decay(sm_ref[...], wgate_ref[...], bgate_ref[...])
    o_ref[:, 0:w] = jnp.exp(la)
    o_ref[:, w:2 * w] = gk_ref[...]
    o_ref[:, 2 * w:3 * w] = gq_ref[...] * (GLA_DK ** -0.5)
    o_ref[:, 3 * w:4 * w] = jnp.zeros((gq_ref.shape[0], w), F32)


def gla_sample_prep(z, wgate, bgate):
    m = z.shape[0]
    w = GLA_HEADS * GLA_DK
    col = lambda wd, off: pl.BlockSpec((m, wd), lambda i: (0, off // wd))
    return pl.pallas_call(
        _gla_prep_kernel,
        grid=(1,),
        in_specs=[col(512, AB_GQ), col(512, AB_GK), col(128, AB_SMALL),
                  pl.BlockSpec((LANES, w), lambda i: (0, 0)), pl.BlockSpec((1, w), lambda i: (0, 0))],
        out_specs=pl.BlockSpec((m, 4 * w), lambda i: (0, 0)),
        out_shape=jax.ShapeDtypeStruct((m, 4 * w), F32),
        compiler_params=_cparams(("arbitrary",)),
        name="gla_sample_prep",
    )(z, z, z, wgate, bgate)


def _gla_step_kernel(p_ref, gv_ref, gr_ref, s_ref, gain_ref, so_ref, y_ref, *, bs):
    gain = gain_ref[...]
    for s in range(bs):
        xt = p_ref[s].T
        for h in range(GLA_HEADS):
            dv = slice(h * GLA_DV, (h + 1) * GLA_DV)
            st = s_ref[s, h] * xt[:, h:h + 1] + xt[:, GLA_HEADS + h:GLA_HEADS + h + 1] * gv_ref[s:s + 1, dv]
            so_ref[s, h] = st
            o = jnp.sum(xt[:, 2 * GLA_HEADS + h:2 * GLA_HEADS + h + 1] * st, axis=0, keepdims=True)
            y_ref[s:s + 1, dv] = _gla_gate_out(o, gr_ref[s:s + 1, dv], gain)


def gla_sample(p, z, state, gain, *, bs):
    m = z.shape[0]
    col = lambda w, off: pl.BlockSpec((bs, w), lambda i: (i, off // w))
    sspec = pl.BlockSpec((bs, GLA_HEADS, GLA_DK, GLA_DV), lambda i: (i, 0, 0, 0))
    return pl.pallas_call(
        functools.partial(_gla_step_kernel, bs=bs),
        grid=(m // bs,),
        in_specs=[pl.BlockSpec((bs, 16, GLA_DK), lambda i: (i, 0, 0)), col(1024, AB_GV), col(1024, AB_GR), sspec,
                  pl.BlockSpec((1, GLA_DV), lambda i: (0, 0))],
        out_specs=[sspec, pl.BlockSpec((bs, GLA_HEADS * GLA_DV), lambda i: (i, 0))],
        out_shape=[jax.ShapeDtypeStruct(state.shape, F32), jax.ShapeDtypeStruct((m, GLA_HEADS * GLA_DV), F32)],
        compiler_params=_cparams(("parallel",)),
        name="gla_sample",
    )(p, z, z, state, gain)


def _dsa_scores_kernel(pt_ref, iq_ref, iw_ref, pool_hbm, o_ref, buf, sem, *, page_base, n_pages, depth):
    b = pl.program_id(0)
    nb = pl.num_programs(0)

    def page_copies(row):
        slot = row % (depth + 1)
        return [pltpu.make_async_copy(pool_hbm.at[page_base + pt_ref[row, j]], buf.at[slot, j], sem.at[slot])
                for j in range(n_pages)]

    @pl.when(b == 0)
    def _():
        for ahead in range(depth):
            for cp in page_copies(ahead):
                cp.start()

    @pl.when(b + depth < nb)
    def _():
        for cp in page_copies(b + depth):
            cp.start()

    for cp in page_copies(b):
        cp.wait()
    slot = b % (depth + 1)
    q8 = iq_ref[0]
    iw = iw_ref[0]
    for j in range(n_pages):
        dots = jnp.dot(q8, buf[slot, j].astype(BF16), preferred_element_type=F32)
        o_ref[0, j:j + 1, :] = jnp.sum(jnp.maximum(dots, 0.0) * iw, axis=0, keepdims=True)


def dsa_sample_scores(page_table, iq, iw, pool_ik_t, page_base, *, depth):
    m, n_pages = page_table.shape
    assert m > depth
    return pl.pallas_call(
        functools.partial(_dsa_scores_kernel, page_base=page_base, n_pages=n_pages, depth=depth),
        grid_spec=pltpu.PrefetchScalarGridSpec(
            num_scalar_prefetch=1,
            grid=(m,),
            in_specs=[pl.BlockSpec((1, IDX_HEADS, IDX_HD), lambda b, pt: (b, 0, 0)),
                      pl.BlockSpec((1, IDX_HEADS, 1), lambda b, pt: (b, 0, 0)),
                      pl.BlockSpec(memory_space=pl.ANY)],
            out_specs=pl.BlockSpec((1, n_pages, PAGE_SIZE), lambda b, pt: (b, 0, 0)),
            scratch_shapes=[pltpu.VMEM((depth + 1, n_pages, IDX_HD, PAGE_SIZE), F32),
                            pltpu.SemaphoreType.DMA((depth + 1,))],
        ),
        out_shape=jax.ShapeDtypeStruct((m, n_pages, PAGE_SIZE), F32),
        compiler_params=_cparams(("arbitrary",)),
        name="dsa_sample_scores",
    )(page_table, iq, iw, pool_ik_t)


def _dsa_select_sample_kernel(sc_ref, iq_ref, iw_ref, ikb_ref, bias_ref, bnew_ref, key_ref, t_ref, *, n_past, topk):
    rows = sc_ref.shape[0]
    n_blocks = n_past // LANES
    for c in range(n_blocks):
        sl = slice(c * LANES, (c + 1) * LANES)
        key_ref[:, sl] = _order_key(sc_ref[:, sl])
    ik = ikb_ref[...].astype(F32)
    iw = iw_ref[...]
    s_new = jnp.zeros((rows, 1), F32)
    for h in range(IDX_HEADS):
        pair = iq_ref[:, (h // 2) * LANES:(h // 2 + 1) * LANES].astype(F32)
        qh = jnp.where(_half_mask(rows, h % 2 == 1), pair, 0.0)
        s_new = s_new + jnp.maximum(jnp.sum(qh * ik, axis=-1, keepdims=True), 0.0) * iw[:, h:h + 1]
    lane = lax.broadcasted_iota(jnp.int32, (rows, LANES), 1)
    key_ref[:, n_past:n_past + LANES] = _order_key(jnp.where(lane == 0, s_new, -jnp.inf))

    ri = lax.broadcasted_iota(jnp.int32, (LANES, DSA_KV_HEADS * LANES), 0)
    ci = lax.broadcasted_iota(jnp.int32, (LANES, DSA_KV_HEADS * LANES), 1)
    spread = jnp.where(ci // DSA_KV_HEADS == ri, 1.0, 0.0).astype(BF16)

    def write(rws, c, sel):
        if c == n_blocks:
            bnew_ref[rws, :] = jnp.where(sel, 0.0, NEG_INF)
        else:
            wide = jnp.dot(jnp.where(sel, 1.0, 0.0).astype(BF16), spread, preferred_element_type=F32)
            w = DSA_KV_HEADS * LANES
            bias_ref[rws, c * w:(c + 1) * w] = jnp.where(wide > 0.5, 0.0, NEG_INF)

    _select_topk(key_ref, t_ref, n_blocks + 1, topk, write)


def dsa_sample_select(scores, iq, iw, ikb, *, topk):
    m, n_past = scores.shape
    full = lambda a: pl.BlockSpec(a.shape, lambda i: (0,) * a.ndim)
    wide = DSA_KV_HEADS * n_past
    return pl.pallas_call(
        functools.partial(_dsa_select_sample_kernel, n_past=n_past, topk=topk),
        grid=(1,),
        in_specs=[full(scores), full(iq), full(iw), full(ikb)],
        out_specs=[pl.BlockSpec((m, wide), lambda i: (0, 0)), pl.BlockSpec((m, LANES), lambda i: (0, 0))],
        out_shape=[jax.ShapeDtypeStruct((m, wide), F32), jax.ShapeDtypeStruct((m, LANES), F32)],
        scratch_shapes=[pltpu.VMEM((m, n_past + LANES), jnp.int32), pltpu.VMEM((m, LANES), jnp.int32)],
        compiler_params=_cparams(("arbitrary",)),
        name="dsa_sample_select",
    )(scores, iq, iw, ikb)


def _dsa_attend_sample_kernel(pt_ref, q_ref, bias_ref, bnew_ref, kn_ref, vn_ref, pk_hbm, pv_hbm, y_ref,
                              kbuf, vbuf, sem, *, page_base, n_pages, chunk):
    b = pl.program_id(0)
    wide = DSA_KV_HEADS * PAGE_SIZE
    n_chunks = n_pages // chunk
    group = DSA_HEADS // DSA_KV_HEADS

    def page_copies(row, c, slot):
        out = []
        for j in range(chunk):
            src = pl.ds(pl.multiple_of((page_base + pt_ref[row, c * chunk + j]) * wide, wide), wide)
            dst = pl.ds(j * wide, wide)
            out.append(pltpu.make_async_copy(pk_hbm.at[src, :], kbuf.at[slot, dst, :], sem.at[0, slot]))
            out.append(pltpu.make_async_copy(pv_hbm.at[src, :], vbuf.at[slot, dst, :], sem.at[1, slot]))
        return out

    @pl.when(b == 0)
    def _():
        for cp in page_copies(0, 0, 0):
            cp.start()

    q8 = q_ref[0]
    first = lax.broadcasted_iota(jnp.int32, (DSA_HEADS, DSA_HD), 0) < group
    hrow = lax.broadcasted_iota(jnp.int32, (DSA_HEADS, wide), 0)
    col = lax.broadcasted_iota(jnp.int32, (DSA_HEADS, wide), 1)
    own = (col % DSA_KV_HEADS) == (hrow // group)
    m_run = jnp.full((DSA_HEADS, DSA_HD), NEG_INF, F32)
    l_run = jnp.zeros((DSA_HEADS, DSA_HD), F32)
    acc = jnp.zeros((DSA_HEADS, DSA_HD), F32)
    for c in range(n_chunks):
        slot = c % 2
        if c + 1 < n_chunks:
            for cp in page_copies(b, c + 1, 1 - slot):
                cp.start()
        else:
            @pl.when(b + 1 < pl.num_programs(0))
            def _():
                for cp in page_copies(b + 1, 0, 1 - slot):
                    cp.start()
        for cp in page_copies(b, c, slot):
            cp.wait()
        ss, oks = [], []
        for j in range(chunk):
            ok = jnp.logical_and(own, bias_ref[0, c * chunk + j:c * chunk + j + 1, :] == 0.0)
            kp = kbuf[slot, j * wide:(j + 1) * wide, :].astype(BF16)
            ss.append(jnp.where(ok, _dot_nt(q8, kp), NEG_INF))
            oks.append(ok)
        mx = ss[0]
        for s in ss[1:]:
            mx = jnp.maximum(mx, s)
        m_new = jnp.maximum(m_run, jnp.max(mx, axis=-1, keepdims=True))
        psum = jnp.zeros((DSA_HEADS, wide), F32)
        pv = jnp.zeros((DSA_HEADS, DSA_HD), F32)
        for j in range(chunk):
            p = jnp.where(oks[j], jnp.exp(ss[j] - m_new[:, 0:1]), 0.0)
            psum = psum + p
            vp = vbuf[slot, j * wide:(j + 1) * wide, :].astype(BF16)
            pv = pv + jnp.dot(p.astype(BF16), vp, preferred_element_type=F32)
        alpha = jnp.exp(m_run - m_new)
        l_run = alpha * l_run + jnp.sum(psum, axis=-1, keepdims=True)
        acc = alpha * acc + pv
        m_run = m_new

    kn = kn_ref[0].astype(BF16).astype(F32)
    vn = vn_ref[0].astype(BF16).astype(F32)
    bn = bnew_ref[0][:, 0:1]
    s_new = jnp.sum(q8.astype(F32) * jnp.where(first, kn[:, :DSA_HD], kn[:, DSA_HD:]), axis=-1, keepdims=True) + bn
    m_new = jnp.maximum(m_run, s_new)
    p_new = jnp.where(bn == 0.0, jnp.exp(s_new - m_new), 0.0)
    alpha = jnp.exp(m_run - m_new)
    acc = alpha * acc + p_new * jnp.where(first, vn[:, :DSA_HD], vn[:, DSA_HD:])
    y_ref[0] = acc / (alpha * l_run + p_new)


def dsa_sample_attend(page_table, q, bias, bnew, kn, vn, pool_k, pool_v, page_base, *, chunk):
    m, n_pages = page_table.shape
    wide = DSA_KV_HEADS * PAGE_SIZE
    assert n_pages % chunk == 0 and (n_pages // chunk) % 2 == 0
    per_b = lambda shape: pl.BlockSpec((1,) + shape, lambda b, pt: (b, 0, 0))
    hbm = pl.BlockSpec(memory_space=pl.ANY)
    return pl.pallas_call(
        functools.partial(_dsa_attend_sample_kernel, page_base=page_base, n_pages=n_pages, chunk=chunk),
        grid_spec=pltpu.PrefetchScalarGridSpec(
            num_scalar_prefetch=1,
            grid=(m,),
            in_specs=[per_b((DSA_HEADS, DSA_HD)), per_b((n_pages, wide)),
                      per_b((1, LANES)), per_b((1, DSA_KV_HEADS * DSA_HD)), per_b((1, DSA_KV_HEADS * DSA_HD)),
                      hbm, hbm],
            out_specs=per_b((DSA_HEADS, DSA_HD)),
            scratch_shapes=[pltpu.VMEM((2, chunk * wide, DSA_HD), F32), pltpu.VMEM((2, chunk * wide, DSA_HD), F32),
                            pltpu.SemaphoreType.DMA((2, 2))],
        ),
        out_shape=jax.ShapeDtypeStruct((m, DSA_HEADS, DSA_HD), F32),
        compiler_params=_cparams(("arbitrary",)),
        name="dsa_sample_attend",
    )(page_table, q, bias, bnew, kn, vn, pool_k, pool_v)


def _swa_step_kernel(q_ref, kt_ref, vt_ref, kn_ref, vn_ref, sink_ref, y_ref, *, bs):
    group = SWA_HEADS // SWA_KV_HEADS
    pairs = [(s, kv) for s in range(bs) for kv in range(SWA_KV_HEADS)]
    qs, scs, news = [], [], []
    for s, kv in pairs:
        qv = q_ref[s, kv * group:(kv + 1) * group, :]
        kn = kn_ref[s, kv:kv + 1, :].astype(BF16).astype(F32)
        scs.append(jnp.dot(qv, kt_ref[s, kv].astype(BF16), preferred_element_type=F32))
        news.append(jnp.sum(qv.astype(F32) * kn, axis=-1, keepdims=True))
    sc = jnp.concatenate(scs, axis=0)
    s_new = jnp.concatenate(news, axis=0)
    sink = jnp.concatenate([sink_ref[...]] * bs, axis=0)
    mx = jnp.maximum(jnp.maximum(jnp.max(sc, axis=-1, keepdims=True), s_new), sink)
    p = jnp.exp(sc - mx)
    p_new = jnp.exp(s_new - mx)
    inv = 1.0 / (jnp.sum(p, axis=-1, keepdims=True) + p_new + jnp.exp(sink - mx))
    pb = p.astype(BF16)
    for n, (s, kv) in enumerate(pairs):
        rows = slice(n * group, (n + 1) * group)
        vn = vn_ref[s, kv:kv + 1, :].astype(BF16).astype(F32)
        o = _dot_nt(pb[rows, :], vt_ref[s, kv].astype(BF16)) + p_new[rows, :] * vn
        y_ref[s, kv * group:(kv + 1) * group, :] = o * inv[rows, :]


def swa_sample(q, kt, vt, kn, vn, sinks, *, bs):
    m = q.shape[0]
    cache = pl.BlockSpec((bs, SWA_KV_HEADS, SWA_HD, WINDOW), lambda i: (i, 0, 0, 0))
    new = pl.BlockSpec((bs, SWA_KV_HEADS, SWA_HD), lambda i: (i, 0, 0))
    return pl.pallas_call(
        functools.partial(_swa_step_kernel, bs=bs),
        grid=(m // bs,),
        in_specs=[pl.BlockSpec((bs, SWA_HEADS, SWA_HD), lambda i: (i, 0, 0)), cache, cache, new, new,
                  pl.BlockSpec((SWA_HEADS, 1), lambda i: (0, 0))],
        out_specs=pl.BlockSpec((bs, SWA_HEADS, SWA_HD), lambda i: (i, 0, 0)),
        out_shape=jax.ShapeDtypeStruct((m, SWA_HEADS, SWA_HD), F32),
        compiler_params=_cparams(("parallel",)),
        name="swa_sample",
    )(q, kt, vt, kn, vn, sinks.reshape(SWA_HEADS, 1))


def _mem_step_kernel(q_ref, k_ref, v_ref, y_ref, *, bs):
    rows = q_ref.shape[1]
    n = N_MEM * MEM_HEADS
    hrow = lax.broadcasted_iota(jnp.int32, (rows, n), 0)
    col = lax.broadcasted_iota(jnp.int32, (rows, n), 1)
    own = (col % MEM_HEADS) == (hrow % MEM_HEADS)
    for s in range(bs):
        q = (q_ref[s] * (MEM_HD ** -0.5)).astype(BF16)
        sc = jnp.where(own, _dot_nt(q, k_ref[s * n:(s + 1) * n, :].astype(BF16)), NEG_INF)
        p = jnp.where(own, jnp.exp(sc - jnp.max(sc, axis=-1, keepdims=True)), 0.0)
        o = jnp.dot(p.astype(BF16), v_ref[s * n:(s + 1) * n, :].astype(BF16), preferred_element_type=F32)
        y_ref[s] = o / jnp.sum(p, axis=-1, keepdims=True)


def mem_attn_sample(q, mk, mv, layer, *, bs):
    m, rows, _ = q.shape
    n = N_MEM * MEM_HEADS
    nb = m // bs
    cache = pl.BlockSpec((bs * n, MEM_HD), lambda i: (layer * nb + i, 0))
    return pl.pallas_call(
        functools.partial(_mem_step_kernel, bs=bs),
        grid=(nb,),
        in_specs=[pl.BlockSpec((bs, rows, MEM_HD), lambda i: (i, 0, 0)), cache, cache],
        out_specs=pl.BlockSpec((bs, rows, MEM_HD), lambda i: (i, 0, 0)),
        out_shape=jax.ShapeDtypeStruct((m, rows, MEM_HD), F32),
        compiler_params=_cparams(("parallel",)),
        name="mem_attn_sample",
    )(q, mk, mv)


TM_FFN, TF_FFN, SUB_FFN = 1024, 512, 512
TM_PROJ, TN_AB, TN_MEM = 1024, 640, 512
TM_OUT = 512
TC_GLA, NSEQ_GLA = 256, 2
TQ_MEM = 512
BS_SAMPLE = 8
DEPTH_SCORES = 2
PG_ATTEND = 16


def _prep_w_in_ab(w):
    sizes = (GLA_HEADS * GLA_DK, GLA_HEADS * GLA_DK, GLA_HEADS * GLA_DV, GLA_HEADS * GLA_DV, GLA_GATE_RANK,
             DSA_HEADS * DSA_HD, DSA_KV_HEADS * DSA_HD, DSA_KV_HEADS * DSA_HD, IDX_HEADS * IDX_HD, IDX_HEADS, IDX_HD)
    offs = np.cumsum((0,) + sizes)
    gq, gk, gv, gr, gd, dq, dk, dv, iq, iw, ik = [w[:, int(offs[j]):int(offs[j + 1])] for j in range(len(sizes))]
    pad = lambda n: jnp.zeros((w.shape[0], n), w.dtype)
    small = jnp.concatenate([ik, gd, iw, pad(LANES - IDX_HD - GLA_GATE_RANK - IDX_HEADS)], axis=1)
    w_gla = jnp.concatenate([gv, gr, gq, gk, small], axis=1)
    w_dsa = jnp.concatenate([dq, dk, dv, iq, small], axis=1)
    return w_gla.astype(BF16), w_dsa.astype(BF16)


def _prep_gate(w_up):
    return jnp.zeros((LANES, GLA_HEADS * GLA_DK), F32).at[SM_GD:SM_GD + GLA_GATE_RANK].set(w_up)


def kernel(x_prompt, x_sample, mem_prompt, cache_dsa_k, cache_dsa_v, cache_dsa_idx_k, state_gla, cache_swa_k, cache_swa_v, cache_mem_k, cache_mem_v, page_table, norm_ffn, w_ffn_gate, w_ffn_up, w_ffn_down, norm_mix, w_in_ab, w_gla_gate_up, b_gla_gate, gla_out_norm, idx_k_norm, w_out_ab, w_in_c, swa_sinks, w_out_c, norm_mem_q, norm_mem_src, w_mem_q, w_mem_kv, w_mem_o, final_norm):
    depth = norm_mix.shape[0]
    bp, seq, _ = x_prompt.shape
    bs = x_sample.shape[0]
    n_pool = cache_dsa_k.shape[1]
    gla_w = GLA_HEADS * GLA_DV

    w_ab = [_prep_w_in_ab(w_in_ab[i]) for i in range(w_in_ab.shape[0])]
    w_gate = [_prep_gate(w_gla_gate_up[i]) for i in range(w_in_ab.shape[0])]
    w_oab = w_out_ab.astype(BF16)
    w_c, w_oc = w_in_c.astype(BF16), w_out_c.astype(BF16)
    w_mq, w_mkv, w_mo = w_mem_q.astype(BF16), w_mem_kv.astype(BF16), w_mem_o.astype(BF16)

    def rope_tabs(pos):
        return _rope_tables(pos, DSA_HD) + _rope_tables(pos, IDX_HD)

    ffn_bf16 = {}

    def ffn_pair(x, layer, half, last, sample):
        fin = final_norm if last else None
        if sample:
            y, *ffn_bf16[layer, half] = ffn(x, norm_ffn[layer, half], w_ffn_gate, w_ffn_up, w_ffn_down, (layer, half),
                                            fin, tm=TM_FFN, tf=TF_FFN, sub=SUB_FFN)
            return y
        return ffn(x, norm_ffn[layer, half], *ffn_bf16[layer, half], None, fin, tm=TM_FFN, tf=TF_FFN, sub=SUB_FFN)

    def prompt_group():
        tabs_p = rope_tabs(jnp.arange(seq))
        x = x_prompt.reshape(bp * seq, D_MODEL)
        mem = mem_prompt.reshape(bp * N_MEM, D_MODEL)
        mem_kv = [norm_proj(mem, norm_mem_src[l], w_mkv[l], tm=TM_PROJ, tn=TN_MEM) for l in range(depth)]
        st_ab_p, st_c_p = [], []
        for l in range(depth):
            i = l // 2
            x = ffn_pair(x, l, 0, False, False)
            if l % 2 == 0:
                z = norm_proj(x, norm_mix[l], w_ab[i][0], tm=TM_PROJ, tn=TN_AB)
                q, k, kb, v, vb, iq, ik, ikb, iw = d_proj_features(x, norm_mix[l], w_ab[i][1], tabs_p, idx_k_norm[i],
                                                                   seq=seq, tm=TM_PROJ, sub=SUB_FFN)
                y_gla, st_t = gla_prompt(z, w_gate[i], b_gla_gate[i].reshape(1, -1), gla_out_norm[i].reshape(1, -1),
                                         batch=bp, seq=seq, tc=TC_GLA, n_seq=NSEQ_GLA)
                y_dsa = dsa_prompt(q, iq, iw, kb, vb, ikb, batch=bp, seq=seq)
                mix_out = ([y_gla, y_dsa], [w_oab[i, :gla_w], w_oab[i, gla_w:]])
                n_pg = seq // PAGE_SIZE
                st_ab_p.append((k.reshape(bp, n_pg, PAGE_SIZE, DSA_KV_HEADS, DSA_HD),
                                v.reshape(bp, n_pg, PAGE_SIZE, DSA_KV_HEADS, DSA_HD),
                                ik.reshape(bp, n_pg, PAGE_SIZE, IDX_HD),
                                jnp.swapaxes(st_t, 2, 3)))
            else:
                q, k, kb, kbs, v, vb, vbs = c_proj_features(x, norm_mix[l], w_c[i], tabs_p[2:], seq=seq,
                                                            tm=TM_PROJ, sub=SUB_FFN)
                y = swa_prompt(q, kb, kbs, vb, vbs, swa_sinks[i], batch=bp, seq=seq)
                mix_out = ([y], [w_oc[i]])
                st_c_p.append((k.reshape(bp, seq, SWA_KV_HEADS, SWA_HD)[:, -WINDOW:],
                               v.reshape(bp, seq, SWA_KV_HEADS, SWA_HD)[:, -WINDOW:]))
            x = mem_attn_prompt(x, *mix_out, norm_mem_q[l], w_mq[l], mem_kv[l], w_mo[l], batch=bp, seq=seq, tq=TQ_MEM)
            x = ffn_pair(x, l, 1, l == depth - 1, False)
        return x.reshape(bp, seq, D_MODEL), st_ab_p, st_c_p, mem_kv

    tabs_s = rope_tabs(PAST_LEN + jnp.arange(1))
    n_pages = page_table.shape[1]
    topk = min(DSA_TOPK_MAX, (PAST_LEN + 1) // 4)
    pool_k = cache_dsa_k.reshape(-1, DSA_HD)
    pool_v = cache_dsa_v.reshape(-1, DSA_HD)
    pool_ik_t = jnp.swapaxes(cache_dsa_idx_k, 2, 3).reshape(-1, IDX_HD, PAGE_SIZE)
    mem_k_rows = cache_mem_k.reshape(-1, MEM_HD)
    mem_v_rows = cache_mem_v.reshape(-1, MEM_HD)
    x = x_sample.reshape(bs, D_MODEL)
    st_ab_s, st_c_s = [], []
    for l in range(depth):
        i = l // 2
        x = ffn_pair(x, l, 0, False, True)
        if l % 2 == 0:
            z = norm_proj(x, norm_mix[l], w_ab[i][0], tm=TM_PROJ, tn=TN_AB)
            q, k, kb, v_new, vb, iq, ik, ikb, iw = d_proj_features(x, norm_mix[l], w_ab[i][1], tabs_s, idx_k_norm[i],
                                                                   seq=1, tm=TM_PROJ, sub=SUB_FFN)
            p = gla_sample_prep(z, w_gate[i], b_gla_gate[i].reshape(1, -1)).reshape(bs, 16, GLA_DK)
            st_new, y_gla = gla_sample(p, z, state_gla[i], gla_out_norm[i].reshape(1, -1), bs=BS_SAMPLE)
            scores = dsa_sample_scores(page_table, iq.reshape(bs, IDX_HEADS, IDX_HD),
                                       iw[:, :IDX_HEADS].reshape(bs, IDX_HEADS, 1), pool_ik_t, i * n_pool, depth=DEPTH_SCORES)
            bias, bnew = dsa_sample_select(scores.reshape(bs, n_pages * PAGE_SIZE), iq, iw, ikb, topk=topk)
            y_dsa = dsa_sample_attend(
                page_table, q.reshape(bs, DSA_HEADS, DSA_HD), bias.reshape(bs, n_pages, DSA_KV_HEADS * PAGE_SIZE),
                bnew.reshape(bs, 1, LANES), k.reshape(bs, 1, 256), v_new.reshape(bs, 1, 256),
                pool_k, pool_v, i * n_pool, chunk=PG_ATTEND).reshape(bs, DSA_HEADS * DSA_HD)
            x = out_proj(x, [y_gla, y_dsa], [w_oab[i, :gla_w], w_oab[i, gla_w:]], tm=TM_OUT)
            st_ab_s.append((k.reshape(bs, 1, DSA_KV_HEADS, DSA_HD), v_new.reshape(bs, 1, DSA_KV_HEADS, DSA_HD),
                            ik.reshape(bs, 1, IDX_HD), st_new))
        else:
            q, k, kb, kbs, v_new, vb, vbs = c_proj_features(x, norm_mix[l], w_c[i], tabs_s[2:], seq=1,
                                                            tm=TM_PROJ, sub=SUB_FFN)
            y = swa_sample(q.reshape(bs, SWA_HEADS, SWA_HD), jnp.transpose(cache_swa_k[i], (0, 2, 3, 1)),
                           jnp.transpose(cache_swa_v[i], (0, 2, 3, 1)), k.reshape(bs, SWA_KV_HEADS, SWA_HD),
                           v_new.reshape(bs, SWA_KV_HEADS, SWA_HD), swa_sinks[i], bs=BS_SAMPLE)
            x = out_proj(x, [y.reshape(bs, SWA_HEADS * SWA_HD)], [w_oc[i]], tm=TM_OUT)
            st_c_s.append((jnp.concatenate([cache_swa_k[i][:, 1:], k.reshape(bs, 1, SWA_KV_HEADS, SWA_HD)], axis=1),
                           jnp.concatenate([cache_swa_v[i][:, 1:], v_new.reshape(bs, 1, SWA_KV_HEADS, SWA_HD)], axis=1)))
        qm = norm_proj(x, norm_mem_q[l], w_mq[l], tm=TM_PROJ, tn=TN_MEM).reshape(bs, MEM_HEADS, MEM_HD)
        qm = jnp.pad(qm, ((0, 0), (0, SUBLANES - MEM_HEADS), (0, 0)))
        om = mem_attn_sample(qm, mem_k_rows, mem_v_rows, l, bs=BS_SAMPLE)
        x = out_proj(x, [om[:, :MEM_HEADS].reshape(bs, MEM_HEADS * MEM_HD)], [w_mo[l]], tm=TM_OUT)
        x = ffn_pair(x, l, 1, l == depth - 1, True)
    y_sample = x.reshape(bs, 1, D_MODEL)

    y_prompt, st_ab_p, st_c_p, mem_kv = prompt_group()

    stk = lambda sts, j: jnp.stack([s[j] for s in sts])
    mw = MEM_HEADS * MEM_HD
    mem_k_p = jnp.stack([kv[:, :mw].reshape(bp, N_MEM, MEM_HEADS, MEM_HD) for kv in mem_kv])
    mem_v_p = jnp.stack([kv[:, mw:].reshape(bp, N_MEM, MEM_HEADS, MEM_HD) for kv in mem_kv])
    return (y_prompt, y_sample, stk(st_ab_p, 0), stk(st_ab_p, 1), stk(st_ab_p, 2),
            stk(st_ab_s, 0), stk(st_ab_s, 1), stk(st_ab_s, 2), stk(st_ab_p, 3), stk(st_ab_s, 3),
            stk(st_c_p, 0), stk(st_c_p, 1), stk(st_c_s, 0), stk(st_c_s, 1), mem_k_p, mem_v_p)
```

```python
import functools

import jax
import jax.numpy as jnp
import numpy as np
from jax import lax
from jax.experimental import pallas as pl
from jax.experimental.pallas import tpu as pltpu

F32 = jnp.float32
BF16 = jnp.bfloat16

D_MODEL = 2048
D_FF = 5632
EPS = 1e-6
ROPE_THETA = 10000.0
NEG_INF = -1e30
PAST_LEN = 8192
PAGE_SIZE = 128
Q_BLOCK = 128
GLA_HEADS, GLA_DK, GLA_DV = 4, 128, 256
GLA_GATE_RANK = 16
GLA_GATE_TAU = 16.0
GLA_CHUNK = 64
DSA_HEADS, DSA_KV_HEADS, DSA_HD = 8, 2, 128
IDX_HEADS, IDX_HD = 8, 64
DSA_TOPK_MAX = 256
SWA_HEADS, SWA_KV_HEADS, SWA_HD = 32, 4, 64
WINDOW = 128
MEM_HEADS, MEM_HD = 4, 128
N_MEM = 256

LANES = 128
SUBLANES = 8
VMEM_LIMIT_BYTES = 60000 * 1024
ELEM_SUB = 128

AB_GV, AB_GR, AB_GQ, AB_GK = 0, 1024, 2048, 2560
AB_SMALL = 3072
AB_WIDTH = 3200
D_DQ, D_DK, D_DV, D_IQ, D_SMALL, D_WIDTH = 0, 1024, 1280, 1536, 2048, 2176
SM_IK, SM_GD, SM_IW = 0, 64, 80
C_Q, C_K, C_V, C_WIDTH = 0, 2048, 2304, 2560


def _cparams(sem):
    return pltpu.CompilerParams(dimension_semantics=sem, vmem_limit_bytes=VMEM_LIMIT_BYTES)


def _rms(x, g):
    y = x * lax.rsqrt(jnp.mean(x * x, axis=-1, keepdims=True) + EPS)
    return y * g


def _ffn_kernel(x_ref, g_ref, wg_ref, wu_ref, wd_ref, fg_ref, o_ref, *rest, final_norm, sub, emit_cast):
    h_ref = rest[-1]
    j = pl.program_id(1)
    tm = x_ref.shape[0]
    if emit_cast:
        for src, dst in zip((wg_ref, wu_ref, wd_ref), rest[:3]):
            dst[...] = src[...].astype(BF16)
        wg_ref, wu_ref, wd_ref = rest[:3]

    esub = min(ELEM_SUB, tm)

    def row_group(r):
        return pl.ds(pl.multiple_of(r * esub, esub), esub)

    @pl.when(j == 0)
    def _():
        def body(r, carry):
            rows = row_group(r)
            h_ref[rows, :] = _rms(x_ref[rows, :], g_ref[...]).astype(BF16)
            o_ref[rows, :] = jnp.zeros((esub, D_MODEL), F32)
            return carry

        lax.fori_loop(0, tm // esub, body, 0)

    for r in range(tm // sub):
        rows = slice(r * sub, (r + 1) * sub)
        h = h_ref[rows, :]
        a = jnp.dot(h, wg_ref[...], preferred_element_type=F32)
        u = jnp.dot(h, wu_ref[...], preferred_element_type=F32)
        act = (a * jax.nn.sigmoid(a) * u).astype(BF16)
        o_ref[rows, :] += jnp.dot(act, wd_ref[...], preferred_element_type=F32)

    @pl.when(j == pl.num_programs(1) - 1)
    def _():
        def body(r, carry):
            rows = row_group(r)
            y = x_ref[rows, :] + 0.5 * o_ref[rows, :]
            if final_norm:
                y = _rms(y, fg_ref[...])
            o_ref[rows, :] = y
            return carry

        lax.fori_loop(0, tm // esub, body, 0)


def ffn(x, g, wg, wu, wd, sel, final_g=None, *, tm, tf, sub):
    m = x.shape[0]
    tm = min(tm, m)
    sub = min(sub, tm)
    fg = g if final_g is None else final_g
    emit_cast = sel is not None
    wspec = lambda shape, imap: pl.BlockSpec(shape, imap)
    if emit_cast:
        assert m == tm and wg.dtype == F32
        layer, half = sel
        w_in = [wspec((None, None, D_MODEL, tf), lambda i, j: (layer, half, 0, j)),
                wspec((None, None, D_MODEL, tf), lambda i, j: (layer, half, 0, j)),
                wspec((None, None, tf, D_MODEL), lambda i, j: (layer, half, j, 0))]
    else:
        assert wg.dtype == BF16 and wg.ndim == 2
        w_in = [wspec((D_MODEL, tf), lambda i, j: (0, j)), wspec((D_MODEL, tf), lambda i, j: (0, j)),
                wspec((tf, D_MODEL), lambda i, j: (j, 0))]
    out_specs = [pl.BlockSpec((tm, D_MODEL), lambda i, j: (i, 0))]
    out_shape = [jax.ShapeDtypeStruct((m, D_MODEL), F32)]
    if emit_cast:
        out_specs += [wspec((D_MODEL, tf), lambda i, j: (0, j)), wspec((D_MODEL, tf), lambda i, j: (0, j)),
                      wspec((tf, D_MODEL), lambda i, j: (j, 0))]
        out_shape += [jax.ShapeDtypeStruct((D_MODEL, D_FF), BF16), jax.ShapeDtypeStruct((D_MODEL, D_FF), BF16),
                      jax.ShapeDtypeStruct((D_FF, D_MODEL), BF16)]
    outs = pl.pallas_call(
        functools.partial(_ffn_kernel, final_norm=final_g is not None, sub=sub, emit_cast=emit_cast),
        grid=(m // tm, D_FF // tf),
        in_specs=[pl.BlockSpec((tm, D_MODEL), lambda i, j: (i, 0)), pl.BlockSpec((1, D_MODEL), lambda i, j: (0, 0))]
                 + w_in + [pl.BlockSpec((1, D_MODEL), lambda i, j: (0, 0))],
        out_specs=out_specs,
        out_shape=out_shape,
        scratch_shapes=[pltpu.VMEM((tm, D_MODEL), BF16)],
        compiler_params=_cparams(("parallel", "arbitrary")),
        name="ffn_cast" if emit_cast else "ffn",
    )(x, g.reshape(1, D_MODEL), wg, wu, wd, fg.reshape(1, D_MODEL))
    return tuple(outs) if emit_cast else outs[0]


def _proj_kernel(x_ref, g_ref, w_ref, o_ref, h_ref, *, sub):
    tm = x_ref.shape[0]

    @pl.when(pl.program_id(1) == 0)
    def _():
        for r in range(tm // sub):
            rows = slice(r * sub, (r + 1) * sub)
            h_ref[rows, :] = _rms(x_ref[rows, :], g_ref[...]).astype(BF16)

    o_ref[...] = jnp.dot(h_ref[...], w_ref[...], preferred_element_type=F32)


def norm_proj(x, g, w, *, tm, tn, sub=512):
    m, n = x.shape[0], w.shape[1]
    tm = min(tm, m)
    w_spec = (pl.BlockSpec((D_MODEL, tn), lambda i, j: (0, j), pipeline_mode=pl.Buffered(1)) if tn == n
              else pl.BlockSpec((D_MODEL, tn), lambda i, j: (0, j)))
    return pl.pallas_call(
        functools.partial(_proj_kernel, sub=min(sub, tm)),
        grid=(m // tm, n // tn),
        in_specs=[
            pl.BlockSpec((tm, D_MODEL), lambda i, j: (i, 0)),
            pl.BlockSpec((1, D_MODEL), lambda i, j: (0, 0)),
            w_spec,
        ],
        out_specs=pl.BlockSpec((tm, tn), lambda i, j: (i, j)),
        out_shape=jax.ShapeDtypeStruct((m, n), F32),
        scratch_shapes=[pltpu.VMEM((tm, D_MODEL), BF16)],
        compiler_params=_cparams(("parallel", "arbitrary")),
        name="norm_proj",
    )(x, g.reshape(1, D_MODEL), w)


def _outproj_kernel(*refs, n_in):
    x_ref = refs[0]
    y_refs = refs[1:1 + n_in]
    w_refs = refs[1 + n_in:1 + 2 * n_in]
    o_ref = refs[1 + 2 * n_in]
    acc = x_ref[...]
    for y_ref, w_ref in zip(y_refs, w_refs):
        acc = acc + jnp.dot(y_ref[...].astype(BF16), w_ref[...], preferred_element_type=F32)
    o_ref[...] = acc


def out_proj(x, ys, ws, *, tm):
    m = x.shape[0]
    tm = min(tm, m)
    n_in = len(ys)
    in_specs = [pl.BlockSpec((tm, D_MODEL), lambda i: (i, 0))]
    in_specs += [pl.BlockSpec((tm, y.shape[1]), lambda i: (i, 0)) for y in ys]
    in_specs += [pl.BlockSpec(w.shape, lambda i: (0, 0)) for w in ws]
    return pl.pallas_call(
        functools.partial(_outproj_kernel, n_in=n_in),
        grid=(m // tm,),
        in_specs=in_specs,
        out_specs=pl.BlockSpec((tm, D_MODEL), lambda i: (i, 0)),
        out_shape=jax.ShapeDtypeStruct((m, D_MODEL), F32),
        compiler_params=_cparams(("parallel",)),
        name="out_proj",
    )(x, *ys, *ws)


def _rope_tables(pos, hd):
    half = hd // 2
    inv = ROPE_THETA ** (-jnp.arange(half, dtype=F32) / half)
    ang = pos.astype(F32)[:, None] * inv[None, :]
    cos, sin = jnp.cos(ang), jnp.sin(ang)
    reps = LANES // hd
    return (jnp.concatenate([cos, cos] * reps, axis=-1),
            jnp.concatenate([-sin, sin] * reps, axis=-1))


def _rope128(x, cos, sin):
    return x * cos + pltpu.roll(x, 64, 1) * sin


def _rope64(x, cos, sin, lower):
    partner = jnp.where(lower, pltpu.roll(x, 96, 1), pltpu.roll(x, 32, 1))
    return x * cos + partner * sin


def _lower32_mask(rows):
    lane = lax.broadcasted_iota(jnp.int32, (rows, LANES), 1)
    return (lane % 64) < 32


def _d_proj_feat_kernel(x_ref, g_ref, w_ref, c128_ref, s128_ref, c64_ref, s64_ref, gik_ref,
                        q_ref, k_ref, kb_ref, v_ref, vb_ref, iqo_ref, ik_ref, ikb_ref, iw_ref, *, sub):
    tm = x_ref.shape[0]
    table_rows = c128_ref.shape[0]
    lower = _lower32_mask(sub)
    lane = lax.broadcasted_iota(jnp.int32, (sub, LANES), 1)
    for r in range(tm // sub):
        rows = slice(r * sub, (r + 1) * sub)
        trows = rows if table_rows > 1 else slice(0, 1)
        c128, s128 = c128_ref[trows, :], s128_ref[trows, :]
        c64, s64 = c64_ref[trows, :], s64_ref[trows, :]
        h = _rms(x_ref[rows, :], g_ref[...]).astype(BF16)
        z = jnp.dot(h, w_ref[...], preferred_element_type=F32)
        for hd in range(DSA_HEADS):
            sl = slice(hd * LANES, (hd + 1) * LANES)
            q_ref[rows, sl] = (_rope128(z[:, D_DQ + hd * LANES:D_DQ + (hd + 1) * LANES], c128, s128)
                               * (DSA_HD ** -0.5)).astype(BF16)
        for hd in range(DSA_KV_HEADS):
            sl = slice(hd * LANES, (hd + 1) * LANES)
            kr = _rope128(z[:, D_DK + hd * LANES:D_DK + (hd + 1) * LANES], c128, s128)
            v = z[:, D_DV + hd * LANES:D_DV + (hd + 1) * LANES]
            k_ref[rows, sl] = kr
            kb_ref[rows, sl] = kr.astype(BF16)
            v_ref[rows, sl] = v
            vb_ref[rows, sl] = v.astype(BF16)
        for p in range(IDX_HEADS * IDX_HD // LANES):
            sl = slice(p * LANES, (p + 1) * LANES)
            iqo_ref[rows, sl] = (_rope64(z[:, D_IQ + p * LANES:D_IQ + (p + 1) * LANES], c64, s64, lower)
                                 * (IDX_HD ** -0.5)).astype(BF16)
        sm = z[:, D_SMALL:D_SMALL + LANES]
        ik = jnp.where(lane < IDX_HD, sm, 0.0)
        ik = ik * lax.rsqrt(jnp.sum(ik * ik, axis=-1, keepdims=True) / IDX_HD + EPS) * gik_ref[...]
        ik = _rope64(ik, c64, s64, lower)
        ik_ref[rows, :] = ik[:, :IDX_HD]
        ikb_ref[rows, :] = jnp.where(lane < IDX_HD, ik, pltpu.roll(ik, 64, 1)).astype(BF16)
        iw_ref[rows, :] = pltpu.roll(sm, LANES - SM_IW, 1) * (IDX_HEADS ** -0.5)


def d_proj_features(x, g, w, tabs, gik, *, seq, tm, sub):
    m = x.shape[0]
    tm = min(tm, m)
    sub = min(sub, tm)
    c128, s128, c64, s64 = tabs
    if c128.shape[0] == 1:
        tab_spec = pl.BlockSpec((1, LANES), lambda i: (0, 0))
    else:
        nt = seq // tm
        tab_spec = pl.BlockSpec((tm, LANES), lambda i: (i % nt, 0))
    row = lambda wd: pl.BlockSpec((tm, wd), lambda i: (i, 0))
    gik_pad = jnp.zeros((1, LANES), F32).at[0, :IDX_HD].set(gik)
    return pl.pallas_call(
        functools.partial(_d_proj_feat_kernel, sub=sub),
        grid=(m // tm,),
        in_specs=[row(D_MODEL), pl.BlockSpec((1, D_MODEL), lambda i: (0, 0)),
                  pl.BlockSpec((D_MODEL, D_WIDTH), lambda i: (0, 0), pipeline_mode=pl.Buffered(1)),
                  tab_spec, tab_spec, tab_spec, tab_spec, pl.BlockSpec((1, LANES), lambda i: (0, 0))],
        out_specs=[row(1024), row(256), row(256), row(256), row(256), row(512), row(IDX_HD), row(128), row(128)],
        out_shape=[jax.ShapeDtypeStruct((m, 1024), BF16),
                   jax.ShapeDtypeStruct((m, 256), F32),
                   jax.ShapeDtypeStruct((m, 256), BF16),
                   jax.ShapeDtypeStruct((m, 256), F32),
                   jax.ShapeDtypeStruct((m, 256), BF16),
                   jax.ShapeDtypeStruct((m, 512), BF16),
                   jax.ShapeDtypeStruct((m, IDX_HD), F32),
                   jax.ShapeDtypeStruct((m, 128), BF16),
                   jax.ShapeDtypeStruct((m, 128), F32)],
        compiler_params=_cparams(("parallel",)),
        name="d_proj_features",
    )(x, g.reshape(1, D_MODEL), w, c128, s128, c64, s64, gik_pad)


def _c_proj_feat_kernel(x_ref, g_ref, w_ref, c64_ref, s64_ref,
                        qo_ref, ko_ref, kb_ref, kbs_ref, vo_ref, vb_ref, vbs_ref, *, sub):
    tm = x_ref.shape[0]
    table_rows = c64_ref.shape[0]
    lower = _lower32_mask(sub)
    for r in range(tm // sub):
        rows = slice(r * sub, (r + 1) * sub)
        trows = rows if table_rows > 1 else slice(0, 1)
        c64, s64 = c64_ref[trows, :], s64_ref[trows, :]
        h = _rms(x_ref[rows, :], g_ref[...]).astype(BF16)
        z = jnp.dot(h, w_ref[...], preferred_element_type=F32)
        for p in range(SWA_HEADS * SWA_HD // LANES):
            sl = slice(p * LANES, (p + 1) * LANES)
            qo_ref[rows, sl] = (_rope64(z[:, C_Q + p * LANES:C_Q + (p + 1) * LANES], c64, s64, lower)
                                * (SWA_HD ** -0.5)).astype(BF16)
        for p in range(SWA_KV_HEADS * SWA_HD // LANES):
            sl = slice(p * LANES, (p + 1) * LANES)
            kr = _rope64(z[:, C_K + p * LANES:C_K + (p + 1) * LANES], c64, s64, lower)
            v = z[:, C_V + p * LANES:C_V + (p + 1) * LANES]
            ko_ref[rows, sl] = kr
            kb_ref[rows, sl] = kr.astype(BF16)
            kbs_ref[rows, sl] = pltpu.roll(kr, 64, 1).astype(BF16)
            vo_ref[rows, sl] = v
            vb_ref[rows, sl] = v.astype(BF16)
            vbs_ref[rows, sl] = pltpu.roll(v, 64, 1).astype(BF16)


def c_proj_features(x, g, w, tabs, *, seq, tm, sub):
    m = x.shape[0]
    tm = min(tm, m)
    sub = min(sub, tm)
    c64, s64 = tabs
    if c64.shape[0] == 1:
        tab_spec = pl.BlockSpec((1, LANES), lambda i: (0, 0))
    else:
        nt = seq // tm
        tab_spec = pl.BlockSpec((tm, LANES), lambda i: (i % nt, 0))
    row = lambda wd: pl.BlockSpec((tm, wd), lambda i: (i, 0))
    return pl.pallas_call(
        functools.partial(_c_proj_feat_kernel, sub=sub),
        grid=(m // tm,),
        in_specs=[row(D_MODEL), pl.BlockSpec((1, D_MODEL), lambda i: (0, 0)),
                  pl.BlockSpec((D_MODEL, C_WIDTH), lambda i: (0, 0), pipeline_mode=pl.Buffered(1)),
                  tab_spec, tab_spec],
        out_specs=[row(2048), row(256), row(256), row(256), row(256), row(256), row(256)],
        out_shape=[jax.ShapeDtypeStruct((m, 2048), BF16),
                   jax.ShapeDtypeStruct((m, 256), F32),
                   jax.ShapeDtypeStruct((m, 256), BF16),
                   jax.ShapeDtypeStruct((m, 256), BF16),
                   jax.ShapeDtypeStruct((m, 256), F32),
                   jax.ShapeDtypeStruct((m, 256), BF16),
                   jax.ShapeDtypeStruct((m, 256), BF16)],
        compiler_params=_cparams(("parallel",)),
        name="c_proj_features",
    )(x, g.reshape(1, D_MODEL), w, c64, s64)


def _dot_nt(a, b, **kw):
    return lax.dot_general(a, b, (((1,), (1,)), ((), ())), preferred_element_type=F32, **kw)


def _dot_tn(a, b, **kw):
    return lax.dot_general(a, b, (((0,), (0,)), ((), ())), preferred_element_type=F32, **kw)


_HI = lax.Precision.HIGHEST


def _log_decay(sm, wgate, bgate):
    pre = jnp.dot(sm, wgate, preferred_element_type=F32, precision=_HI) + bgate
    return (jnp.minimum(pre, 0.0) - jnp.log1p(jnp.exp(-jnp.abs(pre)))) / GLA_GATE_TAU


def _gla_gate_out(o, r, gain):
    g = o * lax.rsqrt(jnp.mean(o * o, axis=-1, keepdims=True) + EPS) * gain
    return g * (r * jax.nn.sigmoid(r))


def _gla_prompt_kernel(gv_ref, gr_ref, gq_ref, gk_ref, sm_ref, wgate_ref, bgate_ref, gain_ref,
                       y_ref, st_ref, s_ref, b_ref, *, n_seq, n_chunks):
    c = pl.program_id(1)

    @pl.when(c == 0)
    def _():
        s_ref[...] = jnp.zeros_like(s_ref)

    tc = n_chunks * GLA_CHUNK
    ri = lax.broadcasted_iota(jnp.int32, (tc, tc), 0)
    ci = lax.broadcasted_iota(jnp.int32, (tc, tc), 1)
    tril = jnp.where(jnp.logical_and(ri // GLA_CHUNK == ci // GLA_CHUNK, ri >= ci), 1.0, 0.0).astype(F32)
    causal = (lax.broadcasted_iota(jnp.int32, (GLA_CHUNK, GLA_CHUNK), 0)
              >= lax.broadcasted_iota(jnp.int32, (GLA_CHUNK, GLA_CHUNK), 1))
    gain = gain_ref[...]
    for s in range(n_seq):
        la = _log_decay(sm_ref[s], wgate_ref[...], bgate_ref[...])
        b_ref[s] = jnp.dot(tril, la, preferred_element_type=F32, precision=_HI)
    for n in range(n_chunks):
        rows = slice(n * GLA_CHUNK, (n + 1) * GLA_CHUNK)
        for s in range(n_seq):
            b = b_ref[s, rows, :]
            b_end = b_ref[s, (n + 1) * GLA_CHUNK - 1:(n + 1) * GLA_CHUNK, :]
            k = gk_ref[s, rows, :]
            q_in = gq_ref[s, rows, :] * (GLA_DK ** -0.5) * jnp.exp(b)
            k_in = k * jnp.exp(-b)
            k_end = k * jnp.exp(b_end - b)
            decay = jnp.exp(b_end)
            for h in range(GLA_HEADS):
                dk = slice(h * GLA_DK, (h + 1) * GLA_DK)
                dv = slice(h * GLA_DV, (h + 1) * GLA_DV)
                v = gv_ref[s, rows, dv].astype(BF16)
                st = s_ref[s, h]
                qh = q_in[:, dk].astype(BF16)
                att = jnp.where(causal, _dot_nt(qh, k_in[:, dk].astype(BF16)), 0.0)
                o = _dot_nt(qh, st.astype(BF16)) + jnp.dot(att.astype(BF16), v, preferred_element_type=F32)
                s_ref[s, h] = st * decay[:, dk] + _dot_tn(v, k_end[:, dk].astype(BF16))
                y_ref[s, rows, dv] = _gla_gate_out(o, gr_ref[s, rows, dv], gain)

    @pl.when(c == pl.num_programs(1) - 1)
    def _():
        st_ref[...] = s_ref[...]


def gla_prompt(z, wgate, bgate, gain, *, batch, seq, tc, n_seq):
    m = z.shape[0]
    z3 = z.reshape(batch, seq, z.shape[1])
    col = lambda w, off: pl.BlockSpec((n_seq, tc, w), lambda b, c: (b, c, off // w))
    const = lambda shape: pl.BlockSpec(shape, lambda b, c: (0,) * len(shape))
    y, st = pl.pallas_call(
        functools.partial(_gla_prompt_kernel, n_seq=n_seq, n_chunks=tc // GLA_CHUNK),
        grid=(batch // n_seq, seq // tc),
        in_specs=[col(1024, AB_GV), col(1024, AB_GR), col(512, AB_GQ), col(512, AB_GK), col(128, AB_SMALL),
                  const((LANES, GLA_HEADS * GLA_DK)), const((1, GLA_HEADS * GLA_DK)), const((1, GLA_DV))],
        out_specs=[pl.BlockSpec((n_seq, tc, GLA_HEADS * GLA_DV), lambda b, c: (b, c, 0)),
                   pl.BlockSpec((n_seq, GLA_HEADS, GLA_DV, GLA_DK), lambda b, c: (b, 0, 0, 0))],
        out_shape=[jax.ShapeDtypeStruct((batch, seq, GLA_HEADS * GLA_DV), F32),
                   jax.ShapeDtypeStruct((batch, GLA_HEADS, GLA_DV, GLA_DK), F32)],
        scratch_shapes=[pltpu.VMEM((n_seq, GLA_HEADS, GLA_DV, GLA_DK), F32),
                        pltpu.VMEM((n_seq, tc, GLA_HEADS * GLA_DK), F32)],
        compiler_params=_cparams(("parallel", "arbitrary")),
        name="gla_prompt",
    )(z3, z3, z3, z3, z3, wgate, bgate, gain)
    return y.reshape(m, GLA_HEADS * GLA_DV), st


INT_MIN = -2 ** 31


def _order_key(score):
    score = jnp.where(score == 0.0, 0.0, score)
    bits = lax.bitcast_convert_type(score, jnp.int32)
    return jnp.where(bits < 0, bits ^ jnp.int32(0x7FFFFFFF), bits)


def _lane_total(x):
    return jnp.dot(x.astype(BF16), jnp.ones((LANES, LANES), BF16), preferred_element_type=F32)


ROW_SUB = 128


def _count_blocks(key_ref, rows, n_blocks, pred):
    acc = jnp.zeros((ROW_SUB, LANES), F32)
    for c in range(n_blocks):
        acc = acc + jnp.where(pred(key_ref[rows, c * LANES:(c + 1) * LANES]), 1.0, 0.0)
    return acc


def _masked_key():
    bits = int(np.array(NEG_INF, np.float32).view(np.int32))
    return bits ^ 0x7FFFFFFF


def _kth_largest_key(key_ref, t_ref, n_blocks, k, active):
    n_sub = key_ref.shape[0] // ROW_SUB
    t_ref[...] = jnp.full(t_ref.shape, INT_MIN, jnp.int32)

    def body(it, carry):
        step = lax.shift_left(jnp.int32(1), 31 - it)
        cands, accs = [], []
        for rb in range(n_sub):
            rows = slice(rb * ROW_SUB, (rb + 1) * ROW_SUB)
            cand = t_ref[rows, :] + step
            cands.append(cand)
            accs.append(_count_blocks(key_ref, rows, active(rb), lambda kc, cand=cand: kc >= cand))
        total = _lane_total(jnp.concatenate(accs, axis=0))
        for rb in range(n_sub):
            rows = slice(rb * ROW_SUB, (rb + 1) * ROW_SUB)
            skipped = float((n_blocks - active(rb)) * LANES)
            count = total[rows, :] + jnp.where(cands[rb] <= _masked_key(), skipped, 0.0)
            t_ref[rows, :] = jnp.where(count >= k, cands[rb], t_ref[rows, :])
        return carry

    lax.fori_loop(0, 32, body, 0)


def _select_topk(key_ref, t_ref, n_blocks, k, write_fn, active=None):
    if active is None:
        active = lambda rb: n_blocks
    _kth_largest_key(key_ref, t_ref, n_blocks, k, active)
    ri = lax.broadcasted_iota(jnp.int32, (LANES, LANES), 0)
    ci = lax.broadcasted_iota(jnp.int32, (LANES, LANES), 1)
    before = jnp.where(ri < ci, 1.0, 0.0).astype(BF16)
    for rb in range(key_ref.shape[0] // ROW_SUB):
        rows = slice(rb * ROW_SUB, (rb + 1) * ROW_SUB)
        t = t_ref[rows, :]
        skipped = float((n_blocks - active(rb)) * LANES)
        above = _lane_total(_count_blocks(key_ref, rows, active(rb), lambda kc: kc > t))
        need = k - above - jnp.where(t < _masked_key(), skipped, 0.0)
        run = jnp.zeros((ROW_SUB, LANES), F32)
        for c in range(n_blocks):
            if c >= active(rb):
                write_fn(rows, c, jnp.zeros((ROW_SUB, LANES), jnp.bool_))
                continue
            kc = key_ref[rows, c * LANES:(c + 1) * LANES]
            eq = jnp.where(kc == t, 1.0, 0.0)
            rank = jnp.dot(eq.astype(BF16), before, preferred_element_type=F32) + run
            take = jnp.where(kc > t, 1.0, jnp.where(rank < need, eq, 0.0))
            write_fn(rows, c, take > 0.0)
            run = run + _lane_total(eq)


def _half_mask(rows, upper):
    lane = lax.broadcasted_iota(jnp.int32, (rows, LANES), 1)
    return (lane >= 64) if upper else (lane < 64)


DSA_STRATUM = 512
KEY_CHUNK = 512


def _dsa_select_prompt_kernel(iq_ref, iw_ref, ikb_ref, bias_ref, key_ref, t_ref, *, row0, n_keys, topk):
    rows = iq_ref.shape[0]
    n_blocks = n_keys // LANES
    lane = lax.broadcasted_iota(jnp.int32, (ROW_SUB, LANES), 1)
    sub = lax.broadcasted_iota(jnp.int32, (ROW_SUB, LANES), 0)

    iw = iw_ref[...]
    qh = []
    for h in range(IDX_HEADS):
        pair = iq_ref[:, (h // 2) * LANES:(h // 2 + 1) * LANES]
        qh.append(jnp.where(_half_mask(rows, h % 2 == 1), pair, jnp.zeros_like(pair)))
    for kc in range(n_keys // KEY_CHUNK):
        ik = ikb_ref[kc * KEY_CHUNK:(kc + 1) * KEY_CHUNK, :]
        score = jnp.zeros((rows, KEY_CHUNK), F32)
        for h in range(IDX_HEADS):
            score = score + jnp.maximum(_dot_nt(qh[h], ik), 0.0) * iw[:, h:h + 1]
        for rb in range(rows // ROW_SUB):
            for cb in range(KEY_CHUNK // LANES):
                c = kc * (KEY_CHUNK // LANES) + cb
                causal = (c * LANES + lane) <= (row0 + rb * ROW_SUB + sub)
                part = score[rb * ROW_SUB:(rb + 1) * ROW_SUB, cb * LANES:(cb + 1) * LANES]
                key_ref[rb * ROW_SUB:(rb + 1) * ROW_SUB, c * LANES:(c + 1) * LANES] = _order_key(
                    jnp.where(causal, part, NEG_INF))

    def write(rws, c, sel):
        causal = (c * LANES + lane) <= (row0 + rws.start + sub)
        bias_ref[rws, c * LANES:(c + 1) * LANES] = jnp.where(jnp.logical_and(sel, causal), 0.0, NEG_INF)

    _select_topk(key_ref, t_ref, n_blocks, topk, write, active=lambda rb: row0 // LANES + rb + 1)


def _dsa_attend_prompt_kernel(q_ref, bias_ref, kb_ref, vb_ref, y_ref):
    bias = bias_ref[...]
    group = DSA_HEADS // DSA_KV_HEADS
    for kv in range(DSA_KV_HEADS):
        kvs = slice(kv * DSA_HD, (kv + 1) * DSA_HD)
        q4 = jnp.concatenate([q_ref[:, (kv * group + g) * DSA_HD:(kv * group + g + 1) * DSA_HD] for g in range(group)],
                             axis=0)
        s4 = _dot_nt(q4, kb_ref[:, kvs])
        ps, dens = [], []
        for g in range(group):
            s = s4[g * Q_BLOCK:(g + 1) * Q_BLOCK, :] + bias
            p = jnp.exp(s - jnp.max(s, axis=-1, keepdims=True))
            dens.append(jnp.sum(p, axis=-1, keepdims=True))
            ps.append(p.astype(BF16))
        o4 = jnp.dot(jnp.concatenate(ps, axis=0), vb_ref[:, kvs], preferred_element_type=F32)
        for g in range(group):
            h = kv * group + g
            y_ref[:, h * DSA_HD:(h + 1) * DSA_HD] = o4[g * Q_BLOCK:(g + 1) * Q_BLOCK, :] / dens[g]


def dsa_prompt(q, iq, iw, kb, vb, ikb, *, batch, seq):
    topk = min(DSA_TOPK_MAX, seq // 4)
    as3 = lambda a: a.reshape(batch, seq, a.shape[-1])
    q, iq, iw, kb, vb, ikb = (as3(a) for a in (q, iq, iw, kb, vb, ikb))
    nsub = DSA_STRATUM // Q_BLOCK
    outs = []
    for r in range(seq // DSA_STRATUM):
        n_keys = (r + 1) * DSA_STRATUM
        strat = lambda w: pl.BlockSpec((None, DSA_STRATUM, w), lambda b: (b, r, 0))
        keys = lambda w: pl.BlockSpec((None, n_keys, w), lambda b: (b, 0, 0))
        bias = pl.pallas_call(
            functools.partial(_dsa_select_prompt_kernel, row0=r * DSA_STRATUM, n_keys=n_keys, topk=topk),
            grid=(batch,),
            in_specs=[strat(512), strat(128), keys(128)],
            out_specs=pl.BlockSpec((None, DSA_STRATUM, n_keys), lambda b: (b, 0, 0)),
            out_shape=jax.ShapeDtypeStruct((batch, DSA_STRATUM, n_keys), F32),
            scratch_shapes=[pltpu.VMEM((DSA_STRATUM, n_keys), jnp.int32), pltpu.VMEM((DSA_STRATUM, LANES), jnp.int32)],
            compiler_params=_cparams(("parallel",)),
            name="dsa_select_prompt",
        )(iq, iw, ikb)
        qrow = lambda w: pl.BlockSpec((None, Q_BLOCK, w), lambda b, i: (b, r * nsub + i, 0))
        keys2 = lambda w: pl.BlockSpec((None, n_keys, w), lambda b, i: (b, 0, 0))
        outs.append(pl.pallas_call(
            _dsa_attend_prompt_kernel,
            grid=(batch, nsub),
            in_specs=[qrow(1024), pl.BlockSpec((None, Q_BLOCK, n_keys), lambda b, i: (b, i, 0)), keys2(256), keys2(256)],
            out_specs=pl.BlockSpec((None, Q_BLOCK, DSA_HEADS * DSA_HD), lambda b, i: (b, i, 0)),
            out_shape=jax.ShapeDtypeStruct((batch, DSA_STRATUM, DSA_HEADS * DSA_HD), F32),
            compiler_params=_cparams(("parallel", "arbitrary")),
            name="dsa_attend_prompt",
        )(q, bias, kb, vb))
    return jnp.concatenate(outs, axis=1).reshape(batch * seq, DSA_HEADS * DSA_HD)


def _swa_head_plan(h):
    group = SWA_HEADS // SWA_KV_HEADS
    kv = h // group
    return h // 2, h % 2, kv // 2, (kv % 2) != (h % 2)


def _swa_prompt_kernel(sink_ref, q_ref, kp_ref, kc_ref, kps_ref, kcs_ref, vp_ref, vc_ref, vps_ref, vcs_ref, y_ref):
    i = pl.program_id(1)
    r = lax.broadcasted_iota(jnp.int32, (Q_BLOCK, 2 * Q_BLOCK), 0)
    c = lax.broadcasted_iota(jnp.int32, (Q_BLOCK, 2 * Q_BLOCK), 1)
    rel = Q_BLOCK + r - c
    ok = (rel >= 0) & (rel <= WINDOW) & ((i - 1) * Q_BLOCK + c >= 0)
    bias = jnp.where(ok, 0.0, NEG_INF)
    keys = (jnp.concatenate([kp_ref[...], kc_ref[...]], axis=0), jnp.concatenate([kps_ref[...], kcs_ref[...]], axis=0))
    vals = (jnp.concatenate([vp_ref[...], vc_ref[...]], axis=0), jnp.concatenate([vps_ref[...], vcs_ref[...]], axis=0))
    lower = _half_mask(Q_BLOCK, False)
    for p in range(SWA_HEADS // 2):
        qpair = q_ref[:, p * LANES:(p + 1) * LANES]
        outs = []
        for h in (2 * p, 2 * p + 1):
            _, half, ks, swapped = _swa_head_plan(h)
            qh = jnp.where(_half_mask(Q_BLOCK, half == 1), qpair, jnp.zeros_like(qpair))
            kk = keys[int(swapped)][:, ks * LANES:(ks + 1) * LANES]
            vv = vals[int(swapped)][:, ks * LANES:(ks + 1) * LANES]
            s = _dot_nt(qh, kk) + bias
            sink = sink_ref[h]
            mx = jnp.maximum(jnp.max(s, axis=-1, keepdims=True), sink)
            pr = jnp.exp(s - mx)
            den = jnp.sum(pr, axis=-1, keepdims=True) + jnp.exp(sink - mx)
            outs.append(jnp.dot(pr.astype(BF16), vv, preferred_element_type=F32) / den)
        y_ref[:, p * LANES:(p + 1) * LANES] = jnp.where(lower, outs[0], outs[1])


def swa_prompt(q, kb, kbs, vb, vbs, sinks, *, batch, seq):
    m = q.shape[0]
    nq = seq // Q_BLOCK
    cur = pl.BlockSpec((Q_BLOCK, 256), lambda b, i, s: (b * nq + i, 0))
    prev = pl.BlockSpec((Q_BLOCK, 256), lambda b, i, s: (b * nq + jnp.maximum(i - 1, 0), 0))
    qspec = pl.BlockSpec((Q_BLOCK, 2048), lambda b, i, s: (b * nq + i, 0))
    return pl.pallas_call(
        _swa_prompt_kernel,
        grid_spec=pltpu.PrefetchScalarGridSpec(
            num_scalar_prefetch=1,
            grid=(batch, nq),
            in_specs=[qspec, prev, cur, prev, cur, prev, cur, prev, cur],
            out_specs=qspec,
        ),
        out_shape=jax.ShapeDtypeStruct((m, SWA_HEADS * SWA_HD), F32),
        compiler_params=_cparams(("parallel", "arbitrary")),
        name="swa_prompt",
    )(sinks, q, kb, kb, kbs, kbs, vb, vb, vbs, vbs)


def _mem_prompt_kernel(x_ref, g_ref, wq_ref, kv_ref, wo_ref, *rest, n_in):
    y_refs, w_refs, o_ref = rest[:n_in], rest[n_in:2 * n_in], rest[2 * n_in]
    x = x_ref[...]
    for y_ref, w_ref in zip(y_refs, w_refs):
        x = x + jnp.dot(y_ref[...].astype(BF16), w_ref[...], preferred_element_type=F32)
    h = _rms(x, g_ref[...]).astype(BF16)
    q = (jnp.dot(h, wq_ref[...], preferred_element_type=F32) * (MEM_HD ** -0.5)).astype(BF16)
    width = MEM_HEADS * MEM_HD
    outs = []
    for hd in range(MEM_HEADS):
        sl = slice(hd * MEM_HD, (hd + 1) * MEM_HD)
        k = kv_ref[:, sl].astype(BF16)
        v = kv_ref[:, width + hd * MEM_HD:width + (hd + 1) * MEM_HD].astype(BF16)
        s = _dot_nt(q[:, sl], k)
        p = jnp.exp(s - jnp.max(s, axis=-1, keepdims=True))
        o = jnp.dot(p.astype(BF16), v, preferred_element_type=F32) / jnp.sum(p, axis=-1, keepdims=True)
        outs.append(o.astype(BF16))
    o_ref[...] = x + jnp.dot(jnp.concatenate(outs, axis=-1), wo_ref[...], preferred_element_type=F32)


def mem_attn_prompt(x, ys, ws, g, wq, kv, wo, *, batch, seq, tq):
    m = x.shape[0]
    nq = seq // tq
    width = MEM_HEADS * MEM_HD
    rows = lambda w: pl.BlockSpec((tq, w), lambda b, i: (b * nq + i, 0))
    const = lambda shape: pl.BlockSpec(shape, lambda b, i: (0, 0), pipeline_mode=pl.Buffered(1))
    return pl.pallas_call(
        functools.partial(_mem_prompt_kernel, n_in=len(ys)),
        grid=(batch, nq),
        in_specs=[rows(D_MODEL),
                  pl.BlockSpec((1, D_MODEL), lambda b, i: (0, 0)),
                  const((D_MODEL, width)),
                  pl.BlockSpec((N_MEM, 2 * width), lambda b, i: (b, 0)),
                  const((width, D_MODEL))]
                 + [rows(y.shape[1]) for y in ys] + [const(w.shape) for w in ws],
        out_specs=rows(D_MODEL),
        out_shape=jax.ShapeDtypeStruct((m, D_MODEL), F32),
        compiler_params=_cparams(("parallel", "arbitrary")),
        name="mem_attn_prompt",
    )(x, g.reshape(1, D_MODEL), wq, kv, wo, *ys, *ws)


def _gla_prep_kernel(gq_ref, gk_ref, sm_ref, wgate_ref, bgate_ref, o_ref):
    w = GLA_HEADS * GLA_DK
    la = _log_decay(sm_ref[...], wgate_ref[...], bgate_ref[...])
    o_ref[:, 0:w] = jnp.exp(la)
    o_ref[:, w:2 * w] = gk_ref[...]
    o_ref[:, 2 * w:3 * w] = gq_ref[...] * (GLA_DK ** -0.5)
    o_ref[:, 3 * w:4 * w] = jnp.zeros((gq_ref.shape[0], w), F32)


def gla_sample_prep(z, wgate, bgate):
    m = z.shape[0]
    w = GLA_HEADS * GLA_DK
    col = lambda wd, off: pl.BlockSpec((m, wd), lambda i: (0, off // wd))
    return pl.pallas_call(
        _gla_prep_kernel,
        grid=(1,),
        in_specs=[col(512, AB_GQ), col(512, AB_GK), col(128, AB_SMALL),
                  pl.BlockSpec((LANES, w), lambda i: (0, 0)), pl.BlockSpec((1, w), lambda i: (0, 0))],
        out_specs=pl.BlockSpec((m, 4 * w), lambda i: (0, 0)),
        out_shape=jax.ShapeDtypeStruct((m, 4 * w), F32),
        compiler_params=_cparams(("arbitrary",)),
        name="gla_sample_prep",
    )(z, z, z, wgate, bgate)


def _gla_step_kernel(p_ref, gv_ref, gr_ref, s_ref, gain_ref, so_ref, y_ref, *, bs):
    gain = gain_ref[...]
    for s in range(bs):
        xt = p_ref[s].T
        for h in range(GLA_HEADS):
            dv = slice(h * GLA_DV, (h + 1) * GLA_DV)
            st = s_ref[s, h] * xt[:, h:h + 1] + xt[:, GLA_HEADS + h:GLA_HEADS + h + 1] * gv_ref[s:s + 1, dv]
            so_ref[s, h] = st
            o = jnp.sum(xt[:, 2 * GLA_HEADS + h:2 * GLA_HEADS + h + 1] * st, axis=0, keepdims=True)
            y_ref[s:s + 1, dv] = _gla_gate_out(o, gr_ref[s:s + 1, dv], gain)


def gla_sample(p, z, state, gain, *, bs):
    m = z.shape[0]
    col = lambda w, off: pl.BlockSpec((bs, w), lambda i: (i, off // w))
    sspec = pl.BlockSpec((bs, GLA_HEADS, GLA_DK, GLA_DV), lambda i: (i, 0, 0, 0))
    return pl.pallas_call(
        functools.partial(_gla_step_kernel, bs=bs),
        grid=(m // bs,),
        in_specs=[pl.BlockSpec((bs, 16, GLA_DK), lambda i: (i, 0, 0)), col(1024, AB_GV), col(1024, AB_GR), sspec,
                  pl.BlockSpec((1, GLA_DV), lambda i: (0, 0))],
        out_specs=[sspec, pl.BlockSpec((bs, GLA_HEADS * GLA_DV), lambda i: (i, 0))],
        out_shape=[jax.ShapeDtypeStruct(state.shape, F32), jax.ShapeDtypeStruct((m, GLA_HEADS * GLA_DV), F32)],
        compiler_params=_cparams(("parallel",)),
        name="gla_sample",
    )(p, z, z, state, gain)


def _dsa_scores_kernel(pt_ref, iq_ref, iw_ref, pool_hbm, o_ref, buf, sem, *, page_base, n_pages, depth):
    b = pl.program_id(0)
    nb = pl.num_programs(0)

    def page_copies(row):
        slot = row % (depth + 1)
        return [pltpu.make_async_copy(pool_hbm.at[page_base + pt_ref[row, j]], buf.at[slot, j], sem.at[slot])
                for j in range(n_pages)]

    @pl.when(b == 0)
    def _():
        for ahead in range(depth):
            for cp in page_copies(ahead):
                cp.start()

    @pl.when(b + depth < nb)
    def _():
        for cp in page_copies(b + depth):
            cp.start()

    for cp in page_copies(b):
        cp.wait()
    slot = b % (depth + 1)
    q8 = iq_ref[0]
    iw = iw_ref[0]
    for j in range(n_pages):
        dots = jnp.dot(q8, buf[slot, j].astype(BF16), preferred_element_type=F32)
        o_ref[0, j:j + 1, :] = jnp.sum(jnp.maximum(dots, 0.0) * iw, axis=0, keepdims=True)


def dsa_sample_scores(page_table, iq, iw, pool_ik_t, page_base, *, depth):
    m, n_pages = page_table.shape
    assert m > depth
    return pl.pallas_call(
        functools.partial(_dsa_scores_kernel, page_base=page_base, n_pages=n_pages, depth=depth),
        grid_spec=pltpu.PrefetchScalarGridSpec(
            num_scalar_prefetch=1,
            grid=(m,),
            in_specs=[pl.BlockSpec((1, IDX_HEADS, IDX_HD), lambda b, pt: (b, 0, 0)),
                      pl.BlockSpec((1, IDX_HEADS, 1), lambda b, pt: (b, 0, 0)),
                      pl.BlockSpec(memory_space=pl.ANY)],
            out_specs=pl.BlockSpec((1, n_pages, PAGE_SIZE), lambda b, pt: (b, 0, 0)),
            scratch_shapes=[pltpu.VMEM((depth + 1, n_pages, IDX_HD, PAGE_SIZE), F32),
                            pltpu.SemaphoreType.DMA((depth + 1,))],
        ),
        out_shape=jax.ShapeDtypeStruct((m, n_pages, PAGE_SIZE), F32),
        compiler_params=_cparams(("arbitrary",)),
        name="dsa_sample_scores",
    )(page_table, iq, iw, pool_ik_t)


def _dsa_select_sample_kernel(sc_ref, iq_ref, iw_ref, ikb_ref, bias_ref, bnew_ref, key_ref, t_ref, *, n_past, topk):
    rows = sc_ref.shape[0]
    n_blocks = n_past // LANES
    for c in range(n_blocks):
        sl = slice(c * LANES, (c + 1) * LANES)
        key_ref[:, sl] = _order_key(sc_ref[:, sl])
    ik = ikb_ref[...].astype(F32)
    iw = iw_ref[...]
    s_new = jnp.zeros((rows, 1), F32)
    for h in range(IDX_HEADS):
        pair = iq_ref[:, (h // 2) * LANES:(h // 2 + 1) * LANES].astype(F32)
        qh = jnp.where(_half_mask(rows, h % 2 == 1), pair, 0.0)
        s_new = s_new + jnp.maximum(jnp.sum(qh * ik, axis=-1, keepdims=True), 0.0) * iw[:, h:h + 1]
    lane = lax.broadcasted_iota(jnp.int32, (rows, LANES), 1)
    key_ref[:, n_past:n_past + LANES] = _order_key(jnp.where(lane == 0, s_new, -jnp.inf))

    ri = lax.broadcasted_iota(jnp.int32, (LANES, DSA_KV_HEADS * LANES), 0)
    ci = lax.broadcasted_iota(jnp.int32, (LANES, DSA_KV_HEADS * LANES), 1)
    spread = jnp.where(ci // DSA_KV_HEADS == ri, 1.0, 0.0).astype(BF16)

    def write(rws, c, sel):
        if c == n_blocks:
            bnew_ref[rws, :] = jnp.where(sel, 0.0, NEG_INF)
        else:
            wide = jnp.dot(jnp.where(sel, 1.0, 0.0).astype(BF16), spread, preferred_element_type=F32)
            w = DSA_KV_HEADS * LANES
            bias_ref[rws, c * w:(c + 1) * w] = jnp.where(wide > 0.5, 0.0, NEG_INF)

    _select_topk(key_ref, t_ref, n_blocks + 1, topk, write)


def dsa_sample_select(scores, iq, iw, ikb, *, topk):
    m, n_past = scores.shape
    full = lambda a: pl.BlockSpec(a.shape, lambda i: (0,) * a.ndim)
    wide = DSA_KV_HEADS * n_past
    return pl.pallas_call(
        functools.partial(_dsa_select_sample_kernel, n_past=n_past, topk=topk),
        grid=(1,),
        in_specs=[full(scores), full(iq), full(iw), full(ikb)],
        out_specs=[pl.BlockSpec((m, wide), lambda i: (0, 0)), pl.BlockSpec((m, LANES), lambda i: (0, 0))],
        out_shape=[jax.ShapeDtypeStruct((m, wide), F32), jax.ShapeDtypeStruct((m, LANES), F32)],
        scratch_shapes=[pltpu.VMEM((m, n_past + LANES), jnp.int32), pltpu.VMEM((m, LANES), jnp.int32)],
        compiler_params=_cparams(("arbitrary",)),
        name="dsa_sample_select",
    )(scores, iq, iw, ikb)


def _dsa_attend_sample_kernel(pt_ref, q_ref, bias_ref, bnew_ref, kn_ref, vn_ref, pk_hbm, pv_hbm, y_ref,
                              kbuf, vbuf, sem, *, page_base, n_pages, chunk):
    b = pl.program_id(0)
    wide = DSA_KV_HEADS * PAGE_SIZE
    n_chunks = n_pages // chunk
    group = DSA_HEADS // DSA_KV_HEADS

    def page_copies(row, c, slot):
        out = []
        for j in range(chunk):
            src = pl.ds(pl.multiple_of((page_base + pt_ref[row, c * chunk + j]) * wide, wide), wide)
            dst = pl.ds(j * wide, wide)
            out.append(pltpu.make_async_copy(pk_hbm.at[src, :], kbuf.at[slot, dst, :], sem.at[0, slot]))
            out.append(pltpu.make_async_copy(pv_hbm.at[src, :], vbuf.at[slot, dst, :], sem.at[1, slot]))
        return out

    @pl.when(b == 0)
    def _():
        for cp in page_copies(0, 0, 0):
            cp.start()

    q8 = q_ref[0]
    first = lax.broadcasted_iota(jnp.int32, (DSA_HEADS, DSA_HD), 0) < group
    hrow = lax.broadcasted_iota(jnp.int32, (DSA_HEADS, wide), 0)
    col = lax.broadcasted_iota(jnp.int32, (DSA_HEADS, wide), 1)
    own = (col % DSA_KV_HEADS) == (hrow // group)
    m_run = jnp.full((DSA_HEADS, DSA_HD), NEG_INF, F32)
    l_run = jnp.zeros((DSA_HEADS, DSA_HD), F32)
    acc = jnp.zeros((DSA_HEADS, DSA_HD), F32)
    for c in range(n_chunks):
        slot = c % 2
        if c + 1 < n_chunks:
            for cp in page_copies(b, c + 1, 1 - slot):
                cp.start()
        else:
            @pl.when(b + 1 < pl.num_programs(0))
            def _():
                for cp in page_copies(b + 1, 0, 1 - slot):
                    cp.start()
        for cp in page_copies(b, c, slot):
            cp.wait()
        ss, oks = [], []
        for j in range(chunk):
            ok = jnp.logical_and(own, bias_ref[0, c * chunk + j:c * chunk + j + 1, :] == 0.0)
            kp = kbuf[slot, j * wide:(j + 1) * wide, :].astype(BF16)
            ss.append(jnp.where(ok, _dot_nt(q8, kp), NEG_INF))
            oks.append(ok)
        mx = ss[0]
        for s in ss[1:]:
            mx = jnp.maximum(mx, s)
        m_new = jnp.maximum(m_run, jnp.max(mx, axis=-1, keepdims=True))
        psum = jnp.zeros((DSA_HEADS, wide), F32)
        pv = jnp.zeros((DSA_HEADS, DSA_HD), F32)
        for j in range(chunk):
            p = jnp.where(oks[j], jnp.exp(ss[j] - m_new[:, 0:1]), 0.0)
            psum = psum + p
            vp = vbuf[slot, j * wide:(j + 1) * wide, :].astype(BF16)
            pv = pv + jnp.dot(p.astype(BF16), vp, preferred_element_type=F32)
        alpha = jnp.exp(m_run - m_new)
        l_run = alpha * l_run + jnp.sum(psum, axis=-1, keepdims=True)
        acc = alpha * acc + pv
        m_run = m_new

    kn = kn_ref[0].astype(BF16).astype(F32)
    vn = vn_ref[0].astype(BF16).astype(F32)
    bn = bnew_ref[0][:, 0:1]
    s_new = jnp.sum(q8.astype(F32) * jnp.where(first, kn[:, :DSA_HD], kn[:, DSA_HD:]), axis=-1, keepdims=True) + bn
    m_new = jnp.maximum(m_run, s_new)
    p_new = jnp.where(bn == 0.0, jnp.exp(s_new - m_new), 0.0)
    alpha = jnp.exp(m_run - m_new)
    acc = alpha * acc + p_new * jnp.where(first, vn[:, :DSA_HD], vn[:, DSA_HD:])
    y_ref[0] = acc / (alpha * l_run + p_new)


def dsa_sample_attend(page_table, q, bias, bnew, kn, vn, pool_k, pool_v, page_base, *, chunk):
    m, n_pages = page_table.shape
    wide = DSA_KV_HEADS * PAGE_SIZE
    assert n_pages % chunk == 0 and (n_pages // chunk) % 2 == 0
    per_b = lambda shape: pl.BlockSpec((1,) + shape, lambda b, pt: (b, 0, 0))
    hbm = pl.BlockSpec(memory_space=pl.ANY)
    return pl.pallas_call(
        functools.partial(_dsa_attend_sample_kernel, page_base=page_base, n_pages=n_pages, chunk=chunk),
        grid_spec=pltpu.PrefetchScalarGridSpec(
            num_scalar_prefetch=1,
            grid=(m,),
            in_specs=[per_b((DSA_HEADS, DSA_HD)), per_b((n_pages, wide)),
                      per_b((1, LANES)), per_b((1, DSA_KV_HEADS * DSA_HD)), per_b((1, DSA_KV_HEADS * DSA_HD)),
                      hbm, hbm],
            out_specs=per_b((DSA_HEADS, DSA_HD)),
            scratch_shapes=[pltpu.VMEM((2, chunk * wide, DSA_HD), F32), pltpu.VMEM((2, chunk * wide, DSA_HD), F32),
                            pltpu.SemaphoreType.DMA((2, 2))],
        ),
        out_shape=jax.ShapeDtypeStruct((m, DSA_HEADS, DSA_HD), F32),
        compiler_params=_cparams(("arbitrary",)),
        name="dsa_sample_attend",
    )(page_table, q, bias, bnew, kn, vn, pool_k, pool_v)


def _swa_step_kernel(q_ref, kt_ref, vt_ref, kn_ref, vn_ref, sink_ref, y_ref, *, bs):
    group = SWA_HEADS // SWA_KV_HEADS
    pairs = [(s, kv) for s in range(bs) for kv in range(SWA_KV_HEADS)]
    qs, scs, news = [], [], []
    for s, kv in pairs:
        qv = q_ref[s, kv * group:(kv + 1) * group, :]
        kn = kn_ref[s, kv:kv + 1, :].astype(BF16).astype(F32)
        scs.append(jnp.dot(qv, kt_ref[s, kv].astype(BF16), preferred_element_type=F32))
        news.append(jnp.sum(qv.astype(F32) * kn, axis=-1, keepdims=True))
    sc = jnp.concatenate(scs, axis=0)
    s_new = jnp.concatenate(news, axis=0)
    sink = jnp.concatenate([sink_ref[...]] * bs, axis=0)
    mx = jnp.maximum(jnp.maximum(jnp.max(sc, axis=-1, keepdims=True), s_new), sink)
    p = jnp.exp(sc - mx)
    p_new = jnp.exp(s_new - mx)
    inv = 1.0 / (jnp.sum(p, axis=-1, keepdims=True) + p_new + jnp.exp(sink - mx))
    pb = p.astype(BF16)
    for n, (s, kv) in enumerate(pairs):
        rows = slice(n * group, (n + 1) * group)
        vn = vn_ref[s, kv:kv + 1, :].astype(BF16).astype(F32)
        o = _dot_nt(pb[rows, :], vt_ref[s, kv].astype(BF16)) + p_new[rows, :] * vn
        y_ref[s, kv * group:(kv + 1) * group, :] = o * inv[rows, :]


def swa_sample(q, kt, vt, kn, vn, sinks, *, bs):
    m = q.shape[0]
    cache = pl.BlockSpec((bs, SWA_KV_HEADS, SWA_HD, WINDOW), lambda i: (i, 0, 0, 0))
    new = pl.BlockSpec((bs, SWA_KV_HEADS, SWA_HD), lambda i: (i, 0, 0))
    return pl.pallas_call(
        functools.partial(_swa_step_kernel, bs=bs),
        grid=(m // bs,),
        in_specs=[pl.BlockSpec((bs, SWA_HEADS, SWA_HD), lambda i: (i, 0, 0)), cache, cache, new, new,
                  pl.BlockSpec((SWA_HEADS, 1), lambda i: (0, 0))],
        out_specs=pl.BlockSpec((bs, SWA_HEADS, SWA_HD), lambda i: (i, 0, 0)),
        out_shape=jax.ShapeDtypeStruct((m, SWA_HEADS, SWA_HD), F32),
        compiler_params=_cparams(("parallel",)),
        name="swa_sample",
    )(q, kt, vt, kn, vn, sinks.reshape(SWA_HEADS, 1))


def _mem_step_kernel(q_ref, k_ref, v_ref, y_ref, *, bs):
    rows = q_ref.shape[1]
    n = N_MEM * MEM_HEADS
    hrow = lax.broadcasted_iota(jnp.int32, (rows, n), 0)
    col = lax.broadcasted_iota(jnp.int32, (rows, n), 1)
    own = (col % MEM_HEADS) == (hrow % MEM_HEADS)
    for s in range(bs):
        q = (q_ref[s] * (MEM_HD ** -0.5)).astype(BF16)
        sc = jnp.where(own, _dot_nt(q, k_ref[s * n:(s + 1) * n, :].astype(BF16)), NEG_INF)
        p = jnp.where(own, jnp.exp(sc - jnp.max(sc, axis=-1, keepdims=True)), 0.0)
        o = jnp.dot(p.astype(BF16), v_ref[s * n:(s + 1) * n, :].astype(BF16), preferred_element_type=F32)
        y_ref[s] = o / jnp.sum(p, axis=-1, keepdims=True)


def mem_attn_sample(q, mk, mv, layer, *, bs):
    m, rows, _ = q.shape
    n = N_MEM * MEM_HEADS
    nb = m // bs
    cache = pl.BlockSpec((bs * n, MEM_HD), lambda i: (layer * nb + i, 0))
    return pl.pallas_call(
        functools.partial(_mem_step_kernel, bs=bs),
        grid=(nb,),
        in_specs=[pl.BlockSpec((bs, rows, MEM_HD), lambda i: (i, 0, 0)), cache, cache],
        out_specs=pl.BlockSpec((bs, rows, MEM_HD), lambda i: (i, 0, 0)),
        out_shape=jax.ShapeDtypeStruct((m, rows, MEM_HD), F32),
        compiler_params=_cparams(("parallel",)),
        name="mem_attn_sample",
    )(q, mk, mv)


TM_FFN, TF_FFN, SUB_FFN = 1024, 512, 512
TM_PROJ, TN_MEM = 1024, 512
TM_GLA_PROJ = 512
TM_OUT = 512
TC_GLA, NSEQ_GLA = 256, 2
TQ_MEM = 512
BS_SAMPLE = 8
DEPTH_SCORES = 2
PG_ATTEND = 16


def _prep_w_in_ab(w):
    sizes = (GLA_HEADS * GLA_DK, GLA_HEADS * GLA_DK, GLA_HEADS * GLA_DV, GLA_HEADS * GLA_DV, GLA_GATE_RANK,
             DSA_HEADS * DSA_HD, DSA_KV_HEADS * DSA_HD, DSA_KV_HEADS * DSA_HD, IDX_HEADS * IDX_HD, IDX_HEADS, IDX_HD)
    offs = np.cumsum((0,) + sizes)
    gq, gk, gv, gr, gd, dq, dk, dv, iq, iw, ik = [w[:, int(offs[j]):int(offs[j + 1])] for j in range(len(sizes))]
    pad = lambda n: jnp.zeros((w.shape[0], n), w.dtype)
    small = jnp.concatenate([ik, gd, iw, pad(LANES - IDX_HD - GLA_GATE_RANK - IDX_HEADS)], axis=1)
    w_gla = jnp.concatenate([gv, gr, gq, gk, small], axis=1)
    w_dsa = jnp.concatenate([dq, dk, dv, iq, small], axis=1)
    return w_gla.astype(BF16), w_dsa.astype(BF16)


def _prep_gate(w_up):
    return jnp.zeros((LANES, GLA_HEADS * GLA_DK), F32).at[SM_GD:SM_GD + GLA_GATE_RANK].set(w_up)


def kernel(x_prompt, x_sample, mem_prompt, cache_dsa_k, cache_dsa_v, cache_dsa_idx_k, state_gla, cache_swa_k, cache_swa_v, cache_mem_k, cache_mem_v, page_table, norm_ffn, w_ffn_gate, w_ffn_up, w_ffn_down, norm_mix, w_in_ab, w_gla_gate_up, b_gla_gate, gla_out_norm, idx_k_norm, w_out_ab, w_in_c, swa_sinks, w_out_c, norm_mem_q, norm_mem_src, w_mem_q, w_mem_kv, w_mem_o, final_norm):
    depth = norm_mix.shape[0]
    bp, seq, _ = x_prompt.shape
    bs = x_sample.shape[0]
    n_pool = cache_dsa_k.shape[1]
    gla_w = GLA_HEADS * GLA_DV

    w_ab = [_prep_w_in_ab(w_in_ab[i]) for i in range(w_in_ab.shape[0])]
    w_gate = [_prep_gate(w_gla_gate_up[i]) for i in range(w_in_ab.shape[0])]
    w_oab = w_out_ab.astype(BF16)
    w_c, w_oc = w_in_c.astype(BF16), w_out_c.astype(BF16)
    w_mq, w_mkv, w_mo = w_mem_q.astype(BF16), w_mem_kv.astype(BF16), w_mem_o.astype(BF16)

    def rope_tabs(pos):
        return _rope_tables(pos, DSA_HD) + _rope_tables(pos, IDX_HD)

    ffn_bf16 = {}

    def ffn_pair(x, layer, half, last, sample):
        fin = final_norm if last else None
        if sample:
            y, *ffn_bf16[layer, half] = ffn(x, norm_ffn[layer, half], w_ffn_gate, w_ffn_up, w_ffn_down, (layer, half),
                                            fin, tm=TM_FFN, tf=TF_FFN, sub=SUB_FFN)
            return y
        return ffn(x, norm_ffn[layer, half], *ffn_bf16[layer, half], None, fin, tm=TM_FFN, tf=TF_FFN, sub=SUB_FFN)

    def prompt_group():
        tabs_p = rope_tabs(jnp.arange(seq))
        x = x_prompt.reshape(bp * seq, D_MODEL)
        mem = mem_prompt.reshape(bp * N_MEM, D_MODEL)
        mem_kv = [norm_proj(mem, norm_mem_src[l], w_mkv[l], tm=TM_PROJ, tn=TN_MEM) for l in range(depth)]
        st_ab_p, st_c_p = [], []
        for l in range(depth):
            i = l // 2
            x = ffn_pair(x, l, 0, False, False)
            if l % 2 == 0:
                z = norm_proj(x, norm_mix[l], w_ab[i][0], tm=TM_GLA_PROJ, tn=AB_WIDTH)
                q, k, kb, v, vb, iq, ik, ikb, iw = d_proj_features(x, norm_mix[l], w_ab[i][1], tabs_p, idx_k_norm[i],
                                                                   seq=seq, tm=TM_PROJ, sub=SUB_FFN)
                y_gla, st_t = gla_prompt(z, w_gate[i], b_gla_gate[i].reshape(1, -1), gla_out_norm[i].reshape(1, -1),
                                         batch=bp, seq=seq, tc=TC_GLA, n_seq=NSEQ_GLA)
                y_dsa = dsa_prompt(q, iq, iw, kb, vb, ikb, batch=bp, seq=seq)
                mix_out = ([y_gla, y_dsa], [w_oab[i, :gla_w], w_oab[i, gla_w:]])
                n_pg = seq // PAGE_SIZE
                st_ab_p.append((k.reshape(bp, n_pg, PAGE_SIZE, DSA_KV_HEADS, DSA_HD),
                                v.reshape(bp, n_pg, PAGE_SIZE, DSA_KV_HEADS, DSA_HD),
                                ik.reshape(bp, n_pg, PAGE_SIZE, IDX_HD),
                                jnp.swapaxes(st_t, 2, 3)))
            else:
                q, k, kb, kbs, v, vb, vbs = c_proj_features(x, norm_mix[l], w_c[i], tabs_p[2:], seq=seq,
                                                            tm=TM_PROJ, sub=SUB_FFN)
                y = swa_prompt(q, kb, kbs, vb, vbs, swa_sinks[i], batch=bp, seq=seq)
                mix_out = ([y], [w_oc[i]])
                st_c_p.append((k.reshape(bp, seq, SWA_KV_HEADS, SWA_HD)[:, -WINDOW:],
                               v.reshape(bp, seq, SWA_KV_HEADS, SWA_HD)[:, -WINDOW:]))
            x = mem_attn_prompt(x, *mix_out, norm_mem_q[l], w_mq[l], mem_kv[l], w_mo[l], batch=bp, seq=seq, tq=TQ_MEM)
            x = ffn_pair(x, l, 1, l == depth - 1, False)
        return x.reshape(bp, seq, D_MODEL), st_ab_p, st_c_p, mem_kv

    tabs_s = rope_tabs(PAST_LEN + jnp.arange(1))
    n_pages = page_table.shape[1]
    topk = min(DSA_TOPK_MAX, (PAST_LEN + 1) // 4)
    pool_k = cache_dsa_k.reshape(-1, DSA_HD)
    pool_v = cache_dsa_v.reshape(-1, DSA_HD)
    pool_ik_t = jnp.swapaxes(cache_dsa_idx_k, 2, 3).reshape(-1, IDX_HD, PAGE_SIZE)
    mem_k_rows = cache_mem_k.reshape(-1, MEM_HD)
    mem_v_rows = cache_mem_v.reshape(-1, MEM_HD)
    x = x_sample.reshape(bs, D_MODEL)
    st_ab_s, st_c_s = [], []
    for l in range(depth):
        i = l // 2
        x = ffn_pair(x, l, 0, False, True)
        if l % 2 == 0:
            z = norm_proj(x, norm_mix[l], w_ab[i][0], tm=TM_GLA_PROJ, tn=AB_WIDTH)
            q, k, kb, v_new, vb, iq, ik, ikb, iw = d_proj_features(x, norm_mix[l], w_ab[i][1], tabs_s, idx_k_norm[i],
                                                                   seq=1, tm=TM_PROJ, sub=SUB_FFN)
            p = gla_sample_prep(z, w_gate[i], b_gla_gate[i].reshape(1, -1)).reshape(bs, 16, GLA_DK)
            st_new, y_gla = gla_sample(p, z, state_gla[i], gla_out_norm[i].reshape(1, -1), bs=BS_SAMPLE)
            scores = dsa_sample_scores(page_table, iq.reshape(bs, IDX_HEADS, IDX_HD),
                                       iw[:, :IDX_HEADS].reshape(bs, IDX_HEADS, 1), pool_ik_t, i * n_pool, depth=DEPTH_SCORES)
            bias, bnew = dsa_sample_select(scores.reshape(bs, n_pages * PAGE_SIZE), iq, iw, ikb, topk=topk)
            y_dsa = dsa_sample_attend(
                page_table, q.reshape(bs, DSA_HEADS, DSA_HD), bias.reshape(bs, n_pages, DSA_KV_HEADS * PAGE_SIZE),
                bnew.reshape(bs, 1, LANES), k.reshape(bs, 1, 256), v_new.reshape(bs, 1, 256),
                pool_k, pool_v, i * n_pool, chunk=PG_ATTEND).reshape(bs, DSA_HEADS * DSA_HD)
            x = out_proj(x, [y_gla, y_dsa], [w_oab[i, :gla_w], w_oab[i, gla_w:]], tm=TM_OUT)
            st_ab_s.append((k.reshape(bs, 1, DSA_KV_HEADS, DSA_HD), v_new.reshape(bs, 1, DSA_KV_HEADS, DSA_HD),
                            ik.reshape(bs, 1, IDX_HD), st_new))
        else:
            q, k, kb, kbs, v_new, vb, vbs = c_proj_features(x, norm_mix[l], w_c[i], tabs_s[2:], seq=1,
                                                            tm=TM_PROJ, sub=SUB_FFN)
            y = swa_sample(q.reshape(bs, SWA_HEADS, SWA_HD), jnp.transpose(cache_swa_k[i], (0, 2, 3, 1)),
                           jnp.transpose(cache_swa_v[i], (0, 2, 3, 1)), k.reshape(bs, SWA_KV_HEADS, SWA_HD),
                           v_new.reshape(bs, SWA_KV_HEADS, SWA_HD), swa_sinks[i], bs=BS_SAMPLE)
            x = out_proj(x, [y.reshape(bs, SWA_HEADS * SWA_HD)], [w_oc[i]], tm=TM_OUT)
            st_c_s.append((jnp.concatenate([cache_swa_k[i][:, 1:], k.reshape(bs, 1, SWA_KV_HEADS, SWA_HD)], axis=1),
                           jnp.concatenate([cache_swa_v[i][:, 1:], v_new.reshape(bs, 1, SWA_KV_HEADS, SWA_HD)], axis=1)))
        qm = norm_proj(x, norm_mem_q[l], w_mq[l], tm=TM_PROJ, tn=TN_MEM).reshape(bs, MEM_HEADS, MEM_HD)
        qm = jnp.pad(qm, ((0, 0), (0, SUBLANES - MEM_HEADS), (0, 0)))
        om = mem_attn_sample(qm, mem_k_rows, mem_v_rows, l, bs=BS_SAMPLE)
        x = out_proj(x, [om[:, :MEM_HEADS].reshape(bs, MEM_HEADS * MEM_HD)], [w_mo[l]], tm=TM_OUT)
        x = ffn_pair(x, l, 1, l == depth - 1, True)
    y_sample = x.reshape(bs, 1, D_MODEL)

    y_prompt, st_ab_p, st_c_p, mem_kv = prompt_group()

    stk = lambda sts, j: jnp.stack([s[j] for s in sts])
    mw = MEM_HEADS * MEM_HD
    mem_k_p = jnp.stack([kv[:, :mw].reshape(bp, N_MEM, MEM_HEADS, MEM_HD) for kv in mem_kv])
    mem_v_p = jnp.stack([kv[:, mw:].reshape(bp, N_MEM, MEM_HEADS, MEM_HD) for kv in mem_kv])
    return (y_prompt, y_sample, stk(st_ab_p, 0), stk(st_ab_p, 1), stk(st_ab_p, 2),
            stk(st_ab_s, 0), stk(st_ab_s, 1), stk(st_ab_s, 2), stk(st_ab_p, 3), stk(st_ab_s, 3),
            stk(st_c_p, 0), stk(st_c_p, 1), stk(st_c_s, 0), stk(st_c_s, 1), mem_k_p, mem_v_p)
```

```python
import functools

import jax
import jax.numpy as jnp
import numpy as np
from jax import lax
from jax.experimental import pallas as pl
from jax.experimental.pallas import tpu as pltpu

F32 = jnp.float32
BF16 = jnp.bfloat16

D_MODEL = 2048
D_FF = 5632
EPS = 1e-6
ROPE_THETA = 10000.0
NEG_INF = -1e30
PAST_LEN = 8192
PAGE_SIZE = 128
Q_BLOCK = 128
GLA_HEADS, GLA_DK, GLA_DV = 4, 128, 256
GLA_GATE_RANK = 16
GLA_GATE_TAU = 16.0
GLA_CHUNK = 64
DSA_HEADS, DSA_KV_HEADS, DSA_HD = 8, 2, 128
IDX_HEADS, IDX_HD = 8, 64
DSA_TOPK_MAX = 256
SWA_HEADS, SWA_KV_HEADS, SWA_HD = 32, 4, 64
WINDOW = 128
MEM_HEADS, MEM_HD = 4, 128
N_MEM = 256

LANES = 128
SUBLANES = 8
VMEM_LIMIT_BYTES = 60000 * 1024
ELEM_SUB = 128

AB_GV, AB_GR, AB_GQ, AB_GK = 0, 1024, 2048, 2560
AB_SMALL = 3072
AB_WIDTH = 3200
D_DQ, D_DK, D_DV, D_IQ, D_SMALL, D_WIDTH = 0, 1024, 1280, 1536, 2048, 2176
SM_IK, SM_GD, SM_IW = 0, 64, 80
C_Q, C_K, C_V, C_WIDTH = 0, 2048, 2304, 2560


def _cparams(sem):
    return pltpu.CompilerParams(dimension_semantics=sem, vmem_limit_bytes=VMEM_LIMIT_BYTES)


def _rms(x, g):
    y = x * lax.rsqrt(jnp.mean(x * x, axis=-1, keepdims=True) + EPS)
    return y * g


def _ffn_kernel(x_ref, g_ref, wg_ref, wu_ref, wd_ref, fg_ref, o_ref, *rest, final_norm, sub, emit_cast):
    h_ref = rest[-1]
    j = pl.program_id(1)
    tm = x_ref.shape[0]
    if emit_cast:
        for src, dst in zip((wg_ref, wu_ref, wd_ref), rest[:3]):
            dst[...] = src[...].astype(BF16)
        wg_ref, wu_ref, wd_ref = rest[:3]

    esub = min(ELEM_SUB, tm)

    def row_group(r):
        return pl.ds(pl.multiple_of(r * esub, esub), esub)

    @pl.when(j == 0)
    def _():
        def body(r, carry):
            rows = row_group(r)
            h_ref[rows, :] = _rms(x_ref[rows, :], g_ref[...]).astype(BF16)
            o_ref[rows, :] = jnp.zeros((esub, D_MODEL), F32)
            return carry

        lax.fori_loop(0, tm // esub, body, 0)

    for r in range(tm // sub):
        rows = slice(r * sub, (r + 1) * sub)
        h = h_ref[rows, :]
        a = jnp.dot(h, wg_ref[...], preferred_element_type=F32)
        u = jnp.dot(h, wu_ref[...], preferred_element_type=F32)
        act = (a * jax.nn.sigmoid(a) * u).astype(BF16)
        o_ref[rows, :] += jnp.dot(act, wd_ref[...], preferred_element_type=F32)

    @pl.when(j == pl.num_programs(1) - 1)
    def _():
        def body(r, carry):
            rows = row_group(r)
            y = x_ref[rows, :] + 0.5 * o_ref[rows, :]
            if final_norm:
                y = _rms(y, fg_ref[...])
            o_ref[rows, :] = y
            return carry

        lax.fori_loop(0, tm // esub, body, 0)


def ffn(x, g, wg, wu, wd, sel, final_g=None, *, tm, tf, sub):
    m = x.shape[0]
    tm = min(tm, m)
    sub = min(sub, tm)
    fg = g if final_g is None else final_g
    emit_cast = sel is not None
    wspec = lambda shape, imap: pl.BlockSpec(shape, imap)
    if emit_cast:
        assert m == tm and wg.dtype == F32
        layer, half = sel
        w_in = [wspec((None, None, D_MODEL, tf), lambda i, j: (layer, half, 0, j)),
                wspec((None, None, D_MODEL, tf), lambda i, j: (layer, half, 0, j)),
                wspec((None, None, tf, D_MODEL), lambda i, j: (layer, half, j, 0))]
    else:
        assert wg.dtype == BF16 and wg.ndim == 2
        w_in = [wspec((D_MODEL, tf), lambda i, j: (0, j)), wspec((D_MODEL, tf), lambda i, j: (0, j)),
                wspec((tf, D_MODEL), lambda i, j: (j, 0))]
    out_specs = [pl.BlockSpec((tm, D_MODEL), lambda i, j: (i, 0))]
    out_shape = [jax.ShapeDtypeStruct((m, D_MODEL), F32)]
    if emit_cast:
        out_specs += [wspec((D_MODEL, tf), lambda i, j: (0, j)), wspec((D_MODEL, tf), lambda i, j: (0, j)),
                      wspec((tf, D_MODEL), lambda i, j: (j, 0))]
        out_shape += [jax.ShapeDtypeStruct((D_MODEL, D_FF), BF16), jax.ShapeDtypeStruct((D_MODEL, D_FF), BF16),
                      jax.ShapeDtypeStruct((D_FF, D_MODEL), BF16)]
    outs = pl.pallas_call(
        functools.partial(_ffn_kernel, final_norm=final_g is not None, sub=sub, emit_cast=emit_cast),
        grid=(m // tm, D_FF // tf),
        in_specs=[pl.BlockSpec((tm, D_MODEL), lambda i, j: (i, 0)), pl.BlockSpec((1, D_MODEL), lambda i, j: (0, 0))]
                 + w_in + [pl.BlockSpec((1, D_MODEL), lambda i, j: (0, 0))],
        out_specs=out_specs,
        out_shape=out_shape,
        scratch_shapes=[pltpu.VMEM((tm, D_MODEL), BF16)],
        compiler_params=_cparams(("parallel", "arbitrary")),
        name="ffn_cast" if emit_cast else "ffn",
    )(x, g.reshape(1, D_MODEL), wg, wu, wd, fg.reshape(1, D_MODEL))
    return tuple(outs) if emit_cast else outs[0]


def _proj_kernel(x_ref, g_ref, w_ref, o_ref, h_ref, *, sub):
    tm = x_ref.shape[0]

    @pl.when(pl.program_id(1) == 0)
    def _():
        for r in range(tm // sub):
            rows = slice(r * sub, (r + 1) * sub)
            h_ref[rows, :] = _rms(x_ref[rows, :], g_ref[...]).astype(BF16)

    o_ref[...] = jnp.dot(h_ref[...], w_ref[...], preferred_element_type=F32)


def norm_proj(x, g, w, *, tm, tn, sub=512):
    m, n = x.shape[0], w.shape[1]
    tm = min(tm, m)
    w_spec = (pl.BlockSpec((D_MODEL, tn), lambda i, j: (0, j), pipeline_mode=pl.Buffered(1)) if tn == n
              else pl.BlockSpec((D_MODEL, tn), lambda i, j: (0, j)))
    return pl.pallas_call(
        functools.partial(_proj_kernel, sub=min(sub, tm)),
        grid=(m // tm, n // tn),
        in_specs=[
            pl.BlockSpec((tm, D_MODEL), lambda i, j: (i, 0)),
            pl.BlockSpec((1, D_MODEL), lambda i, j: (0, 0)),
            w_spec,
        ],
        out_specs=pl.BlockSpec((tm, tn), lambda i, j: (i, j)),
        out_shape=jax.ShapeDtypeStruct((m, n), F32),
        scratch_shapes=[pltpu.VMEM((tm, D_MODEL), BF16)],
        compiler_params=_cparams(("parallel", "arbitrary")),
        name="norm_proj",
    )(x, g.reshape(1, D_MODEL), w)


def _outproj_kernel(*refs, n_in):
    x_ref = refs[0]
    y_refs = refs[1:1 + n_in]
    w_refs = refs[1 + n_in:1 + 2 * n_in]
    o_ref = refs[1 + 2 * n_in]
    acc = x_ref[...]
    for y_ref, w_ref in zip(y_refs, w_refs):
        acc = acc + jnp.dot(y_ref[...].astype(BF16), w_ref[...], preferred_element_type=F32)
    o_ref[...] = acc


def out_proj(x, ys, ws, *, tm):
    m = x.shape[0]
    tm = min(tm, m)
    n_in = len(ys)
    in_specs = [pl.BlockSpec((tm, D_MODEL), lambda i: (i, 0))]
    in_specs += [pl.BlockSpec((tm, y.shape[1]), lambda i: (i, 0)) for y in ys]
    in_specs += [pl.BlockSpec(w.shape, lambda i: (0, 0)) for w in ws]
    return pl.pallas_call(
        functools.partial(_outproj_kernel, n_in=n_in),
        grid=(m // tm,),
        in_specs=in_specs,
        out_specs=pl.BlockSpec((tm, D_MODEL), lambda i: (i, 0)),
        out_shape=jax.ShapeDtypeStruct((m, D_MODEL), F32),
        compiler_params=_cparams(("parallel",)),
        name="out_proj",
    )(x, *ys, *ws)


def _rope_tables(pos, hd):
    half = hd // 2
    inv = ROPE_THETA ** (-jnp.arange(half, dtype=F32) / half)
    ang = pos.astype(F32)[:, None] * inv[None, :]
    cos, sin = jnp.cos(ang), jnp.sin(ang)
    reps = LANES // hd
    return (jnp.concatenate([cos, cos] * reps, axis=-1),
            jnp.concatenate([-sin, sin] * reps, axis=-1))


def _rope128(x, cos, sin):
    return x * cos + pltpu.roll(x, 64, 1) * sin


def _rope64(x, cos, sin, lower):
    partner = jnp.where(lower, pltpu.roll(x, 96, 1), pltpu.roll(x, 32, 1))
    return x * cos + partner * sin


def _lower32_mask(rows):
    lane = lax.broadcasted_iota(jnp.int32, (rows, LANES), 1)
    return (lane % 64) < 32


def _d_proj_feat_kernel(x_ref, g_ref, w_ref, c128_ref, s128_ref, c64_ref, s64_ref, gik_ref,
                        q_ref, k_ref, kb_ref, v_ref, vb_ref, iqo_ref, ik_ref, ikb_ref, iw_ref, *, sub):
    tm = x_ref.shape[0]
    table_rows = c128_ref.shape[0]
    lower = _lower32_mask(sub)
    lane = lax.broadcasted_iota(jnp.int32, (sub, LANES), 1)
    for r in range(tm // sub):
        rows = slice(r * sub, (r + 1) * sub)
        trows = rows if table_rows > 1 else slice(0, 1)
        c128, s128 = c128_ref[trows, :], s128_ref[trows, :]
        c64, s64 = c64_ref[trows, :], s64_ref[trows, :]
        h = _rms(x_ref[rows, :], g_ref[...]).astype(BF16)
        z = jnp.dot(h, w_ref[...], preferred_element_type=F32)
        for hd in range(DSA_HEADS):
            sl = slice(hd * LANES, (hd + 1) * LANES)
            q_ref[rows, sl] = (_rope128(z[:, D_DQ + hd * LANES:D_DQ + (hd + 1) * LANES], c128, s128)
                               * (DSA_HD ** -0.5)).astype(BF16)
        for hd in range(DSA_KV_HEADS):
            sl = slice(hd * LANES, (hd + 1) * LANES)
            kr = _rope128(z[:, D_DK + hd * LANES:D_DK + (hd + 1) * LANES], c128, s128)
            v = z[:, D_DV + hd * LANES:D_DV + (hd + 1) * LANES]
            k_ref[rows, sl] = kr
            kb_ref[rows, sl] = kr.astype(BF16)
            v_ref[rows, sl] = v
            vb_ref[rows, sl] = v.astype(BF16)
        for p in range(IDX_HEADS * IDX_HD // LANES):
            sl = slice(p * LANES, (p + 1) * LANES)
            iqo_ref[rows, sl] = (_rope64(z[:, D_IQ + p * LANES:D_IQ + (p + 1) * LANES], c64, s64, lower)
                                 * (IDX_HD ** -0.5)).astype(BF16)
        sm = z[:, D_SMALL:D_SMALL + LANES]
        ik = jnp.where(lane < IDX_HD, sm, 0.0)
        ik = ik * lax.rsqrt(jnp.sum(ik * ik, axis=-1, keepdims=True) / IDX_HD + EPS) * gik_ref[...]
        ik = _rope64(ik, c64, s64, lower)
        ik_ref[rows, :] = ik[:, :IDX_HD]
        ikb_ref[rows, :] = jnp.where(lane < IDX_HD, ik, pltpu.roll(ik, 64, 1)).astype(BF16)
        iw_ref[rows, :] = pltpu.roll(sm, LANES - SM_IW, 1) * (IDX_HEADS ** -0.5)


def d_proj_features(x, g, w, tabs, gik, *, seq, tm, sub):
    m = x.shape[0]
    tm = min(tm, m)
    sub = min(sub, tm)
    c128, s128, c64, s64 = tabs
    if c128.shape[0] == 1:
        tab_spec = pl.BlockSpec((1, LANES), lambda i: (0, 0))
    else:
        nt = seq // tm
        tab_spec = pl.BlockSpec((tm, LANES), lambda i: (i % nt, 0))
    row = lambda wd: pl.BlockSpec((tm, wd), lambda i: (i, 0))
    gik_pad = jnp.zeros((1, LANES), F32).at[0, :IDX_HD].set(gik)
    return pl.pallas_call(
        functools.partial(_d_proj_feat_kernel, sub=sub),
        grid=(m // tm,),
        in_specs=[row(D_MODEL), pl.BlockSpec((1, D_MODEL), lambda i: (0, 0)),
                  pl.BlockSpec((D_MODEL, D_WIDTH), lambda i: (0, 0), pipeline_mode=pl.Buffered(1)),
                  tab_spec, tab_spec, tab_spec, tab_spec, pl.BlockSpec((1, LANES), lambda i: (0, 0))],
        out_specs=[row(1024), row(256), row(256), row(256), row(256), row(512), row(IDX_HD), row(128), row(128)],
        out_shape=[jax.ShapeDtypeStruct((m, 1024), BF16),
                   jax.ShapeDtypeStruct((m, 256), F32),
                   jax.ShapeDtypeStruct((m, 256), BF16),
                   jax.ShapeDtypeStruct((m, 256), F32),
                   jax.ShapeDtypeStruct((m, 256), BF16),
                   jax.ShapeDtypeStruct((m, 512), BF16),
                   jax.ShapeDtypeStruct((m, IDX_HD), F32),
                   jax.ShapeDtypeStruct((m, 128), BF16),
                   jax.ShapeDtypeStruct((m, 128), F32)],
        compiler_params=_cparams(("parallel",)),
        name="d_proj_features",
    )(x, g.reshape(1, D_MODEL), w, c128, s128, c64, s64, gik_pad)


def _c_proj_feat_kernel(x_ref, g_ref, w_ref, c64_ref, s64_ref,
                        qo_ref, ko_ref, kb_ref, kbs_ref, vo_ref, vb_ref, vbs_ref, *, sub):
    tm = x_ref.shape[0]
    table_rows = c64_ref.shape[0]
    lower = _lower32_mask(sub)
    for r in range(tm // sub):
        rows = slice(r * sub, (r + 1) * sub)
        trows = rows if table_rows > 1 else slice(0, 1)
        c64, s64 = c64_ref[trows, :], s64_ref[trows, :]
        h = _rms(x_ref[rows, :], g_ref[...]).astype(BF16)
        z = jnp.dot(h, w_ref[...], preferred_element_type=F32)
        for p in range(SWA_HEADS * SWA_HD // LANES):
            sl = slice(p * LANES, (p + 1) * LANES)
            qo_ref[rows, sl] = (_rope64(z[:, C_Q + p * LANES:C_Q + (p + 1) * LANES], c64, s64, lower)
                                * (SWA_HD ** -0.5)).astype(BF16)
        for p in range(SWA_KV_HEADS * SWA_HD // LANES):
            sl = slice(p * LANES, (p + 1) * LANES)
            kr = _rope64(z[:, C_K + p * LANES:C_K + (p + 1) * LANES], c64, s64, lower)
            v = z[:, C_V + p * LANES:C_V + (p + 1) * LANES]
            ko_ref[rows, sl] = kr
            kb_ref[rows, sl] = kr.astype(BF16)
            kbs_ref[rows, sl] = pltpu.roll(kr, 64, 1).astype(BF16)
            vo_ref[rows, sl] = v
            vb_ref[rows, sl] = v.astype(BF16)
            vbs_ref[rows, sl] = pltpu.roll(v, 64, 1).astype(BF16)


def c_proj_features(x, g, w, tabs, *, seq, tm, sub):
    m = x.shape[0]
    tm = min(tm, m)
    sub = min(sub, tm)
    c64, s64 = tabs
    if c64.shape[0] == 1:
        tab_spec = pl.BlockSpec((1, LANES), lambda i: (0, 0))
    else:
        nt = seq // tm
        tab_spec = pl.BlockSpec((tm, LANES), lambda i: (i % nt, 0))
    row = lambda wd: pl.BlockSpec((tm, wd), lambda i: (i, 0))
    return pl.pallas_call(
        functools.partial(_c_proj_feat_kernel, sub=sub),
        grid=(m // tm,),
        in_specs=[row(D_MODEL), pl.BlockSpec((1, D_MODEL), lambda i: (0, 0)),
                  pl.BlockSpec((D_MODEL, C_WIDTH), lambda i: (0, 0), pipeline_mode=pl.Buffered(1)),
                  tab_spec, tab_spec],
        out_specs=[row(2048), row(256), row(256), row(256), row(256), row(256), row(256)],
        out_shape=[jax.ShapeDtypeStruct((m, 2048), BF16),
                   jax.ShapeDtypeStruct((m, 256), F32),
                   jax.ShapeDtypeStruct((m, 256), BF16),
                   jax.ShapeDtypeStruct((m, 256), BF16),
                   jax.ShapeDtypeStruct((m, 256), F32),
                   jax.ShapeDtypeStruct((m, 256), BF16),
                   jax.ShapeDtypeStruct((m, 256), BF16)],
        compiler_params=_cparams(("parallel",)),
        name="c_proj_features",
    )(x, g.reshape(1, D_MODEL), w, c64, s64)


def _dot_nt(a, b, **kw):
    return lax.dot_general(a, b, (((1,), (1,)), ((), ())), preferred_element_type=F32, **kw)


def _dot_tn(a, b, **kw):
    return lax.dot_general(a, b, (((0,), (0,)), ((), ())), preferred_element_type=F32, **kw)


_HI = lax.Precision.HIGHEST


def _log_decay(sm, wgate, bgate):
    pre = jnp.dot(sm, wgate, preferred_element_type=F32, precision=_HI) + bgate
    return (jnp.minimum(pre, 0.0) - jnp.log1p(jnp.exp(-jnp.abs(pre)))) / GLA_GATE_TAU


def _gla_gate_out(o, r, gain):
    g = o * lax.rsqrt(jnp.mean(o * o, axis=-1, keepdims=True) + EPS) * gain
    return g * (r * jax.nn.sigmoid(r))


def _gla_prompt_kernel(gv_ref, gr_ref, gq_ref, gk_ref, sm_ref, wgate_ref, bgate_ref, gain_ref,
                       y_ref, st_ref, s_ref, b_ref, *, n_seq, n_chunks):
    c = pl.program_id(1)

    @pl.when(c == 0)
    def _():
        s_ref[...] = jnp.zeros_like(s_ref)

    tc = n_chunks * GLA_CHUNK
    ri = lax.broadcasted_iota(jnp.int32, (tc, tc), 0)
    ci = lax.broadcasted_iota(jnp.int32, (tc, tc), 1)
    tril = jnp.where(jnp.logical_and(ri // GLA_CHUNK == ci // GLA_CHUNK, ri >= ci), 1.0, 0.0).astype(F32)
    causal = (lax.broadcasted_iota(jnp.int32, (GLA_CHUNK, GLA_CHUNK), 0)
              >= lax.broadcasted_iota(jnp.int32, (GLA_CHUNK, GLA_CHUNK), 1))
    gain = gain_ref[...]
    for s in range(n_seq):
        la = _log_decay(sm_ref[s], wgate_ref[...], bgate_ref[...])
        b_ref[s] = jnp.dot(tril, la, preferred_element_type=F32, precision=_HI)
    for n in range(n_chunks):
        rows = slice(n * GLA_CHUNK, (n + 1) * GLA_CHUNK)
        for s in range(n_seq):
            b = b_ref[s, rows, :]
            b_end = b_ref[s, (n + 1) * GLA_CHUNK - 1:(n + 1) * GLA_CHUNK, :]
            k = gk_ref[s, rows, :]
            q_in = gq_ref[s, rows, :] * (GLA_DK ** -0.5) * jnp.exp(b)
            k_in = k * jnp.exp(-b)
            k_end = k * jnp.exp(b_end - b)
            decay = jnp.exp(b_end)
            for h in range(GLA_HEADS):
                dk = slice(h * GLA_DK, (h + 1) * GLA_DK)
                dv = slice(h * GLA_DV, (h + 1) * GLA_DV)
                v = gv_ref[s, rows, dv].astype(BF16)
                st = s_ref[s, h]
                qh = q_in[:, dk].astype(BF16)
                att = jnp.where(causal, _dot_nt(qh, k_in[:, dk].astype(BF16)), 0.0)
                o = _dot_nt(qh, st.astype(BF16)) + jnp.dot(att.astype(BF16), v, preferred_element_type=F32)
                s_ref[s, h] = st * decay[:, dk] + _dot_tn(v, k_end[:, dk].astype(BF16))
                y_ref[s, rows, dv] = _gla_gate_out(o, gr_ref[s, rows, dv], gain)

    @pl.when(c == pl.num_programs(1) - 1)
    def _():
        st_ref[...] = s_ref[...]


def gla_prompt(z, wgate, bgate, gain, *, batch, seq, tc, n_seq):
    m = z.shape[0]
    z3 = z.reshape(batch, seq, z.shape[1])
    col = lambda w, off: pl.BlockSpec((n_seq, tc, w), lambda b, c: (b, c, off // w))
    const = lambda shape: pl.BlockSpec(shape, lambda b, c: (0,) * len(shape))
    y, st = pl.pallas_call(
        functools.partial(_gla_prompt_kernel, n_seq=n_seq, n_chunks=tc // GLA_CHUNK),
        grid=(batch // n_seq, seq // tc),
        in_specs=[col(1024, AB_GV), col(1024, AB_GR), col(512, AB_GQ), col(512, AB_GK), col(128, AB_SMALL),
                  const((LANES, GLA_HEADS * GLA_DK)), const((1, GLA_HEADS * GLA_DK)), const((1, GLA_DV))],
        out_specs=[pl.BlockSpec((n_seq, tc, GLA_HEADS * GLA_DV), lambda b, c: (b, c, 0)),
                   pl.BlockSpec((n_seq, GLA_HEADS, GLA_DV, GLA_DK), lambda b, c: (b, 0, 0, 0))],
        out_shape=[jax.ShapeDtypeStruct((batch, seq, GLA_HEADS * GLA_DV), F32),
                   jax.ShapeDtypeStruct((batch, GLA_HEADS, GLA_DV, GLA_DK), F32)],
        scratch_shapes=[pltpu.VMEM((n_seq, GLA_HEADS, GLA_DV, GLA_DK), F32),
                        pltpu.VMEM((n_seq, tc, GLA_HEADS * GLA_DK), F32)],
        compiler_params=_cparams(("parallel", "arbitrary")),
        name="gla_prompt",
    )(z3, z3, z3, z3, z3, wgate, bgate, gain)
    return y.reshape(m, GLA_HEADS * GLA_DV), st


INT_MIN = -2 ** 31


def _order_key(score):
    score = jnp.where(score == 0.0, 0.0, score)
    bits = lax.bitcast_convert_type(score, jnp.int32)
    return jnp.where(bits < 0, bits ^ jnp.int32(0x7FFFFFFF), bits)


def _lane_total(x):
    return jnp.dot(x.astype(BF16), jnp.ones((LANES, LANES), BF16), preferred_element_type=F32)


ROW_SUB = 128


def _count_blocks(key_ref, rows, n_blocks, pred):
    acc = jnp.zeros((ROW_SUB, LANES), F32)
    for c in range(n_blocks):
        acc = acc + jnp.where(pred(key_ref[rows, c * LANES:(c + 1) * LANES]), 1.0, 0.0)
    return acc


def _masked_key():
    bits = int(np.array(NEG_INF, np.float32).view(np.int32))
    return bits ^ 0x7FFFFFFF


def _kth_largest_key(key_ref, t_ref, n_blocks, k, active):
    n_sub = key_ref.shape[0] // ROW_SUB
    t_ref[...] = jnp.full(t_ref.shape, INT_MIN, jnp.int32)

    def body(it, carry):
        step = lax.shift_left(jnp.int32(1), 31 - it)
        cands, accs = [], []
        for rb in range(n_sub):
            rows = slice(rb * ROW_SUB, (rb + 1) * ROW_SUB)
            cand = t_ref[rows, :] + step
            cands.append(cand)
            accs.append(_count_blocks(key_ref, rows, active(rb), lambda kc, cand=cand: kc >= cand))
        total = _lane_total(jnp.concatenate(accs, axis=0))
        for rb in range(n_sub):
            rows = slice(rb * ROW_SUB, (rb + 1) * ROW_SUB)
            skipped = float((n_blocks - active(rb)) * LANES)
            count = total[rows, :] + jnp.where(cands[rb] <= _masked_key(), skipped, 0.0)
            t_ref[rows, :] = jnp.where(count >= k, cands[rb], t_ref[rows, :])
        return carry

    lax.fori_loop(0, 32, body, 0)


def _select_topk(key_ref, t_ref, n_blocks, k, write_fn, active=None):
    if active is None:
        active = lambda rb: n_blocks
    _kth_largest_key(key_ref, t_ref, n_blocks, k, active)
    ri = lax.broadcasted_iota(jnp.int32, (LANES, LANES), 0)
    ci = lax.broadcasted_iota(jnp.int32, (LANES, LANES), 1)
    before = jnp.where(ri < ci, 1.0, 0.0).astype(BF16)
    for rb in range(key_ref.shape[0] // ROW_SUB):
        rows = slice(rb * ROW_SUB, (rb + 1) * ROW_SUB)
        t = t_ref[rows, :]
        skipped = float((n_blocks - active(rb)) * LANES)
        above = _lane_total(_count_blocks(key_ref, rows, active(rb), lambda kc: kc > t))
        need = k - above - jnp.where(t < _masked_key(), skipped, 0.0)
        run = jnp.zeros((ROW_SUB, LANES), F32)
        for c in range(n_blocks):
            if c >= active(rb):
                write_fn(rows, c, jnp.zeros((ROW_SUB, LANES), jnp.bool_))
                continue
            kc = key_ref[rows, c * LANES:(c + 1) * LANES]
            eq = jnp.where(kc == t, 1.0, 0.0)
            rank = jnp.dot(eq.astype(BF16), before, preferred_element_type=F32) + run
            take = jnp.where(kc > t, 1.0, jnp.where(rank < need, eq, 0.0))
            write_fn(rows, c, take > 0.0)
            run = run + _lane_total(eq)


def _half_mask(rows, upper):
    lane = lax.broadcasted_iota(jnp.int32, (rows, LANES), 1)
    return (lane >= 64) if upper else (lane < 64)


DSA_STRATUM = 512
KEY_CHUNK = 512


def _dsa_select_prompt_kernel(iq_ref, iw_ref, ikb_ref, bias_ref, key_ref, t_ref, *, row0, n_keys, topk):
    rows = iq_ref.shape[0]
    n_blocks = n_keys // LANES
    lane = lax.broadcasted_iota(jnp.int32, (ROW_SUB, LANES), 1)
    sub = lax.broadcasted_iota(jnp.int32, (ROW_SUB, LANES), 0)

    iw = iw_ref[...]
    qh = []
    for h in range(IDX_HEADS):
        pair = iq_ref[:, (h // 2) * LANES:(h // 2 + 1) * LANES]
        qh.append(jnp.where(_half_mask(rows, h % 2 == 1), pair, jnp.zeros_like(pair)))
    for kc in range(n_keys // KEY_CHUNK):
        ik = ikb_ref[kc * KEY_CHUNK:(kc + 1) * KEY_CHUNK, :]
        score = jnp.zeros((rows, KEY_CHUNK), F32)
        for h in range(IDX_HEADS):
            score = score + jnp.maximum(_dot_nt(qh[h], ik), 0.0) * iw[:, h:h + 1]
        for rb in range(rows // ROW_SUB):
            for cb in range(KEY_CHUNK // LANES):
                c = kc * (KEY_CHUNK // LANES) + cb
                causal = (c * LANES + lane) <= (row0 + rb * ROW_SUB + sub)
                part = score[rb * ROW_SUB:(rb + 1) * ROW_SUB, cb * LANES:(cb + 1) * LANES]
                key_ref[rb * ROW_SUB:(rb + 1) * ROW_SUB, c * LANES:(c + 1) * LANES] = _order_key(
                    jnp.where(causal, part, NEG_INF))

    def write(rws, c, sel):
        causal = (c * LANES + lane) <= (row0 + rws.start + sub)
        bias_ref[rws, c * LANES:(c + 1) * LANES] = jnp.where(jnp.logical_and(sel, causal), 0.0, NEG_INF)

    _select_topk(key_ref, t_ref, n_blocks, topk, write, active=lambda rb: row0 // LANES + rb + 1)


def _dsa_attend_prompt_kernel(q_ref, bias_ref, kb_ref, vb_ref, y_ref):
    bias = bias_ref[...]
    group = DSA_HEADS // DSA_KV_HEADS
    for kv in range(DSA_KV_HEADS):
        kvs = slice(kv * DSA_HD, (kv + 1) * DSA_HD)
        q4 = jnp.concatenate([q_ref[:, (kv * group + g) * DSA_HD:(kv * group + g + 1) * DSA_HD] for g in range(group)],
                             axis=0)
        s4 = _dot_nt(q4, kb_ref[:, kvs])
        ps, dens = [], []
        for g in range(group):
            s = s4[g * Q_BLOCK:(g + 1) * Q_BLOCK, :] + bias
            p = jnp.exp(s - jnp.max(s, axis=-1, keepdims=True))
            dens.append(jnp.sum(p, axis=-1, keepdims=True))
            ps.append(p.astype(BF16))
        o4 = jnp.dot(jnp.concatenate(ps, axis=0), vb_ref[:, kvs], preferred_element_type=F32)
        for g in range(group):
            h = kv * group + g
            y_ref[:, h * DSA_HD:(h + 1) * DSA_HD] = o4[g * Q_BLOCK:(g + 1) * Q_BLOCK, :] / dens[g]


def dsa_prompt(q, iq, iw, kb, vb, ikb, *, batch, seq):
    topk = min(DSA_TOPK_MAX, seq // 4)
    as3 = lambda a: a.reshape(batch, seq, a.shape[-1])
    q, iq, iw, kb, vb, ikb = (as3(a) for a in (q, iq, iw, kb, vb, ikb))
    nsub = DSA_STRATUM // Q_BLOCK
    outs = []
    for r in range(seq // DSA_STRATUM):
        n_keys = (r + 1) * DSA_STRATUM
        strat = lambda w: pl.BlockSpec((None, DSA_STRATUM, w), lambda b: (b, r, 0))
        keys = lambda w: pl.BlockSpec((None, n_keys, w), lambda b: (b, 0, 0))
        bias = pl.pallas_call(
            functools.partial(_dsa_select_prompt_kernel, row0=r * DSA_STRATUM, n_keys=n_keys, topk=topk),
            grid=(batch,),
            in_specs=[strat(512), strat(128), keys(128)],
            out_specs=pl.BlockSpec((None, DSA_STRATUM, n_keys), lambda b: (b, 0, 0)),
            out_shape=jax.ShapeDtypeStruct((batch, DSA_STRATUM, n_keys), F32),
            scratch_shapes=[pltpu.VMEM((DSA_STRATUM, n_keys), jnp.int32), pltpu.VMEM((DSA_STRATUM, LANES), jnp.int32)],
            compiler_params=_cparams(("parallel",)),
            name="dsa_select_prompt",
        )(iq, iw, ikb)
        qrow = lambda w: pl.BlockSpec((None, Q_BLOCK, w), lambda b, i: (b, r * nsub + i, 0))
        keys2 = lambda w: pl.BlockSpec((None, n_keys, w), lambda b, i: (b, 0, 0))
        outs.append(pl.pallas_call(
            _dsa_attend_prompt_kernel,
            grid=(batch, nsub),
            in_specs=[qrow(1024), pl.BlockSpec((None, Q_BLOCK, n_keys), lambda b, i: (b, i, 0)), keys2(256), keys2(256)],
            out_specs=pl.BlockSpec((None, Q_BLOCK, DSA_HEADS * DSA_HD), lambda b, i: (b, i, 0)),
            out_shape=jax.ShapeDtypeStruct((batch, DSA_STRATUM, DSA_HEADS * DSA_HD), F32),
            compiler_params=_cparams(("parallel", "arbitrary")),
            name="dsa_attend_prompt",
        )(q, bias, kb, vb))
    return jnp.concatenate(outs, axis=1).reshape(batch * seq, DSA_HEADS * DSA_HD)


def _swa_head_plan(h):
    group = SWA_HEADS // SWA_KV_HEADS
    kv = h // group
    return h // 2, h % 2, kv // 2, (kv % 2) != (h % 2)


def _swa_prompt_kernel(sink_ref, q_ref, kp_ref, kc_ref, kps_ref, kcs_ref, vp_ref, vc_ref, vps_ref, vcs_ref, y_ref):
    i = pl.program_id(1)
    r = lax.broadcasted_iota(jnp.int32, (Q_BLOCK, 2 * Q_BLOCK), 0)
    c = lax.broadcasted_iota(jnp.int32, (Q_BLOCK, 2 * Q_BLOCK), 1)
    rel = Q_BLOCK + r - c
    ok = (rel >= 0) & (rel <= WINDOW) & ((i - 1) * Q_BLOCK + c >= 0)
    bias = jnp.where(ok, 0.0, NEG_INF)
    keys = (jnp.concatenate([kp_ref[...], kc_ref[...]], axis=0), jnp.concatenate([kps_ref[...], kcs_ref[...]], axis=0))
    vals = (jnp.concatenate([vp_ref[...], vc_ref[...]], axis=0), jnp.concatenate([vps_ref[...], vcs_ref[...]], axis=0))
    lower = _half_mask(Q_BLOCK, False)
    for p in range(SWA_HEADS // 2):
        qpair = q_ref[:, p * LANES:(p + 1) * LANES]
        outs = []
        for h in (2 * p, 2 * p + 1):
            _, half, ks, swapped = _swa_head_plan(h)
            qh = jnp.where(_half_mask(Q_BLOCK, half == 1), qpair, jnp.zeros_like(qpair))
            kk = keys[int(swapped)][:, ks * LANES:(ks + 1) * LANES]
            vv = vals[int(swapped)][:, ks * LANES:(ks + 1) * LANES]
            s = _dot_nt(qh, kk) + bias
            sink = sink_ref[h]
            mx = jnp.maximum(jnp.max(s, axis=-1, keepdims=True), sink)
            pr = jnp.exp(s - mx)
            den = jnp.sum(pr, axis=-1, keepdims=True) + jnp.exp(sink - mx)
            outs.append(jnp.dot(pr.astype(BF16), vv, preferred_element_type=F32) / den)
        y_ref[:, p * LANES:(p + 1) * LANES] = jnp.where(lower, outs[0], outs[1])


def swa_prompt(q, kb, kbs, vb, vbs, sinks, *, batch, seq):
    m = q.shape[0]
    nq = seq // Q_BLOCK
    cur = pl.BlockSpec((Q_BLOCK, 256), lambda b, i, s: (b * nq + i, 0))
    prev = pl.BlockSpec((Q_BLOCK, 256), lambda b, i, s: (b * nq + jnp.maximum(i - 1, 0), 0))
    qspec = pl.BlockSpec((Q_BLOCK, 2048), lambda b, i, s: (b * nq + i, 0))
    return pl.pallas_call(
        _swa_prompt_kernel,
        grid_spec=pltpu.PrefetchScalarGridSpec(
            num_scalar_prefetch=1,
            grid=(batch, nq),
            in_specs=[qspec, prev, cur, prev, cur, prev, cur, prev, cur],
            out_specs=qspec,
        ),
        out_shape=jax.ShapeDtypeStruct((m, SWA_HEADS * SWA_HD), F32),
        compiler_params=_cparams(("parallel", "arbitrary")),
        name="swa_prompt",
    )(sinks, q, kb, kb, kbs, kbs, vb, vb, vbs, vbs)


def _mem_prompt_kernel(x_ref, g_ref, wq_ref, kv_ref, wo_ref, *rest, n_in):
    y_refs, w_refs, o_ref = rest[:n_in], rest[n_in:2 * n_in], rest[2 * n_in]
    x = x_ref[...]
    for y_ref, w_ref in zip(y_refs, w_refs):
        x = x + jnp.dot(y_ref[...].astype(BF16), w_ref[...], preferred_element_type=F32)
    h = _rms(x, g_ref[...]).astype(BF16)
    q = (jnp.dot(h, wq_ref[...], preferred_element_type=F32) * (MEM_HD ** -0.5)).astype(BF16)
    width = MEM_HEADS * MEM_HD
    outs = []
    for hd in range(MEM_HEADS):
        sl = slice(hd * MEM_HD, (hd + 1) * MEM_HD)
        k = kv_ref[:, sl].astype(BF16)
        v = kv_ref[:, width + hd * MEM_HD:width + (hd + 1) * MEM_HD].astype(BF16)
        s = _dot_nt(q[:, sl], k)
        p = jnp.exp(s - jnp.max(s, axis=-1, keepdims=True))
        o = jnp.dot(p.astype(BF16), v, preferred_element_type=F32) / jnp.sum(p, axis=-1, keepdims=True)
        outs.append(o.astype(BF16))
    o_ref[...] = x + jnp.dot(jnp.concatenate(outs, axis=-1), wo_ref[...], preferred_element_type=F32)


def mem_attn_prompt(x, ys, ws, g, wq, kv, wo, *, batch, seq, tq):
    m = x.shape[0]
    nq = seq // tq
    width = MEM_HEADS * MEM_HD
    rows = lambda w: pl.BlockSpec((tq, w), lambda b, i: (b * nq + i, 0))
    const = lambda shape: pl.BlockSpec(shape, lambda b, i: (0, 0), pipeline_mode=pl.Buffered(1))
    return pl.pallas_call(
        functools.partial(_mem_prompt_kernel, n_in=len(ys)),
        grid=(batch, nq),
        in_specs=[rows(D_MODEL),
                  pl.BlockSpec((1, D_MODEL), lambda b, i: (0, 0)),
                  const((D_MODEL, width)),
                  pl.BlockSpec((N_MEM, 2 * width), lambda b, i: (b, 0)),
                  const((width, D_MODEL))]
                 + [rows(y.shape[1]) for y in ys] + [const(w.shape) for w in ws],
        out_specs=rows(D_MODEL),
        out_shape=jax.ShapeDtypeStruct((m, D_MODEL), F32),
        compiler_params=_cparams(("parallel", "arbitrary")),
        name="mem_attn_prompt",
    )(x, g.reshape(1, D_MODEL), wq, kv, wo, *ys, *ws)


def _gla_prep_kernel(gq_ref, gk_ref, sm_ref, wgate_ref, bgate_ref, o_ref):
    w = GLA_HEADS * GLA_DK
    la = _log_decay(sm_ref[...], wgate_ref[...], bgate_ref[...])
    o_ref[:, 0:w] = jnp.exp(la)
    o_ref[:, w:2 * w] = gk_ref[...]
    o_ref[:, 2 * w:3 * w] = gq_ref[...] * (GLA_DK ** -0.5)
    o_ref[:, 3 * w:4 * w] = jnp.zeros((gq_ref.shape[0], w), F32)


def gla_sample_prep(z, wgate, bgate):
    m = z.shape[0]
    w = GLA_HEADS * GLA_DK
    col = lambda wd, off: pl.BlockSpec((m, wd), lambda i: (0, off // wd))
    return pl.pallas_call(
        _gla_prep_kernel,
        grid=(1,),
        in_specs=[col(512, AB_GQ), col(512, AB_GK), col(128, AB_SMALL),
                  pl.BlockSpec((LANES, w), lambda i: (0, 0)), pl.BlockSpec((1, w), lambda i: (0, 0))],
        out_specs=pl.BlockSpec((m, 4 * w), lambda i: (0, 0)),
        out_shape=jax.ShapeDtypeStruct((m, 4 * w), F32),
        compiler_params=_cparams(("arbitrary",)),
        name="gla_sample_prep",
    )(z, z, z, wgate, bgate)


def _gla_step_kernel(p_ref, gv_ref, gr_ref, s_ref, gain_ref, so_ref, y_ref, *, bs):
    gain = gain_ref[...]
    for s in range(bs):
        xt = p_ref[s].T
        for h in range(GLA_HEADS):
            dv = slice(h * GLA_DV, (h + 1) * GLA_DV)
            st = s_ref[s, h] * xt[:, h:h + 1] + xt[:, GLA_HEADS + h:GLA_HEADS + h + 1] * gv_ref[s:s + 1, dv]
            so_ref[s, h] = st
            o = jnp.sum(xt[:, 2 * GLA_HEADS + h:2 * GLA_HEADS + h + 1] * st, axis=0, keepdims=True)
            y_ref[s:s + 1, dv] = _gla_gate_out(o, gr_ref[s:s + 1, dv], gain)


def gla_sample(p, z, state, gain, *, bs):
    m = z.shape[0]
    col = lambda w, off: pl.BlockSpec((bs, w), lambda i: (i, off // w))
    sspec = pl.BlockSpec((bs, GLA_HEADS, GLA_DK, GLA_DV), lambda i: (i, 0, 0, 0))
    return pl.pallas_call(
        functools.partial(_gla_step_kernel, bs=bs),
        grid=(m // bs,),
        in_specs=[pl.BlockSpec((bs, 16, GLA_DK), lambda i: (i, 0, 0)), col(1024, AB_GV), col(1024, AB_GR), sspec,
                  pl.BlockSpec((1, GLA_DV), lambda i: (0, 0))],
        out_specs=[sspec, pl.BlockSpec((bs, GLA_HEADS * GLA_DV), lambda i: (i, 0))],
        out_shape=[jax.ShapeDtypeStruct(state.shape, F32), jax.ShapeDtypeStruct((m, GLA_HEADS * GLA_DV), F32)],
        compiler_params=_cparams(("parallel",)),
        name="gla_sample",
    )(p, z, z, state, gain)


def _dsa_scores_kernel(pt_ref, iq_ref, iw_ref, pool_hbm, o_ref, buf, sem, *, page_base, n_pages, depth):
    b = pl.program_id(0)
    nb = pl.num_programs(0)

    def page_copies(row):
        slot = row % (depth + 1)
        return [pltpu.make_async_copy(pool_hbm.at[page_base + pt_ref[row, j]], buf.at[slot, j], sem.at[slot])
                for j in range(n_pages)]

    @pl.when(b == 0)
    def _():
        for ahead in range(depth):
            for cp in page_copies(ahead):
                cp.start()

    @pl.when(b + depth < nb)
    def _():
        for cp in page_copies(b + depth):
            cp.start()

    for cp in page_copies(b):
        cp.wait()
    slot = b % (depth + 1)
    q8 = iq_ref[0]
    iw = iw_ref[0]
    for j in range(n_pages):
        dots = jnp.dot(q8, buf[slot, j].astype(BF16), preferred_element_type=F32)
        o_ref[0, j:j + 1, :] = jnp.sum(jnp.maximum(dots, 0.0) * iw, axis=0, keepdims=True)


def dsa_sample_scores(page_table, iq, iw, pool_ik_t, page_base, *, depth):
    m, n_pages = page_table.shape
    assert m > depth
    return pl.pallas_call(
        functools.partial(_dsa_scores_kernel, page_base=page_base, n_pages=n_pages, depth=depth),
        grid_spec=pltpu.PrefetchScalarGridSpec(
            num_scalar_prefetch=1,
            grid=(m,),
            in_specs=[pl.BlockSpec((1, IDX_HEADS, IDX_HD), lambda b, pt: (b, 0, 0)),
                      pl.BlockSpec((1, IDX_HEADS, 1), lambda b, pt: (b, 0, 0)),
                      pl.BlockSpec(memory_space=pl.ANY)],
            out_specs=pl.BlockSpec((1, n_pages, PAGE_SIZE), lambda b, pt: (b, 0, 0)),
            scratch_shapes=[pltpu.VMEM((depth + 1, n_pages, IDX_HD, PAGE_SIZE), F32),
                            pltpu.SemaphoreType.DMA((depth + 1,))],
        ),
        out_shape=jax.ShapeDtypeStruct((m, n_pages, PAGE_SIZE), F32),
        compiler_params=_cparams(("arbitrary",)),
        name="dsa_sample_scores",
    )(page_table, iq, iw, pool_ik_t)


def _dsa_select_sample_kernel(sc_ref, iq_ref, iw_ref, ikb_ref, bias_ref, bnew_ref, key_ref, t_ref, *, n_past, topk):
    rows = sc_ref.shape[0]
    n_blocks = n_past // LANES
    for c in range(n_blocks):
        sl = slice(c * LANES, (c + 1) * LANES)
        key_ref[:, sl] = _order_key(sc_ref[:, sl])
    ik = ikb_ref[...].astype(F32)
    iw = iw_ref[...]
    s_new = jnp.zeros((rows, 1), F32)
    for h in range(IDX_HEADS):
        pair = iq_ref[:, (h // 2) * LANES:(h // 2 + 1) * LANES].astype(F32)
        qh = jnp.where(_half_mask(rows, h % 2 == 1), pair, 0.0)
        s_new = s_new + jnp.maximum(jnp.sum(qh * ik, axis=-1, keepdims=True), 0.0) * iw[:, h:h + 1]
    lane = lax.broadcasted_iota(jnp.int32, (rows, LANES), 1)
    key_ref[:, n_past:n_past + LANES] = _order_key(jnp.where(lane == 0, s_new, -jnp.inf))

    ri = lax.broadcasted_iota(jnp.int32, (LANES, DSA_KV_HEADS * LANES), 0)
    ci = lax.broadcasted_iota(jnp.int32, (LANES, DSA_KV_HEADS * LANES), 1)
    spread = jnp.where(ci // DSA_KV_HEADS == ri, 1.0, 0.0).astype(BF16)

    def write(rws, c, sel):
        if c == n_blocks:
            bnew_ref[rws, :] = jnp.where(sel, 0.0, NEG_INF)
        else:
            wide = jnp.dot(jnp.where(sel, 1.0, 0.0).astype(BF16), spread, preferred_element_type=F32)
            w = DSA_KV_HEADS * LANES
            bias_ref[rws, c * w:(c + 1) * w] = jnp.where(wide > 0.5, 0.0, NEG_INF)

    _select_topk(key_ref, t_ref, n_blocks + 1, topk, write)


def dsa_sample_select(scores, iq, iw, ikb, *, topk):
    m, n_past = scores.shape
    full = lambda a: pl.BlockSpec(a.shape, lambda i: (0,) * a.ndim)
    wide = DSA_KV_HEADS * n_past
    return pl.pallas_call(
        functools.partial(_dsa_select_sample_kernel, n_past=n_past, topk=topk),
        grid=(1,),
        in_specs=[full(scores), full(iq), full(iw), full(ikb)],
        out_specs=[pl.BlockSpec((m, wide), lambda i: (0, 0)), pl.BlockSpec((m, LANES), lambda i: (0, 0))],
        out_shape=[jax.ShapeDtypeStruct((m, wide), F32), jax.ShapeDtypeStruct((m, LANES), F32)],
        scratch_shapes=[pltpu.VMEM((m, n_past + LANES), jnp.int32), pltpu.VMEM((m, LANES), jnp.int32)],
        compiler_params=_cparams(("arbitrary",)),
        name="dsa_sample_select",
    )(scores, iq, iw, ikb)


def _dsa_attend_sample_kernel(pt_ref, q_ref, bias_ref, bnew_ref, kn_ref, vn_ref, pk_hbm, pv_hbm, y_ref,
                              kbuf, vbuf, sem, *, page_base, n_pages, chunk):
    b = pl.program_id(0)
    wide = DSA_KV_HEADS * PAGE_SIZE
    n_chunks = n_pages // chunk
    group = DSA_HEADS // DSA_KV_HEADS

    def page_copies(row, c, slot):
        out = []
        for j in range(chunk):
            src = pl.ds(pl.multiple_of((page_base + pt_ref[row, c * chunk + j]) * wide, wide), wide)
            dst = pl.ds(j * wide, wide)
            out.append(pltpu.make_async_copy(pk_hbm.at[src, :], kbuf.at[slot, dst, :], sem.at[0, slot]))
            out.append(pltpu.make_async_copy(pv_hbm.at[src, :], vbuf.at[slot, dst, :], sem.at[1, slot]))
        return out

    @pl.when(b == 0)
    def _():
        for c in range(AHEAD_ATTEND):
            for cp in page_copies(0, c, c):
                cp.start()

    q8 = q_ref[0]
    first = lax.broadcasted_iota(jnp.int32, (DSA_HEADS, DSA_HD), 0) < group
    hrow = lax.broadcasted_iota(jnp.int32, (DSA_HEADS, wide), 0)
    col = lax.broadcasted_iota(jnp.int32, (DSA_HEADS, wide), 1)
    own = (col % DSA_KV_HEADS) == (hrow // group)
    m_run = jnp.full((DSA_HEADS, DSA_HD), NEG_INF, F32)
    l_run = jnp.zeros((DSA_HEADS, DSA_HD), F32)
    acc = jnp.zeros((DSA_HEADS, DSA_HD), F32)
    for c in range(n_chunks):
        slot = c
        ahead = c + AHEAD_ATTEND
        if ahead < n_chunks:
            for cp in page_copies(b, ahead, ahead):
                cp.start()
        else:
            @pl.when(b + 1 < pl.num_programs(0))
            def _():
                for cp in page_copies(b + 1, ahead - n_chunks, ahead - n_chunks):
                    cp.start()
        for cp in page_copies(b, c, slot):
            cp.wait()
        ss, oks = [], []
        for j in range(chunk):
            ok = jnp.logical_and(own, bias_ref[0, c * chunk + j:c * chunk + j + 1, :] == 0.0)
            kp = kbuf[slot, j * wide:(j + 1) * wide, :].astype(BF16)
            ss.append(jnp.where(ok, _dot_nt(q8, kp), NEG_INF))
            oks.append(ok)
        mx = ss[0]
        for s in ss[1:]:
            mx = jnp.maximum(mx, s)
        m_new = jnp.maximum(m_run, jnp.max(mx, axis=-1, keepdims=True))
        psum = jnp.zeros((DSA_HEADS, wide), F32)
        pv = jnp.zeros((DSA_HEADS, DSA_HD), F32)
        for j in range(chunk):
            p = jnp.where(oks[j], jnp.exp(ss[j] - m_new[:, 0:1]), 0.0)
            psum = psum + p
            vp = vbuf[slot, j * wide:(j + 1) * wide, :].astype(BF16)
            pv = pv + jnp.dot(p.astype(BF16), vp, preferred_element_type=F32)
        alpha = jnp.exp(m_run - m_new)
        l_run = alpha * l_run + jnp.sum(psum, axis=-1, keepdims=True)
        acc = alpha * acc + pv
        m_run = m_new

    kn = kn_ref[0].astype(BF16).astype(F32)
    vn = vn_ref[0].astype(BF16).astype(F32)
    bn = bnew_ref[0][:, 0:1]
    s_new = jnp.sum(q8.astype(F32) * jnp.where(first, kn[:, :DSA_HD], kn[:, DSA_HD:]), axis=-1, keepdims=True) + bn
    m_new = jnp.maximum(m_run, s_new)
    p_new = jnp.where(bn == 0.0, jnp.exp(s_new - m_new), 0.0)
    alpha = jnp.exp(m_run - m_new)
    acc = alpha * acc + p_new * jnp.where(first, vn[:, :DSA_HD], vn[:, DSA_HD:])
    y_ref[0] = acc / (alpha * l_run + p_new)


def dsa_sample_attend(page_table, q, bias, bnew, kn, vn, pool_k, pool_v, page_base, *, chunk):
    m, n_pages = page_table.shape
    wide = DSA_KV_HEADS * PAGE_SIZE
    n_chunks = n_pages // chunk
    assert n_pages % chunk == 0 and 0 < AHEAD_ATTEND < n_chunks
    per_b = lambda shape: pl.BlockSpec((1,) + shape, lambda b, pt: (b, 0, 0))
    hbm = pl.BlockSpec(memory_space=pl.ANY)
    return pl.pallas_call(
        functools.partial(_dsa_attend_sample_kernel, page_base=page_base, n_pages=n_pages, chunk=chunk),
        grid_spec=pltpu.PrefetchScalarGridSpec(
            num_scalar_prefetch=1,
            grid=(m,),
            in_specs=[per_b((DSA_HEADS, DSA_HD)), per_b((n_pages, wide)),
                      per_b((1, LANES)), per_b((1, DSA_KV_HEADS * DSA_HD)), per_b((1, DSA_KV_HEADS * DSA_HD)),
                      hbm, hbm],
            out_specs=per_b((DSA_HEADS, DSA_HD)),
            scratch_shapes=[pltpu.VMEM((n_chunks, chunk * wide, DSA_HD), F32),
                            pltpu.VMEM((n_chunks, chunk * wide, DSA_HD), F32),
                            pltpu.SemaphoreType.DMA((2, n_chunks))],
        ),
        out_shape=jax.ShapeDtypeStruct((m, DSA_HEADS, DSA_HD), F32),
        compiler_params=_cparams(("arbitrary",)),
        name="dsa_sample_attend",
    )(page_table, q, bias, bnew, kn, vn, pool_k, pool_v)


def _swa_step_kernel(q_ref, kt_ref, vt_ref, kn_ref, vn_ref, sink_ref, y_ref, *, bs):
    group = SWA_HEADS // SWA_KV_HEADS
    pairs = [(s, kv) for s in range(bs) for kv in range(SWA_KV_HEADS)]
    qs, scs, news = [], [], []
    for s, kv in pairs:
        qv = q_ref[s, kv * group:(kv + 1) * group, :]
        kn = kn_ref[s, kv:kv + 1, :].astype(BF16).astype(F32)
        scs.append(jnp.dot(qv, kt_ref[s, kv].astype(BF16), preferred_element_type=F32))
        news.append(jnp.sum(qv.astype(F32) * kn, axis=-1, keepdims=True))
    sc = jnp.concatenate(scs, axis=0)
    s_new = jnp.concatenate(news, axis=0)
    sink = jnp.concatenate([sink_ref[...]] * bs, axis=0)
    mx = jnp.maximum(jnp.maximum(jnp.max(sc, axis=-1, keepdims=True), s_new), sink)
    p = jnp.exp(sc - mx)
    p_new = jnp.exp(s_new - mx)
    inv = 1.0 / (jnp.sum(p, axis=-1, keepdims=True) + p_new + jnp.exp(sink - mx))
    pb = p.astype(BF16)
    for n, (s, kv) in enumerate(pairs):
        rows = slice(n * group, (n + 1) * group)
        vn = vn_ref[s, kv:kv + 1, :].astype(BF16).astype(F32)
        o = _dot_nt(pb[rows, :], vt_ref[s, kv].astype(BF16)) + p_new[rows, :] * vn
        y_ref[s, kv * group:(kv + 1) * group, :] = o * inv[rows, :]


def swa_sample(q, kt, vt, kn, vn, sinks, *, bs):
    m = q.shape[0]
    cache = pl.BlockSpec((bs, SWA_KV_HEADS, SWA_HD, WINDOW), lambda i: (i, 0, 0, 0))
    new = pl.BlockSpec((bs, SWA_KV_HEADS, SWA_HD), lambda i: (i, 0, 0))
    return pl.pallas_call(
        functools.partial(_swa_step_kernel, bs=bs),
        grid=(m // bs,),
        in_specs=[pl.BlockSpec((bs, SWA_HEADS, SWA_HD), lambda i: (i, 0, 0)), cache, cache, new, new,
                  pl.BlockSpec((SWA_HEADS, 1), lambda i: (0, 0))],
        out_specs=pl.BlockSpec((bs, SWA_HEADS, SWA_HD), lambda i: (i, 0, 0)),
        out_shape=jax.ShapeDtypeStruct((m, SWA_HEADS, SWA_HD), F32),
        compiler_params=_cparams(("parallel",)),
        name="swa_sample",
    )(q, kt, vt, kn, vn, sinks.reshape(SWA_HEADS, 1))


def _mem_step_kernel(q_ref, k_ref, v_ref, y_ref, *, bs):
    rows = q_ref.shape[1]
    n = N_MEM * MEM_HEADS
    hrow = lax.broadcasted_iota(jnp.int32, (rows, n), 0)
    col = lax.broadcasted_iota(jnp.int32, (rows, n), 1)
    own = (col % MEM_HEADS) == (hrow % MEM_HEADS)
    for s in range(bs):
        q = (q_ref[s] * (MEM_HD ** -0.5)).astype(BF16)
        sc = jnp.where(own, _dot_nt(q, k_ref[s * n:(s + 1) * n, :].astype(BF16)), NEG_INF)
        p = jnp.where(own, jnp.exp(sc - jnp.max(sc, axis=-1, keepdims=True)), 0.0)
        o = jnp.dot(p.astype(BF16), v_ref[s * n:(s + 1) * n, :].astype(BF16), preferred_element_type=F32)
        y_ref[s] = o / jnp.sum(p, axis=-1, keepdims=True)


def mem_attn_sample(q, mk, mv, layer, *, bs):
    m, rows, _ = q.shape
    n = N_MEM * MEM_HEADS
    nb = m // bs
    cache = pl.BlockSpec((bs * n, MEM_HD), lambda i: (layer * nb + i, 0))
    return pl.pallas_call(
        functools.partial(_mem_step_kernel, bs=bs),
        grid=(nb,),
        in_specs=[pl.BlockSpec((bs, rows, MEM_HD), lambda i: (i, 0, 0)), cache, cache],
        out_specs=pl.BlockSpec((bs, rows, MEM_HD), lambda i: (i, 0, 0)),
        out_shape=jax.ShapeDtypeStruct((m, rows, MEM_HD), F32),
        compiler_params=_cparams(("parallel",)),
        name="mem_attn_sample",
    )(q, mk, mv)


TM_FFN, TF_FFN, SUB_FFN = 1024, 512, 512
TM_PROJ, TN_MEM = 1024, 512
TM_GLA_PROJ = 512
TM_OUT = 512
TC_GLA, NSEQ_GLA = 256, 2
TQ_MEM = 512
BS_SAMPLE = 8
DEPTH_SCORES = 2
PG_ATTEND = 16
AHEAD_ATTEND = 2


def _prep_w_in_ab(w):
    sizes = (GLA_HEADS * GLA_DK, GLA_HEADS * GLA_DK, GLA_HEADS * GLA_DV, GLA_HEADS * GLA_DV, GLA_GATE_RANK,
             DSA_HEADS * DSA_HD, DSA_KV_HEADS * DSA_HD, DSA_KV_HEADS * DSA_HD, IDX_HEADS * IDX_HD, IDX_HEADS, IDX_HD)
    offs = np.cumsum((0,) + sizes)
    gq, gk, gv, gr, gd, dq, dk, dv, iq, iw, ik = [w[:, int(offs[j]):int(offs[j + 1])] for j in range(len(sizes))]
    pad = lambda n: jnp.zeros((w.shape[0], n), w.dtype)
    small = jnp.concatenate([ik, gd, iw, pad(LANES - IDX_HD - GLA_GATE_RANK - IDX_HEADS)], axis=1)
    w_gla = jnp.concatenate([gv, gr, gq, gk, small], axis=1)
    w_dsa = jnp.concatenate([dq, dk, dv, iq, small], axis=1)
    return w_gla.astype(BF16), w_dsa.astype(BF16)


def _prep_gate(w_up):
    return jnp.zeros((LANES, GLA_HEADS * GLA_DK), F32).at[SM_GD:SM_GD + GLA_GATE_RANK].set(w_up)


def kernel(x_prompt, x_sample, mem_prompt, cache_dsa_k, cache_dsa_v, cache_dsa_idx_k, state_gla, cache_swa_k, cache_swa_v, cache_mem_k, cache_mem_v, page_table, norm_ffn, w_ffn_gate, w_ffn_up, w_ffn_down, norm_mix, w_in_ab, w_gla_gate_up, b_gla_gate, gla_out_norm, idx_k_norm, w_out_ab, w_in_c, swa_sinks, w_out_c, norm_mem_q, norm_mem_src, w_mem_q, w_mem_kv, w_mem_o, final_norm):
    depth = norm_mix.shape[0]
    bp, seq, _ = x_prompt.shape
    bs = x_sample.shape[0]
    n_pool = cache_dsa_k.shape[1]
    gla_w = GLA_HEADS * GLA_DV

    w_ab = [_prep_w_in_ab(w_in_ab[i]) for i in range(w_in_ab.shape[0])]
    w_gate = [_prep_gate(w_gla_gate_up[i]) for i in range(w_in_ab.shape[0])]
    w_oab = w_out_ab.astype(BF16)
    w_c, w_oc = w_in_c.astype(BF16), w_out_c.astype(BF16)
    w_mq, w_mkv, w_mo = w_mem_q.astype(BF16), w_mem_kv.astype(BF16), w_mem_o.astype(BF16)

    def rope_tabs(pos):
        return _rope_tables(pos, DSA_HD) + _rope_tables(pos, IDX_HD)

    ffn_bf16 = {}

    def ffn_pair(x, layer, half, last, sample):
        fin = final_norm if last else None
        if sample:
            y, *ffn_bf16[layer, half] = ffn(x, norm_ffn[layer, half], w_ffn_gate, w_ffn_up, w_ffn_down, (layer, half),
                                            fin, tm=TM_FFN, tf=TF_FFN, sub=SUB_FFN)
            return y
        return ffn(x, norm_ffn[layer, half], *ffn_bf16[layer, half], None, fin, tm=TM_FFN, tf=TF_FFN, sub=SUB_FFN)

    def prompt_group():
        tabs_p = rope_tabs(jnp.arange(seq))
        x = x_prompt.reshape(bp * seq, D_MODEL)
        mem = mem_prompt.reshape(bp * N_MEM, D_MODEL)
        mem_kv = [norm_proj(mem, norm_mem_src[l], w_mkv[l], tm=TM_PROJ, tn=TN_MEM) for l in range(depth)]
        st_ab_p, st_c_p = [], []
        for l in range(depth):
            i = l // 2
            x = ffn_pair(x, l, 0, False, False)
            if l % 2 == 0:
                z = norm_proj(x, norm_mix[l], w_ab[i][0], tm=TM_GLA_PROJ, tn=AB_WIDTH)
                q, k, kb, v, vb, iq, ik, ikb, iw = d_proj_features(x, norm_mix[l], w_ab[i][1], tabs_p, idx_k_norm[i],
                                                                   seq=seq, tm=TM_PROJ, sub=SUB_FFN)
                y_gla, st_t = gla_prompt(z, w_gate[i], b_gla_gate[i].reshape(1, -1), gla_out_norm[i].reshape(1, -1),
                                         batch=bp, seq=seq, tc=TC_GLA, n_seq=NSEQ_GLA)
                y_dsa = dsa_prompt(q, iq, iw, kb, vb, ikb, batch=bp, seq=seq)
                mix_out = ([y_gla, y_dsa], [w_oab[i, :gla_w], w_oab[i, gla_w:]])
                n_pg = seq // PAGE_SIZE
                st_ab_p.append((k.reshape(bp, n_pg, PAGE_SIZE, DSA_KV_HEADS, DSA_HD),
                                v.reshape(bp, n_pg, PAGE_SIZE, DSA_KV_HEADS, DSA_HD),
                                ik.reshape(bp, n_pg, PAGE_SIZE, IDX_HD),
                                jnp.swapaxes(st_t, 2, 3)))
            else:
                q, k, kb, kbs, v, vb, vbs = c_proj_features(x, norm_mix[l], w_c[i], tabs_p[2:], seq=seq,
                                                            tm=TM_PROJ, sub=SUB_FFN)
                y = swa_prompt(q, kb, kbs, vb, vbs, swa_sinks[i], batch=bp, seq=seq)
                mix_out = ([y], [w_oc[i]])
                st_c_p.append((k.reshape(bp, seq, SWA_KV_HEADS, SWA_HD)[:, -WINDOW:],
                               v.reshape(bp, seq, SWA_KV_HEADS, SWA_HD)[:, -WINDOW:]))
            x = mem_attn_prompt(x, *mix_out, norm_mem_q[l], w_mq[l], mem_kv[l], w_mo[l], batch=bp, seq=seq, tq=TQ_MEM)
            x = ffn_pair(x, l, 1, l == depth - 1, False)
        return x.reshape(bp, seq, D_MODEL), st_ab_p, st_c_p, mem_kv

    tabs_s = rope_tabs(PAST_LEN + jnp.arange(1))
    n_pages = page_table.shape[1]
    topk = min(DSA_TOPK_MAX, (PAST_LEN + 1) // 4)
    pool_k = cache_dsa_k.reshape(-1, DSA_HD)
    pool_v = cache_dsa_v.reshape(-1, DSA_HD)
    pool_ik_t = jnp.swapaxes(cache_dsa_idx_k, 2, 3).reshape(-1, IDX_HD, PAGE_SIZE)
    mem_k_rows = cache_mem_k.reshape(-1, MEM_HD)
    mem_v_rows = cache_mem_v.reshape(-1, MEM_HD)
    x = x_sample.reshape(bs, D_MODEL)
    st_ab_s, st_c_s = [], []
    for l in range(depth):
        i = l // 2
        x = ffn_pair(x, l, 0, False, True)
        if l % 2 == 0:
            z = norm_proj(x, norm_mix[l], w_ab[i][0], tm=TM_GLA_PROJ, tn=AB_WIDTH)
            q, k, kb, v_new, vb, iq, ik, ikb, iw = d_proj_features(x, norm_mix[l], w_ab[i][1], tabs_s, idx_k_norm[i],
                                                                   seq=1, tm=TM_PROJ, sub=SUB_FFN)
            p = gla_sample_prep(z, w_gate[i], b_gla_gate[i].reshape(1, -1)).reshape(bs, 16, GLA_DK)
            st_new, y_gla = gla_sample(p, z, state_gla[i], gla_out_norm[i].reshape(1, -1), bs=BS_SAMPLE)
            scores = dsa_sample_scores(page_table, iq.reshape(bs, IDX_HEADS, IDX_HD),
                                       iw[:, :IDX_HEADS].reshape(bs, IDX_HEADS, 1), pool_ik_t, i * n_pool, depth=DEPTH_SCORES)
            bias, bnew = dsa_sample_select(scores.reshape(bs, n_pages * PAGE_SIZE), iq, iw, ikb, topk=topk)
            y_dsa = dsa_sample_attend(
                page_table, q.reshape(bs, DSA_HEADS, DSA_HD), bias.reshape(bs, n_pages, DSA_KV_HEADS * PAGE_SIZE),
                bnew.reshape(bs, 1, LANES), k.reshape(bs, 1, 256), v_new.reshape(bs, 1, 256),
                pool_k, pool_v, i * n_pool, chunk=PG_ATTEND).reshape(bs, DSA_HEADS * DSA_HD)
            x = out_proj(x, [y_gla, y_dsa], [w_oab[i, :gla_w], w_oab[i, gla_w:]], tm=TM_OUT)
            st_ab_s.append((k.reshape(bs, 1, DSA_KV_HEADS, DSA_HD), v_new.reshape(bs, 1, DSA_KV_HEADS, DSA_HD),
                            ik.reshape(bs, 1, IDX_HD), st_new))
        else:
            q, k, kb, kbs, v_new, vb, vbs = c_proj_features(x, norm_mix[l], w_c[i], tabs_s[2:], seq=1,
                                                            tm=TM_PROJ, sub=SUB_FFN)
            y = swa_sample(q.reshape(bs, SWA_HEADS, SWA_HD), jnp.transpose(cache_swa_k[i], (0, 2, 3, 1)),
                           jnp.transpose(cache_swa_v[i], (0, 2, 3, 1)), k.reshape(bs, SWA_KV_HEADS, SWA_HD),
                           v_new.reshape(bs, SWA_KV_HEADS, SWA_HD), swa_sinks[i], bs=BS_SAMPLE)
            x = out_proj(x, [y.reshape(bs, SWA_HEADS * SWA_HD)], [w_oc[i]], tm=TM_OUT)
            st_c_s.append((jnp.concatenate([cache_swa_k[i][:, 1:], k.reshape(bs, 1, SWA_KV_HEADS, SWA_HD)], axis=1),
                           jnp.concatenate([cache_swa_v[i][:, 1:], v_new.reshape(bs, 1, SWA_KV_HEADS, SWA_HD)], axis=1)))
        qm = norm_proj(x, norm_mem_q[l], w_mq[l], tm=TM_PROJ, tn=TN_MEM).reshape(bs, MEM_HEADS, MEM_HD)
        qm = jnp.pad(qm, ((0, 0), (0, SUBLANES - MEM_HEADS), (0, 0)))
        om = mem_attn_sample(qm, mem_k_rows, mem_v_rows, l, bs=BS_SAMPLE)
        x = out_proj(x, [om[:, :MEM_HEADS].reshape(bs, MEM_HEADS * MEM_HD)], [w_mo[l]], tm=TM_OUT)
        x = ffn_pair(x, l, 1, l == depth - 1, True)
    y_sample = x.reshape(bs, 1, D_MODEL)

    y_prompt, st_ab_p, st_c_p, mem_kv = prompt_group()

    stk = lambda sts, j: jnp.stack([s[j] for s in sts])
    mw = MEM_HEADS * MEM_HD
    mem_k_p = jnp.stack([kv[:, :mw].reshape(bp, N_MEM, MEM_HEADS, MEM_HD) for kv in mem_kv])
    mem_v_p = jnp.stack([kv[:, mw:].reshape(bp, N_MEM, MEM_HEADS, MEM_HD) for kv in mem_kv])
    return (y_prompt, y_sample, stk(st_ab_p, 0), stk(st_ab_p, 1), stk(st_ab_p, 2),
            stk(st_ab_s, 0), stk(st_ab_s, 1), stk(st_ab_s, 2), stk(st_ab_p, 3), stk(st_ab_s, 3),
            stk(st_c_p, 0), stk(st_c_p, 1), stk(st_c_s, 0), stk(st_c_s, 1), mem_k_p, mem_v_p)
```

```python
import functools

import jax
import jax.numpy as jnp
import numpy as np
from jax import lax
from jax.experimental import pallas as pl
from jax.experimental.pallas import tpu as pltpu

F32 = jnp.float32
BF16 = jnp.bfloat16

D_MODEL = 2048
D_FF = 5632
EPS = 1e-6
ROPE_THETA = 10000.0
NEG_INF = -1e30
PAST_LEN = 8192
PAGE_SIZE = 128
Q_BLOCK = 128
GLA_HEADS, GLA_DK, GLA_DV = 4, 128, 256
GLA_GATE_RANK = 16
GLA_GATE_TAU = 16.0
GLA_CHUNK = 64
DSA_HEADS, DSA_KV_HEADS, DSA_HD = 8, 2, 128
IDX_HEADS, IDX_HD = 8, 64
DSA_TOPK_MAX = 256
SWA_HEADS, SWA_KV_HEADS, SWA_HD = 32, 4, 64
WINDOW = 128
MEM_HEADS, MEM_HD = 4, 128
N_MEM = 256

LANES = 128
SUBLANES = 8
VMEM_LIMIT_BYTES = 60000 * 1024
ELEM_SUB = 128

AB_GV, AB_GR, AB_GQ, AB_GK = 0, 1024, 2048, 2560
AB_SMALL = 3072
AB_WIDTH = 3200
D_DQ, D_DK, D_DV, D_IQ, D_SMALL, D_WIDTH = 0, 1024, 1280, 1536, 2048, 2176
SM_IK, SM_GD, SM_IW = 0, 64, 80
C_Q, C_K, C_V, C_WIDTH = 0, 2048, 2304, 2560


def _cparams(sem):
    return pltpu.CompilerParams(dimension_semantics=sem, vmem_limit_bytes=VMEM_LIMIT_BYTES)


def _rms(x, g):
    y = x * lax.rsqrt(jnp.mean(x * x, axis=-1, keepdims=True) + EPS)
    return y * g


def _ffn_kernel(x_ref, g_ref, wg_ref, wu_ref, wd_ref, fg_ref, o_ref, *rest, final_norm, sub, emit_cast):
    h_ref = rest[-1]
    j = pl.program_id(1)
    tm = x_ref.shape[0]
    if emit_cast:
        for src, dst in zip((wg_ref, wu_ref, wd_ref), rest[:3]):
            dst[...] = src[...].astype(BF16)
        wg_ref, wu_ref, wd_ref = rest[:3]

    esub = min(ELEM_SUB, tm)

    def row_group(r):
        return pl.ds(pl.multiple_of(r * esub, esub), esub)

    @pl.when(j == 0)
    def _():
        def body(r, carry):
            rows = row_group(r)
            h_ref[rows, :] = _rms(x_ref[rows, :], g_ref[...]).astype(BF16)
            o_ref[rows, :] = jnp.zeros((esub, D_MODEL), F32)
            return carry

        lax.fori_loop(0, tm // esub, body, 0)

    for r in range(tm // sub):
        rows = slice(r * sub, (r + 1) * sub)
        h = h_ref[rows, :]
        a = jnp.dot(h, wg_ref[...], preferred_element_type=F32)
        u = jnp.dot(h, wu_ref[...], preferred_element_type=F32)
        act = (a * jax.nn.sigmoid(a) * u).astype(BF16)
        o_ref[rows, :] += jnp.dot(act, wd_ref[...], preferred_element_type=F32)

    @pl.when(j == pl.num_programs(1) - 1)
    def _():
        def body(r, carry):
            rows = row_group(r)
            y = x_ref[rows, :] + 0.5 * o_ref[rows, :]
            if final_norm:
                y = _rms(y, fg_ref[...])
            o_ref[rows, :] = y
            return carry

        lax.fori_loop(0, tm // esub, body, 0)


def ffn(x, g, wg, wu, wd, sel, final_g=None, *, tm, tf, sub):
    m = x.shape[0]
    tm = min(tm, m)
    sub = min(sub, tm)
    fg = g if final_g is None else final_g
    emit_cast = sel is not None
    wspec = lambda shape, imap: pl.BlockSpec(shape, imap)
    if emit_cast:
        assert m == tm and wg.dtype == F32
        layer, half = sel
        w_in = [wspec((None, None, D_MODEL, tf), lambda i, j: (layer, half, 0, j)),
                wspec((None, None, D_MODEL, tf), lambda i, j: (layer, half, 0, j)),
                wspec((None, None, tf, D_MODEL), lambda i, j: (layer, half, j, 0))]
    else:
        assert wg.dtype == BF16 and wg.ndim == 2
        w_in = [wspec((D_MODEL, tf), lambda i, j: (0, j)), wspec((D_MODEL, tf), lambda i, j: (0, j)),
                wspec((tf, D_MODEL), lambda i, j: (j, 0))]
    out_specs = [pl.BlockSpec((tm, D_MODEL), lambda i, j: (i, 0))]
    out_shape = [jax.ShapeDtypeStruct((m, D_MODEL), F32)]
    if emit_cast:
        out_specs += [wspec((D_MODEL, tf), lambda i, j: (0, j)), wspec((D_MODEL, tf), lambda i, j: (0, j)),
                      wspec((tf, D_MODEL), lambda i, j: (j, 0))]
        out_shape += [jax.ShapeDtypeStruct((D_MODEL, D_FF), BF16), jax.ShapeDtypeStruct((D_MODEL, D_FF), BF16),
                      jax.ShapeDtypeStruct((D_FF, D_MODEL), BF16)]
    outs = pl.pallas_call(
        functools.partial(_ffn_kernel, final_norm=final_g is not None, sub=sub, emit_cast=emit_cast),
        grid=(m // tm, D_FF // tf),
        in_specs=[pl.BlockSpec((tm, D_MODEL), lambda i, j: (i, 0)), pl.BlockSpec((1, D_MODEL), lambda i, j: (0, 0))]
                 + w_in + [pl.BlockSpec((1, D_MODEL), lambda i, j: (0, 0))],
        out_specs=out_specs,
        out_shape=out_shape,
        scratch_shapes=[pltpu.VMEM((tm, D_MODEL), BF16)],
        compiler_params=_cparams(("parallel", "arbitrary")),
        name="ffn_cast" if emit_cast else "ffn",
    )(x, g.reshape(1, D_MODEL), wg, wu, wd, fg.reshape(1, D_MODEL))
    return tuple(outs) if emit_cast else outs[0]


def _proj_kernel(x_ref, g_ref, w_ref, o_ref, h_ref, *, sub):
    tm = x_ref.shape[0]

    @pl.when(pl.program_id(1) == 0)
    def _():
        for r in range(tm // sub):
            rows = slice(r * sub, (r + 1) * sub)
            h_ref[rows, :] = _rms(x_ref[rows, :], g_ref[...]).astype(BF16)

    o_ref[...] = jnp.dot(h_ref[...], w_ref[...], preferred_element_type=F32)


def norm_proj(x, g, w, *, tm, tn, sub=512):
    m, n = x.shape[0], w.shape[1]
    tm = min(tm, m)
    w_spec = (pl.BlockSpec((D_MODEL, tn), lambda i, j: (0, j), pipeline_mode=pl.Buffered(1)) if tn == n
              else pl.BlockSpec((D_MODEL, tn), lambda i, j: (0, j)))
    return pl.pallas_call(
        functools.partial(_proj_kernel, sub=min(sub, tm)),
        grid=(m // tm, n // tn),
        in_specs=[
            pl.BlockSpec((tm, D_MODEL), lambda i, j: (i, 0)),
            pl.BlockSpec((1, D_MODEL), lambda i, j: (0, 0)),
            w_spec,
        ],
        out_specs=pl.BlockSpec((tm, tn), lambda i, j: (i, j)),
        out_shape=jax.ShapeDtypeStruct((m, n), F32),
        scratch_shapes=[pltpu.VMEM((tm, D_MODEL), BF16)],
        compiler_params=_cparams(("parallel", "arbitrary")),
        name="norm_proj",
    )(x, g.reshape(1, D_MODEL), w)


def _outproj_kernel(*refs, n_in):
    x_ref = refs[0]
    y_refs = refs[1:1 + n_in]
    w_refs = refs[1 + n_in:1 + 2 * n_in]
    o_ref = refs[1 + 2 * n_in]
    acc = x_ref[...]
    for y_ref, w_ref in zip(y_refs, w_refs):
        acc = acc + jnp.dot(y_ref[...].astype(BF16), w_ref[...], preferred_element_type=F32)
    o_ref[...] = acc


def out_proj(x, ys, ws, *, tm):
    m = x.shape[0]
    tm = min(tm, m)
    n_in = len(ys)
    in_specs = [pl.BlockSpec((tm, D_MODEL), lambda i: (i, 0))]
    in_specs += [pl.BlockSpec((tm, y.shape[1]), lambda i: (i, 0)) for y in ys]
    in_specs += [pl.BlockSpec(w.shape, lambda i: (0, 0)) for w in ws]
    return pl.pallas_call(
        functools.partial(_outproj_kernel, n_in=n_in),
        grid=(m // tm,),
        in_specs=in_specs,
        out_specs=pl.BlockSpec((tm, D_MODEL), lambda i: (i, 0)),
        out_shape=jax.ShapeDtypeStruct((m, D_MODEL), F32),
        compiler_params=_cparams(("parallel",)),
        name="out_proj",
    )(x, *ys, *ws)


def _rope_tables(pos, hd):
    half = hd // 2
    inv = ROPE_THETA ** (-jnp.arange(half, dtype=F32) / half)
    ang = pos.astype(F32)[:, None] * inv[None, :]
    cos, sin = jnp.cos(ang), jnp.sin(ang)
    reps = LANES // hd
    return (jnp.concatenate([cos, cos] * reps, axis=-1),
            jnp.concatenate([-sin, sin] * reps, axis=-1))


def _rope128(x, cos, sin):
    return x * cos + pltpu.roll(x, 64, 1) * sin


def _rope64(x, cos, sin, lower):
    partner = jnp.where(lower, pltpu.roll(x, 96, 1), pltpu.roll(x, 32, 1))
    return x * cos + partner * sin


def _lower32_mask(rows):
    lane = lax.broadcasted_iota(jnp.int32, (rows, LANES), 1)
    return (lane % 64) < 32


def _d_proj_feat_kernel(x_ref, g_ref, w_ref, c128_ref, s128_ref, c64_ref, s64_ref, gik_ref,
                        q_ref, k_ref, kb_ref, v_ref, vb_ref, iqo_ref, ik_ref, ikb_ref, iw_ref, *, sub):
    tm = x_ref.shape[0]
    table_rows = c128_ref.shape[0]
    lower = _lower32_mask(sub)
    lane = lax.broadcasted_iota(jnp.int32, (sub, LANES), 1)
    for r in range(tm // sub):
        rows = slice(r * sub, (r + 1) * sub)
        trows = rows if table_rows > 1 else slice(0, 1)
        c128, s128 = c128_ref[trows, :], s128_ref[trows, :]
        c64, s64 = c64_ref[trows, :], s64_ref[trows, :]
        h = _rms(x_ref[rows, :], g_ref[...]).astype(BF16)
        z = jnp.dot(h, w_ref[...], preferred_element_type=F32)
        for hd in range(DSA_HEADS):
            sl = slice(hd * LANES, (hd + 1) * LANES)
            q_ref[rows, sl] = (_rope128(z[:, D_DQ + hd * LANES:D_DQ + (hd + 1) * LANES], c128, s128)
                               * (DSA_HD ** -0.5)).astype(BF16)
        for hd in range(DSA_KV_HEADS):
            sl = slice(hd * LANES, (hd + 1) * LANES)
            kr = _rope128(z[:, D_DK + hd * LANES:D_DK + (hd + 1) * LANES], c128, s128)
            v = z[:, D_DV + hd * LANES:D_DV + (hd + 1) * LANES]
            k_ref[rows, sl] = kr
            kb_ref[rows, sl] = kr.astype(BF16)
            v_ref[rows, sl] = v
            vb_ref[rows, sl] = v.astype(BF16)
        for p in range(IDX_HEADS * IDX_HD // LANES):
            sl = slice(p * LANES, (p + 1) * LANES)
            iqo_ref[rows, sl] = (_rope64(z[:, D_IQ + p * LANES:D_IQ + (p + 1) * LANES], c64, s64, lower)
                                 * (IDX_HD ** -0.5)).astype(BF16)
        sm = z[:, D_SMALL:D_SMALL + LANES]
        ik = jnp.where(lane < IDX_HD, sm, 0.0)
        ik = ik * lax.rsqrt(jnp.sum(ik * ik, axis=-1, keepdims=True) / IDX_HD + EPS) * gik_ref[...]
        ik = _rope64(ik, c64, s64, lower)
        ik_ref[rows, :] = ik[:, :IDX_HD]
        ikb_ref[rows, :] = jnp.where(lane < IDX_HD, ik, pltpu.roll(ik, 64, 1)).astype(BF16)
        iw_ref[rows, :] = pltpu.roll(sm, LANES - SM_IW, 1) * (IDX_HEADS ** -0.5)


def d_proj_features(x, g, w, tabs, gik, *, seq, tm, sub):
    m = x.shape[0]
    tm = min(tm, m)
    sub = min(sub, tm)
    c128, s128, c64, s64 = tabs
    if c128.shape[0] == 1:
        tab_spec = pl.BlockSpec((1, LANES), lambda i: (0, 0))
    else:
        nt = seq // tm
        tab_spec = pl.BlockSpec((tm, LANES), lambda i: (i % nt, 0))
    row = lambda wd: pl.BlockSpec((tm, wd), lambda i: (i, 0))
    gik_pad = jnp.zeros((1, LANES), F32).at[0, :IDX_HD].set(gik)
    return pl.pallas_call(
        functools.partial(_d_proj_feat_kernel, sub=sub),
        grid=(m // tm,),
        in_specs=[row(D_MODEL), pl.BlockSpec((1, D_MODEL), lambda i: (0, 0)),
                  pl.BlockSpec((D_MODEL, D_WIDTH), lambda i: (0, 0), pipeline_mode=pl.Buffered(1)),
                  tab_spec, tab_spec, tab_spec, tab_spec, pl.BlockSpec((1, LANES), lambda i: (0, 0))],
        out_specs=[row(1024), row(256), row(256), row(256), row(256), row(512), row(IDX_HD), row(128), row(128)],
        out_shape=[jax.ShapeDtypeStruct((m, 1024), BF16),
                   jax.ShapeDtypeStruct((m, 256), F32),
                   jax.ShapeDtypeStruct((m, 256), BF16),
                   jax.ShapeDtypeStruct((m, 256), F32),
                   jax.ShapeDtypeStruct((m, 256), BF16),
                   jax.ShapeDtypeStruct((m, 512), BF16),
                   jax.ShapeDtypeStruct((m, IDX_HD), F32),
                   jax.ShapeDtypeStruct((m, 128), BF16),
                   jax.ShapeDtypeStruct((m, 128), F32)],
        compiler_params=_cparams(("parallel",)),
        name="d_proj_features",
    )(x, g.reshape(1, D_MODEL), w, c128, s128, c64, s64, gik_pad)


def _c_proj_feat_kernel(x_ref, g_ref, w_ref, c64_ref, s64_ref,
                        qo_ref, ko_ref, kb_ref, kbs_ref, vo_ref, vb_ref, vbs_ref, *, sub):
    tm = x_ref.shape[0]
    table_rows = c64_ref.shape[0]
    lower = _lower32_mask(sub)
    for r in range(tm // sub):
        rows = slice(r * sub, (r + 1) * sub)
        trows = rows if table_rows > 1 else slice(0, 1)
        c64, s64 = c64_ref[trows, :], s64_ref[trows, :]
        h = _rms(x_ref[rows, :], g_ref[...]).astype(BF16)
        z = jnp.dot(h, w_ref[...], preferred_element_type=F32)
        for p in range(SWA_HEADS * SWA_HD // LANES):
            sl = slice(p * LANES, (p + 1) * LANES)
            qo_ref[rows, sl] = (_rope64(z[:, C_Q + p * LANES:C_Q + (p + 1) * LANES], c64, s64, lower)
                                * (SWA_HD ** -0.5)).astype(BF16)
        for p in range(SWA_KV_HEADS * SWA_HD // LANES):
            sl = slice(p * LANES, (p + 1) * LANES)
            kr = _rope64(z[:, C_K + p * LANES:C_K + (p + 1) * LANES], c64, s64, lower)
            v = z[:, C_V + p * LANES:C_V + (p + 1) * LANES]
            ko_ref[rows, sl] = kr
            kb_ref[rows, sl] = kr.astype(BF16)
            kbs_ref[rows, sl] = pltpu.roll(kr, 64, 1).astype(BF16)
            vo_ref[rows, sl] = v
            vb_ref[rows, sl] = v.astype(BF16)
            vbs_ref[rows, sl] = pltpu.roll(v, 64, 1).astype(BF16)


def c_proj_features(x, g, w, tabs, *, seq, tm, sub):
    m = x.shape[0]
    tm = min(tm, m)
    sub = min(sub, tm)
    c64, s64 = tabs
    if c64.shape[0] == 1:
        tab_spec = pl.BlockSpec((1, LANES), lambda i: (0, 0))
    else:
        nt = seq // tm
        tab_spec = pl.BlockSpec((tm, LANES), lambda i: (i % nt, 0))
    row = lambda wd: pl.BlockSpec((tm, wd), lambda i: (i, 0))
    return pl.pallas_call(
        functools.partial(_c_proj_feat_kernel, sub=sub),
        grid=(m // tm,),
        in_specs=[row(D_MODEL), pl.BlockSpec((1, D_MODEL), lambda i: (0, 0)),
                  pl.BlockSpec((D_MODEL, C_WIDTH), lambda i: (0, 0), pipeline_mode=pl.Buffered(1)),
                  tab_spec, tab_spec],
        out_specs=[row(2048), row(256), row(256), row(256), row(256), row(256), row(256)],
        out_shape=[jax.ShapeDtypeStruct((m, 2048), BF16),
                   jax.ShapeDtypeStruct((m, 256), F32),
                   jax.ShapeDtypeStruct((m, 256), BF16),
                   jax.ShapeDtypeStruct((m, 256), BF16),
                   jax.ShapeDtypeStruct((m, 256), F32),
                   jax.ShapeDtypeStruct((m, 256), BF16),
                   jax.ShapeDtypeStruct((m, 256), BF16)],
        compiler_params=_cparams(("parallel",)),
        name="c_proj_features",
    )(x, g.reshape(1, D_MODEL), w, c64, s64)


def _dot_nt(a, b, **kw):
    return lax.dot_general(a, b, (((1,), (1,)), ((), ())), preferred_element_type=F32, **kw)


def _dot_tn(a, b, **kw):
    return lax.dot_general(a, b, (((0,), (0,)), ((), ())), preferred_element_type=F32, **kw)


_HI = lax.Precision.HIGHEST


def _log_decay(sm, wgate, bgate):
    pre = jnp.dot(sm, wgate, preferred_element_type=F32, precision=_HI) + bgate
    return (jnp.minimum(pre, 0.0) - jnp.log1p(jnp.exp(-jnp.abs(pre)))) / GLA_GATE_TAU


def _gla_gate_out(o, r, gain):
    g = o * lax.rsqrt(jnp.mean(o * o, axis=-1, keepdims=True) + EPS) * gain
    return g * (r * jax.nn.sigmoid(r))


def _gla_prompt_kernel(gv_ref, gr_ref, gq_ref, gk_ref, sm_ref, wgate_ref, bgate_ref, gain_ref,
                       y_ref, st_ref, s_ref, b_ref, *, n_seq, n_chunks):
    c = pl.program_id(1)

    @pl.when(c == 0)
    def _():
        s_ref[...] = jnp.zeros_like(s_ref)

    tc = n_chunks * GLA_CHUNK
    ri = lax.broadcasted_iota(jnp.int32, (tc, tc), 0)
    ci = lax.broadcasted_iota(jnp.int32, (tc, tc), 1)
    tril = jnp.where(jnp.logical_and(ri // GLA_CHUNK == ci // GLA_CHUNK, ri >= ci), 1.0, 0.0).astype(F32)
    causal = (lax.broadcasted_iota(jnp.int32, (GLA_CHUNK, GLA_CHUNK), 0)
              >= lax.broadcasted_iota(jnp.int32, (GLA_CHUNK, GLA_CHUNK), 1))
    gain = gain_ref[...]
    for s in range(n_seq):
        la = _log_decay(sm_ref[s], wgate_ref[...], bgate_ref[...])
        b_ref[s] = jnp.dot(tril, la, preferred_element_type=F32, precision=_HI)
    for n in range(n_chunks):
        rows = slice(n * GLA_CHUNK, (n + 1) * GLA_CHUNK)
        for s in range(n_seq):
            b = b_ref[s, rows, :]
            b_end = b_ref[s, (n + 1) * GLA_CHUNK - 1:(n + 1) * GLA_CHUNK, :]
            k = gk_ref[s, rows, :]
            q_in = gq_ref[s, rows, :] * (GLA_DK ** -0.5) * jnp.exp(b)
            k_in = k * jnp.exp(-b)
            k_end = k * jnp.exp(b_end - b)
            decay = jnp.exp(b_end)
            for h in range(GLA_HEADS):
                dk = slice(h * GLA_DK, (h + 1) * GLA_DK)
                dv = slice(h * GLA_DV, (h + 1) * GLA_DV)
                v = gv_ref[s, rows, dv].astype(BF16)
                st = s_ref[s, h]
                qh = q_in[:, dk].astype(BF16)
                att = jnp.where(causal, _dot_nt(qh, k_in[:, dk].astype(BF16)), 0.0)
                o = _dot_nt(qh, st.astype(BF16)) + jnp.dot(att.astype(BF16), v, preferred_element_type=F32)
                s_ref[s, h] = st * decay[:, dk] + _dot_tn(v, k_end[:, dk].astype(BF16))
                y_ref[s, rows, dv] = _gla_gate_out(o, gr_ref[s, rows, dv], gain)

    @pl.when(c == pl.num_programs(1) - 1)
    def _():
        st_ref[...] = s_ref[...]


def gla_prompt(z, wgate, bgate, gain, *, batch, seq, tc, n_seq):
    m = z.shape[0]
    z3 = z.reshape(batch, seq, z.shape[1])
    col = lambda w, off: pl.BlockSpec((n_seq, tc, w), lambda b, c: (b, c, off // w))
    const = lambda shape: pl.BlockSpec(shape, lambda b, c: (0,) * len(shape))
    y, st = pl.pallas_call(
        functools.partial(_gla_prompt_kernel, n_seq=n_seq, n_chunks=tc // GLA_CHUNK),
        grid=(batch // n_seq, seq // tc),
        in_specs=[col(1024, AB_GV), col(1024, AB_GR), col(512, AB_GQ), col(512, AB_GK), col(128, AB_SMALL),
                  const((LANES, GLA_HEADS * GLA_DK)), const((1, GLA_HEADS * GLA_DK)), const((1, GLA_DV))],
        out_specs=[pl.BlockSpec((n_seq, tc, GLA_HEADS * GLA_DV), lambda b, c: (b, c, 0)),
                   pl.BlockSpec((n_seq, GLA_HEADS, GLA_DV, GLA_DK), lambda b, c: (b, 0, 0, 0))],
        out_shape=[jax.ShapeDtypeStruct((batch, seq, GLA_HEADS * GLA_DV), F32),
                   jax.ShapeDtypeStruct((batch, GLA_HEADS, GLA_DV, GLA_DK), F32)],
        scratch_shapes=[pltpu.VMEM((n_seq, GLA_HEADS, GLA_DV, GLA_DK), F32),
                        pltpu.VMEM((n_seq, tc, GLA_HEADS * GLA_DK), F32)],
        compiler_params=_cparams(("parallel", "arbitrary")),
        name="gla_prompt",
    )(z3, z3, z3, z3, z3, wgate, bgate, gain)
    return y.reshape(m, GLA_HEADS * GLA_DV), st


INT_MIN = -2 ** 31


def _order_key(score):
    score = jnp.where(score == 0.0, 0.0, score)
    bits = lax.bitcast_convert_type(score, jnp.int32)
    return jnp.where(bits < 0, bits ^ jnp.int32(0x7FFFFFFF), bits)


def _lane_total(x):
    return jnp.dot(x.astype(BF16), jnp.ones((LANES, LANES), BF16), preferred_element_type=F32)


ROW_SUB = 128


def _count_blocks(key_ref, rows, n_blocks, pred):
    acc = jnp.zeros((ROW_SUB, LANES), F32)
    for c in range(n_blocks):
        acc = acc + jnp.where(pred(key_ref[rows, c * LANES:(c + 1) * LANES]), 1.0, 0.0)
    return acc


def _masked_key():
    bits = int(np.array(NEG_INF, np.float32).view(np.int32))
    return bits ^ 0x7FFFFFFF


def _kth_largest_key(key_ref, t_ref, n_blocks, k, active):
    n_sub = key_ref.shape[0] // ROW_SUB
    t_ref[...] = jnp.full(t_ref.shape, INT_MIN, jnp.int32)

    def body(it, carry):
        step = lax.shift_left(jnp.int32(1), 31 - it)
        cands, accs = [], []
        for rb in range(n_sub):
            rows = slice(rb * ROW_SUB, (rb + 1) * ROW_SUB)
            cand = t_ref[rows, :] + step
            cands.append(cand)
            accs.append(_count_blocks(key_ref, rows, active(rb), lambda kc, cand=cand: kc >= cand))
        total = _lane_total(jnp.concatenate(accs, axis=0))
        for rb in range(n_sub):
            rows = slice(rb * ROW_SUB, (rb + 1) * ROW_SUB)
            skipped = float((n_blocks - active(rb)) * LANES)
            count = total[rows, :] + jnp.where(cands[rb] <= _masked_key(), skipped, 0.0)
            t_ref[rows, :] = jnp.where(count >= k, cands[rb], t_ref[rows, :])
        return carry

    lax.fori_loop(0, 32, body, 0)


def _select_topk(key_ref, t_ref, n_blocks, k, write_fn, active=None):
    if active is None:
        active = lambda rb: n_blocks
    _kth_largest_key(key_ref, t_ref, n_blocks, k, active)
    ri = lax.broadcasted_iota(jnp.int32, (LANES, LANES), 0)
    ci = lax.broadcasted_iota(jnp.int32, (LANES, LANES), 1)
    before = jnp.where(ri < ci, 1.0, 0.0).astype(BF16)
    for rb in range(key_ref.shape[0] // ROW_SUB):
        rows = slice(rb * ROW_SUB, (rb + 1) * ROW_SUB)
        t = t_ref[rows, :]
        skipped = float((n_blocks - active(rb)) * LANES)
        above = _lane_total(_count_blocks(key_ref, rows, active(rb), lambda kc: kc > t))
        need = k - above - jnp.where(t < _masked_key(), skipped, 0.0)
        run = jnp.zeros((ROW_SUB, LANES), F32)
        for c in range(n_blocks):
            if c >= active(rb):
                write_fn(rows, c, jnp.zeros((ROW_SUB, LANES), jnp.bool_))
                continue
            kc = key_ref[rows, c * LANES:(c + 1) * LANES]
            eq = jnp.where(kc == t, 1.0, 0.0)
            rank = jnp.dot(eq.astype(BF16), before, preferred_element_type=F32) + run
            take = jnp.where(kc > t, 1.0, jnp.where(rank < need, eq, 0.0))
            write_fn(rows, c, take > 0.0)
            run = run + _lane_total(eq)


def _half_mask(rows, upper):
    lane = lax.broadcasted_iota(jnp.int32, (rows, LANES), 1)
    return (lane >= 64) if upper else (lane < 64)


DSA_STRATUM = 512
KEY_CHUNK = 512


def _dsa_select_prompt_kernel(iq_ref, iw_ref, ikb_ref, bias_ref, key_ref, t_ref, *, row0, n_keys, topk):
    rows = iq_ref.shape[0]
    n_blocks = n_keys // LANES
    lane = lax.broadcasted_iota(jnp.int32, (ROW_SUB, LANES), 1)
    sub = lax.broadcasted_iota(jnp.int32, (ROW_SUB, LANES), 0)

    iw = iw_ref[...]
    qh = []
    for h in range(IDX_HEADS):
        pair = iq_ref[:, (h // 2) * LANES:(h // 2 + 1) * LANES]
        qh.append(jnp.where(_half_mask(rows, h % 2 == 1), pair, jnp.zeros_like(pair)))
    for kc in range(n_keys // KEY_CHUNK):
        ik = ikb_ref[kc * KEY_CHUNK:(kc + 1) * KEY_CHUNK, :]
        score = jnp.zeros((rows, KEY_CHUNK), F32)
        for h in range(IDX_HEADS):
            score = score + jnp.maximum(_dot_nt(qh[h], ik), 0.0) * iw[:, h:h + 1]
        for rb in range(rows // ROW_SUB):
            for cb in range(KEY_CHUNK // LANES):
                c = kc * (KEY_CHUNK // LANES) + cb
                causal = (c * LANES + lane) <= (row0 + rb * ROW_SUB + sub)
                part = score[rb * ROW_SUB:(rb + 1) * ROW_SUB, cb * LANES:(cb + 1) * LANES]
                key_ref[rb * ROW_SUB:(rb + 1) * ROW_SUB, c * LANES:(c + 1) * LANES] = _order_key(
                    jnp.where(causal, part, NEG_INF))

    def write(rws, c, sel):
        causal = (c * LANES + lane) <= (row0 + rws.start + sub)
        bias_ref[rws, c * LANES:(c + 1) * LANES] = jnp.where(jnp.logical_and(sel, causal), 0.0, NEG_INF)

    _select_topk(key_ref, t_ref, n_blocks, topk, write, active=lambda rb: row0 // LANES + rb + 1)


def _dsa_attend_prompt_kernel(q_ref, bias_ref, kb_ref, vb_ref, y_ref):
    bias = bias_ref[...]
    group = DSA_HEADS // DSA_KV_HEADS
    for kv in range(DSA_KV_HEADS):
        kvs = slice(kv * DSA_HD, (kv + 1) * DSA_HD)
        q4 = jnp.concatenate([q_ref[:, (kv * group + g) * DSA_HD:(kv * group + g + 1) * DSA_HD] for g in range(group)],
                             axis=0)
        s4 = _dot_nt(q4, kb_ref[:, kvs])
        ps, dens = [], []
        for g in range(group):
            s = s4[g * Q_BLOCK:(g + 1) * Q_BLOCK, :] + bias
            p = jnp.exp(s - jnp.max(s, axis=-1, keepdims=True))
            dens.append(jnp.sum(p, axis=-1, keepdims=True))
            ps.append(p.astype(BF16))
        o4 = jnp.dot(jnp.concatenate(ps, axis=0), vb_ref[:, kvs], preferred_element_type=F32)
        for g in range(group):
            h = kv * group + g
            y_ref[:, h * DSA_HD:(h + 1) * DSA_HD] = o4[g * Q_BLOCK:(g + 1) * Q_BLOCK, :] / dens[g]


def dsa_prompt(q, iq, iw, kb, vb, ikb, *, batch, seq):
    topk = min(DSA_TOPK_MAX, seq // 4)
    as3 = lambda a: a.reshape(batch, seq, a.shape[-1])
    q, iq, iw, kb, vb, ikb = (as3(a) for a in (q, iq, iw, kb, vb, ikb))
    nsub = DSA_STRATUM // Q_BLOCK
    outs = []
    for r in range(seq // DSA_STRATUM):
        n_keys = (r + 1) * DSA_STRATUM
        strat = lambda w: pl.BlockSpec((None, DSA_STRATUM, w), lambda b: (b, r, 0))
        keys = lambda w: pl.BlockSpec((None, n_keys, w), lambda b: (b, 0, 0))
        bias = pl.pallas_call(
            functools.partial(_dsa_select_prompt_kernel, row0=r * DSA_STRATUM, n_keys=n_keys, topk=topk),
            grid=(batch,),
            in_specs=[strat(512), strat(128), keys(128)],
            out_specs=pl.BlockSpec((None, DSA_STRATUM, n_keys), lambda b: (b, 0, 0)),
            out_shape=jax.ShapeDtypeStruct((batch, DSA_STRATUM, n_keys), F32),
            scratch_shapes=[pltpu.VMEM((DSA_STRATUM, n_keys), jnp.int32), pltpu.VMEM((DSA_STRATUM, LANES), jnp.int32)],
            compiler_params=_cparams(("parallel",)),
            name="dsa_select_prompt",
        )(iq, iw, ikb)
        qrow = lambda w: pl.BlockSpec((None, Q_BLOCK, w), lambda b, i: (b, r * nsub + i, 0))
        keys2 = lambda w: pl.BlockSpec((None, n_keys, w), lambda b, i: (b, 0, 0))
        outs.append(pl.pallas_call(
            _dsa_attend_prompt_kernel,
            grid=(batch, nsub),
            in_specs=[qrow(1024), pl.BlockSpec((None, Q_BLOCK, n_keys), lambda b, i: (b, i, 0)), keys2(256), keys2(256)],
            out_specs=pl.BlockSpec((None, Q_BLOCK, DSA_HEADS * DSA_HD), lambda b, i: (b, i, 0)),
            out_shape=jax.ShapeDtypeStruct((batch, DSA_STRATUM, DSA_HEADS * DSA_HD), F32),
            compiler_params=_cparams(("parallel", "arbitrary")),
            name="dsa_attend_prompt",
        )(q, bias, kb, vb))
    return jnp.concatenate(outs, axis=1).reshape(batch * seq, DSA_HEADS * DSA_HD)


def _swa_head_plan(h):
    group = SWA_HEADS // SWA_KV_HEADS
    kv = h // group
    return h // 2, h % 2, kv // 2, (kv % 2) != (h % 2)


def _swa_prompt_kernel(sink_ref, q_ref, kp_ref, kc_ref, kps_ref, kcs_ref, vp_ref, vc_ref, vps_ref, vcs_ref, y_ref):
    i = pl.program_id(1)
    r = lax.broadcasted_iota(jnp.int32, (Q_BLOCK, 2 * Q_BLOCK), 0)
    c = lax.broadcasted_iota(jnp.int32, (Q_BLOCK, 2 * Q_BLOCK), 1)
    rel = Q_BLOCK + r - c
    ok = (rel >= 0) & (rel <= WINDOW) & ((i - 1) * Q_BLOCK + c >= 0)
    bias = jnp.where(ok, 0.0, NEG_INF)
    keys = (jnp.concatenate([kp_ref[...], kc_ref[...]], axis=0), jnp.concatenate([kps_ref[...], kcs_ref[...]], axis=0))
    vals = (jnp.concatenate([vp_ref[...], vc_ref[...]], axis=0), jnp.concatenate([vps_ref[...], vcs_ref[...]], axis=0))
    lower = _half_mask(Q_BLOCK, False)
    for p in range(SWA_HEADS // 2):
        qpair = q_ref[:, p * LANES:(p + 1) * LANES]
        outs = []
        for h in (2 * p, 2 * p + 1):
            _, half, ks, swapped = _swa_head_plan(h)
            qh = jnp.where(_half_mask(Q_BLOCK, half == 1), qpair, jnp.zeros_like(qpair))
            kk = keys[int(swapped)][:, ks * LANES:(ks + 1) * LANES]
            vv = vals[int(swapped)][:, ks * LANES:(ks + 1) * LANES]
            s = _dot_nt(qh, kk) + bias
            sink = sink_ref[h]
            mx = jnp.maximum(jnp.max(s, axis=-1, keepdims=True), sink)
            pr = jnp.exp(s - mx)
            den = jnp.sum(pr, axis=-1, keepdims=True) + jnp.exp(sink - mx)
            outs.append(jnp.dot(pr.astype(BF16), vv, preferred_element_type=F32) / den)
        y_ref[:, p * LANES:(p + 1) * LANES] = jnp.where(lower, outs[0], outs[1])


def swa_prompt(q, kb, kbs, vb, vbs, sinks, *, batch, seq):
    m = q.shape[0]
    nq = seq // Q_BLOCK
    cur = pl.BlockSpec((Q_BLOCK, 256), lambda b, i, s: (b * nq + i, 0))
    prev = pl.BlockSpec((Q_BLOCK, 256), lambda b, i, s: (b * nq + jnp.maximum(i - 1, 0), 0))
    qspec = pl.BlockSpec((Q_BLOCK, 2048), lambda b, i, s: (b * nq + i, 0))
    return pl.pallas_call(
        _swa_prompt_kernel,
        grid_spec=pltpu.PrefetchScalarGridSpec(
            num_scalar_prefetch=1,
            grid=(batch, nq),
            in_specs=[qspec, prev, cur, prev, cur, prev, cur, prev, cur],
            out_specs=qspec,
        ),
        out_shape=jax.ShapeDtypeStruct((m, SWA_HEADS * SWA_HD), F32),
        compiler_params=_cparams(("parallel", "arbitrary")),
        name="swa_prompt",
    )(sinks, q, kb, kb, kbs, kbs, vb, vb, vbs, vbs)


def _mem_prompt_kernel(x_ref, g_ref, wq_ref, kv_ref, wo_ref, *rest, n_in):
    y_refs, w_refs, o_ref = rest[:n_in], rest[n_in:2 * n_in], rest[2 * n_in]
    x = x_ref[...]
    for y_ref, w_ref in zip(y_refs, w_refs):
        x = x + jnp.dot(y_ref[...].astype(BF16), w_ref[...], preferred_element_type=F32)
    h = _rms(x, g_ref[...]).astype(BF16)
    q = (jnp.dot(h, wq_ref[...], preferred_element_type=F32) * (MEM_HD ** -0.5)).astype(BF16)
    width = MEM_HEADS * MEM_HD
    outs = []
    for hd in range(MEM_HEADS):
        sl = slice(hd * MEM_HD, (hd + 1) * MEM_HD)
        k = kv_ref[:, sl].astype(BF16)
        v = kv_ref[:, width + hd * MEM_HD:width + (hd + 1) * MEM_HD].astype(BF16)
        s = _dot_nt(q[:, sl], k)
        p = jnp.exp(s - jnp.max(s, axis=-1, keepdims=True))
        o = jnp.dot(p.astype(BF16), v, preferred_element_type=F32) / jnp.sum(p, axis=-1, keepdims=True)
        outs.append(o.astype(BF16))
    o_ref[...] = x + jnp.dot(jnp.concatenate(outs, axis=-1), wo_ref[...], preferred_element_type=F32)


def mem_attn_prompt(x, ys, ws, g, wq, kv, wo, *, batch, seq, tq):
    m = x.shape[0]
    nq = seq // tq
    width = MEM_HEADS * MEM_HD
    rows = lambda w: pl.BlockSpec((tq, w), lambda b, i: (b * nq + i, 0))
    const = lambda shape: pl.BlockSpec(shape, lambda b, i: (0, 0), pipeline_mode=pl.Buffered(1))
    return pl.pallas_call(
        functools.partial(_mem_prompt_kernel, n_in=len(ys)),
        grid=(batch, nq),
        in_specs=[rows(D_MODEL),
                  pl.BlockSpec((1, D_MODEL), lambda b, i: (0, 0)),
                  const((D_MODEL, width)),
                  pl.BlockSpec((N_MEM, 2 * width), lambda b, i: (b, 0)),
                  const((width, D_MODEL))]
                 + [rows(y.shape[1]) for y in ys] + [const(w.shape) for w in ws],
        out_specs=rows(D_MODEL),
        out_shape=jax.ShapeDtypeStruct((m, D_MODEL), F32),
        compiler_params=_cparams(("parallel", "arbitrary")),
        name="mem_attn_prompt",
    )(x, g.reshape(1, D_MODEL), wq, kv, wo, *ys, *ws)


def _gla_prep_kernel(gq_ref, gk_ref, sm_ref, wgate_ref, bgate_ref, o_ref):
    w = GLA_HEADS * GLA_DK
    la = _log_decay(sm_ref[...], wgate_ref[...], bgate_ref[...])
    o_ref[:, 0:w] = jnp.exp(la)
    o_ref[:, w:2 * w] = gk_ref[...]
    o_ref[:, 2 * w:3 * w] = gq_ref[...] * (GLA_DK ** -0.5)
    o_ref[:, 3 * w:4 * w] = jnp.zeros((gq_ref.shape[0], w), F32)


def gla_sample_prep(z, wgate, bgate):
    m = z.shape[0]
    w = GLA_HEADS * GLA_DK
    col = lambda wd, off: pl.BlockSpec((m, wd), lambda i: (0, off // wd))
    return pl.pallas_call(
        _gla_prep_kernel,
        grid=(1,),
        in_specs=[col(512, AB_GQ), col(512, AB_GK), col(128, AB_SMALL),
                  pl.BlockSpec((LANES, w), lambda i: (0, 0)), pl.BlockSpec((1, w), lambda i: (0, 0))],
        out_specs=pl.BlockSpec((m, 4 * w), lambda i: (0, 0)),
        out_shape=jax.ShapeDtypeStruct((m, 4 * w), F32),
        compiler_params=_cparams(("arbitrary",)),
        name="gla_sample_prep",
    )(z, z, z, wgate, bgate)


def _gla_step_kernel(p_ref, gv_ref, gr_ref, s_ref, gain_ref, so_ref, y_ref, *, bs):
    gain = gain_ref[...]
    for s in range(bs):
        xt = p_ref[s].T
        for h in range(GLA_HEADS):
            dv = slice(h * GLA_DV, (h + 1) * GLA_DV)
            st = s_ref[s, h] * xt[:, h:h + 1] + xt[:, GLA_HEADS + h:GLA_HEADS + h + 1] * gv_ref[s:s + 1, dv]
            so_ref[s, h] = st
            o = jnp.sum(xt[:, 2 * GLA_HEADS + h:2 * GLA_HEADS + h + 1] * st, axis=0, keepdims=True)
            y_ref[s:s + 1, dv] = _gla_gate_out(o, gr_ref[s:s + 1, dv], gain)


def gla_sample(p, z, state, gain, *, bs):
    m = z.shape[0]
    col = lambda w, off: pl.BlockSpec((bs, w), lambda i: (i, off // w))
    sspec = pl.BlockSpec((bs, GLA_HEADS, GLA_DK, GLA_DV), lambda i: (i, 0, 0, 0))
    return pl.pallas_call(
        functools.partial(_gla_step_kernel, bs=bs),
        grid=(m // bs,),
        in_specs=[pl.BlockSpec((bs, 16, GLA_DK), lambda i: (i, 0, 0)), col(1024, AB_GV), col(1024, AB_GR), sspec,
                  pl.BlockSpec((1, GLA_DV), lambda i: (0, 0))],
        out_specs=[sspec, pl.BlockSpec((bs, GLA_HEADS * GLA_DV), lambda i: (i, 0))],
        out_shape=[jax.ShapeDtypeStruct(state.shape, F32), jax.ShapeDtypeStruct((m, GLA_HEADS * GLA_DV), F32)],
        compiler_params=_cparams(("parallel",)),
        name="gla_sample",
    )(p, z, z, state, gain)


def _dsa_scores_kernel(pt_ref, iq_ref, iw_ref, pool_hbm, o_ref, buf, sem, *, page_base, n_pages, depth):
    b = pl.program_id(0)
    nb = pl.num_programs(0)

    def page_copies(row):
        slot = row % (depth + 1)
        return [pltpu.make_async_copy(pool_hbm.at[page_base + pt_ref[row, j]], buf.at[slot, j], sem.at[slot])
                for j in range(n_pages)]

    @pl.when(b == 0)
    def _():
        for ahead in range(depth):
            for j, cp in enumerate(page_copies(ahead)):
                cp.start(priority=j % 2)

    @pl.when(b + depth < nb)
    def _():
        for j, cp in enumerate(page_copies(b + depth)):
            cp.start(priority=j % 2)

    for cp in page_copies(b):
        cp.wait()
    slot = b % (depth + 1)
    q8 = iq_ref[0]
    iw = iw_ref[0]
    for j in range(n_pages):
        dots = jnp.dot(q8, buf[slot, j].astype(BF16), preferred_element_type=F32)
        o_ref[0, j:j + 1, :] = jnp.sum(jnp.maximum(dots, 0.0) * iw, axis=0, keepdims=True)


def dsa_sample_scores(page_table, iq, iw, pool_ik_t, page_base, *, depth):
    m, n_pages = page_table.shape
    assert m > depth
    return pl.pallas_call(
        functools.partial(_dsa_scores_kernel, page_base=page_base, n_pages=n_pages, depth=depth),
        grid_spec=pltpu.PrefetchScalarGridSpec(
            num_scalar_prefetch=1,
            grid=(m,),
            in_specs=[pl.BlockSpec((1, IDX_HEADS, IDX_HD), lambda b, pt: (b, 0, 0)),
                      pl.BlockSpec((1, IDX_HEADS, 1), lambda b, pt: (b, 0, 0)),
                      pl.BlockSpec(memory_space=pl.ANY)],
            out_specs=pl.BlockSpec((1, n_pages, PAGE_SIZE), lambda b, pt: (b, 0, 0)),
            scratch_shapes=[pltpu.VMEM((depth + 1, n_pages, IDX_HD, PAGE_SIZE), F32),
                            pltpu.SemaphoreType.DMA((depth + 1,))],
        ),
        out_shape=jax.ShapeDtypeStruct((m, n_pages, PAGE_SIZE), F32),
        compiler_params=_cparams(("arbitrary",)),
        name="dsa_sample_scores",
    )(page_table, iq, iw, pool_ik_t)


def _dsa_select_sample_kernel(sc_ref, iq_ref, iw_ref, ikb_ref, bias_ref, bnew_ref, key_ref, t_ref, *, n_past, topk):
    rows = sc_ref.shape[0]
    n_blocks = n_past // LANES
    for c in range(n_blocks):
        sl = slice(c * LANES, (c + 1) * LANES)
        key_ref[:, sl] = _order_key(sc_ref[:, sl])
    ik = ikb_ref[...].astype(F32)
    iw = iw_ref[...]
    s_new = jnp.zeros((rows, 1), F32)
    for h in range(IDX_HEADS):
        pair = iq_ref[:, (h // 2) * LANES:(h // 2 + 1) * LANES].astype(F32)
        qh = jnp.where(_half_mask(rows, h % 2 == 1), pair, 0.0)
        s_new = s_new + jnp.maximum(jnp.sum(qh * ik, axis=-1, keepdims=True), 0.0) * iw[:, h:h + 1]
    lane = lax.broadcasted_iota(jnp.int32, (rows, LANES), 1)
    key_ref[:, n_past:n_past + LANES] = _order_key(jnp.where(lane == 0, s_new, -jnp.inf))

    ri = lax.broadcasted_iota(jnp.int32, (LANES, DSA_KV_HEADS * LANES), 0)
    ci = lax.broadcasted_iota(jnp.int32, (LANES, DSA_KV_HEADS * LANES), 1)
    spread = jnp.where(ci // DSA_KV_HEADS == ri, 1.0, 0.0).astype(BF16)

    def write(rws, c, sel):
        if c == n_blocks:
            bnew_ref[rws, :] = jnp.where(sel, 0.0, NEG_INF)
        else:
            wide = jnp.dot(jnp.where(sel, 1.0, 0.0).astype(BF16), spread, preferred_element_type=F32)
            w = DSA_KV_HEADS * LANES
            bias_ref[rws, c * w:(c + 1) * w] = jnp.where(wide > 0.5, 0.0, NEG_INF)

    _select_topk(key_ref, t_ref, n_blocks + 1, topk, write)


def dsa_sample_select(scores, iq, iw, ikb, *, topk):
    m, n_past = scores.shape
    full = lambda a: pl.BlockSpec(a.shape, lambda i: (0,) * a.ndim)
    wide = DSA_KV_HEADS * n_past
    return pl.pallas_call(
        functools.partial(_dsa_select_sample_kernel, n_past=n_past, topk=topk),
        grid=(1,),
        in_specs=[full(scores), full(iq), full(iw), full(ikb)],
        out_specs=[pl.BlockSpec((m, wide), lambda i: (0, 0)), pl.BlockSpec((m, LANES), lambda i: (0, 0))],
        out_shape=[jax.ShapeDtypeStruct((m, wide), F32), jax.ShapeDtypeStruct((m, LANES), F32)],
        scratch_shapes=[pltpu.VMEM((m, n_past + LANES), jnp.int32), pltpu.VMEM((m, LANES), jnp.int32)],
        compiler_params=_cparams(("arbitrary",)),
        name="dsa_sample_select",
    )(scores, iq, iw, ikb)


def _dsa_attend_sample_kernel(pt_ref, q_ref, bias_ref, bnew_ref, kn_ref, vn_ref, pk_hbm, pv_hbm, y_ref,
                              kbuf, vbuf, sem, *, page_base, n_pages, chunk):
    b = pl.program_id(0)
    wide = DSA_KV_HEADS * PAGE_SIZE
    n_chunks = n_pages // chunk
    group = DSA_HEADS // DSA_KV_HEADS

    def page_copies(row, c, slot):
        out = []
        for j in range(chunk):
            src = pl.ds(pl.multiple_of((page_base + pt_ref[row, c * chunk + j]) * wide, wide), wide)
            dst = pl.ds(j * wide, wide)
            out.append(pltpu.make_async_copy(pk_hbm.at[src, :], kbuf.at[slot, dst, :], sem.at[0, slot]))
            out.append(pltpu.make_async_copy(pv_hbm.at[src, :], vbuf.at[slot, dst, :], sem.at[1, slot]))
        return out

    @pl.when(b == 0)
    def _():
        for c in range(AHEAD_ATTEND):
            for cp in page_copies(0, c, c):
                cp.start()

    q8 = q_ref[0]
    first = lax.broadcasted_iota(jnp.int32, (DSA_HEADS, DSA_HD), 0) < group
    hrow = lax.broadcasted_iota(jnp.int32, (DSA_HEADS, wide), 0)
    col = lax.broadcasted_iota(jnp.int32, (DSA_HEADS, wide), 1)
    own = (col % DSA_KV_HEADS) == (hrow // group)
    m_run = jnp.full((DSA_HEADS, DSA_HD), NEG_INF, F32)
    l_run = jnp.zeros((DSA_HEADS, DSA_HD), F32)
    acc = jnp.zeros((DSA_HEADS, DSA_HD), F32)
    for c in range(n_chunks):
        slot = c
        ahead = c + AHEAD_ATTEND
        if ahead < n_chunks:
            for cp in page_copies(b, ahead, ahead):
                cp.start()
        else:
            @pl.when(b + 1 < pl.num_programs(0))
            def _():
                for cp in page_copies(b + 1, ahead - n_chunks, ahead - n_chunks):
                    cp.start()
        for cp in page_copies(b, c, slot):
            cp.wait()
        ss, oks = [], []
        for j in range(chunk):
            ok = jnp.logical_and(own, bias_ref[0, c * chunk + j:c * chunk + j + 1, :] == 0.0)
            kp = kbuf[slot, j * wide:(j + 1) * wide, :].astype(BF16)
            ss.append(jnp.where(ok, _dot_nt(q8, kp), NEG_INF))
            oks.append(ok)
        mx = ss[0]
        for s in ss[1:]:
            mx = jnp.maximum(mx, s)
        m_new = jnp.maximum(m_run, jnp.max(mx, axis=-1, keepdims=True))
        psum = jnp.zeros((DSA_HEADS, wide), F32)
        pv = jnp.zeros((DSA_HEADS, DSA_HD), F32)
        for j in range(chunk):
            p = jnp.where(oks[j], jnp.exp(ss[j] - m_new[:, 0:1]), 0.0)
            psum = psum + p
            vp = vbuf[slot, j * wide:(j + 1) * wide, :].astype(BF16)
            pv = pv + jnp.dot(p.astype(BF16), vp, preferred_element_type=F32)
        alpha = jnp.exp(m_run - m_new)
        l_run = alpha * l_run + jnp.sum(psum, axis=-1, keepdims=True)
        acc = alpha * acc + pv
        m_run = m_new

    kn = kn_ref[0].astype(BF16).astype(F32)
    vn = vn_ref[0].astype(BF16).astype(F32)
    bn = bnew_ref[0][:, 0:1]
    s_new = jnp.sum(q8.astype(F32) * jnp.where(first, kn[:, :DSA_HD], kn[:, DSA_HD:]), axis=-1, keepdims=True) + bn
    m_new = jnp.maximum(m_run, s_new)
    p_new = jnp.where(bn == 0.0, jnp.exp(s_new - m_new), 0.0)
    alpha = jnp.exp(m_run - m_new)
    acc = alpha * acc + p_new * jnp.where(first, vn[:, :DSA_HD], vn[:, DSA_HD:])
    y_ref[0] = acc / (alpha * l_run + p_new)


def dsa_sample_attend(page_table, q, bias, bnew, kn, vn, pool_k, pool_v, page_base, *, chunk):
    m, n_pages = page_table.shape
    wide = DSA_KV_HEADS * PAGE_SIZE
    n_chunks = n_pages // chunk
    assert n_pages % chunk == 0 and 0 < AHEAD_ATTEND < n_chunks
    per_b = lambda shape: pl.BlockSpec((1,) + shape, lambda b, pt: (b, 0, 0))
    hbm = pl.BlockSpec(memory_space=pl.ANY)
    return pl.pallas_call(
        functools.partial(_dsa_attend_sample_kernel, page_base=page_base, n_pages=n_pages, chunk=chunk),
        grid_spec=pltpu.PrefetchScalarGridSpec(
            num_scalar_prefetch=1,
            grid=(m,),
            in_specs=[per_b((DSA_HEADS, DSA_HD)), per_b((n_pages, wide)),
                      per_b((1, LANES)), per_b((1, DSA_KV_HEADS * DSA_HD)), per_b((1, DSA_KV_HEADS * DSA_HD)),
                      hbm, hbm],
            out_specs=per_b((DSA_HEADS, DSA_HD)),
            scratch_shapes=[pltpu.VMEM((n_chunks, chunk * wide, DSA_HD), F32),
                            pltpu.VMEM((n_chunks, chunk * wide, DSA_HD), F32),
                            pltpu.SemaphoreType.DMA((2, n_chunks))],
        ),
        out_shape=jax.ShapeDtypeStruct((m, DSA_HEADS, DSA_HD), F32),
        compiler_params=_cparams(("arbitrary",)),
        name="dsa_sample_attend",
    )(page_table, q, bias, bnew, kn, vn, pool_k, pool_v)


def _swa_step_kernel(q_ref, kt_ref, vt_ref, kn_ref, vn_ref, sink_ref, y_ref, *, bs):
    group = SWA_HEADS // SWA_KV_HEADS
    pairs = [(s, kv) for s in range(bs) for kv in range(SWA_KV_HEADS)]
    qs, scs, news = [], [], []
    for s, kv in pairs:
        qv = q_ref[s, kv * group:(kv + 1) * group, :]
        kn = kn_ref[s, kv:kv + 1, :].astype(BF16).astype(F32)
        scs.append(jnp.dot(qv, kt_ref[s, kv].astype(BF16), preferred_element_type=F32))
        news.append(jnp.sum(qv.astype(F32) * kn, axis=-1, keepdims=True))
    sc = jnp.concatenate(scs, axis=0)
    s_new = jnp.concatenate(news, axis=0)
    sink = jnp.concatenate([sink_ref[...]] * bs, axis=0)
    mx = jnp.maximum(jnp.maximum(jnp.max(sc, axis=-1, keepdims=True), s_new), sink)
    p = jnp.exp(sc - mx)
    p_new = jnp.exp(s_new - mx)
    inv = 1.0 / (jnp.sum(p, axis=-1, keepdims=True) + p_new + jnp.exp(sink - mx))
    pb = p.astype(BF16)
    for n, (s, kv) in enumerate(pairs):
        rows = slice(n * group, (n + 1) * group)
        vn = vn_ref[s, kv:kv + 1, :].astype(BF16).astype(F32)
        o = _dot_nt(pb[rows, :], vt_ref[s, kv].astype(BF16)) + p_new[rows, :] * vn
        y_ref[s, kv * group:(kv + 1) * group, :] = o * inv[rows, :]


def swa_sample(q, kt, vt, kn, vn, sinks, *, bs):
    m = q.shape[0]
    cache = pl.BlockSpec((bs, SWA_KV_HEADS, SWA_HD, WINDOW), lambda i: (i, 0, 0, 0))
    new = pl.BlockSpec((bs, SWA_KV_HEADS, SWA_HD), lambda i: (i, 0, 0))
    return pl.pallas_call(
        functools.partial(_swa_step_kernel, bs=bs),
        grid=(m // bs,),
        in_specs=[pl.BlockSpec((bs, SWA_HEADS, SWA_HD), lambda i: (i, 0, 0)), cache, cache, new, new,
                  pl.BlockSpec((SWA_HEADS, 1), lambda i: (0, 0))],
        out_specs=pl.BlockSpec((bs, SWA_HEADS, SWA_HD), lambda i: (i, 0, 0)),
        out_shape=jax.ShapeDtypeStruct((m, SWA_HEADS, SWA_HD), F32),
        compiler_params=_cparams(("parallel",)),
        name="swa_sample",
    )(q, kt, vt, kn, vn, sinks.reshape(SWA_HEADS, 1))


def _mem_step_kernel(q_ref, k_ref, v_ref, y_ref, *, bs):
    rows = q_ref.shape[1]
    n = N_MEM * MEM_HEADS
    hrow = lax.broadcasted_iota(jnp.int32, (rows, n), 0)
    col = lax.broadcasted_iota(jnp.int32, (rows, n), 1)
    own = (col % MEM_HEADS) == (hrow % MEM_HEADS)
    for s in range(bs):
        q = (q_ref[s] * (MEM_HD ** -0.5)).astype(BF16)
        sc = jnp.where(own, _dot_nt(q, k_ref[s * n:(s + 1) * n, :].astype(BF16)), NEG_INF)
        p = jnp.where(own, jnp.exp(sc - jnp.max(sc, axis=-1, keepdims=True)), 0.0)
        o = jnp.dot(p.astype(BF16), v_ref[s * n:(s + 1) * n, :].astype(BF16), preferred_element_type=F32)
        y_ref[s] = o / jnp.sum(p, axis=-1, keepdims=True)


def mem_attn_sample(q, mk, mv, layer, *, bs):
    m, rows, _ = q.shape
    n = N_MEM * MEM_HEADS
    nb = m // bs
    cache = pl.BlockSpec((bs * n, MEM_HD), lambda i: (layer * nb + i, 0))
    return pl.pallas_call(
        functools.partial(_mem_step_kernel, bs=bs),
        grid=(nb,),
        in_specs=[pl.BlockSpec((bs, rows, MEM_HD), lambda i: (i, 0, 0)), cache, cache],
        out_specs=pl.BlockSpec((bs, rows, MEM_HD), lambda i: (i, 0, 0)),
        out_shape=jax.ShapeDtypeStruct((m, rows, MEM_HD), F32),
        compiler_params=_cparams(("parallel",)),
        name="mem_attn_sample",
    )(q, mk, mv)


TM_FFN, TF_FFN, SUB_FFN = 1024, 512, 512
TM_PROJ, TN_MEM = 1024, 512
TM_GLA_PROJ = 512
TM_OUT = 512
TC_GLA, NSEQ_GLA = 256, 2
TQ_MEM = 512
BS_SAMPLE = 8
DEPTH_SCORES = 2
PG_ATTEND = 16
AHEAD_ATTEND = 2


def _prep_w_in_ab(w):
    sizes = (GLA_HEADS * GLA_DK, GLA_HEADS * GLA_DK, GLA_HEADS * GLA_DV, GLA_HEADS * GLA_DV, GLA_GATE_RANK,
             DSA_HEADS * DSA_HD, DSA_KV_HEADS * DSA_HD, DSA_KV_HEADS * DSA_HD, IDX_HEADS * IDX_HD, IDX_HEADS, IDX_HD)
    offs = np.cumsum((0,) + sizes)
    gq, gk, gv, gr, gd, dq, dk, dv, iq, iw, ik = [w[:, int(offs[j]):int(offs[j + 1])] for j in range(len(sizes))]
    pad = lambda n: jnp.zeros((w.shape[0], n), w.dtype)
    small = jnp.concatenate([ik, gd, iw, pad(LANES - IDX_HD - GLA_GATE_RANK - IDX_HEADS)], axis=1)
    w_gla = jnp.concatenate([gv, gr, gq, gk, small], axis=1)
    w_dsa = jnp.concatenate([dq, dk, dv, iq, small], axis=1)
    return w_gla.astype(BF16), w_dsa.astype(BF16)


def _prep_gate(w_up):
    return jnp.zeros((LANES, GLA_HEADS * GLA_DK), F32).at[SM_GD:SM_GD + GLA_GATE_RANK].set(w_up)


def kernel(x_prompt, x_sample, mem_prompt, cache_dsa_k, cache_dsa_v, cache_dsa_idx_k, state_gla, cache_swa_k, cache_swa_v, cache_mem_k, cache_mem_v, page_table, norm_ffn, w_ffn_gate, w_ffn_up, w_ffn_down, norm_mix, w_in_ab, w_gla_gate_up, b_gla_gate, gla_out_norm, idx_k_norm, w_out_ab, w_in_c, swa_sinks, w_out_c, norm_mem_q, norm_mem_src, w_mem_q, w_mem_kv, w_mem_o, final_norm):
    depth = norm_mix.shape[0]
    bp, seq, _ = x_prompt.shape
    bs = x_sample.shape[0]
    n_pool = cache_dsa_k.shape[1]
    gla_w = GLA_HEADS * GLA_DV

    w_ab = [_prep_w_in_ab(w_in_ab[i]) for i in range(w_in_ab.shape[0])]
    w_gate = [_prep_gate(w_gla_gate_up[i]) for i in range(w_in_ab.shape[0])]
    w_oab = w_out_ab.astype(BF16)
    w_c, w_oc = w_in_c.astype(BF16), w_out_c.astype(BF16)
    w_mq, w_mkv, w_mo = w_mem_q.astype(BF16), w_mem_kv.astype(BF16), w_mem_o.astype(BF16)

    def rope_tabs(pos):
        return _rope_tables(pos, DSA_HD) + _rope_tables(pos, IDX_HD)

    ffn_bf16 = {}

    def ffn_pair(x, layer, half, last, sample):
        fin = final_norm if last else None
        if sample:
            y, *ffn_bf16[layer, half] = ffn(x, norm_ffn[layer, half], w_ffn_gate, w_ffn_up, w_ffn_down, (layer, half),
                                            fin, tm=TM_FFN, tf=TF_FFN, sub=SUB_FFN)
            return y
        return ffn(x, norm_ffn[layer, half], *ffn_bf16[layer, half], None, fin, tm=TM_FFN, tf=TF_FFN, sub=SUB_FFN)

    def prompt_group():
        tabs_p = rope_tabs(jnp.arange(seq))
        x = x_prompt.reshape(bp * seq, D_MODEL)
        mem = mem_prompt.reshape(bp * N_MEM, D_MODEL)
        mem_kv = [norm_proj(mem, norm_mem_src[l], w_mkv[l], tm=TM_PROJ, tn=TN_MEM) for l in range(depth)]
        st_ab_p, st_c_p = [], []
        for l in range(depth):
            i = l // 2
            x = ffn_pair(x, l, 0, False, False)
            if l % 2 == 0:
                z = norm_proj(x, norm_mix[l], w_ab[i][0], tm=TM_GLA_PROJ, tn=AB_WIDTH)
                q, k, kb, v, vb, iq, ik, ikb, iw = d_proj_features(x, norm_mix[l], w_ab[i][1], tabs_p, idx_k_norm[i],
                                                                   seq=seq, tm=TM_PROJ, sub=SUB_FFN)
                y_gla, st_t = gla_prompt(z, w_gate[i], b_gla_gate[i].reshape(1, -1), gla_out_norm[i].reshape(1, -1),
                                         batch=bp, seq=seq, tc=TC_GLA, n_seq=NSEQ_GLA)
                y_dsa = dsa_prompt(q, iq, iw, kb, vb, ikb, batch=bp, seq=seq)
                mix_out = ([y_gla, y_dsa], [w_oab[i, :gla_w], w_oab[i, gla_w:]])
                n_pg = seq // PAGE_SIZE
                st_ab_p.append((k.reshape(bp, n_pg, PAGE_SIZE, DSA_KV_HEADS, DSA_HD),
                                v.reshape(bp, n_pg, PAGE_SIZE, DSA_KV_HEADS, DSA_HD),
                                ik.reshape(bp, n_pg, PAGE_SIZE, IDX_HD),
                                jnp.swapaxes(st_t, 2, 3)))
            else:
                q, k, kb, kbs, v, vb, vbs = c_proj_features(x, norm_mix[l], w_c[i], tabs_p[2:], seq=seq,
                                                            tm=TM_PROJ, sub=SUB_FFN)
                y = swa_prompt(q, kb, kbs, vb, vbs, swa_sinks[i], batch=bp, seq=seq)
                mix_out = ([y], [w_oc[i]])
                st_c_p.append((k.reshape(bp, seq, SWA_KV_HEADS, SWA_HD)[:, -WINDOW:],
                               v.reshape(bp, seq, SWA_KV_HEADS, SWA_HD)[:, -WINDOW:]))
            x = mem_attn_prompt(x, *mix_out, norm_mem_q[l], w_mq[l], mem_kv[l], w_mo[l], batch=bp, seq=seq, tq=TQ_MEM)
            x = ffn_pair(x, l, 1, l == depth - 1, False)
        return x.reshape(bp, seq, D_MODEL), st_ab_p, st_c_p, mem_kv

    tabs_s = rope_tabs(PAST_LEN + jnp.arange(1))
    n_pages = page_table.shape[1]
    topk = min(DSA_TOPK_MAX, (PAST_LEN + 1) // 4)
    pool_k = cache_dsa_k.reshape(-1, DSA_HD)
    pool_v = cache_dsa_v.reshape(-1, DSA_HD)
    pool_ik_t = jnp.swapaxes(cache_dsa_idx_k, 2, 3).reshape(-1, IDX_HD, PAGE_SIZE)
    mem_k_rows = cache_mem_k.reshape(-1, MEM_HD)
    mem_v_rows = cache_mem_v.reshape(-1, MEM_HD)
    x = x_sample.reshape(bs, D_MODEL)
    st_ab_s, st_c_s = [], []
    for l in range(depth):
        i = l // 2
        x = ffn_pair(x, l, 0, False, True)
        if l % 2 == 0:
            z = norm_proj(x, norm_mix[l], w_ab[i][0], tm=TM_GLA_PROJ, tn=AB_WIDTH)
            q, k, kb, v_new, vb, iq, ik, ikb, iw = d_proj_features(x, norm_mix[l], w_ab[i][1], tabs_s, idx_k_norm[i],
                                                                   seq=1, tm=TM_PROJ, sub=SUB_FFN)
            p = gla_sample_prep(z, w_gate[i], b_gla_gate[i].reshape(1, -1)).reshape(bs, 16, GLA_DK)
            st_new, y_gla = gla_sample(p, z, state_gla[i], gla_out_norm[i].reshape(1, -1), bs=BS_SAMPLE)
            scores = dsa_sample_scores(page_table, iq.reshape(bs, IDX_HEADS, IDX_HD),
                                       iw[:, :IDX_HEADS].reshape(bs, IDX_HEADS, 1), pool_ik_t, i * n_pool, depth=DEPTH_SCORES)
            bias, bnew = dsa_sample_select(scores.reshape(bs, n_pages * PAGE_SIZE), iq, iw, ikb, topk=topk)
            y_dsa = dsa_sample_attend(
                page_table, q.reshape(bs, DSA_HEADS, DSA_HD), bias.reshape(bs, n_pages, DSA_KV_HEADS * PAGE_SIZE),
                bnew.reshape(bs, 1, LANES), k.reshape(bs, 1, 256), v_new.reshape(bs, 1, 256),
                pool_k, pool_v, i * n_pool, chunk=PG_ATTEND).reshape(bs, DSA_HEADS * DSA_HD)
            x = out_proj(x, [y_gla, y_dsa], [w_oab[i, :gla_w], w_oab[i, gla_w:]], tm=TM_OUT)
            st_ab_s.append((k.reshape(bs, 1, DSA_KV_HEADS, DSA_HD), v_new.reshape(bs, 1, DSA_KV_HEADS, DSA_HD),
                            ik.reshape(bs, 1, IDX_HD), st_new))
        else:
            q, k, kb, kbs, v_new, vb, vbs = c_proj_features(x, norm_mix[l], w_c[i], tabs_s[2:], seq=1,
                                                            tm=TM_PROJ, sub=SUB_FFN)
            y = swa_sample(q.reshape(bs, SWA_HEADS, SWA_HD), jnp.transpose(cache_swa_k[i], (0, 2, 3, 1)),
                           jnp.transpose(cache_swa_v[i], (0, 2, 3, 1)), k.reshape(bs, SWA_KV_HEADS, SWA_HD),
                           v_new.reshape(bs, SWA_KV_HEADS, SWA_HD), swa_sinks[i], bs=BS_SAMPLE)
            x = out_proj(x, [y.reshape(bs, SWA_HEADS * SWA_HD)], [w_oc[i]], tm=TM_OUT)
            st_c_s.append((jnp.concatenate([cache_swa_k[i][:, 1:], k.reshape(bs, 1, SWA_KV_HEADS, SWA_HD)], axis=1),
                           jnp.concatenate([cache_swa_v[i][:, 1:], v_new.reshape(bs, 1, SWA_KV_HEADS, SWA_HD)], axis=1)))
        qm = norm_proj(x, norm_mem_q[l], w_mq[l], tm=TM_PROJ, tn=TN_MEM).reshape(bs, MEM_HEADS, MEM_HD)
        qm = jnp.pad(qm, ((0, 0), (0, SUBLANES - MEM_HEADS), (0, 0)))
        om = mem_attn_sample(qm, mem_k_rows, mem_v_rows, l, bs=BS_SAMPLE)
        x = out_proj(x, [om[:, :MEM_HEADS].reshape(bs, MEM_HEADS * MEM_HD)], [w_mo[l]], tm=TM_OUT)
        x = ffn_pair(x, l, 1, l == depth - 1, True)
    y_sample = x.reshape(bs, 1, D_MODEL)

    y_prompt, st_ab_p, st_c_p, mem_kv = prompt_group()

    stk = lambda sts, j: jnp.stack([s[j] for s in sts])
    mw = MEM_HEADS * MEM_HD
    mem_k_p = jnp.stack([kv[:, :mw].reshape(bp, N_MEM, MEM_HEADS, MEM_HD) for kv in mem_kv])
    mem_v_p = jnp.stack([kv[:, mw:].reshape(bp, N_MEM, MEM_HEADS, MEM_HD) for kv in mem_kv])
    return (y_prompt, y_sample, stk(st_ab_p, 0), stk(st_ab_p, 1), stk(st_ab_p, 2),
            stk(st_ab_s, 0), stk(st_ab_s, 1), stk(st_ab_s, 2), stk(st_ab_p, 3), stk(st_ab_s, 3),
            stk(st_c_p, 0), stk(st_c_p, 1), stk(st_c_s, 0), stk(st_c_s, 1), mem_k_p, mem_v_p)
```
